```python
import math
import jax, jax.numpy as jnp
from jax import lax
import numpy as np

D_MODEL = 1024
BATCH = 4
SEQ = 4096
DEPTH = 2

CHUNK = 64
MEM_LEN = 256
Q_BLOCK = 128
RMS_EPS = 1e-6
NEG_INF = -1e30

FOX_HEAD_DIM = 64
FOX_HEADS = (D_MODEL // 2) // FOX_HEAD_DIM
DIFF_QK_DIM = 64
DIFF_V_DIM = 2 * DIFF_QK_DIM
DIFF_HEADS = (D_MODEL // 2) // DIFF_V_DIM
ATTN_IN_WIDTH = 3 * FOX_HEADS * FOX_HEAD_DIM + FOX_HEADS + DIFF_HEADS * (4 * DIFF_QK_DIM + DIFF_V_DIM)
MLSTM_HEADS = 4
MLSTM_V_DIM = D_MODEL // MLSTM_HEADS
MLSTM_QK_DIM = MLSTM_V_DIM // 2
MLSTM_CONV = 4
MLSTM_IN_WIDTH = 2 * MLSTM_HEADS * MLSTM_QK_DIM + 2 * MLSTM_HEADS * MLSTM_V_DIM + 2 * MLSTM_HEADS
XATTN_HEADS = 4
XATTN_HEAD_DIM = D_MODEL // XATTN_HEADS
D_FF = 256 * math.ceil(8 * D_MODEL / 3 / 256)
FFN_CONV = 3
N_EVEN = (DEPTH + 1) // 2
N_ODD = DEPTH // 2

kernel_name = "fox_diff_mlstm_convffn_hybrid"


def rms_norm(x, gain):
    x32 = x.astype(jnp.float32)
    y = x32 * lax.rsqrt(jnp.mean(x32 * x32, axis=-1, keepdims=True) + RMS_EPS)
    return (y * gain.astype(jnp.float32)).astype(x.dtype)


def causal_depthwise_conv(x, w):
    K = w.shape[0]
    S = x.shape[1]
    xp = jnp.pad(x, ((0, 0), (K - 1, 0), (0, 0)))
    y = xp[:, 0:S, :] * w[0]
    for tap in range(1, K):
        y = y + xp[:, tap:tap + S, :] * w[tap]
    return y


def split_heads(t, n_heads):
    B, S, _ = t.shape
    return t.reshape(B, S, n_heads, -1).transpose(0, 2, 1, 3)


def merge_heads(t):
    B, H, S, d = t.shape
    return t.transpose(0, 2, 1, 3).reshape(B, S, H * d)


def to_blocks(t, size):
    B, H, S = t.shape[:3]
    return jnp.moveaxis(t.reshape(B, H, S // size, size, *t.shape[3:]), 2, 0)


def from_blocks(t):
    nb, B, H, size = t.shape[:4]
    return jnp.moveaxis(t, 0, 2).reshape(B, H, nb * size, *t.shape[4:])


def forgetting_attention(q, k, v, log_f):
    S = q.shape[2]
    scale = q.shape[-1] ** -0.5
    F = jnp.cumsum(log_f, axis=-1)
    k_pos = jnp.arange(S)
    starts = jnp.arange(S // Q_BLOCK) * Q_BLOCK

    def block(args):
        start, q_b, F_b = args
        q_pos = start + jnp.arange(Q_BLOCK)
        logits = jnp.einsum('bhqd,bhkd->bhqk', q_b, k, preferred_element_type=jnp.float32) * scale
        logits = logits + F_b[..., :, None] - F[..., None, :]
        logits = jnp.where(k_pos[None, :] <= q_pos[:, None], logits, NEG_INF)
        p = jax.nn.softmax(logits, axis=-1)
        return jnp.einsum('bhqk,bhkd->bhqd', p.astype(v.dtype), v)

    out = lax.map(block, (starts, to_blocks(q, Q_BLOCK), to_blocks(F, Q_BLOCK)))
    return from_blocks(out)


def differential_attention(q1, q2, k1, k2, v, lam):
    S = q1.shape[2]
    scale = q1.shape[-1] ** -0.5
    k_chunk = jnp.arange(S) // CHUNK
    starts = jnp.arange(S // Q_BLOCK) * Q_BLOCK

    def block(args):
        start, q1_b, q2_b = args
        q_chunk = (start + jnp.arange(Q_BLOCK)) // CHUNK
        mask = k_chunk[None, :] <= q_chunk[:, None]
        l1 = jnp.einsum('bhqd,bhkd->bhqk', q1_b, k1, preferred_element_type=jnp.float32) * scale
        l2 = jnp.einsum('bhqd,bhkd->bhqk', q2_b, k2, preferred_element_type=jnp.float32) * scale
        p1 = jax.nn.softmax(jnp.where(mask, l1, NEG_INF), axis=-1)
        p2 = jax.nn.softmax(jnp.where(mask, l2, NEG_INF), axis=-1)
        p = p1 - lam * p2
        return jnp.einsum('bhqk,bhkd->bhqd', p.astype(v.dtype), v)

    out = lax.map(block, (starts, to_blocks(q1, Q_BLOCK), to_blocks(q2, Q_BLOCK)))
    return from_blocks(out)


def fox_diff_mixer(xn, w_in, fox_bf, lq1, lk1, lq2, lk2, subln, w_out, lambda_init):
    B, S, _ = xn.shape
    proj = xn @ w_in
    fw = FOX_HEADS * FOX_HEAD_DIM
    dqk = DIFF_HEADS * 2 * DIFF_QK_DIM
    cuts = np.cumsum([fw, fw, fw, FOX_HEADS, dqk, dqk]).tolist()
    fq, fk, fv, ff, dq, dk, dv = jnp.split(proj, cuts, axis=-1)
    log_f = jax.nn.log_sigmoid((ff + fox_bf).astype(jnp.float32)).transpose(0, 2, 1)
    fox_out = forgetting_attention(split_heads(fq, FOX_HEADS), split_heads(fk, FOX_HEADS),
                                   split_heads(fv, FOX_HEADS), log_f)
    dq = dq.reshape(B, S, DIFF_HEADS, 2, DIFF_QK_DIM).transpose(0, 2, 3, 1, 4)
    dk = dk.reshape(B, S, DIFF_HEADS, 2, DIFF_QK_DIM).transpose(0, 2, 3, 1, 4)
    lam = (jnp.exp(jnp.sum(lq1.astype(jnp.float32) * lk1.astype(jnp.float32)))
           - jnp.exp(jnp.sum(lq2.astype(jnp.float32) * lk2.astype(jnp.float32))) + lambda_init)
    d_out = differential_attention(dq[:, :, 0], dq[:, :, 1], dk[:, :, 0], dk[:, :, 1],
                                   split_heads(dv, DIFF_HEADS), lam)
    d_out = rms_norm(d_out, subln) * (1.0 - lambda_init)
    mixed = jnp.concatenate([merge_heads(fox_out), merge_heads(d_out)], axis=-1)
    return mixed @ w_out


def mlstm_chunkwise(q, k, v, i_pre, log_f):
    B, H, _, dqk = q.shape
    dv = v.shape[-1]
    causal = jnp.tril(jnp.ones((CHUNK, CHUNK), dtype=bool))

    def step(carry, xs):
        C, n, m = carry
        qc, kc, vc, ic, fc = xs
        b = jnp.cumsum(fc, axis=-1)
        g = b[..., -1]
        D = jnp.where(causal, b[..., :, None] - b[..., None, :] + ic[..., None, :], NEG_INF)
        inter = b + m[..., None]
        m_t = jnp.maximum(inter, jnp.max(D, axis=-1))
        w_inter = jnp.exp(inter - m_t)
        W = jnp.exp(D - m_t[..., None])
        qk = jnp.einsum('bhtd,bhsd->bhts', qc, kc) * W
        num = w_inter[..., None] * jnp.einsum('bhtd,bhde->bhte', qc, C) + jnp.einsum('bhts,bhse->bhte', qk, vc)
        den = w_inter * jnp.einsum('bhtd,bhd->bht', qc, n) + jnp.sum(qk, axis=-1)
        h = num / jnp.maximum(jnp.abs(den), jnp.exp(-m_t))[..., None]
        a = g[..., None] - b + ic
        m_new = jnp.maximum(g + m, jnp.max(a, axis=-1))
        decay = jnp.exp(g + m - m_new)
        wk = jnp.exp(a - m_new[..., None])
        C_new = decay[..., None, None] * C + jnp.einsum('bhs,bhsd,bhse->bhde', wk, kc, vc)
        n_new = decay[..., None] * n + jnp.einsum('bhs,bhsd->bhd', wk, kc)
        return (C_new, n_new, m_new), h

    init = (jnp.zeros((B, H, dqk, dv), jnp.float32), jnp.zeros((B, H, dqk), jnp.float32),
            jnp.zeros((B, H), jnp.float32))
    xs = (to_blocks(q, CHUNK), to_blocks(k, CHUNK), to_blocks(v, CHUNK),
          to_blocks(i_pre, CHUNK), to_blocks(log_f, CHUNK))
    _, h = lax.scan(step, init, xs)
    return from_blocks(h)


def mlstm_mixer(xn, w_in, conv_qk, b_i, b_f, head_norm, w_out):
    proj = xn @ w_in
    qkw = MLSTM_HEADS * MLSTM_QK_DIM
    vw = MLSTM_HEADS * MLSTM_V_DIM
    cuts = np.cumsum([2 * qkw, vw, MLSTM_HEADS, MLSTM_HEADS]).tolist()
    qk_pre, v, ig, fg, og = jnp.split(proj, cuts, axis=-1)
    qk = jax.nn.silu(causal_depthwise_conv(qk_pre, conv_qk))
    q, k = jnp.split(qk, 2, axis=-1)
    q = split_heads(q, MLSTM_HEADS).astype(jnp.float32) * (MLSTM_QK_DIM ** -0.5)
    k = split_heads(k, MLSTM_HEADS).astype(jnp.float32)
    v = split_heads(v, MLSTM_HEADS).astype(jnp.float32)
    i_pre = (ig + b_i).astype(jnp.float32).transpose(0, 2, 1)
    log_f = jax.nn.log_sigmoid((fg + b_f).astype(jnp.float32)).transpose(0, 2, 1)
    h = mlstm_chunkwise(q, k, v, i_pre, log_f)
    h = rms_norm(h, head_norm.reshape(MLSTM_HEADS, 1, MLSTM_V_DIM))
    h = merge_heads(h).astype(xn.dtype) * jax.nn.sigmoid(og)
    return h @ w_out


def memory_cross_attention(xn, memn, wq, wkv, wo):
    q = split_heads(xn @ wq, XATTN_HEADS)
    k, v = jnp.split(memn @ wkv, 2, axis=-1)
    k = split_heads(k, XATTN_HEADS)
    v = split_heads(v, XATTN_HEADS)
    logits = jnp.einsum('bhqd,bhmd->bhqm', q, k, preferred_element_type=jnp.float32) * (XATTN_HEAD_DIM ** -0.5)
    p = jax.nn.softmax(logits, axis=-1)
    out = jnp.einsum('bhqm,bhmd->bhqd', p.astype(v.dtype), v)
    return merge_heads(out) @ wo


def conv_ffn(xn, w_up, conv_w, conv_b, w_down):
    h = causal_depthwise_conv(xn @ w_up, conv_w) + conv_b
    g, u = jnp.split(h, 2, axis=-1)
    return (jax.nn.gelu(g) * u) @ w_down


def setup_inputs(seed: int = 0) -> dict:
    key = jax.random.key(seed)
    ks = iter(jax.random.split(key, 40))

    def nrm(shape, scale):
        return jax.random.normal(next(ks), shape, jnp.float32) * scale

    def gain(shape):
        return 1.0 + nrm(shape, 0.02)

    out_scale = (2.0 * D_MODEL) ** -0.5
    return {
        'x': nrm((BATCH, SEQ, D_MODEL), 1.0),
        'mem': nrm((BATCH, MEM_LEN, D_MODEL), 1.0),
        'mix_norm': gain((DEPTH, D_MODEL)),
        'xattn_norm': gain((DEPTH, D_MODEL)),
        'mem_norm': gain((DEPTH, D_MODEL)),
        'ffn_norm': gain((DEPTH, D_MODEL)),
        'attn_w_in': nrm((N_EVEN, D_MODEL, ATTN_IN_WIDTH), D_MODEL ** -0.5),
        'attn_fox_bf': jnp.linspace(1.0, 4.0, FOX_HEADS, dtype=jnp.float32) + nrm((N_EVEN, FOX_HEADS), 0.1),
        'diff_lq1': nrm((N_EVEN, DIFF_QK_DIM), 0.1),
        'diff_lk1': nrm((N_EVEN, DIFF_QK_DIM), 0.1),
        'diff_lq2': nrm((N_EVEN, DIFF_QK_DIM), 0.1),
        'diff_lk2': nrm((N_EVEN, DIFF_QK_DIM), 0.1),
        'diff_subln': gain((N_EVEN, DIFF_V_DIM)),
        'attn_w_out': nrm((N_EVEN, D_MODEL, D_MODEL), out_scale),
        'mlstm_w_in': nrm((N_ODD, D_MODEL, MLSTM_IN_WIDTH), D_MODEL ** -0.5),
        'mlstm_conv_qk': nrm((N_ODD, MLSTM_CONV, 2 * MLSTM_HEADS * MLSTM_QK_DIM), MLSTM_CONV ** -0.5),
        'mlstm_b_i': nrm((N_ODD, MLSTM_HEADS), 0.1),
        'mlstm_b_f': jnp.linspace(3.0, 6.0, MLSTM_HEADS, dtype=jnp.float32) + nrm((N_ODD, MLSTM_HEADS), 0.1),
        'mlstm_head_norm': gain((N_ODD, MLSTM_HEADS * MLSTM_V_DIM)),
        'mlstm_w_out': nrm((N_ODD, D_MODEL, D_MODEL), out_scale),
        'xattn_wq': nrm((DEPTH, D_MODEL, D_MODEL), D_MODEL ** -0.5),
        'xattn_wkv': nrm((DEPTH, D_MODEL, 2 * D_MODEL), D_MODEL ** -0.5),
        'xattn_wo': nrm((DEPTH, D_MODEL, D_MODEL), out_scale),
        'ffn_w_up': nrm((DEPTH, D_MODEL, 2 * D_FF), D_MODEL ** -0.5),
        'ffn_conv_w': nrm((DEPTH, FFN_CONV, 2 * D_FF), FFN_CONV ** -0.5),
        'ffn_conv_b': nrm((DEPTH, 2 * D_FF), 0.02),
        'ffn_w_down': nrm((DEPTH, D_FF, D_MODEL), (2.0 * D_FF) ** -0.5),
        'final_norm': gain((D_MODEL,)),
    }


def reference(x, mem, mix_norm, xattn_norm, mem_norm, ffn_norm,
              attn_w_in, attn_fox_bf, diff_lq1, diff_lk1, diff_lq2, diff_lk2, diff_subln, attn_w_out,
              mlstm_w_in, mlstm_conv_qk, mlstm_b_i, mlstm_b_f, mlstm_head_norm, mlstm_w_out,
              xattn_wq, xattn_wkv, xattn_wo,
              ffn_w_up, ffn_conv_w, ffn_conv_b, ffn_w_down, final_norm):
    for layer in range(DEPTH):
        j = layer // 2
        xn = rms_norm(x, mix_norm[layer])
        if layer % 2 == 0:
            lambda_init = 0.8 - 0.6 * math.exp(-0.3 * layer)
            mixed = fox_diff_mixer(xn, attn_w_in[j], attn_fox_bf[j], diff_lq1[j], diff_lk1[j],
                                   diff_lq2[j], diff_lk2[j], diff_subln[j], attn_w_out[j], lambda_init)
        else:
            mixed = mlstm_mixer(xn, mlstm_w_in[j], mlstm_conv_qk[j], mlstm_b_i[j], mlstm_b_f[j],
                                mlstm_head_norm[j], mlstm_w_out[j])
        x = x + mixed
        x = x + memory_cross_attention(rms_norm(x, xattn_norm[layer]), rms_norm(mem, mem_norm[layer]),
                                       xattn_wq[layer], xattn_wkv[layer], xattn_wo[layer])
        x = x + conv_ffn(rms_norm(x, ffn_norm[layer]), ffn_w_up[layer], ffn_conv_w[layer],
                         ffn_conv_b[layer], ffn_w_down[layer])
    return rms_norm(x, final_norm)
```

```python
import functools
import math

import jax
import jax.numpy as jnp
from jax import lax
from jax.experimental import pallas as pl
from jax.experimental.pallas import tpu as pltpu

F32 = jnp.float32
BF16 = jnp.bfloat16

D_MODEL = 1024
RMS_EPS = 1e-6
NEG_INF = -1e30
CHUNK = 64
FOX_HEADS, FOX_DIM = 8, 64
DIFF_HEADS, DIFF_QK, DIFF_V = 4, 64, 128
ML_HEADS, ML_QK, ML_V, ML_CONV = 4, 128, 256, 4
X_HEADS, X_DIM = 4, 256
D_FF = 2816
FFN_CONV = 3
LANES = 128
BF16_ROWS = 16
VMEM_LIMIT = 56 * 1024 * 1024

ROW_TILE = 512
ATT_TILE = 256
ML_CHUNK = 256
X_TILE = 512
FFN_COLS = 1408


def _params(*sem):
    return pltpu.CompilerParams(dimension_semantics=sem, vmem_limit_bytes=VMEM_LIMIT)


def _rms(x, gain):
    return x * lax.rsqrt(jnp.mean(x * x, axis=-1, keepdims=True) + RMS_EPS) * gain


def _sigmoid(x):
    return 1.0 / (1.0 + jnp.exp(-x))


def _dot(a, b):
    return jnp.dot(a, b, preferred_element_type=F32)


def _dot_nt(a, b):
    return lax.dot_general(a, b, (((1,), (1,)), ((), ())), preferred_element_type=F32)


def _dot_tn(a, b):
    return lax.dot_general(a, b, (((0,), (0,)), ((), ())), preferred_element_type=F32)


def _norm_matmul_kernel(*refs, has_gates):
    if has_gates:
        x_ref, g_ref, w_ref, wg_ref, o_ref, og_ref, xn_ref = refs
    else:
        x_ref, g_ref, w_ref, o_ref, xn_ref = refs

    @pl.when(pl.program_id(1) == 0)
    def _():
        xn = _rms(x_ref[...], g_ref[...]).astype(BF16)
        xn_ref[...] = xn
        if has_gates:
            og_ref[...] = _dot(xn, wg_ref[...])

    o_ref[...] = _dot(xn_ref[...], w_ref[...]).astype(o_ref.dtype)


def norm_matmul(x, gain, w, w_gates=None, *, tm=ROW_TILE, tn=1024):
    T, D = x.shape
    N = w.shape[1]
    has_gates = w_gates is not None
    in_specs = [pl.BlockSpec((tm, D), lambda i, j: (i, 0)),
                pl.BlockSpec((1, D), lambda i, j: (0, 0)),
                pl.BlockSpec((D, tn), lambda i, j: (0, j))]
    out_specs = [pl.BlockSpec((tm, tn), lambda i, j: (i, j))]
    out_shape = [jax.ShapeDtypeStruct((T, N), BF16)]
    args = [x, gain.reshape(1, D), w]
    if has_gates:
        in_specs.append(pl.BlockSpec((D, LANES), lambda i, j: (0, 0)))
        out_specs.append(pl.BlockSpec((tm, LANES), lambda i, j: (i, 0)))
        out_shape.append(jax.ShapeDtypeStruct((T, LANES), F32))
        args.append(w_gates)
    outs = pl.pallas_call(
        functools.partial(_norm_matmul_kernel, has_gates=has_gates),
        grid=(T // tm, N // tn),
        in_specs=in_specs, out_specs=out_specs, out_shape=out_shape,
        scratch_shapes=[pltpu.VMEM((tm, D), BF16)],
        compiler_params=_params("parallel", "arbitrary"),
        name="norm_matmul",
    )(*args)
    return outs if has_gates else outs[0]


def _matmul_residual_kernel(*refs, n_in):
    res_ref, o_ref = refs[2 * n_in], refs[2 * n_in + 1]
    acc = res_ref[...]
    for a_ref, w_ref in zip(refs[:n_in], refs[n_in:2 * n_in]):
        acc = acc + _dot(a_ref[...], w_ref[...])
    o_ref[...] = acc


def matmul_residual(acts, weights, res, *, tm=ROW_TILE):
    T, D = res.shape
    n_in = len(acts)
    in_specs = ([pl.BlockSpec((tm, a.shape[1]), lambda i: (i, 0)) for a in acts]
                + [pl.BlockSpec(w.shape, lambda i: (0, 0)) for w in weights]
                + [pl.BlockSpec((tm, D), lambda i: (i, 0))])
    return pl.pallas_call(
        functools.partial(_matmul_residual_kernel, n_in=n_in),
        grid=(T // tm,),
        in_specs=in_specs,
        out_specs=pl.BlockSpec((tm, D), lambda i: (i, 0)),
        out_shape=jax.ShapeDtypeStruct((T, D), F32),
        compiler_params=_params("parallel"),
        name="matmul_residual",
    )(*acts, *weights, res)


def _online_softmax_step(s, v, m, l, acc):
    m_new = jnp.maximum(m, jnp.max(s, axis=1, keepdims=True))
    p = jnp.exp(s - m_new)
    alpha = jnp.exp(m - m_new)
    l = alpha * l + jnp.sum(p, axis=1, keepdims=True)
    acc = alpha * acc + _dot(p.astype(BF16), v)
    return m_new, l, acc


def _fox_kernel(q_ref, k_ref, v_ref, fc_ref, fr_ref, o_ref, *, tq):
    i = pl.program_id(1)
    lane = lax.broadcasted_iota(jnp.int32, (1, LANES), 1)
    row = lax.broadcasted_iota(jnp.int32, (tq, tq), 0)
    col = lax.broadcasted_iota(jnp.int32, (tq, tq), 1)
    causal = col <= row
    fc = fc_ref[0]
    for pair in range(FOX_HEADS // 2):
        cols = slice(pair * LANES, (pair + 1) * LANES)
        q = q_ref[0, :, cols]
        pair_out = []
        for half in range(2):
            h = 2 * pair + half
            in_head = (lane >= half * FOX_DIM) & (lane < (half + 1) * FOX_DIM)
            qh = jnp.where(in_head, q, jnp.zeros_like(q)) * (FOX_DIM ** -0.5)
            fq = fc[:, h:h + 1]

            def logits(j, qh=qh, fq=fq, h=h, cols=cols):
                start = pl.multiple_of(j * tq, tq)
                k = k_ref[0, pl.ds(start, tq), cols]
                v = v_ref[0, pl.ds(start, tq), cols]
                s = _dot_nt(qh, k) + (fq - fr_ref[0, h:h + 1, pl.ds(start, tq)])
                return s, v

            def body(j, carry, logits=logits):
                s, v = logits(j)
                return _online_softmax_step(s, v, *carry)

            init = (jnp.full((tq, 1), NEG_INF, F32), jnp.zeros((tq, 1), F32), jnp.zeros((tq, LANES), F32))
            carry = lax.fori_loop(0, i, body, init)
            s, v = logits(i)
            _, l, acc = _online_softmax_step(jnp.where(causal, s, NEG_INF), v, *carry)
            pair_out.append(acc / l)
        o_ref[0, :, cols] = jnp.where(lane < FOX_DIM, pair_out[0], pair_out[1]).astype(o_ref.dtype)


def fox_attention(proj, f_col, f_row, *, tq=ATT_TILE):
    B, S, _ = proj.shape
    width = FOX_HEADS * FOX_DIM
    return pl.pallas_call(
        functools.partial(_fox_kernel, tq=tq),
        grid=(B, S // tq),
        in_specs=[pl.BlockSpec((1, tq, width), lambda b, i: (b, i, 0)),
                  pl.BlockSpec((1, S, width), lambda b, i: (b, 0, 1)),
                  pl.BlockSpec((1, S, width), lambda b, i: (b, 0, 2)),
                  pl.BlockSpec((1, tq, FOX_HEADS), lambda b, i: (b, i, 0)),
                  pl.BlockSpec((1, FOX_HEADS, S), lambda b, i: (b, 0, 0))],
        out_specs=pl.BlockSpec((1, tq, width), lambda b, i: (b, i, 0)),
        out_shape=jax.ShapeDtypeStruct((B, S, width), BF16),
        compiler_params=_params("parallel", "arbitrary"),
        name="fox_attention",
    )(proj, proj, proj, f_col, f_row)


def _diff_kernel(q_ref, k_ref, v_ref, lq1_ref, lk1_ref, lq2_ref, lk2_ref, sub_ref, o_ref, *, tq, lambda_init):
    i = pl.program_id(1)
    lane = lax.broadcasted_iota(jnp.int32, (1, LANES), 1)
    row = lax.broadcasted_iota(jnp.int32, (tq, tq), 0)
    col = lax.broadcasted_iota(jnp.int32, (tq, tq), 1)
    visible = (col // CHUNK) <= (row // CHUNK)
    lam = (jnp.exp(jnp.sum(lq1_ref[...] * lk1_ref[...], axis=1, keepdims=True))
           - jnp.exp(jnp.sum(lq2_ref[...] * lk2_ref[...], axis=1, keepdims=True)) + lambda_init)
    for h in range(DIFF_HEADS):
        cols = slice(h * LANES, (h + 1) * LANES)
        q = q_ref[0, :, cols]
        zero = jnp.zeros_like(q)
        q1 = jnp.where(lane < DIFF_QK, q, zero) * (DIFF_QK ** -0.5)
        q2 = jnp.where(lane >= DIFF_QK, q, zero) * (DIFF_QK ** -0.5)

        def tiles(j, cols=cols):
            start = pl.multiple_of(j * tq, tq)
            return k_ref[0, pl.ds(start, tq), cols], v_ref[0, pl.ds(start, tq), cols]

        def body(j, carry, q1=q1, q2=q2, tiles=tiles):
            k, v = tiles(j)
            c1 = _online_softmax_step(_dot_nt(q1, k), v, *carry[:3])
            c2 = _online_softmax_step(_dot_nt(q2, k), v, *carry[3:])
            return c1 + c2

        one = (jnp.full((tq, 1), NEG_INF, F32), jnp.zeros((tq, 1), F32), jnp.zeros((tq, LANES), F32))
        carry = lax.fori_loop(0, i, body, one + one)
        k, v = tiles(i)
        _, l1, a1 = _online_softmax_step(jnp.where(visible, _dot_nt(q1, k), NEG_INF), v, *carry[:3])
        _, l2, a2 = _online_softmax_step(jnp.where(visible, _dot_nt(q2, k), NEG_INF), v, *carry[3:])
        out = a1 / l1 - lam * (a2 / l2)
        out = _rms(out, sub_ref[...]) * (1.0 - lambda_init)
        o_ref[0, :, cols] = out.astype(o_ref.dtype)


def diff_attention(proj, lq1, lk1, lq2, lk2, subln, lambda_init, *, tq=ATT_TILE):
    B, S, _ = proj.shape
    width = DIFF_HEADS * DIFF_V
    small = lambda n: pl.BlockSpec((1, n), lambda b, i: (0, 0))
    return pl.pallas_call(
        functools.partial(_diff_kernel, tq=tq, lambda_init=lambda_init),
        grid=(B, S // tq),
        in_specs=[pl.BlockSpec((1, tq, width), lambda b, i: (b, i, 3)),
                  pl.BlockSpec((1, S, width), lambda b, i: (b, 0, 4)),
                  pl.BlockSpec((1, S, width), lambda b, i: (b, 0, 5)),
                  small(DIFF_QK), small(DIFF_QK), small(DIFF_QK), small(DIFF_QK), small(DIFF_V)],
        out_specs=pl.BlockSpec((1, tq, width), lambda b, i: (b, i, 0)),
        out_shape=jax.ShapeDtypeStruct((B, S, width), BF16),
        compiler_params=_params("parallel", "arbitrary"),
        name="diff_attention",
    )(proj, proj, proj, lq1.reshape(1, -1), lk1.reshape(1, -1), lq2.reshape(1, -1), lk2.reshape(1, -1),
      subln.reshape(1, -1))


def _mlstm_kernel(qk_ref, halo_ref, v_ref, og_ref, gc_ref, gr_ref, cw_ref, hn_ref, o_ref, c_ref, n_ref, m_ref,
                  *, L):
    c = pl.program_id(1)

    @pl.when(c == 0)
    def _():
        c_ref[...] = jnp.zeros_like(c_ref)
        n_ref[...] = jnp.zeros_like(n_ref)
        m_ref[...] = jnp.zeros_like(m_ref)

    x = qk_ref[0].astype(F32)
    halo = jnp.where(c == 0, 0.0, halo_ref[0].astype(F32))
    xe = jnp.concatenate([halo, x], axis=0)
    cw = cw_ref[...]
    y = cw[ML_CONV - 1:ML_CONV] * x
    for tap in range(ML_CONV - 1):
        off = BF16_ROWS - (ML_CONV - 1) + tap
        y = y + cw[tap:tap + 1] * xe[off:off + L]
    y = y * _sigmoid(y)

    row = lax.broadcasted_iota(jnp.int32, (L, L), 0)
    col = lax.broadcasted_iota(jnp.int32, (L, L), 1)
    causal = col <= row
    gc = gc_ref[0]
    gr = gr_ref[0]
    kw_off = ML_HEADS * ML_QK
    for h in range(ML_HEADS):
        q = (y[:, h * ML_QK:(h + 1) * ML_QK] * (ML_QK ** -0.5)).astype(BF16)
        kf = y[:, kw_off + h * ML_QK:kw_off + (h + 1) * ML_QK]
        k = kf.astype(BF16)
        v = v_ref[0, :, h * ML_V:(h + 1) * ML_V]
        r_col = gc[:, h:h + 1] - gc[:, ML_HEADS + h:ML_HEADS + h + 1]
        b_col = gc[:, ML_HEADS + h:ML_HEADS + h + 1]
        b_row = gr[ML_HEADS + h:ML_HEADS + h + 1, :]
        r_row = gr[h:h + 1, :] - b_row
        g = b_row[:, L - 1:L]
        C = c_ref[h]
        n = n_ref[h]
        m = m_ref[h][:, 0:1]

        dm = jnp.where(causal, r_row, NEG_INF)
        mt = jnp.maximum(m, jnp.max(dm, axis=1, keepdims=True))
        w_inter = jnp.exp(m - mt)
        s = _dot_nt(q, k) * jnp.exp(dm - mt)
        num = w_inter * _dot(q, C.astype(BF16)) + _dot(s.astype(BF16), v)
        den = (w_inter * jnp.sum(q.astype(F32) * n, axis=1, keepdims=True)
               + jnp.sum(s, axis=1, keepdims=True))
        hh = num / jnp.maximum(jnp.abs(den), jnp.exp(-(b_col + mt)))

        m_next = jnp.maximum(m, jnp.max(r_row, axis=1, keepdims=True))
        decay = jnp.exp(m - m_next)
        kw = kf * jnp.exp(r_col - m_next)
        c_ref[h] = decay * C + _dot_tn(kw.astype(BF16), v)
        n_ref[h] = decay * n + jnp.sum(kw, axis=0, keepdims=True)
        m_ref[h] = jnp.broadcast_to(g + m_next, (1, LANES))

        vs = slice(h * ML_V, (h + 1) * ML_V)
        hh = _rms(hh, hn_ref[:, vs])
        o_ref[0, :, vs] = (hh * _sigmoid(og_ref[0, :, vs].astype(F32))).astype(o_ref.dtype)


def mlstm(proj, g_col, g_row, conv_w, head_norm, *, L=ML_CHUNK):
    B, S, _ = proj.shape
    W = D_MODEL
    halo_blocks = L // BF16_ROWS
    return pl.pallas_call(
        functools.partial(_mlstm_kernel, L=L),
        grid=(B, S // L),
        in_specs=[pl.BlockSpec((1, L, W), lambda b, c: (b, c, 0)),
                  pl.BlockSpec((1, BF16_ROWS, W), lambda b, c: (b, jnp.maximum(c * halo_blocks - 1, 0), 0)),
                  pl.BlockSpec((1, L, W), lambda b, c: (b, c, 1)),
                  pl.BlockSpec((1, L, W), lambda b, c: (b, c, 2)),
                  pl.BlockSpec((1, L, 2 * ML_HEADS), lambda b, c: (b, c, 0)),
                  pl.BlockSpec((1, 2 * ML_HEADS, L), lambda b, c: (b, 0, c)),
                  pl.BlockSpec((ML_CONV, W), lambda b, c: (0, 0)),
                  pl.BlockSpec((1, W), lambda b, c: (0, 0))],
        out_specs=pl.BlockSpec((1, L, W), lambda b, c: (b, c, 0)),
        out_shape=jax.ShapeDtypeStruct((B, S, W), BF16),
        scratch_shapes=[pltpu.VMEM((ML_HEADS, ML_QK, ML_V), F32),
                        pltpu.VMEM((ML_HEADS, 1, ML_QK), F32),
                        pltpu.VMEM((ML_HEADS, 1, LANES), F32)],
        compiler_params=_params("parallel", "arbitrary"),
        name="mlstm",
    )(proj, proj, proj, proj, g_col, g_row, conv_w, head_norm.reshape(1, W))


def _xattn_kernel(x_ref, g_ref, wq_ref, k_ref, v_ref, wo_ref, o_ref):
    x = x_ref[0]
    xn = _rms(x, g_ref[...]).astype(BF16)
    q = (_dot(xn, wq_ref[...]) * (X_DIM ** -0.5)).astype(BF16)
    heads = []
    for h in range(X_HEADS):
        cols = slice(h * X_DIM, (h + 1) * X_DIM)
        s = _dot_nt(q[:, cols], k_ref[0, :, cols])
        p = jnp.exp(s - jnp.max(s, axis=1, keepdims=True))
        l = jnp.sum(p, axis=1, keepdims=True)
        heads.append((_dot(p.astype(BF16), v_ref[0, :, cols]) / l).astype(BF16))
    o_ref[0] = x + _dot(jnp.concatenate(heads, axis=1), wo_ref[...])


def memory_cross_attention(x, gain, wq, kv, wo, *, tq=X_TILE):
    B, S, D = x.shape
    M = kv.shape[1]
    return pl.pallas_call(
        _xattn_kernel,
        grid=(B, S // tq),
        in_specs=[pl.BlockSpec((1, tq, D), lambda b, i: (b, i, 0)),
                  pl.BlockSpec((1, D), lambda b, i: (0, 0)),
                  pl.BlockSpec((D, D), lambda b, i: (0, 0)),
                  pl.BlockSpec((1, M, D), lambda b, i: (b, 0, 0)),
                  pl.BlockSpec((1, M, D), lambda b, i: (b, 0, 1)),
                  pl.BlockSpec((D, D), lambda b, i: (0, 0))],
        out_specs=pl.BlockSpec((1, tq, D), lambda b, i: (b, i, 0)),
        out_shape=jax.ShapeDtypeStruct((B, S, D), F32),
        compiler_params=_params("parallel", "parallel"),
        name="memory_cross_attention",
    )(x, gain.reshape(1, D), wq, kv, kv, wo)


def _gelu_tanh(x):
    return 0.5 * x * (1.0 + jnp.tanh(math.sqrt(2.0 / math.pi) * (x + 0.044715 * (x * x * x))))


def _ffn_kernel(x_ref, halo_ref, g_ref, wg_ref, wu_ref, cwg_ref, cwu_ref, cbg_ref, cbu_ref, wd_ref, fg_ref,
                o_ref, xn_ref, acc_ref, *, tm, tiles_per_seq, final_norm):
    i, j = pl.program_id(0), pl.program_id(1)

    @pl.when(j == 0)
    def _():
        x = x_ref[...]
        halo = _rms(halo_ref[...], g_ref[...])
        xn_ref[0:BF16_ROWS, :] = jnp.where(i % tiles_per_seq == 0, 0.0, halo).astype(BF16)
        xn_ref[BF16_ROWS:, :] = _rms(x, g_ref[...]).astype(BF16)
        acc_ref[...] = x

    xe = xn_ref[...]

    def conv_up(w_ref, cw_ref, cb_ref):
        up = _dot(xe, w_ref[...])
        cw = cw_ref[...]
        out = cb_ref[...] + cw[FFN_CONV - 1:FFN_CONV] * up[BF16_ROWS:BF16_ROWS + tm]
        for tap in range(FFN_CONV - 1):
            off = BF16_ROWS - (FFN_CONV - 1) + tap
            out = out + cw[tap:tap + 1] * up[off:off + tm]
        return out

    act = _gelu_tanh(conv_up(wg_ref, cwg_ref, cbg_ref)) * conv_up(wu_ref, cwu_ref, cbu_ref)
    acc_ref[...] += _dot(act.astype(BF16), wd_ref[...])

    @pl.when(j == pl.num_programs(1) - 1)
    def _():
        y = acc_ref[...]
        o_ref[...] = _rms(y, fg_ref[...]) if final_norm else y


def conv_ffn(x, gain, w_up, conv_w, conv_b, w_down, seq_len, final_gain=None, *, tm=ROW_TILE, tn=FFN_COLS):
    T, D = x.shape
    nj = D_FF // tn
    halo_blocks = tm // BF16_ROWS
    final_norm = final_gain is not None
    fg = (final_gain if final_norm else gain).reshape(1, D)
    conv_b = conv_b.reshape(1, -1)
    return pl.pallas_call(
        functools.partial(_ffn_kernel, tm=tm, tiles_per_seq=seq_len // tm, final_norm=final_norm),
        grid=(T // tm, nj),
        in_specs=[pl.BlockSpec((tm, D), lambda i, j: (i, 0)),
                  pl.BlockSpec((BF16_ROWS, D), lambda i, j: (jnp.maximum(i * halo_blocks - 1, 0), 0)),
                  pl.BlockSpec((1, D), lambda i, j: (0, 0)),
                  pl.BlockSpec((D, tn), lambda i, j: (0, j)),
                  pl.BlockSpec((D, tn), lambda i, j: (0, nj + j)),
                  pl.BlockSpec((FFN_CONV, tn), lambda i, j: (0, j)),
                  pl.BlockSpec((FFN_CONV, tn), lambda i, j: (0, nj + j)),
                  pl.BlockSpec((1, tn), lambda i, j: (0, j)),
                  pl.BlockSpec((1, tn), lambda i, j: (0, nj + j)),
                  pl.BlockSpec((tn, D), lambda i, j: (j, 0)),
                  pl.BlockSpec((1, D), lambda i, j: (0, 0))],
        out_specs=pl.BlockSpec((tm, D), lambda i, j: (i, 0)),
        out_shape=jax.ShapeDtypeStruct((T, D), F32),
        scratch_shapes=[pltpu.VMEM((tm + BF16_ROWS, D), BF16), pltpu.VMEM((tm, D), F32)],
        compiler_params=_params("parallel", "arbitrary"),
        name="conv_ffn",
    )(x, x, gain.reshape(1, D), w_up, w_up, conv_w, conv_w, conv_b, conv_b, w_down, fg)


def _pad_gate_cols(w):
    return jnp.pad(w, ((0, 0), (0, LANES - w.shape[1]))).astype(BF16)


def _fox_diff_layer(x, B, S, gain, w_in, fox_bf, lq1, lk1, lq2, lk2, subln, w_out, lambda_init):
    fw = FOX_HEADS * FOX_DIM
    g0 = 3 * fw
    w_main = jnp.concatenate([w_in[:, :g0], w_in[:, g0 + FOX_HEADS:]], axis=1).astype(BF16)
    proj, gates = norm_matmul(x, gain, w_main, _pad_gate_cols(w_in[:, g0:g0 + FOX_HEADS]))
    log_f = jax.nn.log_sigmoid(gates[:, :FOX_HEADS] + fox_bf).reshape(B, S, FOX_HEADS)
    f_col = jnp.cumsum(log_f, axis=1)
    proj = proj.reshape(B, S, -1)
    fox = fox_attention(proj, f_col, f_col.transpose(0, 2, 1))
    dif = diff_attention(proj, lq1, lk1, lq2, lk2, subln, lambda_init)
    w_out = w_out.astype(BF16)
    return matmul_residual([fox.reshape(B * S, -1), dif.reshape(B * S, -1)], [w_out[:fw], w_out[fw:]], x)


def _mlstm_layer(x, B, S, gain, w_in, conv_qk, b_i, b_f, head_norm, w_out):
    g0 = 2 * ML_HEADS * ML_QK + ML_HEADS * ML_V
    w_main = jnp.concatenate([w_in[:, :g0], w_in[:, g0 + 2 * ML_HEADS:]], axis=1).astype(BF16)
    proj, gates = norm_matmul(x, gain, w_main, _pad_gate_cols(w_in[:, g0:g0 + 2 * ML_HEADS]))
    i_pre = (gates[:, :ML_HEADS] + b_i).reshape(B, S, ML_HEADS)
    log_f = jax.nn.log_sigmoid(gates[:, ML_HEADS:2 * ML_HEADS] + b_f)
    b = jnp.cumsum(log_f.reshape(B, S // ML_CHUNK, ML_CHUNK, ML_HEADS), axis=2).reshape(B, S, ML_HEADS)
    g_col = jnp.concatenate([i_pre, b], axis=-1)
    h = mlstm(proj.reshape(B, S, -1), g_col, g_col.transpose(0, 2, 1), conv_qk, head_norm)
    return matmul_residual([h.reshape(B * S, -1)], [w_out.astype(BF16)], x)


def kernel(x, mem, mix_norm, xattn_norm, mem_norm, ffn_norm, attn_w_in, attn_fox_bf, diff_lq1, diff_lk1, diff_lq2, diff_lk2, diff_subln, attn_w_out, mlstm_w_in, mlstm_conv_qk, mlstm_b_i, mlstm_b_f, mlstm_head_norm, mlstm_w_out, xattn_wq, xattn_wkv, xattn_wo, ffn_w_up, ffn_conv_w, ffn_conv_b, ffn_w_down, final_norm):
    B, S, D = x.shape
    M = mem.shape[1]
    depth = mix_norm.shape[0]
    x = x.reshape(B * S, D)
    mem2 = mem.reshape(B * M, D)
    for layer in range(depth):
        j = layer // 2
        if layer % 2 == 0:
            lambda_init = 0.8 - 0.6 * math.exp(-0.3 * layer)
            x = _fox_diff_layer(x, B, S, mix_norm[layer], attn_w_in[j], attn_fox_bf[j], diff_lq1[j], diff_lk1[j],
                                diff_lq2[j], diff_lk2[j], diff_subln[j], attn_w_out[j], lambda_init)
        else:
            x = _mlstm_layer(x, B, S, mix_norm[layer], mlstm_w_in[j], mlstm_conv_qk[j], mlstm_b_i[j], mlstm_b_f[j],
                             mlstm_head_norm[j], mlstm_w_out[j])
        kv = norm_matmul(mem2, mem_norm[layer], xattn_wkv[layer].astype(BF16)).reshape(B, M, 2 * D)
        x = memory_cross_attention(x.reshape(B, S, D), xattn_norm[layer], xattn_wq[layer].astype(BF16), kv,
                                   xattn_wo[layer].astype(BF16)).reshape(B * S, D)
        x = conv_ffn(x, ffn_norm[layer], ffn_w_up[layer].astype(BF16), ffn_conv_w[layer], ffn_conv_b[layer],
                     ffn_w_down[layer].astype(BF16), S, final_norm if layer == depth - 1 else None)
    return x.reshape(B, S, D)
```

```python
import functools
import math

import jax
import jax.numpy as jnp
from jax import lax
from jax.experimental import pallas as pl
from jax.experimental.pallas import tpu as pltpu

F32 = jnp.float32
BF16 = jnp.bfloat16

D_MODEL = 1024
RMS_EPS = 1e-6
NEG_INF = -1e30
CHUNK = 64
FOX_HEADS, FOX_DIM = 8, 64
DIFF_HEADS, DIFF_QK, DIFF_V = 4, 64, 128
ML_HEADS, ML_QK, ML_V, ML_CONV = 4, 128, 256, 4
X_HEADS, X_DIM = 4, 256
D_FF = 2816
FFN_CONV = 3
LANES = 128
BF16_ROWS = 16
VMEM_LIMIT = 56 * 1024 * 1024

ROW_TILE = 512
ATT_TILE = 256
ML_CHUNK = 256
X_TILE = 512
FFN_COLS = 1408
TRANSPOSE_COLS = 512
F_PIECES = 3
FOX_BIAS_K_LANE = 0
FOX_BIAS_Q_LANE = 32


def _params(*sem):
    return pltpu.CompilerParams(dimension_semantics=sem, vmem_limit_bytes=VMEM_LIMIT)


def _rms(x, gain):
    return x * lax.rsqrt(jnp.mean(x * x, axis=-1, keepdims=True) + RMS_EPS) * gain


def _sigmoid(x):
    return 1.0 / (1.0 + jnp.exp(-x))


def _dot(a, b):
    return jnp.dot(a, b, preferred_element_type=F32)


def _dot_nt(a, b):
    return lax.dot_general(a, b, (((1,), (1,)), ((), ())), preferred_element_type=F32)


def _dot_tn(a, b):
    return lax.dot_general(a, b, (((0,), (0,)), ((), ())), preferred_element_type=F32)


def _norm_matmul_kernel(*refs, has_gates):
    if has_gates:
        x_ref, g_ref, w_ref, wg_ref, o_ref, og_ref, xn_ref = refs
    else:
        x_ref, g_ref, w_ref, o_ref, xn_ref = refs

    @pl.when(pl.program_id(1) == 0)
    def _():
        xn = _rms(x_ref[...], g_ref[...]).astype(BF16)
        xn_ref[...] = xn
        if has_gates:
            og_ref[...] = _dot(xn, wg_ref[...])

    o_ref[...] = _dot(xn_ref[...], w_ref[...]).astype(o_ref.dtype)


def norm_matmul(x, gain, w, w_gates=None, *, tm=ROW_TILE, tn=1024):
    T, D = x.shape
    N = w.shape[1]
    has_gates = w_gates is not None
    in_specs = [pl.BlockSpec((tm, D), lambda i, j: (i, 0)),
                pl.BlockSpec((1, D), lambda i, j: (0, 0)),
                pl.BlockSpec((D, tn), lambda i, j: (0, j))]
    out_specs = [pl.BlockSpec((tm, tn), lambda i, j: (i, j))]
    out_shape = [jax.ShapeDtypeStruct((T, N), BF16)]
    args = [x, gain.reshape(1, D), w]
    if has_gates:
        in_specs.append(pl.BlockSpec((D, LANES), lambda i, j: (0, 0)))
        out_specs.append(pl.BlockSpec((tm, LANES), lambda i, j: (i, 0)))
        out_shape.append(jax.ShapeDtypeStruct((T, LANES), F32))
        args.append(w_gates)
    outs = pl.pallas_call(
        functools.partial(_norm_matmul_kernel, has_gates=has_gates),
        grid=(T // tm, N // tn),
        in_specs=in_specs, out_specs=out_specs, out_shape=out_shape,
        scratch_shapes=[pltpu.VMEM((tm, D), BF16)],
        compiler_params=_params("parallel", "arbitrary"),
        name="norm_matmul",
    )(*args)
    return outs if has_gates else outs[0]


def _matmul_residual_kernel(*refs, n_in):
    res_ref, o_ref = refs[2 * n_in], refs[2 * n_in + 1]
    acc = res_ref[...]
    for a_ref, w_ref in zip(refs[:n_in], refs[n_in:2 * n_in]):
        acc = acc + _dot(a_ref[...], w_ref[...])
    o_ref[...] = acc


def matmul_residual(acts, weights, res, *, tm=ROW_TILE):
    T, D = res.shape
    n_in = len(acts)
    in_specs = ([pl.BlockSpec((tm, a.shape[1]), lambda i: (i, 0)) for a in acts]
                + [pl.BlockSpec(w.shape, lambda i: (0, 0)) for w in weights]
                + [pl.BlockSpec((tm, D), lambda i: (i, 0))])
    return pl.pallas_call(
        functools.partial(_matmul_residual_kernel, n_in=n_in),
        grid=(T // tm,),
        in_specs=in_specs,
        out_specs=pl.BlockSpec((tm, D), lambda i: (i, 0)),
        out_shape=jax.ShapeDtypeStruct((T, D), F32),
        compiler_params=_params("parallel"),
        name="matmul_residual",
    )(*acts, *weights, res)


def _build_vt(v_ref, vt_ref, n_heads, rows):
    S = v_ref.shape[1]
    r = lax.broadcasted_iota(jnp.int32, (LANES, LANES), 0)
    c = lax.broadcasted_iota(jnp.int32, (LANES, LANES), 1)
    eye = jnp.where(r == c, 1.0, 0.0).astype(BF16)
    per_group = LANES // rows
    for g in range(n_heads // per_group):
        for c0 in range(0, S, TRANSPOSE_COLS):
            cs = slice(c0, c0 + TRANSPOSE_COLS)
            vt = _dot_nt(eye, v_ref[0, cs, g * LANES:(g + 1) * LANES]).astype(BF16)
            for k in range(per_group):
                vt_ref[g * per_group + k, 0:rows, cs] = vt[k * rows:(k + 1) * rows]
    for h in range(n_heads):
        vt_ref[h, rows:rows + BF16_ROWS, :] = jnp.ones((BF16_ROWS, S), BF16)


def _softmax_step(s_t, vt, m_ref, acc_ref, idx):
    m_prev = m_ref[idx]
    m_new = jnp.maximum(m_prev, jnp.max(s_t, axis=0, keepdims=True))
    p = jnp.exp(s_t - m_new).astype(BF16)
    acc_ref[idx] = jnp.exp(m_prev - m_new) * acc_ref[idx] + _dot(vt, p)
    m_ref[idx] = m_new


def _causal_steps(i, step):
    def body(j, carry):
        step(j, False)
        return carry
    lax.fori_loop(0, i, body, 0)
    step(i, True)


def _fox_kernel(q_ref, k_ref, v_ref, fq_ref, fk_ref, o_ref, vt_ref, qc_ref, acc_ref, m_ref, *, tq):
    i = pl.program_id(1)

    @pl.when(i == 0)
    def _():
        _build_vt(v_ref, vt_ref, FOX_HEADS, FOX_DIM)

    lane = lax.broadcasted_iota(jnp.int32, (1, LANES), 1)
    fq = fq_ref[0]
    for h in range(FOX_HEADS):
        pair, half = divmod(h, 2)
        q = q_ref[0, :, pair * LANES:(pair + 1) * LANES]
        in_head = (lane >= half * FOX_DIM) & (lane < (half + 1) * FOX_DIM)
        lo_k, lo_q = FOX_BIAS_K_LANE + F_PIECES * h, FOX_BIAS_Q_LANE + F_PIECES * h
        mine = ((lane >= lo_k) & (lane < lo_k + F_PIECES)) | ((lane >= lo_q) & (lane < lo_q + F_PIECES))
        qc_ref[h, :, 0:LANES] = jnp.where(in_head, q, jnp.zeros_like(q)) * (FOX_DIM ** -0.5)
        qc_ref[h, :, LANES:2 * LANES] = jnp.where(mine, fq, jnp.zeros_like(fq))
    m_ref[...] = jnp.full(m_ref.shape, NEG_INF, F32)
    acc_ref[...] = jnp.zeros(acc_ref.shape, F32)

    key = lax.broadcasted_iota(jnp.int32, (tq, tq), 0)
    qry = lax.broadcasted_iota(jnp.int32, (tq, tq), 1)
    causal = key <= qry

    def step(j, masked):
        rows = pl.ds(pl.multiple_of(j * tq, tq), tq)
        fk = fk_ref[0, rows, :]
        for pair in range(FOX_HEADS // 2):
            kc = jnp.concatenate([k_ref[0, rows, pair * LANES:(pair + 1) * LANES], fk], axis=1)
            for half in range(2):
                h = 2 * pair + half
                s_t = _dot_nt(kc, qc_ref[h])
                if masked:
                    s_t = jnp.where(causal, s_t, NEG_INF)
                _softmax_step(s_t, vt_ref[h, :, rows], m_ref, acc_ref, h)

    _causal_steps(i, step)

    for pair in range(FOX_HEADS // 2):
        halves = []
        for half in range(2):
            a = acc_ref[2 * pair + half]
            halves.append(a[0:FOX_DIM] / a[FOX_DIM:FOX_DIM + 1])
        o_ref[0, :, pair * LANES:(pair + 1) * LANES] = jnp.concatenate(halves, axis=0).T.astype(o_ref.dtype)


def fox_attention(proj, fq, fk, *, tq=ATT_TILE):
    B, S, _ = proj.shape
    width = FOX_HEADS * FOX_DIM
    rows = FOX_DIM + BF16_ROWS
    return pl.pallas_call(
        functools.partial(_fox_kernel, tq=tq),
        grid=(B, S // tq),
        in_specs=[pl.BlockSpec((1, tq, width), lambda b, i: (b, i, 0)),
                  pl.BlockSpec((1, S, width), lambda b, i: (b, 0, 1)),
                  pl.BlockSpec((1, S, width), lambda b, i: (b, 0, 2)),
                  pl.BlockSpec((1, tq, LANES), lambda b, i: (b, i, 0)),
                  pl.BlockSpec((1, S, LANES), lambda b, i: (b, 0, 0))],
        out_specs=pl.BlockSpec((1, tq, width), lambda b, i: (b, i, 0)),
        out_shape=jax.ShapeDtypeStruct((B, S, width), BF16),
        scratch_shapes=[pltpu.VMEM((FOX_HEADS, rows, S), BF16),
                        pltpu.VMEM((FOX_HEADS, tq, 2 * LANES), BF16),
                        pltpu.VMEM((FOX_HEADS, rows, tq), F32),
                        pltpu.VMEM((FOX_HEADS, 1, tq), F32)],
        compiler_params=_params("parallel", "arbitrary"),
        name="fox_attention",
    )(proj, proj, proj, fq, fk)


def _diff_kernel(q_ref, k_ref, v_ref, lq1_ref, lk1_ref, lq2_ref, lk2_ref, sub_ref, o_ref, vt_ref, qc_ref, acc_ref,
                 m_ref, *, tq, lambda_init):
    i = pl.program_id(1)

    @pl.when(i == 0)
    def _():
        _build_vt(v_ref, vt_ref, DIFF_HEADS, DIFF_V)

    lane = lax.broadcasted_iota(jnp.int32, (1, LANES), 1)
    for h in range(DIFF_HEADS):
        q = q_ref[0, :, h * LANES:(h + 1) * LANES]
        zero = jnp.zeros_like(q)
        qc_ref[2 * h] = jnp.where(lane < DIFF_QK, q, zero) * (DIFF_QK ** -0.5)
        qc_ref[2 * h + 1] = jnp.where(lane >= DIFF_QK, q, zero) * (DIFF_QK ** -0.5)
    m_ref[...] = jnp.full(m_ref.shape, NEG_INF, F32)
    acc_ref[...] = jnp.zeros(acc_ref.shape, F32)

    key = lax.broadcasted_iota(jnp.int32, (tq, tq), 0)
    qry = lax.broadcasted_iota(jnp.int32, (tq, tq), 1)
    visible = (key // CHUNK) <= (qry // CHUNK)

    def step(j, masked):
        rows = pl.ds(pl.multiple_of(j * tq, tq), tq)
        for h in range(DIFF_HEADS):
            k = k_ref[0, rows, h * LANES:(h + 1) * LANES]
            vt = vt_ref[h, :, rows]
            for c in range(2):
                s_t = _dot_nt(k, qc_ref[2 * h + c])
                if masked:
                    s_t = jnp.where(visible, s_t, NEG_INF)
                _softmax_step(s_t, vt, m_ref, acc_ref, 2 * h + c)

    _causal_steps(i, step)

    lam = (jnp.exp(jnp.sum(lq1_ref[...] * lk1_ref[...], axis=1, keepdims=True))
           - jnp.exp(jnp.sum(lq2_ref[...] * lk2_ref[...], axis=1, keepdims=True)) + lambda_init)
    for h in range(DIFF_HEADS):
        a1, a2 = acc_ref[2 * h], acc_ref[2 * h + 1]
        o_t = a1[0:DIFF_V] / a1[DIFF_V:DIFF_V + 1] - lam * (a2[0:DIFF_V] / a2[DIFF_V:DIFF_V + 1])
        out = _rms(o_t.T, sub_ref[...]) * (1.0 - lambda_init)
        o_ref[0, :, h * LANES:(h + 1) * LANES] = out.astype(o_ref.dtype)


def diff_attention(proj, lq1, lk1, lq2, lk2, subln, lambda_init, *, tq=ATT_TILE):
    B, S, _ = proj.shape
    width = DIFF_HEADS * DIFF_V
    rows = DIFF_V + BF16_ROWS
    small = lambda n: pl.BlockSpec((1, n), lambda b, i: (0, 0))
    return pl.pallas_call(
        functools.partial(_diff_kernel, tq=tq, lambda_init=lambda_init),
        grid=(B, S // tq),
        in_specs=[pl.BlockSpec((1, tq, width), lambda b, i: (b, i, 3)),
                  pl.BlockSpec((1, S, width), lambda b, i: (b, 0, 4)),
                  pl.BlockSpec((1, S, width), lambda b, i: (b, 0, 5)),
                  small(DIFF_QK), small(DIFF_QK), small(DIFF_QK), small(DIFF_QK), small(DIFF_V)],
        out_specs=pl.BlockSpec((1, tq, width), lambda b, i: (b, i, 0)),
        out_shape=jax.ShapeDtypeStruct((B, S, width), BF16),
        scratch_shapes=[pltpu.VMEM((DIFF_HEADS, rows, S), BF16),
                        pltpu.VMEM((2 * DIFF_HEADS, tq, LANES), BF16),
                        pltpu.VMEM((2 * DIFF_HEADS, rows, tq), F32),
                        pltpu.VMEM((2 * DIFF_HEADS, 1, tq), F32)],
        compiler_params=_params("parallel", "arbitrary"),
        name="diff_attention",
    )(proj, proj, proj, lq1.reshape(1, -1), lk1.reshape(1, -1), lq2.reshape(1, -1), lk2.reshape(1, -1),
      subln.reshape(1, -1))


def _mlstm_kernel(qk_ref, halo_ref, v_ref, og_ref, gc_ref, gr_ref, cw_ref, hn_ref, o_ref, c_ref, n_ref, m_ref,
                  *, L):
    c = pl.program_id(1)

    @pl.when(c == 0)
    def _():
        c_ref[...] = jnp.zeros_like(c_ref)
        n_ref[...] = jnp.zeros_like(n_ref)
        m_ref[...] = jnp.zeros_like(m_ref)

    x = qk_ref[0].astype(F32)
    halo = jnp.where(c == 0, 0.0, halo_ref[0].astype(F32))
    xe = jnp.concatenate([halo, x], axis=0)
    cw = cw_ref[...]
    y = cw[ML_CONV - 1:ML_CONV] * x
    for tap in range(ML_CONV - 1):
        off = BF16_ROWS - (ML_CONV - 1) + tap
        y = y + cw[tap:tap + 1] * xe[off:off + L]
    y = y * _sigmoid(y)

    row = lax.broadcasted_iota(jnp.int32, (L, L), 0)
    col = lax.broadcasted_iota(jnp.int32, (L, L), 1)
    causal = col <= row
    gc = gc_ref[0]
    gr = gr_ref[0]
    kw_off = ML_HEADS * ML_QK
    for h in range(ML_HEADS):
        q = (y[:, h * ML_QK:(h + 1) * ML_QK] * (ML_QK ** -0.5)).astype(BF16)
        kf = y[:, kw_off + h * ML_QK:kw_off + (h + 1) * ML_QK]
        k = kf.astype(BF16)
        v = v_ref[0, :, h * ML_V:(h + 1) * ML_V]
        r_col = gc[:, h:h + 1] - gc[:, ML_HEADS + h:ML_HEADS + h + 1]
        b_col = gc[:, ML_HEADS + h:ML_HEADS + h + 1]
        b_row = gr[ML_HEADS + h:ML_HEADS + h + 1, :]
        r_row = gr[h:h + 1, :] - b_row
        g = b_row[:, L - 1:L]
        C = c_ref[h]
        n = n_ref[h]
        m = m_ref[h][:, 0:1]

        dm = jnp.where(causal, r_row, NEG_INF)
        mt = jnp.maximum(m, jnp.max(dm, axis=1, keepdims=True))
        w_inter = jnp.exp(m - mt)
        s = _dot_nt(q, k) * jnp.exp(dm - mt)
        num = w_inter * _dot(q, C.astype(BF16)) + _dot(s.astype(BF16), v)
        den = (w_inter * jnp.sum(q.astype(F32) * n, axis=1, keepdims=True)
               + jnp.sum(s, axis=1, keepdims=True))
        hh = num / jnp.maximum(jnp.abs(den), jnp.exp(-(b_col + mt)))

        m_next = jnp.maximum(m, jnp.max(r_row, axis=1, keepdims=True))
        decay = jnp.exp(m - m_next)
        kw = kf * jnp.exp(r_col - m_next)
        c_ref[h] = decay * C + _dot_tn(kw.astype(BF16), v)
        n_ref[h] = decay * n + jnp.sum(kw, axis=0, keepdims=True)
        m_ref[h] = jnp.broadcast_to(g + m_next, (1, LANES))

        vs = slice(h * ML_V, (h + 1) * ML_V)
        hh = _rms(hh, hn_ref[:, vs])
        o_ref[0, :, vs] = (hh * _sigmoid(og_ref[0, :, vs].astype(F32))).astype(o_ref.dtype)


def mlstm(proj, g_col, g_row, conv_w, head_norm, *, L=ML_CHUNK):
    B, S, _ = proj.shape
    W = D_MODEL
    halo_blocks = L // BF16_ROWS
    return pl.pallas_call(
        functools.partial(_mlstm_kernel, L=L),
        grid=(B, S // L),
        in_specs=[pl.BlockSpec((1, L, W), lambda b, c: (b, c, 0)),
                  pl.BlockSpec((1, BF16_ROWS, W), lambda b, c: (b, jnp.maximum(c * halo_blocks - 1, 0), 0)),
                  pl.BlockSpec((1, L, W), lambda b, c: (b, c, 1)),
                  pl.BlockSpec((1, L, W), lambda b, c: (b, c, 2)),
                  pl.BlockSpec((1, L, 2 * ML_HEADS), lambda b, c: (b, c, 0)),
                  pl.BlockSpec((1, 2 * ML_HEADS, L), lambda b, c: (b, 0, c)),
                  pl.BlockSpec((ML_CONV, W), lambda b, c: (0, 0)),
                  pl.BlockSpec((1, W), lambda b, c: (0, 0))],
        out_specs=pl.BlockSpec((1, L, W), lambda b, c: (b, c, 0)),
        out_shape=jax.ShapeDtypeStruct((B, S, W), BF16),
        scratch_shapes=[pltpu.VMEM((ML_HEADS, ML_QK, ML_V), F32),
                        pltpu.VMEM((ML_HEADS, 1, ML_QK), F32),
                        pltpu.VMEM((ML_HEADS, 1, LANES), F32)],
        compiler_params=_params("parallel", "arbitrary"),
        name="mlstm",
    )(proj, proj, proj, proj, g_col, g_row, conv_w, head_norm.reshape(1, W))


def _xattn_kernel(x_ref, g_ref, wq_ref, k_ref, v_ref, wo_ref, o_ref):
    x = x_ref[0]
    xn = _rms(x, g_ref[...]).astype(BF16)
    q = (_dot(xn, wq_ref[...]) * (X_DIM ** -0.5)).astype(BF16)
    heads = []
    for h in range(X_HEADS):
        cols = slice(h * X_DIM, (h + 1) * X_DIM)
        s = _dot_nt(q[:, cols], k_ref[0, :, cols])
        p = jnp.exp(s - jnp.max(s, axis=1, keepdims=True))
        l = jnp.sum(p, axis=1, keepdims=True)
        heads.append((_dot(p.astype(BF16), v_ref[0, :, cols]) / l).astype(BF16))
    o_ref[0] = x + _dot(jnp.concatenate(heads, axis=1), wo_ref[...])


def memory_cross_attention(x, gain, wq, kv, wo, *, tq=X_TILE):
    B, S, D = x.shape
    M = kv.shape[1]
    return pl.pallas_call(
        _xattn_kernel,
        grid=(B, S // tq),
        in_specs=[pl.BlockSpec((1, tq, D), lambda b, i: (b, i, 0)),
                  pl.BlockSpec((1, D), lambda b, i: (0, 0)),
                  pl.BlockSpec((D, D), lambda b, i: (0, 0)),
                  pl.BlockSpec((1, M, D), lambda b, i: (b, 0, 0)),
                  pl.BlockSpec((1, M, D), lambda b, i: (b, 0, 1)),
                  pl.BlockSpec((D, D), lambda b, i: (0, 0))],
        out_specs=pl.BlockSpec((1, tq, D), lambda b, i: (b, i, 0)),
        out_shape=jax.ShapeDtypeStruct((B, S, D), F32),
        compiler_params=_params("parallel", "parallel"),
        name="memory_cross_attention",
    )(x, gain.reshape(1, D), wq, kv, kv, wo)


def _gelu_tanh(x):
    return 0.5 * x * (1.0 + jnp.tanh(math.sqrt(2.0 / math.pi) * (x + 0.044715 * (x * x * x))))


def _ffn_kernel(x_ref, halo_ref, g_ref, wg_ref, wu_ref, cwg_ref, cwu_ref, cbg_ref, cbu_ref, wd_ref, fg_ref,
                o_ref, xn_ref, acc_ref, *, tm, tiles_per_seq, final_norm):
    i, j = pl.program_id(0), pl.program_id(1)

    @pl.when(j == 0)
    def _():
        x = x_ref[...]
        halo = _rms(halo_ref[...], g_ref[...])
        xn_ref[0:BF16_ROWS, :] = jnp.where(i % tiles_per_seq == 0, 0.0, halo).astype(BF16)
        xn_ref[BF16_ROWS:, :] = _rms(x, g_ref[...]).astype(BF16)
        acc_ref[...] = x

    xe = xn_ref[...]

    def conv_up(w_ref, cw_ref, cb_ref):
        up = _dot(xe, w_ref[...])
        cw = cw_ref[...]
        out = cb_ref[...] + cw[FFN_CONV - 1:FFN_CONV] * up[BF16_ROWS:BF16_ROWS + tm]
        for tap in range(FFN_CONV - 1):
            off = BF16_ROWS - (FFN_CONV - 1) + tap
            out = out + cw[tap:tap + 1] * up[off:off + tm]
        return out

    act = _gelu_tanh(conv_up(wg_ref, cwg_ref, cbg_ref)) * conv_up(wu_ref, cwu_ref, cbu_ref)
    acc_ref[...] += _dot(act.astype(BF16), wd_ref[...])

    @pl.when(j == pl.num_programs(1) - 1)
    def _():
        y = acc_ref[...]
        o_ref[...] = _rms(y, fg_ref[...]) if final_norm else y


def conv_ffn(x, gain, w_up, conv_w, conv_b, w_down, seq_len, final_gain=None, *, tm=ROW_TILE, tn=FFN_COLS):
    T, D = x.shape
    nj = D_FF // tn
    halo_blocks = tm // BF16_ROWS
    final_norm = final_gain is not None
    fg = (final_gain if final_norm else gain).reshape(1, D)
    conv_b = conv_b.reshape(1, -1)
    return pl.pallas_call(
        functools.partial(_ffn_kernel, tm=tm, tiles_per_seq=seq_len // tm, final_norm=final_norm),
        grid=(T // tm, nj),
        in_specs=[pl.BlockSpec((tm, D), lambda i, j: (i, 0)),
                  pl.BlockSpec((BF16_ROWS, D), lambda i, j: (jnp.maximum(i * halo_blocks - 1, 0), 0)),
                  pl.BlockSpec((1, D), lambda i, j: (0, 0)),
                  pl.BlockSpec((D, tn), lambda i, j: (0, j)),
                  pl.BlockSpec((D, tn), lambda i, j: (0, nj + j)),
                  pl.BlockSpec((FFN_CONV, tn), lambda i, j: (0, j)),
                  pl.BlockSpec((FFN_CONV, tn), lambda i, j: (0, nj + j)),
                  pl.BlockSpec((1, tn), lambda i, j: (0, j)),
                  pl.BlockSpec((1, tn), lambda i, j: (0, nj + j)),
                  pl.BlockSpec((tn, D), lambda i, j: (j, 0)),
                  pl.BlockSpec((1, D), lambda i, j: (0, 0))],
        out_specs=pl.BlockSpec((tm, D), lambda i, j: (i, 0)),
        out_shape=jax.ShapeDtypeStruct((T, D), F32),
        scratch_shapes=[pltpu.VMEM((tm + BF16_ROWS, D), BF16), pltpu.VMEM((tm, D), F32)],
        compiler_params=_params("parallel", "arbitrary"),
        name="conv_ffn",
    )(x, x, gain.reshape(1, D), w_up, w_up, conv_w, conv_w, conv_b, conv_b, w_down, fg)


def _pad_gate_cols(w):
    return jnp.pad(w, ((0, 0), (0, LANES - w.shape[1]))).astype(BF16)


def _fox_bias_operands(f):
    B, S, H = f.shape
    to_bf16_grid = functools.partial(lax.reduce_precision, exponent_bits=8, mantissa_bits=7)
    hi = to_bf16_grid(f)
    mid = to_bf16_grid(f - hi)
    lo = f - hi - mid
    pieces = jnp.stack([hi, mid, lo], axis=-1).astype(BF16).reshape(B, S, F_PIECES * H)
    ones = jnp.ones_like(pieces)
    gap = jnp.zeros((B, S, FOX_BIAS_Q_LANE - F_PIECES * H), BF16)
    tail = jnp.zeros((B, S, LANES - FOX_BIAS_Q_LANE - F_PIECES * H), BF16)
    fq = jnp.concatenate([ones, gap, pieces, tail], axis=-1)
    fk = jnp.concatenate([-pieces, gap, ones, tail], axis=-1)
    return fq, fk


def _fox_diff_layer(x, B, S, gain, w_in, fox_bf, lq1, lk1, lq2, lk2, subln, w_out, lambda_init):
    fw = FOX_HEADS * FOX_DIM
    g0 = 3 * fw
    w_main = jnp.concatenate([w_in[:, :g0], w_in[:, g0 + FOX_HEADS:]], axis=1).astype(BF16)
    proj, gates = norm_matmul(x, gain, w_main, _pad_gate_cols(w_in[:, g0:g0 + FOX_HEADS]))
    log_f = jax.nn.log_sigmoid(gates[:, :FOX_HEADS] + fox_bf).reshape(B, S, FOX_HEADS)
    fq, fk = _fox_bias_operands(jnp.cumsum(log_f, axis=1))
    proj = proj.reshape(B, S, -1)
    fox = fox_attention(proj, fq, fk)
    dif = diff_attention(proj, lq1, lk1, lq2, lk2, subln, lambda_init)
    w_out = w_out.astype(BF16)
    return matmul_residual([fox.reshape(B * S, -1), dif.reshape(B * S, -1)], [w_out[:fw], w_out[fw:]], x)


def _mlstm_layer(x, B, S, gain, w_in, conv_qk, b_i, b_f, head_norm, w_out):
    g0 = 2 * ML_HEADS * ML_QK + ML_HEADS * ML_V
    w_main = jnp.concatenate([w_in[:, :g0], w_in[:, g0 + 2 * ML_HEADS:]], axis=1).astype(BF16)
    proj, gates = norm_matmul(x, gain, w_main, _pad_gate_cols(w_in[:, g0:g0 + 2 * ML_HEADS]))
    i_pre = (gates[:, :ML_HEADS] + b_i).reshape(B, S, ML_HEADS)
    log_f = jax.nn.log_sigmoid(gates[:, ML_HEADS:2 * ML_HEADS] + b_f)
    b = jnp.cumsum(log_f.reshape(B, S // ML_CHUNK, ML_CHUNK, ML_HEADS), axis=2).reshape(B, S, ML_HEADS)
    g_col = jnp.concatenate([i_pre, b], axis=-1)
    h = mlstm(proj.reshape(B, S, -1), g_col, g_col.transpose(0, 2, 1), conv_qk, head_norm)
    return matmul_residual([h.reshape(B * S, -1)], [w_out.astype(BF16)], x)


def kernel(x, mem, mix_norm, xattn_norm, mem_norm, ffn_norm, attn_w_in, attn_fox_bf, diff_lq1, diff_lk1, diff_lq2, diff_lk2, diff_subln, attn_w_out, mlstm_w_in, mlstm_conv_qk, mlstm_b_i, mlstm_b_f, mlstm_head_norm, mlstm_w_out, xattn_wq, xattn_wkv, xattn_wo, ffn_w_up, ffn_conv_w, ffn_conv_b, ffn_w_down, final_norm):
    B, S, D = x.shape
    M = mem.shape[1]
    depth = mix_norm.shape[0]
    x = x.reshape(B * S, D)
    mem2 = mem.reshape(B * M, D)
    for layer in range(depth):
        j = layer // 2
        if layer % 2 == 0:
            lambda_init = 0.8 - 0.6 * math.exp(-0.3 * layer)
            x = _fox_diff_layer(x, B, S, mix_norm[layer], attn_w_in[j], attn_fox_bf[j], diff_lq1[j], diff_lk1[j],
                                diff_lq2[j], diff_lk2[j], diff_subln[j], attn_w_out[j], lambda_init)
        else:
            x = _mlstm_layer(x, B, S, mix_norm[layer], mlstm_w_in[j], mlstm_conv_qk[j], mlstm_b_i[j], mlstm_b_f[j],
                             mlstm_head_norm[j], mlstm_w_out[j])
        kv = norm_matmul(mem2, mem_norm[layer], xattn_wkv[layer].astype(BF16)).reshape(B, M, 2 * D)
        x = memory_cross_attention(x.reshape(B, S, D), xattn_norm[layer], xattn_wq[layer].astype(BF16), kv,
                                   xattn_wo[layer].astype(BF16)).reshape(B * S, D)
        x = conv_ffn(x, ffn_norm[layer], ffn_w_up[layer].astype(BF16), ffn_conv_w[layer], ffn_conv_b[layer],
                     ffn_w_down[layer].astype(BF16), S, final_norm if layer == depth - 1 else None)
    return x.reshape(B, S, D)
```

```python
import functools
import math

import jax
import jax.numpy as jnp
from jax import lax
from jax.experimental import pallas as pl
from jax.experimental.pallas import tpu as pltpu

F32 = jnp.float32
BF16 = jnp.bfloat16

D_MODEL = 1024
RMS_EPS = 1e-6
NEG_INF = -1e30
CHUNK = 64
FOX_HEADS, FOX_DIM = 8, 64
DIFF_HEADS, DIFF_QK, DIFF_V = 4, 64, 128
ML_HEADS, ML_QK, ML_V, ML_CONV = 4, 128, 256, 4
X_HEADS, X_DIM = 4, 256
D_FF = 2816
FFN_CONV = 3
LANES = 128
BF16_ROWS = 16
VMEM_LIMIT = 56 * 1024 * 1024

ROW_TILE = 512
ATT_TILE = 256
ML_CHUNK = 256
X_TILE = 512
FFN_COLS = 1408
TRANSPOSE_COLS = 512
F_PIECES = 3
FOX_BIAS_K_LANE = 0
FOX_BIAS_Q_LANE = 32


def _params(*sem):
    return pltpu.CompilerParams(dimension_semantics=sem, vmem_limit_bytes=VMEM_LIMIT)


def _rms(x, gain):
    return x * lax.rsqrt(jnp.mean(x * x, axis=-1, keepdims=True) + RMS_EPS) * gain


def _sigmoid(x):
    return 1.0 / (1.0 + jnp.exp(-x))


def _dot(a, b):
    return jnp.dot(a, b, preferred_element_type=F32)


def _dot_nt(a, b):
    return lax.dot_general(a, b, (((1,), (1,)), ((), ())), preferred_element_type=F32)


def _dot_tn(a, b):
    return lax.dot_general(a, b, (((0,), (0,)), ((), ())), preferred_element_type=F32)


def _norm_matmul_kernel(*refs, has_gates):
    if has_gates:
        x_ref, g_ref, w_ref, wg_ref, o_ref, og_ref, xn_ref = refs
    else:
        x_ref, g_ref, w_ref, o_ref, xn_ref = refs

    @pl.when(pl.program_id(1) == 0)
    def _():
        xn = _rms(x_ref[...], g_ref[...]).astype(BF16)
        xn_ref[...] = xn
        if has_gates:
            og_ref[...] = _dot(xn, wg_ref[...])

    o_ref[...] = _dot(xn_ref[...], w_ref[...]).astype(o_ref.dtype)


def norm_matmul(x, gain, w, w_gates=None, *, tm=ROW_TILE, tn=1024):
    T, D = x.shape
    N = w.shape[1]
    has_gates = w_gates is not None
    in_specs = [pl.BlockSpec((tm, D), lambda i, j: (i, 0)),
                pl.BlockSpec((1, D), lambda i, j: (0, 0)),
                pl.BlockSpec((D, tn), lambda i, j: (0, j))]
    out_specs = [pl.BlockSpec((tm, tn), lambda i, j: (i, j))]
    out_shape = [jax.ShapeDtypeStruct((T, N), BF16)]
    args = [x, gain.reshape(1, D), w]
    if has_gates:
        in_specs.append(pl.BlockSpec((D, LANES), lambda i, j: (0, 0)))
        out_specs.append(pl.BlockSpec((tm, LANES), lambda i, j: (i, 0)))
        out_shape.append(jax.ShapeDtypeStruct((T, LANES), F32))
        args.append(w_gates)
    outs = pl.pallas_call(
        functools.partial(_norm_matmul_kernel, has_gates=has_gates),
        grid=(T // tm, N // tn),
        in_specs=in_specs, out_specs=out_specs, out_shape=out_shape,
        scratch_shapes=[pltpu.VMEM((tm, D), BF16)],
        compiler_params=_params("parallel", "arbitrary"),
        name="norm_matmul",
    )(*args)
    return outs if has_gates else outs[0]


def _matmul_residual_kernel(*refs, n_in):
    res_ref, o_ref = refs[2 * n_in], refs[2 * n_in + 1]
    acc = res_ref[...]
    for a_ref, w_ref in zip(refs[:n_in], refs[n_in:2 * n_in]):
        acc = acc + _dot(a_ref[...], w_ref[...])
    o_ref[...] = acc


def matmul_residual(acts, weights, res, *, tm=ROW_TILE):
    T, D = res.shape
    n_in = len(acts)
    in_specs = ([pl.BlockSpec((tm, a.shape[1]), lambda i: (i, 0)) for a in acts]
                + [pl.BlockSpec(w.shape, lambda i: (0, 0)) for w in weights]
                + [pl.BlockSpec((tm, D), lambda i: (i, 0))])
    return pl.pallas_call(
        functools.partial(_matmul_residual_kernel, n_in=n_in),
        grid=(T // tm,),
        in_specs=in_specs,
        out_specs=pl.BlockSpec((tm, D), lambda i: (i, 0)),
        out_shape=jax.ShapeDtypeStruct((T, D), F32),
        compiler_params=_params("parallel"),
        name="matmul_residual",
    )(*acts, *weights, res)


def _build_vt(v_ref, vt_ref, n_heads, rows):
    S = v_ref.shape[1]
    r = lax.broadcasted_iota(jnp.int32, (LANES, LANES), 0)
    c = lax.broadcasted_iota(jnp.int32, (LANES, LANES), 1)
    eye = jnp.where(r == c, 1.0, 0.0).astype(BF16)
    per_group = LANES // rows
    for g in range(n_heads // per_group):
        for c0 in range(0, S, TRANSPOSE_COLS):
            cs = slice(c0, c0 + TRANSPOSE_COLS)
            vt = _dot_nt(eye, v_ref[0, cs, g * LANES:(g + 1) * LANES]).astype(BF16)
            for k in range(per_group):
                vt_ref[g * per_group + k, 0:rows, cs] = vt[k * rows:(k + 1) * rows]
    for h in range(n_heads):
        vt_ref[h, rows:rows + BF16_ROWS, :] = jnp.ones((BF16_ROWS, S), BF16)


def _softmax_step(s_t, vt, m_ref, acc_ref, idx):
    m_prev = m_ref[idx]
    m_new = jnp.maximum(m_prev, jnp.max(s_t, axis=0, keepdims=True))
    p = jnp.exp(s_t - m_new).astype(BF16)
    acc_ref[idx] = jnp.exp(m_prev - m_new) * acc_ref[idx] + _dot(vt, p)
    m_ref[idx] = m_new


def _causal_steps(i, logits, attend, n_streams, buf_a, buf_b):
    def phase(j, src, dst):
        for n in range(n_streams):
            logits(j + 1, dst, n)
            attend(j, src, n, False)

    for n in range(n_streams):
        logits(0, buf_a, n)

    def body(jj, carry):
        phase(2 * jj, buf_a, buf_b)
        phase(2 * jj + 1, buf_b, buf_a)
        return carry

    lax.fori_loop(0, lax.shift_right_logical(i, 1), body, 0)
    odd = lax.rem(i, 2) == 1

    @pl.when(odd)
    def _():
        phase(i - 1, buf_a, buf_b)
        for n in range(n_streams):
            attend(i, buf_b, n, True)

    @pl.when(jnp.logical_not(odd))
    def _():
        for n in range(n_streams):
            attend(i, buf_a, n, True)


def _fox_kernel(q_ref, k_ref, v_ref, fq_ref, fk_ref, o_ref, vt_ref, qc_ref, acc_ref, m_ref, sa_ref, sb_ref, *, tq):
    i = pl.program_id(1)

    @pl.when(i == 0)
    def _():
        _build_vt(v_ref, vt_ref, FOX_HEADS, FOX_DIM)

    lane = lax.broadcasted_iota(jnp.int32, (1, LANES), 1)
    fq = fq_ref[0]
    for h in range(FOX_HEADS):
        pair, half = divmod(h, 2)
        q = q_ref[0, :, pair * LANES:(pair + 1) * LANES]
        in_head = (lane >= half * FOX_DIM) & (lane < (half + 1) * FOX_DIM)
        lo_k, lo_q = FOX_BIAS_K_LANE + F_PIECES * h, FOX_BIAS_Q_LANE + F_PIECES * h
        mine = ((lane >= lo_k) & (lane < lo_k + F_PIECES)) | ((lane >= lo_q) & (lane < lo_q + F_PIECES))
        qc_ref[h, :, 0:LANES] = jnp.where(in_head, q, jnp.zeros_like(q)) * (FOX_DIM ** -0.5)
        qc_ref[h, :, LANES:2 * LANES] = jnp.where(mine, fq, jnp.zeros_like(fq))
    m_ref[...] = jnp.full(m_ref.shape, NEG_INF, F32)
    acc_ref[...] = jnp.zeros(acc_ref.shape, F32)

    key = lax.broadcasted_iota(jnp.int32, (tq, tq), 0)
    qry = lax.broadcasted_iota(jnp.int32, (tq, tq), 1)
    causal = key <= qry

    def key_rows(j):
        return pl.ds(pl.multiple_of(j * tq, tq), tq)

    def logits(j, buf, h):
        pair = h // 2
        kc = jnp.concatenate([k_ref[0, key_rows(j), pair * LANES:(pair + 1) * LANES], fk_ref[0, key_rows(j), :]],
                             axis=1)
        buf[h] = _dot_nt(kc, qc_ref[h])

    def attend(j, buf, h, masked):
        s_t = buf[h]
        if masked:
            s_t = jnp.where(causal, s_t, NEG_INF)
        _softmax_step(s_t, vt_ref[h, :, key_rows(j)], m_ref, acc_ref, h)

    _causal_steps(i, logits, attend, FOX_HEADS, sa_ref, sb_ref)

    for pair in range(FOX_HEADS // 2):
        halves = []
        for half in range(2):
            a = acc_ref[2 * pair + half]
            halves.append(a[0:FOX_DIM] / a[FOX_DIM:FOX_DIM + 1])
        o_ref[0, :, pair * LANES:(pair + 1) * LANES] = jnp.concatenate(halves, axis=0).T.astype(o_ref.dtype)


def fox_attention(proj, fq, fk, *, tq=ATT_TILE):
    B, S, _ = proj.shape
    width = FOX_HEADS * FOX_DIM
    rows = FOX_DIM + BF16_ROWS
    return pl.pallas_call(
        functools.partial(_fox_kernel, tq=tq),
        grid=(B, S // tq),
        in_specs=[pl.BlockSpec((1, tq, width), lambda b, i: (b, i, 0)),
                  pl.BlockSpec((1, S, width), lambda b, i: (b, 0, 1)),
                  pl.BlockSpec((1, S, width), lambda b, i: (b, 0, 2)),
                  pl.BlockSpec((1, tq, LANES), lambda b, i: (b, i, 0)),
                  pl.BlockSpec((1, S, LANES), lambda b, i: (b, 0, 0))],
        out_specs=pl.BlockSpec((1, tq, width), lambda b, i: (b, i, 0)),
        out_shape=jax.ShapeDtypeStruct((B, S, width), BF16),
        scratch_shapes=[pltpu.VMEM((FOX_HEADS, rows, S), BF16),
                        pltpu.VMEM((FOX_HEADS, tq, 2 * LANES), BF16),
                        pltpu.VMEM((FOX_HEADS, rows, tq), F32),
                        pltpu.VMEM((FOX_HEADS, 1, tq), F32),
                        pltpu.VMEM((FOX_HEADS, tq, tq), F32),
                        pltpu.VMEM((FOX_HEADS, tq, tq), F32)],
        compiler_params=_params("parallel", "arbitrary"),
        name="fox_attention",
    )(proj, proj, proj, fq, fk)


def _diff_kernel(q_ref, k_ref, v_ref, lq1_ref, lk1_ref, lq2_ref, lk2_ref, sub_ref, o_ref, vt_ref, qc_ref, acc_ref,
                 m_ref, sa_ref, sb_ref, *, tq, lambda_init):
    i = pl.program_id(1)

    @pl.when(i == 0)
    def _():
        _build_vt(v_ref, vt_ref, DIFF_HEADS, DIFF_V)

    lane = lax.broadcasted_iota(jnp.int32, (1, LANES), 1)
    for h in range(DIFF_HEADS):
        q = q_ref[0, :, h * LANES:(h + 1) * LANES]
        zero = jnp.zeros_like(q)
        qc_ref[2 * h] = jnp.where(lane < DIFF_QK, q, zero) * (DIFF_QK ** -0.5)
        qc_ref[2 * h + 1] = jnp.where(lane >= DIFF_QK, q, zero) * (DIFF_QK ** -0.5)
    m_ref[...] = jnp.full(m_ref.shape, NEG_INF, F32)
    acc_ref[...] = jnp.zeros(acc_ref.shape, F32)

    key = lax.broadcasted_iota(jnp.int32, (tq, tq), 0)
    qry = lax.broadcasted_iota(jnp.int32, (tq, tq), 1)
    visible = (key // CHUNK) <= (qry // CHUNK)

    def key_rows(j):
        return pl.ds(pl.multiple_of(j * tq, tq), tq)

    def logits(j, buf, n):
        h = n // 2
        buf[n] = _dot_nt(k_ref[0, key_rows(j), h * LANES:(h + 1) * LANES], qc_ref[n])

    def attend(j, buf, n, masked):
        s_t = buf[n]
        if masked:
            s_t = jnp.where(visible, s_t, NEG_INF)
        _softmax_step(s_t, vt_ref[n // 2, :, key_rows(j)], m_ref, acc_ref, n)

    _causal_steps(i, logits, attend, 2 * DIFF_HEADS, sa_ref, sb_ref)

    lam = (jnp.exp(jnp.sum(lq1_ref[...] * lk1_ref[...], axis=1, keepdims=True))
           - jnp.exp(jnp.sum(lq2_ref[...] * lk2_ref[...], axis=1, keepdims=True)) + lambda_init)
    for h in range(DIFF_HEADS):
        a1, a2 = acc_ref[2 * h], acc_ref[2 * h + 1]
        o_t = a1[0:DIFF_V] / a1[DIFF_V:DIFF_V + 1] - lam * (a2[0:DIFF_V] / a2[DIFF_V:DIFF_V + 1])
        out = _rms(o_t.T, sub_ref[...]) * (1.0 - lambda_init)
        o_ref[0, :, h * LANES:(h + 1) * LANES] = out.astype(o_ref.dtype)


def diff_attention(proj, lq1, lk1, lq2, lk2, subln, lambda_init, *, tq=ATT_TILE):
    B, S, _ = proj.shape
    width = DIFF_HEADS * DIFF_V
    rows = DIFF_V + BF16_ROWS
    small = lambda n: pl.BlockSpec((1, n), lambda b, i: (0, 0))
    return pl.pallas_call(
        functools.partial(_diff_kernel, tq=tq, lambda_init=lambda_init),
        grid=(B, S // tq),
        in_specs=[pl.BlockSpec((1, tq, width), lambda b, i: (b, i, 3)),
                  pl.BlockSpec((1, S, width), lambda b, i: (b, 0, 4)),
                  pl.BlockSpec((1, S, width), lambda b, i: (b, 0, 5)),
                  small(DIFF_QK), small(DIFF_QK), small(DIFF_QK), small(DIFF_QK), small(DIFF_V)],
        out_specs=pl.BlockSpec((1, tq, width), lambda b, i: (b, i, 0)),
        out_shape=jax.ShapeDtypeStruct((B, S, width), BF16),
        scratch_shapes=[pltpu.VMEM((DIFF_HEADS, rows, S), BF16),
                        pltpu.VMEM((2 * DIFF_HEADS, tq, LANES), BF16),
                        pltpu.VMEM((2 * DIFF_HEADS, rows, tq), F32),
                        pltpu.VMEM((2 * DIFF_HEADS, 1, tq), F32),
                        pltpu.VMEM((2 * DIFF_HEADS, tq, tq), F32),
                        pltpu.VMEM((2 * DIFF_HEADS, tq, tq), F32)],
        compiler_params=_params("parallel", "arbitrary"),
        name="diff_attention",
    )(proj, proj, proj, lq1.reshape(1, -1), lk1.reshape(1, -1), lq2.reshape(1, -1), lk2.reshape(1, -1),
      subln.reshape(1, -1))


def _mlstm_kernel(qk_ref, halo_ref, v_ref, og_ref, gc_ref, gr_ref, cw_ref, hn_ref, o_ref, c_ref, n_ref, m_ref,
                  *, L):
    c = pl.program_id(1)

    @pl.when(c == 0)
    def _():
        c_ref[...] = jnp.zeros_like(c_ref)
        n_ref[...] = jnp.zeros_like(n_ref)
        m_ref[...] = jnp.zeros_like(m_ref)

    x = qk_ref[0].astype(F32)
    halo = jnp.where(c == 0, 0.0, halo_ref[0].astype(F32))
    xe = jnp.concatenate([halo, x], axis=0)
    cw = cw_ref[...]
    y = cw[ML_CONV - 1:ML_CONV] * x
    for tap in range(ML_CONV - 1):
        off = BF16_ROWS - (ML_CONV - 1) + tap
        y = y + cw[tap:tap + 1] * xe[off:off + L]
    y = y * _sigmoid(y)

    row = lax.broadcasted_iota(jnp.int32, (L, L), 0)
    col = lax.broadcasted_iota(jnp.int32, (L, L), 1)
    causal = col <= row
    gc = gc_ref[0]
    gr = gr_ref[0]
    kw_off = ML_HEADS * ML_QK
    for h in range(ML_HEADS):
        q = (y[:, h * ML_QK:(h + 1) * ML_QK] * (ML_QK ** -0.5)).astype(BF16)
        kf = y[:, kw_off + h * ML_QK:kw_off + (h + 1) * ML_QK]
        k = kf.astype(BF16)
        v = v_ref[0, :, h * ML_V:(h + 1) * ML_V]
        r_col = gc[:, h:h + 1] - gc[:, ML_HEADS + h:ML_HEADS + h + 1]
        b_col = gc[:, ML_HEADS + h:ML_HEADS + h + 1]
        b_row = gr[ML_HEADS + h:ML_HEADS + h + 1, :]
        r_row = gr[h:h + 1, :] - b_row
        g = b_row[:, L - 1:L]
        C = c_ref[h]
        n = n_ref[h]
        m = m_ref[h][:, 0:1]

        dm = jnp.where(causal, r_row, NEG_INF)
        mt = jnp.maximum(m, jnp.max(dm, axis=1, keepdims=True))
        w_inter = jnp.exp(m - mt)
        s = _dot_nt(q, k) * jnp.exp(dm - mt)
        num = w_inter * _dot(q, C.astype(BF16)) + _dot(s.astype(BF16), v)
        den = (w_inter * jnp.sum(q.astype(F32) * n, axis=1, keepdims=True)
               + jnp.sum(s, axis=1, keepdims=True))
        hh = num / jnp.maximum(jnp.abs(den), jnp.exp(-(b_col + mt)))

        m_next = jnp.maximum(m, jnp.max(r_row, axis=1, keepdims=True))
        decay = jnp.exp(m - m_next)
        kw = kf * jnp.exp(r_col - m_next)
        c_ref[h] = decay * C + _dot_tn(kw.astype(BF16), v)
        n_ref[h] = decay * n + jnp.sum(kw, axis=0, keepdims=True)
        m_ref[h] = jnp.broadcast_to(g + m_next, (1, LANES))

        vs = slice(h * ML_V, (h + 1) * ML_V)
        hh = _rms(hh, hn_ref[:, vs])
        o_ref[0, :, vs] = (hh * _sigmoid(og_ref[0, :, vs].astype(F32))).astype(o_ref.dtype)


def mlstm(proj, g_col, g_row, conv_w, head_norm, *, L=ML_CHUNK):
    B, S, _ = proj.shape
    W = D_MODEL
    halo_blocks = L // BF16_ROWS
    return pl.pallas_call(
        functools.partial(_mlstm_kernel, L=L),
        grid=(B, S // L),
        in_specs=[pl.BlockSpec((1, L, W), lambda b, c: (b, c, 0)),
                  pl.BlockSpec((1, BF16_ROWS, W), lambda b, c: (b, jnp.maximum(c * halo_blocks - 1, 0), 0)),
                  pl.BlockSpec((1, L, W), lambda b, c: (b, c, 1)),
                  pl.BlockSpec((1, L, W), lambda b, c: (b, c, 2)),
                  pl.BlockSpec((1, L, 2 * ML_HEADS), lambda b, c: (b, c, 0)),
                  pl.BlockSpec((1, 2 * ML_HEADS, L), lambda b, c: (b, 0, c)),
                  pl.BlockSpec((ML_CONV, W), lambda b, c: (0, 0)),
                  pl.BlockSpec((1, W), lambda b, c: (0, 0))],
        out_specs=pl.BlockSpec((1, L, W), lambda b, c: (b, c, 0)),
        out_shape=jax.ShapeDtypeStruct((B, S, W), BF16),
        scratch_shapes=[pltpu.VMEM((ML_HEADS, ML_QK, ML_V), F32),
                        pltpu.VMEM((ML_HEADS, 1, ML_QK), F32),
                        pltpu.VMEM((ML_HEADS, 1, LANES), F32)],
        compiler_params=_params("parallel", "arbitrary"),
        name="mlstm",
    )(proj, proj, proj, proj, g_col, g_row, conv_w, head_norm.reshape(1, W))


def _xattn_kernel(x_ref, g_ref, wq_ref, k_ref, v_ref, wo_ref, o_ref):
    x = x_ref[0]
    xn = _rms(x, g_ref[...]).astype(BF16)
    q = (_dot(xn, wq_ref[...]) * (X_DIM ** -0.5)).astype(BF16)
    heads = []
    for h in range(X_HEADS):
        cols = slice(h * X_DIM, (h + 1) * X_DIM)
        s = _dot_nt(q[:, cols], k_ref[0, :, cols])
        p = jnp.exp(s - jnp.max(s, axis=1, keepdims=True))
        l = jnp.sum(p, axis=1, keepdims=True)
        heads.append((_dot(p.astype(BF16), v_ref[0, :, cols]) / l).astype(BF16))
    o_ref[0] = x + _dot(jnp.concatenate(heads, axis=1), wo_ref[...])


def memory_cross_attention(x, gain, wq, kv, wo, *, tq=X_TILE):
    B, S, D = x.shape
    M = kv.shape[1]
    return pl.pallas_call(
        _xattn_kernel,
        grid=(B, S // tq),
        in_specs=[pl.BlockSpec((1, tq, D), lambda b, i: (b, i, 0)),
                  pl.BlockSpec((1, D), lambda b, i: (0, 0)),
                  pl.BlockSpec((D, D), lambda b, i: (0, 0)),
                  pl.BlockSpec((1, M, D), lambda b, i: (b, 0, 0)),
                  pl.BlockSpec((1, M, D), lambda b, i: (b, 0, 1)),
                  pl.BlockSpec((D, D), lambda b, i: (0, 0))],
        out_specs=pl.BlockSpec((1, tq, D), lambda b, i: (b, i, 0)),
        out_shape=jax.ShapeDtypeStruct((B, S, D), F32),
        compiler_params=_params("parallel", "parallel"),
        name="memory_cross_attention",
    )(x, gain.reshape(1, D), wq, kv, kv, wo)


def _gelu_tanh(x):
    return 0.5 * x * (1.0 + jnp.tanh(math.sqrt(2.0 / math.pi) * (x + 0.044715 * (x * x * x))))


def _ffn_kernel(x_ref, halo_ref, g_ref, wg_ref, wu_ref, cwg_ref, cwu_ref, cbg_ref, cbu_ref, wd_ref, fg_ref,
                o_ref, xn_ref, acc_ref, *, tm, tiles_per_seq, final_norm):
    i, j = pl.program_id(0), pl.program_id(1)

    @pl.when(j == 0)
    def _():
        x = x_ref[...]
        halo = _rms(halo_ref[...], g_ref[...])
        xn_ref[0:BF16_ROWS, :] = jnp.where(i % tiles_per_seq == 0, 0.0, halo).astype(BF16)
        xn_ref[BF16_ROWS:, :] = _rms(x, g_ref[...]).astype(BF16)
        acc_ref[...] = x

    xe = xn_ref[...]

    def conv_up(w_ref, cw_ref, cb_ref):
        up = _dot(xe, w_ref[...])
        cw = cw_ref[...]
        out = cb_ref[...] + cw[FFN_CONV - 1:FFN_CONV] * up[BF16_ROWS:BF16_ROWS + tm]
        for tap in range(FFN_CONV - 1):
            off = BF16_ROWS - (FFN_CONV - 1) + tap
            out = out + cw[tap:tap + 1] * up[off:off + tm]
        return out

    act = _gelu_tanh(conv_up(wg_ref, cwg_ref, cbg_ref)) * conv_up(wu_ref, cwu_ref, cbu_ref)
    acc_ref[...] += _dot(act.astype(BF16), wd_ref[...])

    @pl.when(j == pl.num_programs(1) - 1)
    def _():
        y = acc_ref[...]
        o_ref[...] = _rms(y, fg_ref[...]) if final_norm else y


def conv_ffn(x, gain, w_up, conv_w, conv_b, w_down, seq_len, final_gain=None, *, tm=ROW_TILE, tn=FFN_COLS):
    T, D = x.shape
    nj = D_FF // tn
    halo_blocks = tm // BF16_ROWS
    final_norm = final_gain is not None
    fg = (final_gain if final_norm else gain).reshape(1, D)
    conv_b = conv_b.reshape(1, -1)
    return pl.pallas_call(
        functools.partial(_ffn_kernel, tm=tm, tiles_per_seq=seq_len // tm, final_norm=final_norm),
        grid=(T // tm, nj),
        in_specs=[pl.BlockSpec((tm, D), lambda i, j: (i, 0)),
                  pl.BlockSpec((BF16_ROWS, D), lambda i, j: (jnp.maximum(i * halo_blocks - 1, 0), 0)),
                  pl.BlockSpec((1, D), lambda i, j: (0, 0)),
                  pl.BlockSpec((D, tn), lambda i, j: (0, j)),
                  pl.BlockSpec((D, tn), lambda i, j: (0, nj + j)),
                  pl.BlockSpec((FFN_CONV, tn), lambda i, j: (0, j)),
                  pl.BlockSpec((FFN_CONV, tn), lambda i, j: (0, nj + j)),
                  pl.BlockSpec((1, tn), lambda i, j: (0, j)),
                  pl.BlockSpec((1, tn), lambda i, j: (0, nj + j)),
                  pl.BlockSpec((tn, D), lambda i, j: (j, 0)),
                  pl.BlockSpec((1, D), lambda i, j: (0, 0))],
        out_specs=pl.BlockSpec((tm, D), lambda i, j: (i, 0)),
        out_shape=jax.ShapeDtypeStruct((T, D), F32),
        scratch_shapes=[pltpu.VMEM((tm + BF16_ROWS, D), BF16), pltpu.VMEM((tm, D), F32)],
        compiler_params=_params("parallel", "arbitrary"),
        name="conv_ffn",
    )(x, x, gain.reshape(1, D), w_up, w_up, conv_w, conv_w, conv_b, conv_b, w_down, fg)


def _pad_gate_cols(w):
    return jnp.pad(w, ((0, 0), (0, LANES - w.shape[1]))).astype(BF16)


def _fox_bias_operands(f):
    B, S, H = f.shape
    to_bf16_grid = functools.partial(lax.reduce_precision, exponent_bits=8, mantissa_bits=7)
    hi = to_bf16_grid(f)
    mid = to_bf16_grid(f - hi)
    lo = f - hi - mid
    pieces = jnp.stack([hi, mid, lo], axis=-1).astype(BF16).reshape(B, S, F_PIECES * H)
    ones = jnp.ones_like(pieces)
    gap = jnp.zeros((B, S, FOX_BIAS_Q_LANE - F_PIECES * H), BF16)
    tail = jnp.zeros((B, S, LANES - FOX_BIAS_Q_LANE - F_PIECES * H), BF16)
    fq = jnp.concatenate([ones, gap, pieces, tail], axis=-1)
    fk = jnp.concatenate([-pieces, gap, ones, tail], axis=-1)
    return fq, fk


def _fox_diff_layer(x, B, S, gain, w_in, fox_bf, lq1, lk1, lq2, lk2, subln, w_out, lambda_init):
    fw = FOX_HEADS * FOX_DIM
    g0 = 3 * fw
    w_main = jnp.concatenate([w_in[:, :g0], w_in[:, g0 + FOX_HEADS:]], axis=1).astype(BF16)
    proj, gates = norm_matmul(x, gain, w_main, _pad_gate_cols(w_in[:, g0:g0 + FOX_HEADS]))
    log_f = jax.nn.log_sigmoid(gates[:, :FOX_HEADS] + fox_bf).reshape(B, S, FOX_HEADS)
    fq, fk = _fox_bias_operands(jnp.cumsum(log_f, axis=1))
    proj = proj.reshape(B, S, -1)
    fox = fox_attention(proj, fq, fk)
    dif = diff_attention(proj, lq1, lk1, lq2, lk2, subln, lambda_init)
    w_out = w_out.astype(BF16)
    return matmul_residual([fox.reshape(B * S, -1), dif.reshape(B * S, -1)], [w_out[:fw], w_out[fw:]], x)


def _mlstm_layer(x, B, S, gain, w_in, conv_qk, b_i, b_f, head_norm, w_out):
    g0 = 2 * ML_HEADS * ML_QK + ML_HEADS * ML_V
    w_main = jnp.concatenate([w_in[:, :g0], w_in[:, g0 + 2 * ML_HEADS:]], axis=1).astype(BF16)
    proj, gates = norm_matmul(x, gain, w_main, _pad_gate_cols(w_in[:, g0:g0 + 2 * ML_HEADS]))
    i_pre = (gates[:, :ML_HEADS] + b_i).reshape(B, S, ML_HEADS)
    log_f = jax.nn.log_sigmoid(gates[:, ML_HEADS:2 * ML_HEADS] + b_f)
    b = jnp.cumsum(log_f.reshape(B, S // ML_CHUNK, ML_CHUNK, ML_HEADS), axis=2).reshape(B, S, ML_HEADS)
    g_col = jnp.concatenate([i_pre, b], axis=-1)
    h = mlstm(proj.reshape(B, S, -1), g_col, g_col.transpose(0, 2, 1), conv_qk, head_norm)
    return matmul_residual([h.reshape(B * S, -1)], [w_out.astype(BF16)], x)


def kernel(x, mem, mix_norm, xattn_norm, mem_norm, ffn_norm, attn_w_in, attn_fox_bf, diff_lq1, diff_lk1, diff_lq2, diff_lk2, diff_subln, attn_w_out, mlstm_w_in, mlstm_conv_qk, mlstm_b_i, mlstm_b_f, mlstm_head_norm, mlstm_w_out, xattn_wq, xattn_wkv, xattn_wo, ffn_w_up, ffn_conv_w, ffn_conv_b, ffn_w_down, final_norm):
    B, S, D = x.shape
    M = mem.shape[1]
    depth = mix_norm.shape[0]
    x = x.reshape(B * S, D)
    mem2 = mem.reshape(B * M, D)
    for layer in range(depth):
        j = layer // 2
        if layer % 2 == 0:
            lambda_init = 0.8 - 0.6 * math.exp(-0.3 * layer)
            x = _fox_diff_layer(x, B, S, mix_norm[layer], attn_w_in[j], attn_fox_bf[j], diff_lq1[j], diff_lk1[j],
                                diff_lq2[j], diff_lk2[j], diff_subln[j], attn_w_out[j], lambda_init)
        else:
            x = _mlstm_layer(x, B, S, mix_norm[layer], mlstm_w_in[j], mlstm_conv_qk[j], mlstm_b_i[j], mlstm_b_f[j],
                             mlstm_head_norm[j], mlstm_w_out[j])
        kv = norm_matmul(mem2, mem_norm[layer], xattn_wkv[layer].astype(BF16)).reshape(B, M, 2 * D)
        x = memory_cross_attention(x.reshape(B, S, D), xattn_norm[layer], xattn_wq[layer].astype(BF16), kv,
                                   xattn_wo[layer].astype(BF16)).reshape(B * S, D)
        x = conv_ffn(x, ffn_norm[layer], ffn_w_up[layer].astype(BF16), ffn_conv_w[layer], ffn_conv_b[layer],
                     ffn_w_down[layer].astype(BF16), S, final_norm if layer == depth - 1 else None)
    return x.reshape(B, S, D)
```

```python
import functools
import math

import jax
import jax.numpy as jnp
from jax import lax
from jax.experimental import pallas as pl
from jax.experimental.pallas import tpu as pltpu

F32 = jnp.float32
BF16 = jnp.bfloat16

D_MODEL = 1024
RMS_EPS = 1e-6
NEG_INF = -1e30
CHUNK = 64
FOX_HEADS, FOX_DIM = 8, 64
DIFF_HEADS, DIFF_QK, DIFF_V = 4, 64, 128
ML_HEADS, ML_QK, ML_V, ML_CONV = 4, 128, 256, 4
X_HEADS, X_DIM = 4, 256
D_FF = 2816
FFN_CONV = 3
LANES = 128
BF16_ROWS = 16
VMEM_LIMIT = 56 * 1024 * 1024

ROW_TILE = 512
ATT_TILE = 256
ML_CHUNK = 256
X_TILE = 512
FFN_CHUNK = 256
ROW_PHASES = 4
DOWN_CHUNKS = 4
TRANSPOSE_COLS = 512
F_PIECES = 3
FOX_BIAS_K_LANE = 0
FOX_BIAS_Q_LANE = 32


def _params(*sem):
    return pltpu.CompilerParams(dimension_semantics=sem, vmem_limit_bytes=VMEM_LIMIT)


def _resident(shape):
    return pl.BlockSpec(shape, lambda *_: (0,) * len(shape), pipeline_mode=pl.Buffered(1))


def _rms(x, gain):
    return x * lax.rsqrt(jnp.mean(x * x, axis=-1, keepdims=True) + RMS_EPS) * gain


def _sigmoid(x):
    return 1.0 / (1.0 + jnp.exp(-x))


def _dot(a, b):
    return jnp.dot(a, b, preferred_element_type=F32)


def _dot_nt(a, b):
    return lax.dot_general(a, b, (((1,), (1,)), ((), ())), preferred_element_type=F32)


def _dot_tn(a, b):
    return lax.dot_general(a, b, (((0,), (0,)), ((), ())), preferred_element_type=F32)


def _norm_matmul_kernel(*refs, has_gates, tn):
    if has_gates:
        x_ref, g_ref, w_ref, wg_ref, o_ref, og_ref = refs
    else:
        x_ref, g_ref, w_ref, o_ref = refs
    xn = _rms(x_ref[...], g_ref[...]).astype(BF16)
    if has_gates:
        og_ref[...] = _dot(xn, wg_ref[...])
    for c0 in range(0, o_ref.shape[1], tn):
        o_ref[:, c0:c0 + tn] = _dot(xn, w_ref[:, c0:c0 + tn]).astype(o_ref.dtype)


def norm_matmul(x, gain, w, w_gates=None, *, tm=ROW_TILE, tn=1024):
    T, D = x.shape
    N = w.shape[1]
    has_gates = w_gates is not None
    in_specs = [pl.BlockSpec((tm, D), lambda i: (i, 0)), _resident((1, D)), _resident(w.shape)]
    out_specs = [pl.BlockSpec((tm, N), lambda i: (i, 0))]
    out_shape = [jax.ShapeDtypeStruct((T, N), BF16)]
    args = [x, gain.reshape(1, D), w]
    if has_gates:
        in_specs.append(_resident(w_gates.shape))
        out_specs.append(pl.BlockSpec((tm, LANES), lambda i: (i, 0)))
        out_shape.append(jax.ShapeDtypeStruct((T, LANES), F32))
        args.append(w_gates)
    outs = pl.pallas_call(
        functools.partial(_norm_matmul_kernel, has_gates=has_gates, tn=tn),
        grid=(T // tm,),
        in_specs=in_specs, out_specs=out_specs, out_shape=out_shape,
        compiler_params=_params("parallel"),
        name="norm_matmul",
    )(*args)
    return outs if has_gates else outs[0]


def _matmul_residual_kernel(*refs, n_in):
    res_ref, o_ref = refs[2 * n_in], refs[2 * n_in + 1]
    acc = res_ref[...]
    for a_ref, w_ref in zip(refs[:n_in], refs[n_in:2 * n_in]):
        acc = acc + _dot(a_ref[...], w_ref[...])
    o_ref[...] = acc


def matmul_residual(acts, weights, res, *, tm=ROW_TILE):
    T, D = res.shape
    n_in = len(acts)
    in_specs = ([pl.BlockSpec((tm, a.shape[1]), lambda i: (i, 0)) for a in acts]
                + [_resident(w.shape) for w in weights]
                + [pl.BlockSpec((tm, D), lambda i: (i, 0))])
    return pl.pallas_call(
        functools.partial(_matmul_residual_kernel, n_in=n_in),
        grid=(T // tm,),
        in_specs=in_specs,
        out_specs=pl.BlockSpec((tm, D), lambda i: (i, 0)),
        out_shape=jax.ShapeDtypeStruct((T, D), F32),
        compiler_params=_params("parallel"),
        name="matmul_residual",
    )(*acts, *weights, res)


def _build_vt(v_ref, vt_ref, n_heads, rows):
    S = v_ref.shape[1]
    r = lax.broadcasted_iota(jnp.int32, (LANES, LANES), 0)
    c = lax.broadcasted_iota(jnp.int32, (LANES, LANES), 1)
    eye = jnp.where(r == c, 1.0, 0.0).astype(BF16)
    per_group = LANES // rows
    for g in range(n_heads // per_group):
        for c0 in range(0, S, TRANSPOSE_COLS):
            cs = slice(c0, c0 + TRANSPOSE_COLS)
            vt = _dot_nt(eye, v_ref[0, cs, g * LANES:(g + 1) * LANES]).astype(BF16)
            for k in range(per_group):
                vt_ref[g * per_group + k, 0:rows, cs] = vt[k * rows:(k + 1) * rows]
    for h in range(n_heads):
        vt_ref[h, rows:rows + BF16_ROWS, :] = jnp.ones((BF16_ROWS, S), BF16)


def _softmax_step(s_t, vt, m_ref, acc_ref, idx):
    m_prev = m_ref[idx]
    m_new = jnp.maximum(m_prev, jnp.max(s_t, axis=0, keepdims=True))
    p = jnp.exp(s_t - m_new).astype(BF16)
    acc_ref[idx] = jnp.exp(m_prev - m_new) * acc_ref[idx] + _dot(vt, p)
    m_ref[idx] = m_new


def _causal_steps(i, logits, attend, n_streams, buf_a, buf_b):
    def phase(j, src, dst):
        for n in range(n_streams):
            logits(j + 1, dst, n)
            attend(j, src, n, False)

    for n in range(n_streams):
        logits(0, buf_a, n)

    def body(jj, carry):
        phase(2 * jj, buf_a, buf_b)
        phase(2 * jj + 1, buf_b, buf_a)
        return carry

    lax.fori_loop(0, lax.shift_right_logical(i, 1), body, 0)
    odd = lax.rem(i, 2) == 1

    @pl.when(odd)
    def _():
        phase(i - 1, buf_a, buf_b)
        for n in range(n_streams):
            attend(i, buf_b, n, True)

    @pl.when(jnp.logical_not(odd))
    def _():
        for n in range(n_streams):
            attend(i, buf_a, n, True)


def _fox_kernel(q_ref, k_ref, v_ref, fq_ref, fk_ref, o_ref, vt_ref, qc_ref, acc_ref, m_ref, sa_ref, sb_ref, *, tq):
    i = pl.program_id(1)

    @pl.when(i == 0)
    def _():
        _build_vt(v_ref, vt_ref, FOX_HEADS, FOX_DIM)

    lane = lax.broadcasted_iota(jnp.int32, (1, LANES), 1)
    fq = fq_ref[0]
    for h in range(FOX_HEADS):
        pair, half = divmod(h, 2)
        q = q_ref[0, :, pair * LANES:(pair + 1) * LANES]
        in_head = (lane >= half * FOX_DIM) & (lane < (half + 1) * FOX_DIM)
        lo_k, lo_q = FOX_BIAS_K_LANE + F_PIECES * h, FOX_BIAS_Q_LANE + F_PIECES * h
        mine = ((lane >= lo_k) & (lane < lo_k + F_PIECES)) | ((lane >= lo_q) & (lane < lo_q + F_PIECES))
        qc_ref[h, :, 0:LANES] = jnp.where(in_head, q, jnp.zeros_like(q)) * (FOX_DIM ** -0.5)
        qc_ref[h, :, LANES:2 * LANES] = jnp.where(mine, fq, jnp.zeros_like(fq))
    m_ref[...] = jnp.full(m_ref.shape, NEG_INF, F32)
    acc_ref[...] = jnp.zeros(acc_ref.shape, F32)

    key = lax.broadcasted_iota(jnp.int32, (tq, tq), 0)
    qry = lax.broadcasted_iota(jnp.int32, (tq, tq), 1)
    causal = key <= qry

    def key_rows(j):
        return pl.ds(pl.multiple_of(j * tq, tq), tq)

    def logits(j, buf, h):
        pair = h // 2
        kc = jnp.concatenate([k_ref[0, key_rows(j), pair * LANES:(pair + 1) * LANES], fk_ref[0, key_rows(j), :]],
                             axis=1)
        buf[h] = _dot_nt(kc, qc_ref[h])

    def attend(j, buf, h, masked):
        s_t = buf[h]
        if masked:
            s_t = jnp.where(causal, s_t, NEG_INF)
        _softmax_step(s_t, vt_ref[h, :, key_rows(j)], m_ref, acc_ref, h)

    _causal_steps(i, logits, attend, FOX_HEADS, sa_ref, sb_ref)

    for pair in range(FOX_HEADS // 2):
        halves = []
        for half in range(2):
            a = acc_ref[2 * pair + half]
            halves.append(a[0:FOX_DIM] / a[FOX_DIM:FOX_DIM + 1])
        o_ref[0, :, pair * LANES:(pair + 1) * LANES] = jnp.concatenate(halves, axis=0).T.astype(o_ref.dtype)


def fox_attention(proj, fq, fk, *, tq=ATT_TILE):
    B, S, _ = proj.shape
    width = FOX_HEADS * FOX_DIM
    rows = FOX_DIM + BF16_ROWS
    return pl.pallas_call(
        functools.partial(_fox_kernel, tq=tq),
        grid=(B, S // tq),
        in_specs=[pl.BlockSpec((1, tq, width), lambda b, i: (b, i, 0)),
                  pl.BlockSpec((1, S, width), lambda b, i: (b, 0, 1)),
                  pl.BlockSpec((1, S, width), lambda b, i: (b, 0, 2)),
                  pl.BlockSpec((1, tq, LANES), lambda b, i: (b, i, 0)),
                  pl.BlockSpec((1, S, LANES), lambda b, i: (b, 0, 0))],
        out_specs=pl.BlockSpec((1, tq, width), lambda b, i: (b, i, 0)),
        out_shape=jax.ShapeDtypeStruct((B, S, width), BF16),
        scratch_shapes=[pltpu.VMEM((FOX_HEADS, rows, S), BF16),
                        pltpu.VMEM((FOX_HEADS, tq, 2 * LANES), BF16),
                        pltpu.VMEM((FOX_HEADS, rows, tq), F32),
                        pltpu.VMEM((FOX_HEADS, 1, tq), F32),
                        pltpu.VMEM((FOX_HEADS, tq, tq), F32),
                        pltpu.VMEM((FOX_HEADS, tq, tq), F32)],
        compiler_params=_params("parallel", "arbitrary"),
        name="fox_attention",
    )(proj, proj, proj, fq, fk)


def _diff_kernel(q_ref, k_ref, v_ref, lq1_ref, lk1_ref, lq2_ref, lk2_ref, sub_ref, o_ref, vt_ref, qc_ref, acc_ref,
                 m_ref, sa_ref, sb_ref, *, tq, lambda_init):
    i = pl.program_id(1)

    @pl.when(i == 0)
    def _():
        _build_vt(v_ref, vt_ref, DIFF_HEADS, DIFF_V)

    lane = lax.broadcasted_iota(jnp.int32, (1, LANES), 1)
    for h in range(DIFF_HEADS):
        q = q_ref[0, :, h * LANES:(h + 1) * LANES]
        zero = jnp.zeros_like(q)
        qc_ref[2 * h] = jnp.where(lane < DIFF_QK, q, zero) * (DIFF_QK ** -0.5)
        qc_ref[2 * h + 1] = jnp.where(lane >= DIFF_QK, q, zero) * (DIFF_QK ** -0.5)
    m_ref[...] = jnp.full(m_ref.shape, NEG_INF, F32)
    acc_ref[...] = jnp.zeros(acc_ref.shape, F32)

    key = lax.broadcasted_iota(jnp.int32, (tq, tq), 0)
    qry = lax.broadcasted_iota(jnp.int32, (tq, tq), 1)
    visible = (key // CHUNK) <= (qry // CHUNK)

    def key_rows(j):
        return pl.ds(pl.multiple_of(j * tq, tq), tq)

    def logits(j, buf, n):
        h = n // 2
        buf[n] = _dot_nt(k_ref[0, key_rows(j), h * LANES:(h + 1) * LANES], qc_ref[n])

    def attend(j, buf, n, masked):
        s_t = buf[n]
        if masked:
            s_t = jnp.where(visible, s_t, NEG_INF)
        _softmax_step(s_t, vt_ref[n // 2, :, key_rows(j)], m_ref, acc_ref, n)

    _causal_steps(i, logits, attend, 2 * DIFF_HEADS, sa_ref, sb_ref)

    lam = (jnp.exp(jnp.sum(lq1_ref[...] * lk1_ref[...], axis=1, keepdims=True))
           - jnp.exp(jnp.sum(lq2_ref[...] * lk2_ref[...], axis=1, keepdims=True)) + lambda_init)
    for h in range(DIFF_HEADS):
        a1, a2 = acc_ref[2 * h], acc_ref[2 * h + 1]
        o_t = a1[0:DIFF_V] / a1[DIFF_V:DIFF_V + 1] - lam * (a2[0:DIFF_V] / a2[DIFF_V:DIFF_V + 1])
        out = _rms(o_t.T, sub_ref[...]) * (1.0 - lambda_init)
        o_ref[0, :, h * LANES:(h + 1) * LANES] = out.astype(o_ref.dtype)


def diff_attention(proj, lq1, lk1, lq2, lk2, subln, lambda_init, *, tq=ATT_TILE):
    B, S, _ = proj.shape
    width = DIFF_HEADS * DIFF_V
    rows = DIFF_V + BF16_ROWS
    small = lambda n: pl.BlockSpec((1, n), lambda b, i: (0, 0))
    return pl.pallas_call(
        functools.partial(_diff_kernel, tq=tq, lambda_init=lambda_init),
        grid=(B, S // tq),
        in_specs=[pl.BlockSpec((1, tq, width), lambda b, i: (b, i, 3)),
                  pl.BlockSpec((1, S, width), lambda b, i: (b, 0, 4)),
                  pl.BlockSpec((1, S, width), lambda b, i: (b, 0, 5)),
                  small(DIFF_QK), small(DIFF_QK), small(DIFF_QK), small(DIFF_QK), small(DIFF_V)],
        out_specs=pl.BlockSpec((1, tq, width), lambda b, i: (b, i, 0)),
        out_shape=jax.ShapeDtypeStruct((B, S, width), BF16),
        scratch_shapes=[pltpu.VMEM((DIFF_HEADS, rows, S), BF16),
                        pltpu.VMEM((2 * DIFF_HEADS, tq, LANES), BF16),
                        pltpu.VMEM((2 * DIFF_HEADS, rows, tq), F32),
                        pltpu.VMEM((2 * DIFF_HEADS, 1, tq), F32),
                        pltpu.VMEM((2 * DIFF_HEADS, tq, tq), F32),
                        pltpu.VMEM((2 * DIFF_HEADS, tq, tq), F32)],
        compiler_params=_params("parallel", "arbitrary"),
        name="diff_attention",
    )(proj, proj, proj, lq1.reshape(1, -1), lk1.reshape(1, -1), lq2.reshape(1, -1), lk2.reshape(1, -1),
      subln.reshape(1, -1))


def _mlstm_kernel(qk_ref, halo_ref, v_ref, og_ref, gc_ref, gr_ref, cw_ref, hn_ref, o_ref, c_ref, n_ref, m_ref,
                  *, L):
    c = pl.program_id(1)

    @pl.when(c == 0)
    def _():
        c_ref[...] = jnp.zeros_like(c_ref)
        n_ref[...] = jnp.zeros_like(n_ref)
        m_ref[...] = jnp.zeros_like(m_ref)

    x = qk_ref[0].astype(F32)
    halo = jnp.where(c == 0, 0.0, halo_ref[0].astype(F32))
    xe = jnp.concatenate([halo, x], axis=0)
    cw = cw_ref[...]
    y = cw[ML_CONV - 1:ML_CONV] * x
    for tap in range(ML_CONV - 1):
        off = BF16_ROWS - (ML_CONV - 1) + tap
        y = y + cw[tap:tap + 1] * xe[off:off + L]
    y = y * _sigmoid(y)

    row = lax.broadcasted_iota(jnp.int32, (L, L), 0)
    col = lax.broadcasted_iota(jnp.int32, (L, L), 1)
    causal = col <= row
    gc = gc_ref[0]
    gr = gr_ref[0]
    kw_off = ML_HEADS * ML_QK
    for h in range(ML_HEADS):
        q = (y[:, h * ML_QK:(h + 1) * ML_QK] * (ML_QK ** -0.5)).astype(BF16)
        kf = y[:, kw_off + h * ML_QK:kw_off + (h + 1) * ML_QK]
        k = kf.astype(BF16)
        v = v_ref[0, :, h * ML_V:(h + 1) * ML_V]
        r_col = gc[:, h:h + 1] - gc[:, ML_HEADS + h:ML_HEADS + h + 1]
        b_col = gc[:, ML_HEADS + h:ML_HEADS + h + 1]
        b_row = gr[ML_HEADS + h:ML_HEADS + h + 1, :]
        r_row = gr[h:h + 1, :] - b_row
        g = b_row[:, L - 1:L]
        C = c_ref[h]
        n = n_ref[h]
        m = m_ref[h][:, 0:1]

        dm = jnp.where(causal, r_row, NEG_INF)
        mt = jnp.maximum(m, jnp.max(dm, axis=1, keepdims=True))
        w_inter = jnp.exp(m - mt)
        s = _dot_nt(q, k) * jnp.exp(dm - mt)
        num = w_inter * _dot(q, C.astype(BF16)) + _dot(s.astype(BF16), v)
        den = (w_inter * jnp.sum(q.astype(F32) * n, axis=1, keepdims=True)
               + jnp.sum(s, axis=1, keepdims=True))
        hh = num / jnp.maximum(jnp.abs(den), jnp.exp(-(b_col + mt)))

        m_next = jnp.maximum(m, jnp.max(r_row, axis=1, keepdims=True))
        decay = jnp.exp(m - m_next)
        kw = kf * jnp.exp(r_col - m_next)
        c_ref[h] = decay * C + _dot_tn(kw.astype(BF16), v)
        n_ref[h] = decay * n + jnp.sum(kw, axis=0, keepdims=True)
        m_ref[h] = jnp.broadcast_to(g + m_next, (1, LANES))

        vs = slice(h * ML_V, (h + 1) * ML_V)
        hh = _rms(hh, hn_ref[:, vs])
        o_ref[0, :, vs] = (hh * _sigmoid(og_ref[0, :, vs].astype(F32))).astype(o_ref.dtype)


def mlstm(proj, g_col, g_row, conv_w, head_norm, *, L=ML_CHUNK):
    B, S, _ = proj.shape
    W = D_MODEL
    halo_blocks = L // BF16_ROWS
    return pl.pallas_call(
        functools.partial(_mlstm_kernel, L=L),
        grid=(B, S // L),
        in_specs=[pl.BlockSpec((1, L, W), lambda b, c: (b, c, 0)),
                  pl.BlockSpec((1, BF16_ROWS, W), lambda b, c: (b, jnp.maximum(c * halo_blocks - 1, 0), 0)),
                  pl.BlockSpec((1, L, W), lambda b, c: (b, c, 1)),
                  pl.BlockSpec((1, L, W), lambda b, c: (b, c, 2)),
                  pl.BlockSpec((1, L, 2 * ML_HEADS), lambda b, c: (b, c, 0)),
                  pl.BlockSpec((1, 2 * ML_HEADS, L), lambda b, c: (b, 0, c)),
                  pl.BlockSpec((ML_CONV, W), lambda b, c: (0, 0)),
                  pl.BlockSpec((1, W), lambda b, c: (0, 0))],
        out_specs=pl.BlockSpec((1, L, W), lambda b, c: (b, c, 0)),
        out_shape=jax.ShapeDtypeStruct((B, S, W), BF16),
        scratch_shapes=[pltpu.VMEM((ML_HEADS, ML_QK, ML_V), F32),
                        pltpu.VMEM((ML_HEADS, 1, ML_QK), F32),
                        pltpu.VMEM((ML_HEADS, 1, LANES), F32)],
        compiler_params=_params("parallel", "arbitrary"),
        name="mlstm",
    )(proj, proj, proj, proj, g_col, g_row, conv_w, head_norm.reshape(1, W))


def _xattn_kernel(x_ref, g_ref, wq_ref, k_ref, v_ref, wo_ref, o_ref):
    x = x_ref[0]
    xn = _rms(x, g_ref[...]).astype(BF16)
    q = (_dot(xn, wq_ref[...]) * (X_DIM ** -0.5)).astype(BF16)
    heads = []
    for h in range(X_HEADS):
        cols = slice(h * X_DIM, (h + 1) * X_DIM)
        s = _dot_nt(q[:, cols], k_ref[0, :, cols])
        p = jnp.exp(s - jnp.max(s, axis=1, keepdims=True))
        l = jnp.sum(p, axis=1, keepdims=True)
        heads.append((_dot(p.astype(BF16), v_ref[0, :, cols]) / l).astype(BF16))
    o_ref[0] = x + _dot(jnp.concatenate(heads, axis=1), wo_ref[...])


def memory_cross_attention(x, gain, wq, kv, wo, *, tq=X_TILE):
    B, S, D = x.shape
    M = kv.shape[1]
    return pl.pallas_call(
        _xattn_kernel,
        grid=(B, S // tq),
        in_specs=[pl.BlockSpec((1, tq, D), lambda b, i: (b, i, 0)),
                  pl.BlockSpec((1, D), lambda b, i: (0, 0)),
                  pl.BlockSpec((D, D), lambda b, i: (0, 0)),
                  pl.BlockSpec((1, M, D), lambda b, i: (b, 0, 0)),
                  pl.BlockSpec((1, M, D), lambda b, i: (b, 0, 1)),
                  pl.BlockSpec((D, D), lambda b, i: (0, 0))],
        out_specs=pl.BlockSpec((1, tq, D), lambda b, i: (b, i, 0)),
        out_shape=jax.ShapeDtypeStruct((B, S, D), F32),
        compiler_params=_params("parallel", "parallel"),
        name="memory_cross_attention",
    )(x, gain.reshape(1, D), wq, kv, kv, wo)


def _gelu_tanh(x):
    k = -2.0 * math.sqrt(2.0 / math.pi) * math.log2(math.e)
    return x / (1.0 + jnp.exp2(x * (k * 0.044715 * (x * x) + k)))


def _ffn_kernel(x_ref, halo_ref, g_ref, wup_ref, cw_ref, cb_ref, wd_ref, fg_ref, o_ref, xn_ref, up_ref, act_ref,
                nat_ref, *, tm, tiles_per_seq, final_norm):
    i = pl.program_id(0)
    halo = _rms(halo_ref[...], g_ref[...])
    xn_ref[0:BF16_ROWS, :] = jnp.where(i % tiles_per_seq == 0, 0.0, halo).astype(BF16)
    xn_ref[BF16_ROWS:, :] = _rms(x_ref[...], g_ref[...]).astype(BF16)
    xe = xn_ref[...]

    n_chunks = D_FF // FFN_CHUNK
    slabs = FFN_CHUNK // LANES
    rows = tm // ROW_PHASES

    def up_project(c):
        for half in range(2):
            col0 = half * D_FF + c * FFN_CHUNK
            up = _dot(xe, wup_ref[:, col0:col0 + FFN_CHUNK])
            for s in range(slabs):
                up_ref[c % 2, half, s] = up[:, s * LANES:(s + 1) * LANES]

    def conv(c, half, s, phase):
        col0 = half * D_FF + c * FFN_CHUNK + s * LANES
        cw = cw_ref[:, col0:col0 + LANES]
        out = cb_ref[:, col0:col0 + LANES]
        for tap in range(FFN_CONV):
            first = BF16_ROWS + phase - (FFN_CONV - 1 - tap)
            out = out + cw[tap:tap + 1] * up_ref[c % 2, half, s, pl.ds(first, rows, stride=ROW_PHASES), :]
        return out

    acc = None
    piece_start = 0
    up_project(0)
    for c in range(n_chunks):
        if c + 1 < n_chunks:
            up_project(c + 1)
        for phase in range(ROW_PHASES):
            for s in range(slabs):
                act = _gelu_tanh(conv(c, 0, s, phase)) * conv(c, 1, s, phase)
                act_ref[phase * rows:(phase + 1) * rows,
                        c * FFN_CHUNK + s * LANES:c * FFN_CHUNK + (s + 1) * LANES] = act.astype(BF16)
        if (c + 1) % DOWN_CHUNKS == 0 or c + 1 == n_chunks:
            piece = slice(piece_start * FFN_CHUNK, (c + 1) * FFN_CHUNK)
            part = _dot(act_ref[:, piece], wd_ref[piece, :])
            acc = part if acc is None else acc + part
            piece_start = c + 1

    for phase in range(ROW_PHASES):
        for s in range(D_MODEL // LANES):
            nat_ref[s, pl.ds(phase, rows, stride=ROW_PHASES), :] = acc[phase * rows:(phase + 1) * rows,
                                                                       s * LANES:(s + 1) * LANES]
    y = x_ref[...] + jnp.concatenate([nat_ref[s] for s in range(D_MODEL // LANES)], axis=1)
    o_ref[...] = _rms(y, fg_ref[...]) if final_norm else y


def conv_ffn(x, gain, w_up, conv_w, conv_b, w_down, seq_len, final_gain=None, *, tm=ROW_TILE):
    T, D = x.shape
    halo_blocks = tm // BF16_ROWS
    final_norm = final_gain is not None
    fg = (final_gain if final_norm else gain).reshape(1, D)
    conv_b = conv_b.reshape(1, -1)
    return pl.pallas_call(
        functools.partial(_ffn_kernel, tm=tm, tiles_per_seq=seq_len // tm, final_norm=final_norm),
        grid=(T // tm,),
        in_specs=[pl.BlockSpec((tm, D), lambda i: (i, 0)),
                  pl.BlockSpec((BF16_ROWS, D), lambda i: (jnp.maximum(i * halo_blocks - 1, 0), 0)),
                  _resident((1, D)), _resident(w_up.shape), _resident(conv_w.shape), _resident(conv_b.shape),
                  _resident(w_down.shape), _resident((1, D))],
        out_specs=pl.BlockSpec((tm, D), lambda i: (i, 0)),
        out_shape=jax.ShapeDtypeStruct((T, D), F32),
        scratch_shapes=[pltpu.VMEM((tm + BF16_ROWS, D), BF16),
                        pltpu.VMEM((2, 2, FFN_CHUNK // LANES, tm + BF16_ROWS, LANES), F32),
                        pltpu.VMEM((tm, D_FF), BF16),
                        pltpu.VMEM((D // LANES, tm, LANES), F32)],
        compiler_params=_params("parallel"),
        name="conv_ffn",
    )(x, x, gain.reshape(1, D), w_up, conv_w, conv_b, w_down, fg)


def _pad_gate_cols(w):
    return jnp.pad(w, ((0, 0), (0, LANES - w.shape[1]))).astype(BF16)


def _fox_bias_operands(f):
    B, S, H = f.shape
    to_bf16_grid = functools.partial(lax.reduce_precision, exponent_bits=8, mantissa_bits=7)
    hi = to_bf16_grid(f)
    mid = to_bf16_grid(f - hi)
    lo = f - hi - mid
    pieces = jnp.stack([hi, mid, lo], axis=-1).astype(BF16).reshape(B, S, F_PIECES * H)
    ones = jnp.ones_like(pieces)
    gap = jnp.zeros((B, S, FOX_BIAS_Q_LANE - F_PIECES * H), BF16)
    tail = jnp.zeros((B, S, LANES - FOX_BIAS_Q_LANE - F_PIECES * H), BF16)
    fq = jnp.concatenate([ones, gap, pieces, tail], axis=-1)
    fk = jnp.concatenate([-pieces, gap, ones, tail], axis=-1)
    return fq, fk


def _fox_diff_layer(x, B, S, gain, w_in, fox_bf, lq1, lk1, lq2, lk2, subln, w_out, lambda_init):
    fw = FOX_HEADS * FOX_DIM
    g0 = 3 * fw
    w_main = jnp.concatenate([w_in[:, :g0], w_in[:, g0 + FOX_HEADS:]], axis=1).astype(BF16)
    proj, gates = norm_matmul(x, gain, w_main, _pad_gate_cols(w_in[:, g0:g0 + FOX_HEADS]))
    log_f = jax.nn.log_sigmoid(gates[:, :FOX_HEADS] + fox_bf).reshape(B, S, FOX_HEADS)
    fq, fk = _fox_bias_operands(jnp.cumsum(log_f, axis=1))
    proj = proj.reshape(B, S, -1)
    fox = fox_attention(proj, fq, fk)
    dif = diff_attention(proj, lq1, lk1, lq2, lk2, subln, lambda_init)
    w_out = w_out.astype(BF16)
    return matmul_residual([fox.reshape(B * S, -1), dif.reshape(B * S, -1)], [w_out[:fw], w_out[fw:]], x)


def _mlstm_layer(x, B, S, gain, w_in, conv_qk, b_i, b_f, head_norm, w_out):
    g0 = 2 * ML_HEADS * ML_QK + ML_HEADS * ML_V
    w_main = jnp.concatenate([w_in[:, :g0], w_in[:, g0 + 2 * ML_HEADS:]], axis=1).astype(BF16)
    proj, gates = norm_matmul(x, gain, w_main, _pad_gate_cols(w_in[:, g0:g0 + 2 * ML_HEADS]))
    i_pre = (gates[:, :ML_HEADS] + b_i).reshape(B, S, ML_HEADS)
    log_f = jax.nn.log_sigmoid(gates[:, ML_HEADS:2 * ML_HEADS] + b_f)
    b = jnp.cumsum(log_f.reshape(B, S // ML_CHUNK, ML_CHUNK, ML_HEADS), axis=2).reshape(B, S, ML_HEADS)
    g_col = jnp.concatenate([i_pre, b], axis=-1)
    h = mlstm(proj.reshape(B, S, -1), g_col, g_col.transpose(0, 2, 1), conv_qk, head_norm)
    return matmul_residual([h.reshape(B * S, -1)], [w_out.astype(BF16)], x)


def kernel(x, mem, mix_norm, xattn_norm, mem_norm, ffn_norm, attn_w_in, attn_fox_bf, diff_lq1, diff_lk1, diff_lq2, diff_lk2, diff_subln, attn_w_out, mlstm_w_in, mlstm_conv_qk, mlstm_b_i, mlstm_b_f, mlstm_head_norm, mlstm_w_out, xattn_wq, xattn_wkv, xattn_wo, ffn_w_up, ffn_conv_w, ffn_conv_b, ffn_w_down, final_norm):
    B, S, D = x.shape
    M = mem.shape[1]
    depth = mix_norm.shape[0]
    x = x.reshape(B * S, D)
    mem2 = mem.reshape(B * M, D)
    for layer in range(depth):
        j = layer // 2
        if layer % 2 == 0:
            lambda_init = 0.8 - 0.6 * math.exp(-0.3 * layer)
            x = _fox_diff_layer(x, B, S, mix_norm[layer], attn_w_in[j], attn_fox_bf[j], diff_lq1[j], diff_lk1[j],
                                diff_lq2[j], diff_lk2[j], diff_subln[j], attn_w_out[j], lambda_init)
        else:
            x = _mlstm_layer(x, B, S, mix_norm[layer], mlstm_w_in[j], mlstm_conv_qk[j], mlstm_b_i[j], mlstm_b_f[j],
                             mlstm_head_norm[j], mlstm_w_out[j])
        kv = norm_matmul(mem2, mem_norm[layer], xattn_wkv[layer].astype(BF16)).reshape(B, M, 2 * D)
        x = memory_cross_attention(x.reshape(B, S, D), xattn_norm[layer], xattn_wq[layer].astype(BF16), kv,
                                   xattn_wo[layer].astype(BF16)).reshape(B * S, D)
        x = conv_ffn(x, ffn_norm[layer], ffn_w_up[layer].astype(BF16), ffn_conv_w[layer], ffn_conv_b[layer],
                     ffn_w_down[layer].astype(BF16), S, final_norm if layer == depth - 1 else None)
    return x.reshape(B, S, D)
```

```python
import functools
import math

import jax
import jax.numpy as jnp
from jax import lax
from jax.experimental import pallas as pl
from jax.experimental.pallas import tpu as pltpu

F32 = jnp.float32
BF16 = jnp.bfloat16

D_MODEL = 1024
RMS_EPS = 1e-6
NEG_INF = -1e30
CHUNK = 64
FOX_HEADS, FOX_DIM = 8, 64
DIFF_HEADS, DIFF_QK, DIFF_V = 4, 64, 128
ML_HEADS, ML_QK, ML_V, ML_CONV = 4, 128, 256, 4
X_HEADS, X_DIM = 4, 256
D_FF = 2816
FFN_CONV = 3
LANES = 128
BF16_ROWS = 16
VMEM_LIMIT = 56 * 1024 * 1024

ROW_TILE = 512
ATT_Q_TILE = 512
ATT_K_TILE = 256
LOG2_E = math.log2(math.e)
ML_CHUNK = 256
ML_SEQS = 2
X_TILE = 512
FFN_CHUNK = 256
ROW_PHASES = 4
DOWN_CHUNKS = 4
TRANSPOSE_COLS = 512
F_PIECES = 3
FOX_BIAS_K_LANE = 0
FOX_BIAS_Q_LANE = 32


def _params(*sem):
    return pltpu.CompilerParams(dimension_semantics=sem, vmem_limit_bytes=VMEM_LIMIT)


def _resident(shape):
    return pl.BlockSpec(shape, lambda *_: (0,) * len(shape), pipeline_mode=pl.Buffered(1))


def _rms(x, gain):
    return x * lax.rsqrt(jnp.mean(x * x, axis=-1, keepdims=True) + RMS_EPS) * gain


def _sigmoid(x):
    return 1.0 / (1.0 + jnp.exp(-x))


def _dot(a, b):
    return jnp.dot(a, b, preferred_element_type=F32)


def _dot_nt(a, b):
    return lax.dot_general(a, b, (((1,), (1,)), ((), ())), preferred_element_type=F32)


def _dot_tn(a, b):
    return lax.dot_general(a, b, (((0,), (0,)), ((), ())), preferred_element_type=F32)


def _norm_matmul_kernel(*refs, has_gates, tn):
    if has_gates:
        x_ref, g_ref, w_ref, wg_ref, o_ref, og_ref = refs
    else:
        x_ref, g_ref, w_ref, o_ref = refs
    xn = _rms(x_ref[...], g_ref[...]).astype(BF16)
    if has_gates:
        og_ref[...] = _dot(xn, wg_ref[...])
    for c0 in range(0, o_ref.shape[1], tn):
        o_ref[:, c0:c0 + tn] = _dot(xn, w_ref[:, c0:c0 + tn]).astype(o_ref.dtype)


def norm_matmul(x, gain, w, w_gates=None, *, tm=ROW_TILE, tn=1024):
    T, D = x.shape
    N = w.shape[1]
    has_gates = w_gates is not None
    in_specs = [pl.BlockSpec((tm, D), lambda i: (i, 0)), _resident((1, D)), _resident(w.shape)]
    out_specs = [pl.BlockSpec((tm, N), lambda i: (i, 0))]
    out_shape = [jax.ShapeDtypeStruct((T, N), BF16)]
    args = [x, gain.reshape(1, D), w]
    if has_gates:
        in_specs.append(_resident(w_gates.shape))
        out_specs.append(pl.BlockSpec((tm, LANES), lambda i: (i, 0)))
        out_shape.append(jax.ShapeDtypeStruct((T, LANES), F32))
        args.append(w_gates)
    outs = pl.pallas_call(
        functools.partial(_norm_matmul_kernel, has_gates=has_gates, tn=tn),
        grid=(T // tm,),
        in_specs=in_specs, out_specs=out_specs, out_shape=out_shape,
        compiler_params=_params("parallel"),
        name="norm_matmul",
    )(*args)
    return outs if has_gates else outs[0]


def _matmul_residual_kernel(*refs, n_in):
    res_ref, o_ref = refs[2 * n_in], refs[2 * n_in + 1]
    acc = res_ref[...]
    for a_ref, w_ref in zip(refs[:n_in], refs[n_in:2 * n_in]):
        acc = acc + _dot(a_ref[...], w_ref[...])
    o_ref[...] = acc


def matmul_residual(acts, weights, res, *, tm=ROW_TILE):
    T, D = res.shape
    n_in = len(acts)
    in_specs = ([pl.BlockSpec((tm, a.shape[1]), lambda i: (i, 0)) for a in acts]
                + [_resident(w.shape) for w in weights]
                + [pl.BlockSpec((tm, D), lambda i: (i, 0))])
    return pl.pallas_call(
        functools.partial(_matmul_residual_kernel, n_in=n_in),
        grid=(T // tm,),
        in_specs=in_specs,
        out_specs=pl.BlockSpec((tm, D), lambda i: (i, 0)),
        out_shape=jax.ShapeDtypeStruct((T, D), F32),
        compiler_params=_params("parallel"),
        name="matmul_residual",
    )(*acts, *weights, res)


def _build_vt(v_ref, vt_ref, n_heads, rows):
    S = v_ref.shape[1]
    r = lax.broadcasted_iota(jnp.int32, (LANES, LANES), 0)
    c = lax.broadcasted_iota(jnp.int32, (LANES, LANES), 1)
    eye = jnp.where(r == c, 1.0, 0.0).astype(BF16)
    per_group = LANES // rows
    for g in range(n_heads // per_group):
        for c0 in range(0, S, TRANSPOSE_COLS):
            cs = slice(c0, c0 + TRANSPOSE_COLS)
            vt = _dot_nt(eye, v_ref[0, cs, g * LANES:(g + 1) * LANES]).astype(BF16)
            for k in range(per_group):
                vt_ref[g * per_group + k, 0:rows, cs] = vt[k * rows:(k + 1) * rows]
    for h in range(n_heads):
        vt_ref[h, rows:rows + BF16_ROWS, :] = jnp.ones((BF16_ROWS, S), BF16)


def _softmax_step(s_t, vt, m_ref, acc_ref, idx):
    m_prev = m_ref[idx]
    m_new = jnp.maximum(m_prev, jnp.max(s_t, axis=0, keepdims=True))
    p = jnp.exp2(s_t - m_new).astype(BF16)
    acc_ref[idx] = jnp.exp2(m_prev - m_new) * acc_ref[idx] + _dot(vt, p)
    m_ref[idx] = m_new


def _causal_steps(i, logits, attend, n_streams, buf_a, buf_b):
    def phase(j, src, dst, mask=None):
        for n in range(n_streams):
            logits(j + 1, dst, n)
            attend(j, src, n, mask)

    for n in range(n_streams):
        logits(0, buf_a, n)

    def body(jj, carry):
        phase(2 * jj, buf_a, buf_b)
        phase(2 * jj + 1, buf_b, buf_a)
        return carry

    lax.fori_loop(0, i, body, 0)
    phase(2 * i, buf_a, buf_b, 0)
    for n in range(n_streams):
        attend(2 * i + 1, buf_b, n, 1)


def _fox_kernel(q_ref, k_ref, v_ref, fq_ref, fk_ref, o_ref, vt_ref, qc_ref, acc_ref, m_ref, sa_ref, sb_ref, *, tq, tk):
    i = pl.program_id(1)

    @pl.when(i == 0)
    def _():
        _build_vt(v_ref, vt_ref, FOX_HEADS, FOX_DIM)

    lane = lax.broadcasted_iota(jnp.int32, (1, LANES), 1)
    fq = fq_ref[0]
    for h in range(FOX_HEADS):
        pair, half = divmod(h, 2)
        q = q_ref[0, :, pair * LANES:(pair + 1) * LANES]
        in_head = (lane >= half * FOX_DIM) & (lane < (half + 1) * FOX_DIM)
        lo_k, lo_q = FOX_BIAS_K_LANE + F_PIECES * h, FOX_BIAS_Q_LANE + F_PIECES * h
        mine = ((lane >= lo_k) & (lane < lo_k + F_PIECES)) | ((lane >= lo_q) & (lane < lo_q + F_PIECES))
        qc_ref[h, :, 0:LANES] = jnp.where(in_head, q, jnp.zeros_like(q))
        qc_ref[h, :, LANES:2 * LANES] = jnp.where(mine, fq, jnp.zeros_like(fq))
    m_ref[...] = jnp.full(m_ref.shape, NEG_INF, F32)
    acc_ref[...] = jnp.zeros(acc_ref.shape, F32)

    key = lax.broadcasted_iota(jnp.int32, (tk, tq), 0)
    qry = lax.broadcasted_iota(jnp.int32, (tk, tq), 1)
    causal = [key + half * tk <= qry for half in range(tq // tk)]

    def key_rows(j):
        return pl.ds(pl.multiple_of(j * tk, tk), tk)

    def logits(j, buf, h):
        pair = h // 2
        kc = jnp.concatenate([k_ref[0, key_rows(j), pair * LANES:(pair + 1) * LANES], fk_ref[0, key_rows(j), :]],
                             axis=1)
        buf[h] = _dot_nt(kc, qc_ref[h])

    def attend(j, buf, h, mask):
        s_t = buf[h]
        if mask is not None:
            s_t = jnp.where(causal[mask], s_t, NEG_INF)
        _softmax_step(s_t, vt_ref[h, :, key_rows(j)], m_ref, acc_ref, h)

    _causal_steps(i, logits, attend, FOX_HEADS, sa_ref, sb_ref)

    for pair in range(FOX_HEADS // 2):
        halves = []
        for half in range(2):
            a = acc_ref[2 * pair + half]
            halves.append(a[0:FOX_DIM] / a[FOX_DIM:FOX_DIM + 1])
        o_ref[0, :, pair * LANES:(pair + 1) * LANES] = jnp.concatenate(halves, axis=0).T.astype(o_ref.dtype)


def fox_attention(proj, fq, fk, *, tq=ATT_Q_TILE, tk=ATT_K_TILE):
    B, S, _ = proj.shape
    width = FOX_HEADS * FOX_DIM
    rows = FOX_DIM + BF16_ROWS
    return pl.pallas_call(
        functools.partial(_fox_kernel, tq=tq, tk=tk),
        grid=(B, S // tq),
        in_specs=[pl.BlockSpec((1, tq, width), lambda b, i: (b, i, 0)),
                  pl.BlockSpec((1, S, width), lambda b, i: (b, 0, 1)),
                  pl.BlockSpec((1, S, width), lambda b, i: (b, 0, 2)),
                  pl.BlockSpec((1, tq, LANES), lambda b, i: (b, i, 0)),
                  pl.BlockSpec((1, S, LANES), lambda b, i: (b, 0, 0))],
        out_specs=pl.BlockSpec((1, tq, width), lambda b, i: (b, i, 0)),
        out_shape=jax.ShapeDtypeStruct((B, S, width), BF16),
        scratch_shapes=[pltpu.VMEM((FOX_HEADS, rows, S), BF16),
                        pltpu.VMEM((FOX_HEADS, tq, 2 * LANES), BF16),
                        pltpu.VMEM((FOX_HEADS, rows, tq), F32),
                        pltpu.VMEM((FOX_HEADS, 1, tq), F32),
                        pltpu.VMEM((FOX_HEADS, tk, tq), F32),
                        pltpu.VMEM((FOX_HEADS, tk, tq), F32)],
        compiler_params=_params("parallel", "arbitrary"),
        name="fox_attention",
    )(proj, proj, proj, fq, fk)


def _diff_kernel(q_ref, k_ref, v_ref, lq1_ref, lk1_ref, lq2_ref, lk2_ref, sub_ref, o_ref, vt_ref, qc_ref, acc_ref,
                 m_ref, sa_ref, sb_ref, *, tq, tk, lambda_init):
    i = pl.program_id(1)

    @pl.when(i == 0)
    def _():
        _build_vt(v_ref, vt_ref, DIFF_HEADS, DIFF_V)

    lane = lax.broadcasted_iota(jnp.int32, (1, LANES), 1)
    for h in range(DIFF_HEADS):
        q = q_ref[0, :, h * LANES:(h + 1) * LANES]
        zero = jnp.zeros_like(q)
        qc_ref[2 * h] = jnp.where(lane < DIFF_QK, q, zero)
        qc_ref[2 * h + 1] = jnp.where(lane >= DIFF_QK, q, zero)
    m_ref[...] = jnp.full(m_ref.shape, NEG_INF, F32)
    acc_ref[...] = jnp.zeros(acc_ref.shape, F32)

    key = lax.broadcasted_iota(jnp.int32, (tk, tq), 0)
    qry = lax.broadcasted_iota(jnp.int32, (tk, tq), 1)
    visible = [(key + half * tk) // CHUNK <= qry // CHUNK for half in range(tq // tk)]

    def key_rows(j):
        return pl.ds(pl.multiple_of(j * tk, tk), tk)

    def logits(j, buf, n):
        h = n // 2
        buf[n] = _dot_nt(k_ref[0, key_rows(j), h * LANES:(h + 1) * LANES], qc_ref[n])

    def attend(j, buf, n, mask):
        s_t = buf[n]
        if mask is not None:
            s_t = jnp.where(visible[mask], s_t, NEG_INF)
        _softmax_step(s_t, vt_ref[n // 2, :, key_rows(j)], m_ref, acc_ref, n)

    _causal_steps(i, logits, attend, 2 * DIFF_HEADS, sa_ref, sb_ref)

    lam = (jnp.exp(jnp.sum(lq1_ref[...] * lk1_ref[...], axis=1, keepdims=True))
           - jnp.exp(jnp.sum(lq2_ref[...] * lk2_ref[...], axis=1, keepdims=True)) + lambda_init)
    for h in range(DIFF_HEADS):
        a1, a2 = acc_ref[2 * h], acc_ref[2 * h + 1]
        o_t = a1[0:DIFF_V] / a1[DIFF_V:DIFF_V + 1] - lam * (a2[0:DIFF_V] / a2[DIFF_V:DIFF_V + 1])
        out = _rms(o_t.T, sub_ref[...]) * (1.0 - lambda_init)
        o_ref[0, :, h * LANES:(h + 1) * LANES] = out.astype(o_ref.dtype)


def diff_attention(proj, lq1, lk1, lq2, lk2, subln, lambda_init, *, tq=ATT_Q_TILE, tk=ATT_K_TILE):
    B, S, _ = proj.shape
    width = DIFF_HEADS * DIFF_V
    rows = DIFF_V + BF16_ROWS
    small = lambda n: pl.BlockSpec((1, n), lambda b, i: (0, 0))
    return pl.pallas_call(
        functools.partial(_diff_kernel, tq=tq, tk=tk, lambda_init=lambda_init),
        grid=(B, S // tq),
        in_specs=[pl.BlockSpec((1, tq, width), lambda b, i: (b, i, 3)),
                  pl.BlockSpec((1, S, width), lambda b, i: (b, 0, 4)),
                  pl.BlockSpec((1, S, width), lambda b, i: (b, 0, 5)),
                  small(DIFF_QK), small(DIFF_QK), small(DIFF_QK), small(DIFF_QK), small(DIFF_V)],
        out_specs=pl.BlockSpec((1, tq, width), lambda b, i: (b, i, 0)),
        out_shape=jax.ShapeDtypeStruct((B, S, width), BF16),
        scratch_shapes=[pltpu.VMEM((DIFF_HEADS, rows, S), BF16),
                        pltpu.VMEM((2 * DIFF_HEADS, tq, LANES), BF16),
                        pltpu.VMEM((2 * DIFF_HEADS, rows, tq), F32),
                        pltpu.VMEM((2 * DIFF_HEADS, 1, tq), F32),
                        pltpu.VMEM((2 * DIFF_HEADS, tk, tq), F32),
                        pltpu.VMEM((2 * DIFF_HEADS, tk, tq), F32)],
        compiler_params=_params("parallel", "arbitrary"),
        name="diff_attention",
    )(proj, proj, proj, lq1.reshape(1, -1), lk1.reshape(1, -1), lq2.reshape(1, -1), lk2.reshape(1, -1),
      subln.reshape(1, -1))


def _mlstm_kernel(qk_ref, halo_ref, v_ref, og_ref, gc_ref, gr_ref, cw_ref, hn_ref, o_ref, c_ref, n_ref, m_ref,
                  xe_ref, y_ref, *, L, nb):
    c = pl.program_id(1)

    @pl.when(c == 0)
    def _():
        c_ref[...] = jnp.zeros_like(c_ref)
        n_ref[...] = jnp.zeros_like(n_ref)
        m_ref[...] = jnp.zeros_like(m_ref)

    rows = L // ROW_PHASES
    for b in range(nb):
        for s in range(2 * ML_HEADS):
            cols = slice(s * LANES, (s + 1) * LANES)
            xe_ref[b, s, 0:BF16_ROWS, :] = jnp.where(c == 0, 0.0, halo_ref[b, :, cols].astype(F32))
            xe_ref[b, s, BF16_ROWS:, :] = qk_ref[b, :, cols].astype(F32)
            cw = cw_ref[:, cols]
            for phase in range(ROW_PHASES):
                y = None
                for tap in range(ML_CONV):
                    first = BF16_ROWS + phase - (ML_CONV - 1 - tap)
                    term = cw[tap:tap + 1] * xe_ref[b, s, pl.ds(first, rows, stride=ROW_PHASES), :]
                    y = term if y is None else y + term
                y_ref[b, s, pl.ds(phase, rows, stride=ROW_PHASES), :] = y * _sigmoid(y)

    row = lax.broadcasted_iota(jnp.int32, (L, L), 0)
    col = lax.broadcasted_iota(jnp.int32, (L, L), 1)
    causal = col <= row
    for b in range(nb):
        gc = gc_ref[b]
        gr = gr_ref[b]
        for h in range(ML_HEADS):
            q = (y_ref[b, h] * (ML_QK ** -0.5)).astype(BF16)
            kf = y_ref[b, ML_HEADS + h]
            k = kf.astype(BF16)
            v = v_ref[b, :, h * ML_V:(h + 1) * ML_V]
            r_col = gc[:, h:h + 1] - gc[:, ML_HEADS + h:ML_HEADS + h + 1]
            b_col = gc[:, ML_HEADS + h:ML_HEADS + h + 1]
            b_row = gr[ML_HEADS + h:ML_HEADS + h + 1, :]
            r_row = gr[h:h + 1, :] - b_row
            g = b_row[:, L - 1:L]
            C = c_ref[b, h]
            n = n_ref[b, h]
            m = m_ref[b, h][:, 0:1]

            dm = jnp.where(causal, r_row, NEG_INF)
            mt = jnp.maximum(m, jnp.max(dm, axis=1, keepdims=True))
            w_inter = jnp.exp(m - mt)
            s = _dot_nt(q, k) * jnp.exp(dm - mt)
            num = w_inter * _dot(q, C.astype(BF16)) + _dot(s.astype(BF16), v)
            den = (w_inter * jnp.sum(q.astype(F32) * n, axis=1, keepdims=True)
                   + jnp.sum(s, axis=1, keepdims=True))
            hh = num / jnp.maximum(jnp.abs(den), jnp.exp(-(b_col + mt)))

            m_next = jnp.maximum(m, jnp.max(r_row, axis=1, keepdims=True))
            decay = jnp.exp(m - m_next)
            kw = kf * jnp.exp(r_col - m_next)
            c_ref[b, h] = decay * C + _dot_tn(kw.astype(BF16), v)
            n_ref[b, h] = decay * n + jnp.sum(kw, axis=0, keepdims=True)
            m_ref[b, h] = jnp.broadcast_to(g + m_next, (1, LANES))

            vs = slice(h * ML_V, (h + 1) * ML_V)
            hh = _rms(hh, hn_ref[:, vs])
            o_ref[b, :, vs] = (hh * _sigmoid(og_ref[b, :, vs].astype(F32))).astype(o_ref.dtype)


def mlstm(proj, g_col, g_row, conv_w, head_norm, *, L=ML_CHUNK, nb=ML_SEQS):
    B, S, _ = proj.shape
    W = D_MODEL
    halo_blocks = L // BF16_ROWS
    return pl.pallas_call(
        functools.partial(_mlstm_kernel, L=L, nb=nb),
        grid=(B // nb, S // L),
        in_specs=[pl.BlockSpec((nb, L, W), lambda b, c: (b, c, 0)),
                  pl.BlockSpec((nb, BF16_ROWS, W), lambda b, c: (b, jnp.maximum(c * halo_blocks - 1, 0), 0)),
                  pl.BlockSpec((nb, L, W), lambda b, c: (b, c, 1)),
                  pl.BlockSpec((nb, L, W), lambda b, c: (b, c, 2)),
                  pl.BlockSpec((nb, L, 2 * ML_HEADS), lambda b, c: (b, c, 0)),
                  pl.BlockSpec((nb, 2 * ML_HEADS, L), lambda b, c: (b, 0, c)),
                  pl.BlockSpec((ML_CONV, W), lambda b, c: (0, 0)),
                  pl.BlockSpec((1, W), lambda b, c: (0, 0))],
        out_specs=pl.BlockSpec((nb, L, W), lambda b, c: (b, c, 0)),
        out_shape=jax.ShapeDtypeStruct((B, S, W), BF16),
        scratch_shapes=[pltpu.VMEM((nb, ML_HEADS, ML_QK, ML_V), F32),
                        pltpu.VMEM((nb, ML_HEADS, 1, ML_QK), F32),
                        pltpu.VMEM((nb, ML_HEADS, 1, LANES), F32),
                        pltpu.VMEM((nb, 2 * ML_HEADS, L + BF16_ROWS, LANES), F32),
                        pltpu.VMEM((nb, 2 * ML_HEADS, L, LANES), F32)],
        compiler_params=_params("parallel", "arbitrary"),
        name="mlstm",
    )(proj, proj, proj, proj, g_col, g_row, conv_w, head_norm.reshape(1, W))


def _xattn_kernel(x_ref, g_ref, wq_ref, k_ref, v_ref, wo_ref, o_ref):
    x = x_ref[0]
    xn = _rms(x, g_ref[...]).astype(BF16)
    q = (_dot(xn, wq_ref[...]) * (X_DIM ** -0.5)).astype(BF16)
    heads = []
    for h in range(X_HEADS):
        cols = slice(h * X_DIM, (h + 1) * X_DIM)
        s = _dot_nt(q[:, cols], k_ref[0, :, cols])
        p = jnp.exp(s - jnp.max(s, axis=1, keepdims=True))
        l = jnp.sum(p, axis=1, keepdims=True)
        heads.append((_dot(p.astype(BF16), v_ref[0, :, cols]) / l).astype(BF16))
    o_ref[0] = x + _dot(jnp.concatenate(heads, axis=1), wo_ref[...])


def memory_cross_attention(x, gain, wq, kv, wo, *, tq=X_TILE):
    B, S, D = x.shape
    M = kv.shape[1]
    return pl.pallas_call(
        _xattn_kernel,
        grid=(B, S // tq),
        in_specs=[pl.BlockSpec((1, tq, D), lambda b, i: (b, i, 0)),
                  pl.BlockSpec((1, D), lambda b, i: (0, 0)),
                  pl.BlockSpec((D, D), lambda b, i: (0, 0)),
                  pl.BlockSpec((1, M, D), lambda b, i: (b, 0, 0)),
                  pl.BlockSpec((1, M, D), lambda b, i: (b, 0, 1)),
                  pl.BlockSpec((D, D), lambda b, i: (0, 0))],
        out_specs=pl.BlockSpec((1, tq, D), lambda b, i: (b, i, 0)),
        out_shape=jax.ShapeDtypeStruct((B, S, D), F32),
        compiler_params=_params("parallel", "parallel"),
        name="memory_cross_attention",
    )(x, gain.reshape(1, D), wq, kv, kv, wo)


def _gelu_tanh(x):
    k = -2.0 * math.sqrt(2.0 / math.pi) * math.log2(math.e)
    return x / (1.0 + jnp.exp2(x * (k * 0.044715 * (x * x) + k)))


def _ffn_kernel(x_ref, halo_ref, g_ref, wup_ref, cw_ref, cb_ref, wd_ref, fg_ref, o_ref, xn_ref, up_ref, act_ref,
                nat_ref, *, tm, tiles_per_seq, final_norm):
    i = pl.program_id(0)
    halo = _rms(halo_ref[...], g_ref[...])
    xn_ref[0:BF16_ROWS, :] = jnp.where(i % tiles_per_seq == 0, 0.0, halo).astype(BF16)
    xn_ref[BF16_ROWS:, :] = _rms(x_ref[...], g_ref[...]).astype(BF16)
    xe = xn_ref[...]

    n_chunks = D_FF // FFN_CHUNK
    slabs = FFN_CHUNK // LANES
    rows = tm // ROW_PHASES

    def up_project(c):
        for half in range(2):
            col0 = half * D_FF + c * FFN_CHUNK
            up = _dot(xe, wup_ref[:, col0:col0 + FFN_CHUNK])
            for s in range(slabs):
                up_ref[c % 2, half, s] = up[:, s * LANES:(s + 1) * LANES]

    def conv(c, half, s, phase):
        col0 = half * D_FF + c * FFN_CHUNK + s * LANES
        cw = cw_ref[:, col0:col0 + LANES]
        out = cb_ref[:, col0:col0 + LANES]
        for tap in range(FFN_CONV):
            first = BF16_ROWS + phase - (FFN_CONV - 1 - tap)
            out = out + cw[tap:tap + 1] * up_ref[c % 2, half, s, pl.ds(first, rows, stride=ROW_PHASES), :]
        return out

    acc = None
    piece_start = 0
    up_project(0)
    for c in range(n_chunks):
        if c + 1 < n_chunks:
            up_project(c + 1)
        for phase in range(ROW_PHASES):
            for s in range(slabs):
                act = _gelu_tanh(conv(c, 0, s, phase)) * conv(c, 1, s, phase)
                act_ref[phase * rows:(phase + 1) * rows,
                        c * FFN_CHUNK + s * LANES:c * FFN_CHUNK + (s + 1) * LANES] = act.astype(BF16)
        if (c + 1) % DOWN_CHUNKS == 0 or c + 1 == n_chunks:
            piece = slice(piece_start * FFN_CHUNK, (c + 1) * FFN_CHUNK)
            part = _dot(act_ref[:, piece], wd_ref[piece, :])
            acc = part if acc is None else acc + part
            piece_start = c + 1

    for phase in range(ROW_PHASES):
        for s in range(D_MODEL // LANES):
            nat_ref[s, pl.ds(phase, rows, stride=ROW_PHASES), :] = acc[phase * rows:(phase + 1) * rows,
                                                                       s * LANES:(s + 1) * LANES]
    y = x_ref[...] + jnp.concatenate([nat_ref[s] for s in range(D_MODEL // LANES)], axis=1)
    o_ref[...] = _rms(y, fg_ref[...]) if final_norm else y


def conv_ffn(x, gain, w_up, conv_w, conv_b, w_down, seq_len, final_gain=None, *, tm=ROW_TILE):
    T, D = x.shape
    halo_blocks = tm // BF16_ROWS
    final_norm = final_gain is not None
    fg = (final_gain if final_norm else gain).reshape(1, D)
    conv_b = conv_b.reshape(1, -1)
    return pl.pallas_call(
        functools.partial(_ffn_kernel, tm=tm, tiles_per_seq=seq_len // tm, final_norm=final_norm),
        grid=(T // tm,),
        in_specs=[pl.BlockSpec((tm, D), lambda i: (i, 0)),
                  pl.BlockSpec((BF16_ROWS, D), lambda i: (jnp.maximum(i * halo_blocks - 1, 0), 0)),
                  _resident((1, D)), _resident(w_up.shape), _resident(conv_w.shape), _resident(conv_b.shape),
                  _resident(w_down.shape), _resident((1, D))],
        out_specs=pl.BlockSpec((tm, D), lambda i: (i, 0)),
        out_shape=jax.ShapeDtypeStruct((T, D), F32),
        scratch_shapes=[pltpu.VMEM((tm + BF16_ROWS, D), BF16),
                        pltpu.VMEM((2, 2, FFN_CHUNK // LANES, tm + BF16_ROWS, LANES), F32),
                        pltpu.VMEM((tm, D_FF), BF16),
                        pltpu.VMEM((D // LANES, tm, LANES), F32)],
        compiler_params=_params("parallel"),
        name="conv_ffn",
    )(x, x, gain.reshape(1, D), w_up, conv_w, conv_b, w_down, fg)


def _pad_gate_cols(w):
    return jnp.pad(w, ((0, 0), (0, LANES - w.shape[1]))).astype(BF16)


def _fox_bias_operands(f):
    B, S, H = f.shape
    to_bf16_grid = functools.partial(lax.reduce_precision, exponent_bits=8, mantissa_bits=7)
    hi = to_bf16_grid(f)
    mid = to_bf16_grid(f - hi)
    lo = f - hi - mid
    pieces = jnp.stack([hi, mid, lo], axis=-1).astype(BF16).reshape(B, S, F_PIECES * H)
    ones = jnp.ones_like(pieces)
    gap = jnp.zeros((B, S, FOX_BIAS_Q_LANE - F_PIECES * H), BF16)
    tail = jnp.zeros((B, S, LANES - FOX_BIAS_Q_LANE - F_PIECES * H), BF16)
    fq = jnp.concatenate([ones, gap, pieces, tail], axis=-1)
    fk = jnp.concatenate([-pieces, gap, ones, tail], axis=-1)
    return fq, fk


def _fox_diff_layer(x, B, S, gain, w_in, fox_bf, lq1, lk1, lq2, lk2, subln, w_out, lambda_init):
    fw = FOX_HEADS * FOX_DIM
    g0 = 3 * fw
    dq0 = g0 + FOX_HEADS
    dqw = DIFF_HEADS * 2 * DIFF_QK
    w_main = jnp.concatenate([w_in[:, :fw] * (FOX_DIM ** -0.5 * LOG2_E), w_in[:, fw:g0],
                              w_in[:, dq0:dq0 + dqw] * (DIFF_QK ** -0.5 * LOG2_E), w_in[:, dq0 + dqw:]],
                             axis=1).astype(BF16)
    proj, gates = norm_matmul(x, gain, w_main, _pad_gate_cols(w_in[:, g0:g0 + FOX_HEADS]))
    log_f = jax.nn.log_sigmoid(gates[:, :FOX_HEADS] + fox_bf).reshape(B, S, FOX_HEADS)
    fq, fk = _fox_bias_operands(jnp.cumsum(log_f, axis=1) * LOG2_E)
    proj = proj.reshape(B, S, -1)
    fox = fox_attention(proj, fq, fk)
    dif = diff_attention(proj, lq1, lk1, lq2, lk2, subln, lambda_init)
    w_out = w_out.astype(BF16)
    return matmul_residual([fox.reshape(B * S, -1), dif.reshape(B * S, -1)], [w_out[:fw], w_out[fw:]], x)


def _mlstm_layer(x, B, S, gain, w_in, conv_qk, b_i, b_f, head_norm, w_out):
    g0 = 2 * ML_HEADS * ML_QK + ML_HEADS * ML_V
    w_main = jnp.concatenate([w_in[:, :g0], w_in[:, g0 + 2 * ML_HEADS:]], axis=1).astype(BF16)
    proj, gates = norm_matmul(x, gain, w_main, _pad_gate_cols(w_in[:, g0:g0 + 2 * ML_HEADS]))
    i_pre = (gates[:, :ML_HEADS] + b_i).reshape(B, S, ML_HEADS)
    log_f = jax.nn.log_sigmoid(gates[:, ML_HEADS:2 * ML_HEADS] + b_f)
    b = jnp.cumsum(log_f.reshape(B, S // ML_CHUNK, ML_CHUNK, ML_HEADS), axis=2).reshape(B, S, ML_HEADS)
    g_col = jnp.concatenate([i_pre, b], axis=-1)
    h = mlstm(proj.reshape(B, S, -1), g_col, g_col.transpose(0, 2, 1), conv_qk, head_norm)
    return matmul_residual([h.reshape(B * S, -1)], [w_out.astype(BF16)], x)


def kernel(x, mem, mix_norm, xattn_norm, mem_norm, ffn_norm, attn_w_in, attn_fox_bf, diff_lq1, diff_lk1, diff_lq2, diff_lk2, diff_subln, attn_w_out, mlstm_w_in, mlstm_conv_qk, mlstm_b_i, mlstm_b_f, mlstm_head_norm, mlstm_w_out, xattn_wq, xattn_wkv, xattn_wo, ffn_w_up, ffn_conv_w, ffn_conv_b, ffn_w_down, final_norm):
    B, S, D = x.shape
    M = mem.shape[1]
    depth = mix_norm.shape[0]
    x = x.reshape(B * S, D)
    mem2 = mem.reshape(B * M, D)
    for layer in range(depth):
        j = layer // 2
        if layer % 2 == 0:
            lambda_init = 0.8 - 0.6 * math.exp(-0.3 * layer)
            x = _fox_diff_layer(x, B, S, mix_norm[layer], attn_w_in[j], attn_fox_bf[j], diff_lq1[j], diff_lk1[j],
                                diff_lq2[j], diff_lk2[j], diff_subln[j], attn_w_out[j], lambda_init)
        else:
            x = _mlstm_layer(x, B, S, mix_norm[layer], mlstm_w_in[j], mlstm_conv_qk[j], mlstm_b_i[j], mlstm_b_f[j],
                             mlstm_head_norm[j], mlstm_w_out[j])
        kv = norm_matmul(mem2, mem_norm[layer], xattn_wkv[layer].astype(BF16)).reshape(B, M, 2 * D)
        x = memory_cross_attention(x.reshape(B, S, D), xattn_norm[layer], xattn_wq[layer].astype(BF16), kv,
                                   xattn_wo[layer].astype(BF16)).reshape(B * S, D)
        x = conv_ffn(x, ffn_norm[layer], ffn_w_up[layer].astype(BF16), ffn_conv_w[layer], ffn_conv_b[layer],
                     ffn_w_down[layer].astype(BF16), S, final_norm if layer == depth - 1 else None)
    return x.reshape(B, S, D)
```

```python
import functools
import math

import jax
import jax.numpy as jnp
from jax import lax
from jax.experimental import pallas as pl
from jax.experimental.pallas import tpu as pltpu

F32 = jnp.float32
BF16 = jnp.bfloat16

D_MODEL = 1024
RMS_EPS = 1e-6
NEG_INF = -1e30
CHUNK = 64
FOX_HEADS, FOX_DIM = 8, 64
DIFF_HEADS, DIFF_QK, DIFF_V = 4, 64, 128
ML_HEADS, ML_QK, ML_V, ML_CONV = 4, 128, 256, 4
X_HEADS, X_DIM = 4, 256
D_FF = 2816
FFN_CONV = 3
LANES = 128
BF16_ROWS = 16
VMEM_LIMIT = 56 * 1024 * 1024

ROW_TILE = 512
ATT_Q_TILE = 512
ATT_K_TILE = 256
LOG2_E = math.log2(math.e)
ML_CHUNK = 256
ML_SEQS = 2
X_TILE = 512
FFN_CHUNK = 256
ROW_PHASES = 4
DOWN_CHUNKS = 4
TRANSPOSE_COLS = 512
F_PIECES = 3
FOX_BIAS_K_LANE = 0
FOX_BIAS_Q_LANE = 32


def _params(*sem):
    return pltpu.CompilerParams(dimension_semantics=sem, vmem_limit_bytes=VMEM_LIMIT)


def _resident(shape):
    return pl.BlockSpec(shape, lambda *_: (0,) * len(shape), pipeline_mode=pl.Buffered(1))


def _rms(x, gain):
    return x * lax.rsqrt(jnp.mean(x * x, axis=-1, keepdims=True) + RMS_EPS) * gain


def _sigmoid(x):
    return 1.0 / (1.0 + jnp.exp(-x))


def _dot(a, b):
    return jnp.dot(a, b, preferred_element_type=F32)


def _dot_nt(a, b):
    return lax.dot_general(a, b, (((1,), (1,)), ((), ())), preferred_element_type=F32)


def _dot_tn(a, b):
    return lax.dot_general(a, b, (((0,), (0,)), ((), ())), preferred_element_type=F32)


def _norm_matmul_kernel(*refs, has_gates, tn):
    if has_gates:
        x_ref, g_ref, w_ref, wg_ref, o_ref, og_ref = refs
    else:
        x_ref, g_ref, w_ref, o_ref = refs
    xn = _rms(x_ref[...], g_ref[...]).astype(BF16)
    if has_gates:
        og_ref[...] = _dot(xn, wg_ref[...])
    for c0 in range(0, o_ref.shape[1], tn):
        o_ref[:, c0:c0 + tn] = _dot(xn, w_ref[:, c0:c0 + tn]).astype(o_ref.dtype)


def norm_matmul(x, gain, w, w_gates=None, *, tm=ROW_TILE, tn=1024):
    T, D = x.shape
    N = w.shape[1]
    has_gates = w_gates is not None
    in_specs = [pl.BlockSpec((tm, D), lambda i: (i, 0)), _resident((1, D)), _resident(w.shape)]
    out_specs = [pl.BlockSpec((tm, N), lambda i: (i, 0))]
    out_shape = [jax.ShapeDtypeStruct((T, N), BF16)]
    args = [x, gain.reshape(1, D), w]
    if has_gates:
        in_specs.append(_resident(w_gates.shape))
        out_specs.append(pl.BlockSpec((tm, LANES), lambda i: (i, 0)))
        out_shape.append(jax.ShapeDtypeStruct((T, LANES), F32))
        args.append(w_gates)
    outs = pl.pallas_call(
        functools.partial(_norm_matmul_kernel, has_gates=has_gates, tn=tn),
        grid=(T // tm,),
        in_specs=in_specs, out_specs=out_specs, out_shape=out_shape,
        compiler_params=_params("parallel"),
        name="norm_matmul",
    )(*args)
    return outs if has_gates else outs[0]


def _build_vt(v_ref, vt_ref, n_heads, rows):
    S = v_ref.shape[1]
    r = lax.broadcasted_iota(jnp.int32, (LANES, LANES), 0)
    c = lax.broadcasted_iota(jnp.int32, (LANES, LANES), 1)
    eye = jnp.where(r == c, 1.0, 0.0).astype(BF16)
    per_group = LANES // rows
    for g in range(n_heads // per_group):
        for c0 in range(0, S, TRANSPOSE_COLS):
            cs = slice(c0, c0 + TRANSPOSE_COLS)
            vt = _dot_nt(eye, v_ref[0, cs, g * LANES:(g + 1) * LANES]).astype(BF16)
            for k in range(per_group):
                vt_ref[g * per_group + k, 0:rows, cs] = vt[k * rows:(k + 1) * rows]
    for h in range(n_heads):
        vt_ref[h, rows:rows + BF16_ROWS, :] = jnp.ones((BF16_ROWS, S), BF16)


def _softmax_step(s_t, vt, m_ref, acc_ref, idx):
    m_prev = m_ref[idx]
    m_new = jnp.maximum(m_prev, jnp.max(s_t, axis=0, keepdims=True))
    p = jnp.exp2(s_t - m_new).astype(BF16)
    acc_ref[idx] = jnp.exp2(m_prev - m_new) * acc_ref[idx] + _dot(vt, p)
    m_ref[idx] = m_new


def _causal_steps(i, logits, attend, n_streams, buf_a, buf_b):
    def phase(j, src, dst, mask=None):
        for n in range(n_streams):
            logits(j + 1, dst, n)
            attend(j, src, n, mask)

    for n in range(n_streams):
        logits(0, buf_a, n)

    def body(jj, carry):
        phase(2 * jj, buf_a, buf_b)
        phase(2 * jj + 1, buf_b, buf_a)
        return carry

    lax.fori_loop(0, i, body, 0)
    phase(2 * i, buf_a, buf_b, 0)
    for n in range(n_streams):
        attend(2 * i + 1, buf_b, n, 1)


def _fox_kernel(q_ref, k_ref, v_ref, fq_ref, fk_ref, o_ref, vt_ref, qc_ref, acc_ref, m_ref, sa_ref, sb_ref, *, tq, tk):
    i = pl.program_id(1)

    @pl.when(i == 0)
    def _():
        _build_vt(v_ref, vt_ref, FOX_HEADS, FOX_DIM)

    lane = lax.broadcasted_iota(jnp.int32, (1, LANES), 1)
    fq = fq_ref[0]
    for h in range(FOX_HEADS):
        pair, half = divmod(h, 2)
        q = q_ref[0, :, pair * LANES:(pair + 1) * LANES]
        in_head = (lane >= half * FOX_DIM) & (lane < (half + 1) * FOX_DIM)
        lo_k, lo_q = FOX_BIAS_K_LANE + F_PIECES * h, FOX_BIAS_Q_LANE + F_PIECES * h
        mine = ((lane >= lo_k) & (lane < lo_k + F_PIECES)) | ((lane >= lo_q) & (lane < lo_q + F_PIECES))
        qc_ref[h, :, 0:LANES] = jnp.where(in_head, q, jnp.zeros_like(q))
        qc_ref[h, :, LANES:2 * LANES] = jnp.where(mine, fq, jnp.zeros_like(fq))
    m_ref[...] = jnp.full(m_ref.shape, NEG_INF, F32)
    acc_ref[...] = jnp.zeros(acc_ref.shape, F32)

    key = lax.broadcasted_iota(jnp.int32, (tk, tq), 0)
    qry = lax.broadcasted_iota(jnp.int32, (tk, tq), 1)
    causal = [key + half * tk <= qry for half in range(tq // tk)]

    def key_rows(j):
        return pl.ds(pl.multiple_of(j * tk, tk), tk)

    def logits(j, buf, h):
        pair = h // 2
        kc = jnp.concatenate([k_ref[0, key_rows(j), pair * LANES:(pair + 1) * LANES], fk_ref[0, key_rows(j), :]],
                             axis=1)
        buf[h] = _dot_nt(kc, qc_ref[h])

    def attend(j, buf, h, mask):
        s_t = buf[h]
        if mask is not None:
            s_t = jnp.where(causal[mask], s_t, NEG_INF)
        _softmax_step(s_t, vt_ref[h, :, key_rows(j)], m_ref, acc_ref, h)

    _causal_steps(i, logits, attend, FOX_HEADS, sa_ref, sb_ref)

    for pair in range(FOX_HEADS // 2):
        halves = []
        for half in range(2):
            a = acc_ref[2 * pair + half]
            halves.append(a[0:FOX_DIM] / a[FOX_DIM:FOX_DIM + 1])
        o_ref[0, :, pair * LANES:(pair + 1) * LANES] = jnp.concatenate(halves, axis=0).T.astype(o_ref.dtype)


def fox_attention(proj, fq, fk, *, tq=ATT_Q_TILE, tk=ATT_K_TILE):
    B, S, _ = proj.shape
    width = FOX_HEADS * FOX_DIM
    rows = FOX_DIM + BF16_ROWS
    return pl.pallas_call(
        functools.partial(_fox_kernel, tq=tq, tk=tk),
        grid=(B, S // tq),
        in_specs=[pl.BlockSpec((1, tq, width), lambda b, i: (b, i, 0)),
                  pl.BlockSpec((1, S, width), lambda b, i: (b, 0, 1)),
                  pl.BlockSpec((1, S, width), lambda b, i: (b, 0, 2)),
                  pl.BlockSpec((1, tq, LANES), lambda b, i: (b, i, 0)),
                  pl.BlockSpec((1, S, LANES), lambda b, i: (b, 0, 0))],
        out_specs=pl.BlockSpec((1, tq, width), lambda b, i: (b, i, 0)),
        out_shape=jax.ShapeDtypeStruct((B, S, width), BF16),
        scratch_shapes=[pltpu.VMEM((FOX_HEADS, rows, S), BF16),
                        pltpu.VMEM((FOX_HEADS, tq, 2 * LANES), BF16),
                        pltpu.VMEM((FOX_HEADS, rows, tq), F32),
                        pltpu.VMEM((FOX_HEADS, 1, tq), F32),
                        pltpu.VMEM((FOX_HEADS, tk, tq), F32),
                        pltpu.VMEM((FOX_HEADS, tk, tq), F32)],
        compiler_params=_params("parallel", "arbitrary"),
        name="fox_attention",
    )(proj, proj, proj, fq, fk)


def _diff_kernel(q_ref, k_ref, v_ref, lq1_ref, lk1_ref, lq2_ref, lk2_ref, sub_ref, o_ref, vt_ref, qc_ref, acc_ref,
                 m_ref, sa_ref, sb_ref, *, tq, tk, lambda_init):
    i = pl.program_id(1)

    @pl.when(i == 0)
    def _():
        _build_vt(v_ref, vt_ref, DIFF_HEADS, DIFF_V)

    lane = lax.broadcasted_iota(jnp.int32, (1, LANES), 1)
    for h in range(DIFF_HEADS):
        q = q_ref[0, :, h * LANES:(h + 1) * LANES]
        zero = jnp.zeros_like(q)
        qc_ref[2 * h] = jnp.where(lane < DIFF_QK, q, zero)
        qc_ref[2 * h + 1] = jnp.where(lane >= DIFF_QK, q, zero)
    m_ref[...] = jnp.full(m_ref.shape, NEG_INF, F32)
    acc_ref[...] = jnp.zeros(acc_ref.shape, F32)

    key = lax.broadcasted_iota(jnp.int32, (tk, tq), 0)
    qry = lax.broadcasted_iota(jnp.int32, (tk, tq), 1)
    visible = [(key + half * tk) // CHUNK <= qry // CHUNK for half in range(tq // tk)]

    def key_rows(j):
        return pl.ds(pl.multiple_of(j * tk, tk), tk)

    def logits(j, buf, n):
        h = n // 2
        buf[n] = _dot_nt(k_ref[0, key_rows(j), h * LANES:(h + 1) * LANES], qc_ref[n])

    def attend(j, buf, n, mask):
        s_t = buf[n]
        if mask is not None:
            s_t = jnp.where(visible[mask], s_t, NEG_INF)
        _softmax_step(s_t, vt_ref[n // 2, :, key_rows(j)], m_ref, acc_ref, n)

    _causal_steps(i, logits, attend, 2 * DIFF_HEADS, sa_ref, sb_ref)

    lam = (jnp.exp(jnp.sum(lq1_ref[...] * lk1_ref[...], axis=1, keepdims=True))
           - jnp.exp(jnp.sum(lq2_ref[...] * lk2_ref[...], axis=1, keepdims=True)) + lambda_init)
    for h in range(DIFF_HEADS):
        a1, a2 = acc_ref[2 * h], acc_ref[2 * h + 1]
        o_t = a1[0:DIFF_V] / a1[DIFF_V:DIFF_V + 1] - lam * (a2[0:DIFF_V] / a2[DIFF_V:DIFF_V + 1])
        out = _rms(o_t.T, sub_ref[...]) * (1.0 - lambda_init)
        o_ref[0, :, h * LANES:(h + 1) * LANES] = out.astype(o_ref.dtype)


def diff_attention(proj, lq1, lk1, lq2, lk2, subln, lambda_init, *, tq=ATT_Q_TILE, tk=ATT_K_TILE):
    B, S, _ = proj.shape
    width = DIFF_HEADS * DIFF_V
    rows = DIFF_V + BF16_ROWS
    small = lambda n: pl.BlockSpec((1, n), lambda b, i: (0, 0))
    return pl.pallas_call(
        functools.partial(_diff_kernel, tq=tq, tk=tk, lambda_init=lambda_init),
        grid=(B, S // tq),
        in_specs=[pl.BlockSpec((1, tq, width), lambda b, i: (b, i, 3)),
                  pl.BlockSpec((1, S, width), lambda b, i: (b, 0, 4)),
                  pl.BlockSpec((1, S, width), lambda b, i: (b, 0, 5)),
                  small(DIFF_QK), small(DIFF_QK), small(DIFF_QK), small(DIFF_QK), small(DIFF_V)],
        out_specs=pl.BlockSpec((1, tq, width), lambda b, i: (b, i, 0)),
        out_shape=jax.ShapeDtypeStruct((B, S, width), BF16),
        scratch_shapes=[pltpu.VMEM((DIFF_HEADS, rows, S), BF16),
                        pltpu.VMEM((2 * DIFF_HEADS, tq, LANES), BF16),
                        pltpu.VMEM((2 * DIFF_HEADS, rows, tq), F32),
                        pltpu.VMEM((2 * DIFF_HEADS, 1, tq), F32),
                        pltpu.VMEM((2 * DIFF_HEADS, tk, tq), F32),
                        pltpu.VMEM((2 * DIFF_HEADS, tk, tq), F32)],
        compiler_params=_params("parallel", "arbitrary"),
        name="diff_attention",
    )(proj, proj, proj, lq1.reshape(1, -1), lk1.reshape(1, -1), lq2.reshape(1, -1), lk2.reshape(1, -1),
      subln.reshape(1, -1))


def _mlstm_kernel(qk_ref, halo_ref, v_ref, og_ref, gc_ref, gr_ref, cw_ref, hn_ref, o_ref, ct_ref, m_ref,
                  xe_ref, y_ref, vt_ref, *, L, nb):
    c = pl.program_id(1)

    @pl.when(c == 0)
    def _():
        ct_ref[...] = jnp.zeros_like(ct_ref)
        m_ref[...] = jnp.zeros_like(m_ref)
        for n in range(nb * ML_HEADS):
            vt_ref[n, ML_V:ML_V + BF16_ROWS, :] = jnp.ones((BF16_ROWS, L), BF16)

    rows = L // ROW_PHASES
    for b in range(nb):
        for s in range(2 * ML_HEADS):
            cols = slice(s * LANES, (s + 1) * LANES)
            xe_ref[b, s, 0:BF16_ROWS, :] = jnp.where(c == 0, 0.0, halo_ref[b, :, cols].astype(F32))
            xe_ref[b, s, BF16_ROWS:, :] = qk_ref[b, :, cols].astype(F32)
            cw = cw_ref[:, cols]
            for phase in range(ROW_PHASES):
                y = None
                for tap in range(ML_CONV):
                    first = BF16_ROWS + phase - (ML_CONV - 1 - tap)
                    term = cw[tap:tap + 1] * xe_ref[b, s, pl.ds(first, rows, stride=ROW_PHASES), :]
                    y = term if y is None else y + term
                y_ref[b, s, pl.ds(phase, rows, stride=ROW_PHASES), :] = y * _sigmoid(y)

    src = lax.broadcasted_iota(jnp.int32, (L, L), 0)
    dst = lax.broadcasted_iota(jnp.int32, (L, L), 1)
    causal = src <= dst
    eye = jnp.where(src == dst, 1.0, 0.0).astype(BF16)
    for b in range(nb):
        gc = gc_ref[b]
        gr = gr_ref[b]
        for h in range(ML_HEADS):
            n = b * ML_HEADS + h
            q = (y_ref[b, h] * (ML_QK ** -0.5)).astype(BF16)
            kf = y_ref[b, ML_HEADS + h]
            k = kf.astype(BF16)
            vt_ref[n, 0:ML_V, :] = _dot_nt(eye, v_ref[b, :, h * ML_V:(h + 1) * ML_V]).astype(BF16)
            v_t = vt_ref[n]
            r_col = gc[:, h:h + 1] - gc[:, ML_HEADS + h:ML_HEADS + h + 1]
            b_row = gr[ML_HEADS + h:ML_HEADS + h + 1, :]
            r_row = gr[h:h + 1, :] - b_row
            g = b_row[:, L - 1:L]
            ct = ct_ref[n]
            m = m_ref[n][:, 0:1]

            dm = jnp.where(causal, r_col, NEG_INF)
            mt = jnp.maximum(m, jnp.max(dm, axis=0, keepdims=True))
            s_t = _dot_nt(k, q) * jnp.exp(dm - mt)
            both = jnp.exp(m - mt) * _dot_nt(ct.astype(BF16), q) + _dot(v_t, s_t.astype(BF16))
            den = both[ML_V:ML_V + 1]
            hh = both[0:ML_V] / jnp.maximum(jnp.abs(den), jnp.exp(-(b_row + mt)))
            hh = hh * lax.rsqrt(jnp.mean(hh * hh, axis=0, keepdims=True) + RMS_EPS)
            hi = hh.astype(BF16)
            lo = (hh - hi.astype(F32)).astype(BF16)
            hn = _dot_nt(eye, hi) + _dot_nt(eye, lo)

            m_next = jnp.maximum(m, jnp.max(r_row, axis=1, keepdims=True))
            kw = (kf * jnp.exp(r_col - m_next)).astype(BF16)
            ct_ref[n] = jnp.exp(m - m_next) * ct + _dot(v_t, kw)
            m_ref[n] = jnp.broadcast_to(g + m_next, (1, LANES))

            vs = slice(h * ML_V, (h + 1) * ML_V)
            o_ref[b, :, vs] = (hn * hn_ref[:, vs] * _sigmoid(og_ref[b, :, vs].astype(F32))).astype(o_ref.dtype)


def mlstm(proj, g_col, g_row, conv_w, head_norm, *, L=ML_CHUNK, nb=ML_SEQS):
    B, S, _ = proj.shape
    W = D_MODEL
    halo_blocks = L // BF16_ROWS
    assert L == ML_V, "one identity matrix serves both the v and the output transposes"
    return pl.pallas_call(
        functools.partial(_mlstm_kernel, L=L, nb=nb),
        grid=(B // nb, S // L),
        in_specs=[pl.BlockSpec((nb, L, W), lambda b, c: (b, c, 0)),
                  pl.BlockSpec((nb, BF16_ROWS, W), lambda b, c: (b, jnp.maximum(c * halo_blocks - 1, 0), 0)),
                  pl.BlockSpec((nb, L, W), lambda b, c: (b, c, 1)),
                  pl.BlockSpec((nb, L, W), lambda b, c: (b, c, 2)),
                  pl.BlockSpec((nb, L, 2 * ML_HEADS), lambda b, c: (b, c, 0)),
                  pl.BlockSpec((nb, 2 * ML_HEADS, L), lambda b, c: (b, 0, c)),
                  pl.BlockSpec((ML_CONV, W), lambda b, c: (0, 0)),
                  pl.BlockSpec((1, W), lambda b, c: (0, 0))],
        out_specs=pl.BlockSpec((nb, L, W), lambda b, c: (b, c, 0)),
        out_shape=jax.ShapeDtypeStruct((B, S, W), BF16),
        scratch_shapes=[pltpu.VMEM((nb * ML_HEADS, ML_V + BF16_ROWS, ML_QK), F32),
                        pltpu.VMEM((nb * ML_HEADS, 1, LANES), F32),
                        pltpu.VMEM((nb, 2 * ML_HEADS, L + BF16_ROWS, LANES), F32),
                        pltpu.VMEM((nb, 2 * ML_HEADS, L, LANES), F32),
                        pltpu.VMEM((nb * ML_HEADS, ML_V + BF16_ROWS, L), BF16)],
        compiler_params=_params("parallel", "arbitrary"),
        name="mlstm",
    )(proj, proj, proj, proj, g_col, g_row, conv_w, head_norm.reshape(1, W))


def _xattn_kernel(*refs, n_in):
    x_ref, g_ref, wq_ref, k_ref, v_ref, wo_ref, o_ref = refs[2 * n_in:]
    x = x_ref[0]
    for a_ref, w_ref in zip(refs[:n_in], refs[n_in:2 * n_in]):
        x = x + _dot(a_ref[0], w_ref[...])
    xn = _rms(x, g_ref[...]).astype(BF16)
    q = (_dot(xn, wq_ref[...]) * (X_DIM ** -0.5)).astype(BF16)
    heads = []
    for h in range(X_HEADS):
        cols = slice(h * X_DIM, (h + 1) * X_DIM)
        s = _dot_nt(q[:, cols], k_ref[0, :, cols])
        p = jnp.exp(s - jnp.max(s, axis=1, keepdims=True))
        l = jnp.sum(p, axis=1, keepdims=True)
        heads.append((_dot(p.astype(BF16), v_ref[0, :, cols]) / l).astype(BF16))
    o_ref[0] = x + _dot(jnp.concatenate(heads, axis=1), wo_ref[...])


def mix_out_cross_attention(acts, w_mix, x, gain, wq, kv, wo, *, tq=X_TILE):
    B, S, D = x.shape
    M = kv.shape[1]
    return pl.pallas_call(
        functools.partial(_xattn_kernel, n_in=len(acts)),
        grid=(B, S // tq),
        in_specs=([pl.BlockSpec((1, tq, a.shape[2]), lambda b, i: (b, i, 0)) for a in acts]
                  + [_resident(w.shape) for w in w_mix]
                  + [pl.BlockSpec((1, tq, D), lambda b, i: (b, i, 0)),
                     _resident((1, D)), _resident((D, D)),
                     pl.BlockSpec((1, M, D), lambda b, i: (b, 0, 0)),
                     pl.BlockSpec((1, M, D), lambda b, i: (b, 0, 1)),
                     _resident((D, D))]),
        out_specs=pl.BlockSpec((1, tq, D), lambda b, i: (b, i, 0)),
        out_shape=jax.ShapeDtypeStruct((B, S, D), F32),
        compiler_params=_params("parallel", "parallel"),
        name="mix_out_cross_attention",
    )(*acts, *w_mix, x, gain.reshape(1, D), wq, kv, kv, wo)


def _gelu_tanh(x):
    k = -2.0 * math.sqrt(2.0 / math.pi) * math.log2(math.e)
    return x / (1.0 + jnp.exp2(x * (k * 0.044715 * (x * x) + k)))


def _ffn_kernel(x_ref, halo_ref, g_ref, wup_ref, cw_ref, cb_ref, wd_ref, fg_ref, o_ref, xn_ref, up_ref, act_ref,
                nat_ref, *, tm, tiles_per_seq, final_norm):
    i = pl.program_id(0)
    halo = _rms(halo_ref[...], g_ref[...])
    xn_ref[0:BF16_ROWS, :] = jnp.where(i % tiles_per_seq == 0, 0.0, halo).astype(BF16)
    xn_ref[BF16_ROWS:, :] = _rms(x_ref[...], g_ref[...]).astype(BF16)
    xe = xn_ref[...]

    n_chunks = D_FF // FFN_CHUNK
    slabs = FFN_CHUNK // LANES
    rows = tm // ROW_PHASES

    def up_project(c):
        for half in range(2):
            col0 = half * D_FF + c * FFN_CHUNK
            up = _dot(xe, wup_ref[:, col0:col0 + FFN_CHUNK])
            for s in range(slabs):
                up_ref[c % 2, half, s] = up[:, s * LANES:(s + 1) * LANES]

    def conv(c, half, s, phase):
        col0 = half * D_FF + c * FFN_CHUNK + s * LANES
        cw = cw_ref[:, col0:col0 + LANES]
        out = cb_ref[:, col0:col0 + LANES]
        for tap in range(FFN_CONV):
            first = BF16_ROWS + phase - (FFN_CONV - 1 - tap)
            out = out + cw[tap:tap + 1] * up_ref[c % 2, half, s, pl.ds(first, rows, stride=ROW_PHASES), :]
        return out

    acc = None
    piece_start = 0
    up_project(0)
    for c in range(n_chunks):
        if c + 1 < n_chunks:
            up_project(c + 1)
        for phase in range(ROW_PHASES):
            for s in range(slabs):
                act = _gelu_tanh(conv(c, 0, s, phase)) * conv(c, 1, s, phase)
                act_ref[phase * rows:(phase + 1) * rows,
                        c * FFN_CHUNK + s * LANES:c * FFN_CHUNK + (s + 1) * LANES] = act.astype(BF16)
        if (c + 1) % DOWN_CHUNKS == 0 or c + 1 == n_chunks:
            piece = slice(piece_start * FFN_CHUNK, (c + 1) * FFN_CHUNK)
            part = _dot(act_ref[:, piece], wd_ref[piece, :])
            acc = part if acc is None else acc + part
            piece_start = c + 1

    for phase in range(ROW_PHASES):
        for s in range(D_MODEL // LANES):
            nat_ref[s, pl.ds(phase, rows, stride=ROW_PHASES), :] = acc[phase * rows:(phase + 1) * rows,
                                                                       s * LANES:(s + 1) * LANES]
    y = x_ref[...] + jnp.concatenate([nat_ref[s] for s in range(D_MODEL // LANES)], axis=1)
    o_ref[...] = _rms(y, fg_ref[...]) if final_norm else y


def conv_ffn(x, gain, w_up, conv_w, conv_b, w_down, seq_len, final_gain=None, *, tm=ROW_TILE):
    T, D = x.shape
    halo_blocks = tm // BF16_ROWS
    final_norm = final_gain is not None
    fg = (final_gain if final_norm else gain).reshape(1, D)
    conv_b = conv_b.reshape(1, -1)
    return pl.pallas_call(
        functools.partial(_ffn_kernel, tm=tm, tiles_per_seq=seq_len // tm, final_norm=final_norm),
        grid=(T // tm,),
        in_specs=[pl.BlockSpec((tm, D), lambda i: (i, 0)),
                  pl.BlockSpec((BF16_ROWS, D), lambda i: (jnp.maximum(i * halo_blocks - 1, 0), 0)),
                  _resident((1, D)), _resident(w_up.shape), _resident(conv_w.shape), _resident(conv_b.shape),
                  _resident(w_down.shape), _resident((1, D))],
        out_specs=pl.BlockSpec((tm, D), lambda i: (i, 0)),
        out_shape=jax.ShapeDtypeStruct((T, D), F32),
        scratch_shapes=[pltpu.VMEM((tm + BF16_ROWS, D), BF16),
                        pltpu.VMEM((2, 2, FFN_CHUNK // LANES, tm + BF16_ROWS, LANES), F32),
                        pltpu.VMEM((tm, D_FF), BF16),
                        pltpu.VMEM((D // LANES, tm, LANES), F32)],
        compiler_params=_params("parallel"),
        name="conv_ffn",
    )(x, x, gain.reshape(1, D), w_up, conv_w, conv_b, w_down, fg)


def _pad_gate_cols(w):
    return jnp.pad(w, ((0, 0), (0, LANES - w.shape[1]))).astype(BF16)


def _fox_bias_operands(f):
    B, S, H = f.shape
    to_bf16_grid = functools.partial(lax.reduce_precision, exponent_bits=8, mantissa_bits=7)
    hi = to_bf16_grid(f)
    mid = to_bf16_grid(f - hi)
    lo = f - hi - mid
    pieces = jnp.stack([hi, mid, lo], axis=-1).astype(BF16).reshape(B, S, F_PIECES * H)
    ones = jnp.ones_like(pieces)
    gap = jnp.zeros((B, S, FOX_BIAS_Q_LANE - F_PIECES * H), BF16)
    tail = jnp.zeros((B, S, LANES - FOX_BIAS_Q_LANE - F_PIECES * H), BF16)
    fq = jnp.concatenate([ones, gap, pieces, tail], axis=-1)
    fk = jnp.concatenate([-pieces, gap, ones, tail], axis=-1)
    return fq, fk


def _fox_diff_mixer(x, B, S, gain, w_in, fox_bf, lq1, lk1, lq2, lk2, subln, w_out, lambda_init):
    fw = FOX_HEADS * FOX_DIM
    g0 = 3 * fw
    dq0 = g0 + FOX_HEADS
    dqw = DIFF_HEADS * 2 * DIFF_QK
    w_main = jnp.concatenate([w_in[:, :fw] * (FOX_DIM ** -0.5 * LOG2_E), w_in[:, fw:g0],
                              w_in[:, dq0:dq0 + dqw] * (DIFF_QK ** -0.5 * LOG2_E), w_in[:, dq0 + dqw:]],
                             axis=1).astype(BF16)
    proj, gates = norm_matmul(x, gain, w_main, _pad_gate_cols(w_in[:, g0:g0 + FOX_HEADS]))
    log_f = jax.nn.log_sigmoid(gates[:, :FOX_HEADS] + fox_bf).reshape(B, S, FOX_HEADS)
    fq, fk = _fox_bias_operands(jnp.cumsum(log_f, axis=1) * LOG2_E)
    proj = proj.reshape(B, S, -1)
    fox = fox_attention(proj, fq, fk)
    dif = diff_attention(proj, lq1, lk1, lq2, lk2, subln, lambda_init)
    w_out = w_out.astype(BF16)
    return [fox, dif], [w_out[:fw], w_out[fw:]]


def _mlstm_mixer(x, B, S, gain, w_in, conv_qk, b_i, b_f, head_norm, w_out):
    g0 = 2 * ML_HEADS * ML_QK + ML_HEADS * ML_V
    w_main = jnp.concatenate([w_in[:, :g0], w_in[:, g0 + 2 * ML_HEADS:]], axis=1).astype(BF16)
    proj, gates = norm_matmul(x, gain, w_main, _pad_gate_cols(w_in[:, g0:g0 + 2 * ML_HEADS]))
    i_pre = (gates[:, :ML_HEADS] + b_i).reshape(B, S, ML_HEADS)
    log_f = jax.nn.log_sigmoid(gates[:, ML_HEADS:2 * ML_HEADS] + b_f)
    b = jnp.cumsum(log_f.reshape(B, S // ML_CHUNK, ML_CHUNK, ML_HEADS), axis=2).reshape(B, S, ML_HEADS)
    g_col = jnp.concatenate([i_pre, b], axis=-1)
    h = mlstm(proj.reshape(B, S, -1), g_col, g_col.transpose(0, 2, 1), conv_qk, head_norm)
    return [h], [w_out.astype(BF16)]


def kernel(x, mem, mix_norm, xattn_norm, mem_norm, ffn_norm, attn_w_in, attn_fox_bf, diff_lq1, diff_lk1, diff_lq2, diff_lk2, diff_subln, attn_w_out, mlstm_w_in, mlstm_conv_qk, mlstm_b_i, mlstm_b_f, mlstm_head_norm, mlstm_w_out, xattn_wq, xattn_wkv, xattn_wo, ffn_w_up, ffn_conv_w, ffn_conv_b, ffn_w_down, final_norm):
    B, S, D = x.shape
    M = mem.shape[1]
    depth = mix_norm.shape[0]
    x = x.reshape(B * S, D)
    mem2 = mem.reshape(B * M, D)
    for layer in range(depth):
        j = layer // 2
        if layer % 2 == 0:
            lambda_init = 0.8 - 0.6 * math.exp(-0.3 * layer)
            mixed, w_mix = _fox_diff_mixer(x, B, S, mix_norm[layer], attn_w_in[j], attn_fox_bf[j], diff_lq1[j],
                                           diff_lk1[j], diff_lq2[j], diff_lk2[j], diff_subln[j], attn_w_out[j],
                                           lambda_init)
        else:
            mixed, w_mix = _mlstm_mixer(x, B, S, mix_norm[layer], mlstm_w_in[j], mlstm_conv_qk[j], mlstm_b_i[j],
                                        mlstm_b_f[j], mlstm_head_norm[j], mlstm_w_out[j])
        kv = norm_matmul(mem2, mem_norm[layer], xattn_wkv[layer].astype(BF16)).reshape(B, M, 2 * D)
        x = mix_out_cross_attention(mixed, w_mix, x.reshape(B, S, D), xattn_norm[layer],
                                    xattn_wq[layer].astype(BF16), kv, xattn_wo[layer].astype(BF16)).reshape(B * S, D)
        x = conv_ffn(x, ffn_norm[layer], ffn_w_up[layer].astype(BF16), ffn_conv_w[layer], ffn_conv_b[layer],
                     ffn_w_down[layer].astype(BF16), S, final_norm if layer == depth - 1 else None)
    return x.reshape(B, S, D)
```

```python
import functools
import math

import jax
import jax.numpy as jnp
from jax import lax
from jax.experimental import pallas as pl
from jax.experimental.pallas import tpu as pltpu

F32 = jnp.float32
BF16 = jnp.bfloat16

D_MODEL = 1024
RMS_EPS = 1e-6
NEG_INF = -1e30
CHUNK = 64
FOX_HEADS, FOX_DIM = 8, 64
DIFF_HEADS, DIFF_QK, DIFF_V = 4, 64, 128
ML_HEADS, ML_QK, ML_V, ML_CONV = 4, 128, 256, 4
X_HEADS, X_DIM = 4, 256
D_FF = 2816
FFN_CONV = 3
LANES = 128
BF16_ROWS = 16
VMEM_LIMIT = 56 * 1024 * 1024

ROW_TILE = 512
ATT_Q_TILE = 512
ATT_K_TILE = 256
LOG2_E = math.log2(math.e)
ML_CHUNK = 256
ML_SEQS = 2
X_TILE = 512
PROJ_COLS = 1024
FFN_CHUNK = 256
ROW_PHASES = 4
DOWN_CHUNKS = 4
TRANSPOSE_COLS = 512
F_PIECES = 3
FOX_BIAS_K_LANE = 0
FOX_BIAS_Q_LANE = 32


def _params(*sem):
    return pltpu.CompilerParams(dimension_semantics=sem, vmem_limit_bytes=VMEM_LIMIT)


def _resident(shape):
    return pl.BlockSpec(shape, lambda *_: (0,) * len(shape), pipeline_mode=pl.Buffered(1))


def _rms(x, gain):
    return x * lax.rsqrt(jnp.mean(x * x, axis=-1, keepdims=True) + RMS_EPS) * gain


def _sigmoid(x):
    return 1.0 / (1.0 + jnp.exp(-x))


def _dot(a, b):
    return jnp.dot(a, b, preferred_element_type=F32)


def _dot_nt(a, b):
    return lax.dot_general(a, b, (((1,), (1,)), ((), ())), preferred_element_type=F32)


def _split3(x):
    hi = x.astype(BF16)
    r1 = x - hi.astype(F32)
    mid = r1.astype(BF16)
    return hi, mid, (r1 - mid.astype(F32)).astype(BF16)


def _select_dot(sel, x):
    hi, mid, lo = _split3(x)
    return _dot(sel, hi) + _dot(sel, mid) + _dot(sel, lo)


def _block_tri(n, block):
    r = lax.broadcasted_iota(jnp.int32, (n, n), 0)
    c = lax.broadcasted_iota(jnp.int32, (n, n), 1)
    return jnp.where((c <= r) & (r // block == c // block), 1.0, 0.0).astype(BF16)


def _log_sigmoid(x):
    return jnp.minimum(x, 0.0) - jnp.log1p(jnp.exp(-jnp.abs(x)))


def _norm_matmul_kernel(x_ref, g_ref, w_ref, o_ref, *, tn):
    xn = _rms(x_ref[...], g_ref[...]).astype(BF16)
    for c0 in range(0, o_ref.shape[1], tn):
        o_ref[:, c0:c0 + tn] = _dot(xn, w_ref[:, c0:c0 + tn]).astype(o_ref.dtype)


def norm_matmul(x, gain, w, *, tm=ROW_TILE, tn=PROJ_COLS):
    T, D = x.shape
    N = w.shape[1]
    return pl.pallas_call(
        functools.partial(_norm_matmul_kernel, tn=tn),
        grid=(T // tm,),
        in_specs=[pl.BlockSpec((tm, D), lambda i: (i, 0)), _resident((1, D)), _resident(w.shape)],
        out_specs=pl.BlockSpec((tm, N), lambda i: (i, 0)),
        out_shape=jax.ShapeDtypeStruct((T, N), BF16),
        compiler_params=_params("parallel"),
        name="norm_matmul",
    )(x, gain.reshape(1, D), w)


def _fox_proj_kernel(x_ref, g_ref, w_ref, wg_ref, bf_ref, o_ref, fq_ref, fk_ref, carry_ref, *, tm, tn, tiles_per_seq):
    i = pl.program_id(0)

    @pl.when(i % tiles_per_seq == 0)
    def _():
        carry_ref[...] = jnp.zeros_like(carry_ref)

    xn = _rms(x_ref[...], g_ref[...]).astype(BF16)

    def project(part):
        width = o_ref.shape[1] // 3
        for c0 in range(part * width, (part + 1) * width, tn):
            o_ref[:, c0:c0 + tn] = _dot(xn, w_ref[:, c0:c0 + tn]).astype(o_ref.dtype)

    gates = _dot(xn, wg_ref[...])
    project(0)
    log_f = _log_sigmoid(gates + bf_ref[...]) * LOG2_E
    f = carry_ref[...] + _select_dot(_block_tri(tm, tm), log_f)
    carry_ref[...] = f[tm - 1:tm, :]
    project(1)

    src = lax.broadcasted_iota(jnp.int32, (LANES, LANES), 0)
    dst = lax.broadcasted_iota(jnp.int32, (LANES, LANES), 1)
    lane = lax.broadcasted_iota(jnp.int32, (1, LANES), 1)
    n_bias = F_PIECES * FOX_HEADS
    fq = jnp.where((lane >= FOX_BIAS_K_LANE) & (lane < FOX_BIAS_K_LANE + n_bias), 1.0, 0.0)
    fk = jnp.where((lane >= FOX_BIAS_Q_LANE) & (lane < FOX_BIAS_Q_LANE + n_bias), 1.0, 0.0)
    for c, piece in enumerate(_split3(f)):
        head = src < FOX_HEADS
        to_q = jnp.where(head & (dst == FOX_BIAS_Q_LANE + F_PIECES * src + c), 1.0, 0.0).astype(BF16)
        to_k = jnp.where(head & (dst == FOX_BIAS_K_LANE + F_PIECES * src + c), -1.0, 0.0).astype(BF16)
        fq = fq + _dot(piece, to_q)
        fk = fk + _dot(piece, to_k)
    fq_ref[...] = fq.astype(BF16)
    fk_ref[...] = fk.astype(BF16)
    project(2)


def fox_projection(x, gain, w, w_gates, gate_bias, seq_len, *, tm=ROW_TILE, tn=PROJ_COLS):
    T, D = x.shape
    N = w.shape[1]
    row_block = lambda n: pl.BlockSpec((tm, n), lambda i: (i, 0))
    return pl.pallas_call(
        functools.partial(_fox_proj_kernel, tm=tm, tn=tn, tiles_per_seq=seq_len // tm),
        grid=(T // tm,),
        in_specs=[row_block(D), _resident((1, D)), _resident(w.shape), _resident(w_gates.shape),
                  _resident((1, LANES))],
        out_specs=[row_block(N), row_block(LANES), row_block(LANES)],
        out_shape=[jax.ShapeDtypeStruct((T, N), BF16), jax.ShapeDtypeStruct((T, LANES), BF16),
                   jax.ShapeDtypeStruct((T, LANES), BF16)],
        scratch_shapes=[pltpu.VMEM((1, LANES), F32)],
        compiler_params=_params("arbitrary"),
        name="fox_projection",
    )(x, gain.reshape(1, D), w, w_gates, gate_bias)


def _mlstm_proj_kernel(x_ref, halo_ref, g_ref, w_ref, wg_ref, gb_ref, cw_ref, o_ref, gc_ref, gr_ref, xn_ref, up_ref,
                       nat_ref, *, tm, tiles_per_seq):
    i = pl.program_id(0)
    halo = _rms(halo_ref[...], g_ref[...])
    xn_ref[0:BF16_ROWS, :] = jnp.where(i % tiles_per_seq == 0, 0.0, halo).astype(BF16)
    xn_ref[BF16_ROWS:, :] = _rms(x_ref[...], g_ref[...]).astype(BF16)
    xe = xn_ref[...]
    xn = xe[BF16_ROWS:]

    qk_slabs = 2 * ML_HEADS * ML_QK // LANES
    rows = tm // ROW_PHASES
    v0 = qk_slabs * LANES
    og0 = v0 + ML_HEADS * ML_V
    for s0 in range(0, qk_slabs, 2):
        up = _dot(xe, w_ref[:, s0 * LANES:(s0 + 2) * LANES])
        up_ref[s0] = up[:, 0:LANES]
        up_ref[s0 + 1] = up[:, LANES:2 * LANES]
    pre = _dot(xn, wg_ref[...]) + gb_ref[...]
    o_ref[:, v0:og0] = _dot(xn, w_ref[:, v0:og0]).astype(o_ref.dtype)
    for s in range(qk_slabs):
        cw = cw_ref[:, s * LANES:(s + 1) * LANES]
        for phase in range(ROW_PHASES):
            y = None
            for tap in range(ML_CONV):
                first = BF16_ROWS + phase - (ML_CONV - 1 - tap)
                term = cw[tap:tap + 1] * up_ref[s, pl.ds(first, rows, stride=ROW_PHASES), :]
                y = term if y is None else y + term
            y = y * _sigmoid(y)
            if s < qk_slabs // 2:
                y = y * (ML_QK ** -0.5)
            nat_ref[s, pl.ds(phase, rows, stride=ROW_PHASES), :] = y
        o_ref[:, s * LANES:(s + 1) * LANES] = nat_ref[s].astype(o_ref.dtype)

    og = _dot(xn, w_ref[:, og0:])
    lane = lax.broadcasted_iota(jnp.int32, (1, LANES), 1)
    val = jnp.where(lane < ML_HEADS, pre, _log_sigmoid(pre)) * LOG2_E
    gc = jnp.where(lane < ML_HEADS, val, _select_dot(_block_tri(tm, ML_CHUNK), val))
    gc_ref[...] = gc
    r = lax.broadcasted_iota(jnp.int32, (2 * ML_HEADS, LANES), 0)
    c = lax.broadcasted_iota(jnp.int32, (2 * ML_HEADS, LANES), 1)
    pick = jnp.where(r == c, 1.0, 0.0).astype(BF16)
    hi, mid, lo = _split3(gc)
    gr_ref[0] = _dot_nt(pick, hi) + _dot_nt(pick, mid) + _dot_nt(pick, lo)
    o_ref[:, og0:] = _sigmoid(og).astype(o_ref.dtype)


def mlstm_projection(x, gain, w, w_gates, gate_bias, conv_w, seq_len, *, tm=ROW_TILE):
    T, D = x.shape
    N = w.shape[1]
    tiles_per_seq = seq_len // tm
    halo_blocks = tm // BF16_ROWS
    row_block = lambda n: pl.BlockSpec((tm, n), lambda i: (i, 0))
    return pl.pallas_call(
        functools.partial(_mlstm_proj_kernel, tm=tm, tiles_per_seq=tiles_per_seq),
        grid=(T // tm,),
        in_specs=[row_block(D),
                  pl.BlockSpec((BF16_ROWS, D), lambda i: (jnp.maximum(i * halo_blocks - 1, 0), 0)),
                  _resident((1, D)), _resident(w.shape), _resident(w_gates.shape), _resident((1, LANES)),
                  _resident(conv_w.shape)],
        out_specs=[row_block(N), row_block(LANES),
                   pl.BlockSpec((1, 2 * ML_HEADS, tm), lambda i: (i // tiles_per_seq, 0, i % tiles_per_seq))],
        out_shape=[jax.ShapeDtypeStruct((T, N), BF16), jax.ShapeDtypeStruct((T, LANES), F32),
                   jax.ShapeDtypeStruct((T // seq_len, 2 * ML_HEADS, seq_len), F32)],
        scratch_shapes=[pltpu.VMEM((tm + BF16_ROWS, D), BF16),
                        pltpu.VMEM((2 * ML_HEADS * ML_QK // LANES, tm + BF16_ROWS, LANES), F32),
                        pltpu.VMEM((2 * ML_HEADS * ML_QK // LANES, tm, LANES), F32)],
        compiler_params=_params("parallel"),
        name="mlstm_projection",
    )(x, x, gain.reshape(1, D), w, w_gates, gate_bias, conv_w)


def _build_vt(v_ref, vt_ref, n_heads, rows):
    S = v_ref.shape[1]
    r = lax.broadcasted_iota(jnp.int32, (LANES, LANES), 0)
    c = lax.broadcasted_iota(jnp.int32, (LANES, LANES), 1)
    eye = jnp.where(r == c, 1.0, 0.0).astype(BF16)
    per_group = LANES // rows
    for g in range(n_heads // per_group):
        for c0 in range(0, S, TRANSPOSE_COLS):
            cs = slice(c0, c0 + TRANSPOSE_COLS)
            vt = _dot_nt(eye, v_ref[0, cs, g * LANES:(g + 1) * LANES]).astype(BF16)
            for k in range(per_group):
                vt_ref[g * per_group + k, 0:rows, cs] = vt[k * rows:(k + 1) * rows]
    for h in range(n_heads):
        vt_ref[h, rows:rows + BF16_ROWS, :] = jnp.ones((BF16_ROWS, S), BF16)


def _softmax_step(s_t, vt, m_ref, acc_ref, idx):
    m_prev = m_ref[idx]
    m_new = jnp.maximum(m_prev, jnp.max(s_t, axis=0, keepdims=True))
    p = jnp.exp2(s_t - m_new).astype(BF16)
    acc_ref[idx] = jnp.exp2(m_prev - m_new) * acc_ref[idx] + _dot(vt, p)
    m_ref[idx] = m_new


def _causal_steps(i, logits, attend, n_streams, buf_a, buf_b):
    def phase(j, src, dst, mask=None):
        for n in range(n_streams):
            logits(j + 1, dst, n)
            attend(j, src, n, mask)

    for n in range(n_streams):
        logits(0, buf_a, n)

    def body(jj, carry):
        phase(2 * jj, buf_a, buf_b)
        phase(2 * jj + 1, buf_b, buf_a)
        return carry

    lax.fori_loop(0, i, body, 0)
    phase(2 * i, buf_a, buf_b, 0)
    for n in range(n_streams):
        attend(2 * i + 1, buf_b, n, 1)


def _fox_kernel(q_ref, k_ref, v_ref, fq_ref, fk_ref, o_ref, vt_ref, qc_ref, acc_ref, m_ref, sa_ref, sb_ref, *, tq, tk):
    i = pl.program_id(1)

    @pl.when(i == 0)
    def _():
        _build_vt(v_ref, vt_ref, FOX_HEADS, FOX_DIM)

    lane = lax.broadcasted_iota(jnp.int32, (1, LANES), 1)
    fq = fq_ref[0]
    for h in range(FOX_HEADS):
        pair, half = divmod(h, 2)
        q = q_ref[0, :, pair * LANES:(pair + 1) * LANES]
        in_head = (lane >= half * FOX_DIM) & (lane < (half + 1) * FOX_DIM)
        lo_k, lo_q = FOX_BIAS_K_LANE + F_PIECES * h, FOX_BIAS_Q_LANE + F_PIECES * h
        mine = ((lane >= lo_k) & (lane < lo_k + F_PIECES)) | ((lane >= lo_q) & (lane < lo_q + F_PIECES))
        qc_ref[h, :, 0:LANES] = jnp.where(in_head, q, jnp.zeros_like(q))
        qc_ref[h, :, LANES:2 * LANES] = jnp.where(mine, fq, jnp.zeros_like(fq))
    m_ref[...] = jnp.full(m_ref.shape, NEG_INF, F32)
    acc_ref[...] = jnp.zeros(acc_ref.shape, F32)

    key = lax.broadcasted_iota(jnp.int32, (tk, tq), 0)
    qry = lax.broadcasted_iota(jnp.int32, (tk, tq), 1)
    causal = [key + half * tk <= qry for half in range(tq // tk)]

    def key_rows(j):
        return pl.ds(pl.multiple_of(j * tk, tk), tk)

    def logits(j, buf, h):
        pair = h // 2
        kc = jnp.concatenate([k_ref[0, key_rows(j), pair * LANES:(pair + 1) * LANES], fk_ref[0, key_rows(j), :]],
                             axis=1)
        buf[h] = _dot_nt(kc, qc_ref[h])

    def attend(j, buf, h, mask):
        s_t = buf[h]
        if mask is not None:
            s_t = jnp.where(causal[mask], s_t, NEG_INF)
        _softmax_step(s_t, vt_ref[h, :, key_rows(j)], m_ref, acc_ref, h)

    _causal_steps(i, logits, attend, FOX_HEADS, sa_ref, sb_ref)

    for pair in range(FOX_HEADS // 2):
        halves = []
        for half in range(2):
            a = acc_ref[2 * pair + half]
            halves.append(a[0:FOX_DIM] / a[FOX_DIM:FOX_DIM + 1])
        o_ref[0, :, pair * LANES:(pair + 1) * LANES] = jnp.concatenate(halves, axis=0).T.astype(o_ref.dtype)


def fox_attention(proj, fq, fk, *, tq=ATT_Q_TILE, tk=ATT_K_TILE):
    B, S, _ = proj.shape
    width = FOX_HEADS * FOX_DIM
    rows = FOX_DIM + BF16_ROWS
    return pl.pallas_call(
        functools.partial(_fox_kernel, tq=tq, tk=tk),
        grid=(B, S // tq),
        in_specs=[pl.BlockSpec((1, tq, width), lambda b, i: (b, i, 0)),
                  pl.BlockSpec((1, S, width), lambda b, i: (b, 0, 1)),
                  pl.BlockSpec((1, S, width), lambda b, i: (b, 0, 2)),
                  pl.BlockSpec((1, tq, LANES), lambda b, i: (b, i, 0)),
                  pl.BlockSpec((1, S, LANES), lambda b, i: (b, 0, 0))],
        out_specs=pl.BlockSpec((1, tq, width), lambda b, i: (b, i, 0)),
        out_shape=jax.ShapeDtypeStruct((B, S, width), BF16),
        scratch_shapes=[pltpu.VMEM((FOX_HEADS, rows, S), BF16),
                        pltpu.VMEM((FOX_HEADS, tq, 2 * LANES), BF16),
                        pltpu.VMEM((FOX_HEADS, rows, tq), F32),
                        pltpu.VMEM((FOX_HEADS, 1, tq), F32),
                        pltpu.VMEM((FOX_HEADS, tk, tq), F32),
                        pltpu.VMEM((FOX_HEADS, tk, tq), F32)],
        compiler_params=_params("parallel", "arbitrary"),
        name="fox_attention",
    )(proj, proj, proj, fq, fk)


def _diff_kernel(q_ref, k_ref, v_ref, lq1_ref, lk1_ref, lq2_ref, lk2_ref, sub_ref, o_ref, vt_ref, qc_ref, acc_ref,
                 m_ref, sa_ref, sb_ref, *, tq, tk, lambda_init):
    i = pl.program_id(1)

    @pl.when(i == 0)
    def _():
        _build_vt(v_ref, vt_ref, DIFF_HEADS, DIFF_V)

    lane = lax.broadcasted_iota(jnp.int32, (1, LANES), 1)
    for h in range(DIFF_HEADS):
        q = q_ref[0, :, h * LANES:(h + 1) * LANES]
        zero = jnp.zeros_like(q)
        qc_ref[2 * h] = jnp.where(lane < DIFF_QK, q, zero)
        qc_ref[2 * h + 1] = jnp.where(lane >= DIFF_QK, q, zero)
    m_ref[...] = jnp.full(m_ref.shape, NEG_INF, F32)
    acc_ref[...] = jnp.zeros(acc_ref.shape, F32)

    key = lax.broadcasted_iota(jnp.int32, (tk, tq), 0)
    qry = lax.broadcasted_iota(jnp.int32, (tk, tq), 1)
    visible = [(key + half * tk) // CHUNK <= qry // CHUNK for half in range(tq // tk)]

    def key_rows(j):
        return pl.ds(pl.multiple_of(j * tk, tk), tk)

    def logits(j, buf, n):
        h = n // 2
        buf[n] = _dot_nt(k_ref[0, key_rows(j), h * LANES:(h + 1) * LANES], qc_ref[n])

    def attend(j, buf, n, mask):
        s_t = buf[n]
        if mask is not None:
            s_t = jnp.where(visible[mask], s_t, NEG_INF)
        _softmax_step(s_t, vt_ref[n // 2, :, key_rows(j)], m_ref, acc_ref, n)

    _causal_steps(i, logits, attend, 2 * DIFF_HEADS, sa_ref, sb_ref)

    lam = (jnp.exp(jnp.sum(lq1_ref[...] * lk1_ref[...], axis=1, keepdims=True))
           - jnp.exp(jnp.sum(lq2_ref[...] * lk2_ref[...], axis=1, keepdims=True)) + lambda_init)
    for h in range(DIFF_HEADS):
        a1, a2 = acc_ref[2 * h], acc_ref[2 * h + 1]
        o_t = a1[0:DIFF_V] / a1[DIFF_V:DIFF_V + 1] - lam * (a2[0:DIFF_V] / a2[DIFF_V:DIFF_V + 1])
        out = _rms(o_t.T, sub_ref[...]) * (1.0 - lambda_init)
        o_ref[0, :, h * LANES:(h + 1) * LANES] = out.astype(o_ref.dtype)


def diff_attention(proj, lq1, lk1, lq2, lk2, subln, lambda_init, *, tq=ATT_Q_TILE, tk=ATT_K_TILE):
    B, S, _ = proj.shape
    width = DIFF_HEADS * DIFF_V
    rows = DIFF_V + BF16_ROWS
    small = lambda n: pl.BlockSpec((1, n), lambda b, i: (0, 0))
    return pl.pallas_call(
        functools.partial(_diff_kernel, tq=tq, tk=tk, lambda_init=lambda_init),
        grid=(B, S // tq),
        in_specs=[pl.BlockSpec((1, tq, width), lambda b, i: (b, i, 3)),
                  pl.BlockSpec((1, S, width), lambda b, i: (b, 0, 4)),
                  pl.BlockSpec((1, S, width), lambda b, i: (b, 0, 5)),
                  small(DIFF_QK), small(DIFF_QK), small(DIFF_QK), small(DIFF_QK), small(DIFF_V)],
        out_specs=pl.BlockSpec((1, tq, width), lambda b, i: (b, i, 0)),
        out_shape=jax.ShapeDtypeStruct((B, S, width), BF16),
        scratch_shapes=[pltpu.VMEM((DIFF_HEADS, rows, S), BF16),
                        pltpu.VMEM((2 * DIFF_HEADS, tq, LANES), BF16),
                        pltpu.VMEM((2 * DIFF_HEADS, rows, tq), F32),
                        pltpu.VMEM((2 * DIFF_HEADS, 1, tq), F32),
                        pltpu.VMEM((2 * DIFF_HEADS, tk, tq), F32),
                        pltpu.VMEM((2 * DIFF_HEADS, tk, tq), F32)],
        compiler_params=_params("parallel", "arbitrary"),
        name="diff_attention",
    )(proj, proj, proj, lq1.reshape(1, -1), lk1.reshape(1, -1), lq2.reshape(1, -1), lk2.reshape(1, -1),
      subln.reshape(1, -1))


def _mlstm_kernel(qk_ref, v_ref, sg_ref, gc_ref, gr_ref, hn_ref, o_ref, ct_ref, m_ref, vt_ref, *, L, nb):
    c = pl.program_id(1)

    @pl.when(c == 0)
    def _():
        ct_ref[...] = jnp.zeros_like(ct_ref)
        m_ref[...] = jnp.zeros_like(m_ref)
        for n in range(nb * ML_HEADS):
            vt_ref[n, ML_V:ML_V + BF16_ROWS, :] = jnp.ones((BF16_ROWS, L), BF16)

    src = lax.broadcasted_iota(jnp.int32, (L, L), 0)
    dst = lax.broadcasted_iota(jnp.int32, (L, L), 1)
    causal = src <= dst
    eye = jnp.where(src == dst, 1.0, 0.0).astype(BF16)
    k0 = ML_HEADS * ML_QK
    for b in range(nb):
        gc = gc_ref[b]
        gr = gr_ref[b]
        for h in range(ML_HEADS):
            n = b * ML_HEADS + h
            q = qk_ref[b, :, h * ML_QK:(h + 1) * ML_QK]
            k = qk_ref[b, :, k0 + h * ML_QK:k0 + (h + 1) * ML_QK]
            vt_ref[n, 0:ML_V, :] = _dot_nt(eye, v_ref[b, :, h * ML_V:(h + 1) * ML_V]).astype(BF16)
            v_t = vt_ref[n]
            r_col = gc[:, h:h + 1] - gc[:, ML_HEADS + h:ML_HEADS + h + 1]
            b_row = gr[ML_HEADS + h:ML_HEADS + h + 1, :]
            r_row = gr[h:h + 1, :] - b_row
            g = b_row[:, L - 1:L]
            ct = ct_ref[n]
            m = m_ref[n][:, 0:1]

            dm = jnp.where(causal, r_col, NEG_INF)
            mt = jnp.maximum(m, jnp.max(dm, axis=0, keepdims=True))
            s_t = _dot_nt(k, q) * jnp.exp2(dm - mt)
            both = jnp.exp2(m - mt) * _dot_nt(ct.astype(BF16), q) + _dot(v_t, s_t.astype(BF16))
            den = both[ML_V:ML_V + 1]
            hh = both[0:ML_V] / jnp.maximum(jnp.abs(den), jnp.exp2(-(b_row + mt)))
            hh = hh * lax.rsqrt(jnp.mean(hh * hh, axis=0, keepdims=True) + RMS_EPS)

            m_next = jnp.maximum(m, jnp.max(r_row, axis=1, keepdims=True))
            kw = (k.astype(F32) * jnp.exp2(r_col - m_next)).astype(BF16)
            ct_ref[n] = jnp.exp2(m - m_next) * ct + _dot(v_t, kw)
            m_ref[n] = jnp.broadcast_to(g + m_next, (1, LANES))

            vs = slice(h * ML_V, (h + 1) * ML_V)
            o_ref[b, :, vs] = (hh.T * hn_ref[:, vs] * sg_ref[b, :, vs].astype(F32)).astype(o_ref.dtype)


def mlstm(proj, g_col, g_row, head_norm, *, L=ML_CHUNK, nb=ML_SEQS):
    B, S, _ = proj.shape
    W = D_MODEL
    assert L == ML_V, "one identity matrix serves the v transposes"
    return pl.pallas_call(
        functools.partial(_mlstm_kernel, L=L, nb=nb),
        grid=(B // nb, S // L),
        in_specs=[pl.BlockSpec((nb, L, W), lambda b, c: (b, c, 0)),
                  pl.BlockSpec((nb, L, W), lambda b, c: (b, c, 1)),
                  pl.BlockSpec((nb, L, W), lambda b, c: (b, c, 2)),
                  pl.BlockSpec((nb, L, LANES), lambda b, c: (b, c, 0)),
                  pl.BlockSpec((nb, 2 * ML_HEADS, L), lambda b, c: (b, 0, c)),
                  pl.BlockSpec((1, W), lambda b, c: (0, 0))],
        out_specs=pl.BlockSpec((nb, L, W), lambda b, c: (b, c, 0)),
        out_shape=jax.ShapeDtypeStruct((B, S, W), BF16),
        scratch_shapes=[pltpu.VMEM((nb * ML_HEADS, ML_V + BF16_ROWS, ML_QK), F32),
                        pltpu.VMEM((nb * ML_HEADS, 1, LANES), F32),
                        pltpu.VMEM((nb * ML_HEADS, ML_V + BF16_ROWS, L), BF16)],
        compiler_params=_params("parallel", "arbitrary"),
        name="mlstm",
    )(proj, proj, proj, g_col, g_row, head_norm.reshape(1, W))


def _xattn_kernel(*refs, n_in):
    x_ref, g_ref, wq_ref, k_ref, v_ref, wo_ref, o_ref = refs[2 * n_in:]
    x = x_ref[0]
    for a_ref, w_ref in zip(refs[:n_in], refs[n_in:2 * n_in]):
        x = x + _dot(a_ref[0], w_ref[...])
    xn = _rms(x, g_ref[...]).astype(BF16)
    q = (_dot(xn, wq_ref[...]) * (X_DIM ** -0.5)).astype(BF16)
    heads = []
    for h in range(X_HEADS):
        cols = slice(h * X_DIM, (h + 1) * X_DIM)
        s = _dot_nt(q[:, cols], k_ref[0, :, cols])
        p = jnp.exp(s - jnp.max(s, axis=1, keepdims=True))
        l = jnp.sum(p, axis=1, keepdims=True)
        heads.append((_dot(p.astype(BF16), v_ref[0, :, cols]) / l).astype(BF16))
    o_ref[0] = x + _dot(jnp.concatenate(heads, axis=1), wo_ref[...])


def mix_out_cross_attention(acts, w_mix, x, gain, wq, kv, wo, *, tq=X_TILE):
    B, S, D = x.shape
    M = kv.shape[1]
    return pl.pallas_call(
        functools.partial(_xattn_kernel, n_in=len(acts)),
        grid=(B, S // tq),
        in_specs=([pl.BlockSpec((1, tq, a.shape[2]), lambda b, i: (b, i, 0)) for a in acts]
                  + [_resident(w.shape) for w in w_mix]
                  + [pl.BlockSpec((1, tq, D), lambda b, i: (b, i, 0)),
                     _resident((1, D)), _resident((D, D)),
                     pl.BlockSpec((1, M, D), lambda b, i: (b, 0, 0)),
                     pl.BlockSpec((1, M, D), lambda b, i: (b, 0, 1)),
                     _resident((D, D))]),
        out_specs=pl.BlockSpec((1, tq, D), lambda b, i: (b, i, 0)),
        out_shape=jax.ShapeDtypeStruct((B, S, D), F32),
        compiler_params=_params("parallel", "parallel"),
        name="mix_out_cross_attention",
    )(*acts, *w_mix, x, gain.reshape(1, D), wq, kv, kv, wo)


def _gelu_tanh(x):
    k = -2.0 * math.sqrt(2.0 / math.pi) * math.log2(math.e)
    return x / (1.0 + jnp.exp2(x * (k * 0.044715 * (x * x) + k)))


def _ffn_kernel(x_ref, halo_ref, g_ref, wup_ref, cw_ref, cb_ref, wd_ref, fg_ref, o_ref, xn_ref, up_ref, act_ref,
                nat_ref, *, tm, tiles_per_seq, final_norm):
    i = pl.program_id(0)
    halo = _rms(halo_ref[...], g_ref[...])
    xn_ref[0:BF16_ROWS, :] = jnp.where(i % tiles_per_seq == 0, 0.0, halo).astype(BF16)
    xn_ref[BF16_ROWS:, :] = _rms(x_ref[...], g_ref[...]).astype(BF16)
    xe = xn_ref[...]

    n_chunks = D_FF // FFN_CHUNK
    slabs = FFN_CHUNK // LANES
    rows = tm // ROW_PHASES

    def up_project(c):
        for half in range(2):
            col0 = half * D_FF + c * FFN_CHUNK
            up = _dot(xe, wup_ref[:, col0:col0 + FFN_CHUNK])
            for s in range(slabs):
                up_ref[c % 2, half, s] = up[:, s * LANES:(s + 1) * LANES]

    def conv(c, half, s, phase):
        col0 = half * D_FF + c * FFN_CHUNK + s * LANES
        cw = cw_ref[:, col0:col0 + LANES]
        out = cb_ref[:, col0:col0 + LANES]
        for tap in range(FFN_CONV):
            first = BF16_ROWS + phase - (FFN_CONV - 1 - tap)
            out = out + cw[tap:tap + 1] * up_ref[c % 2, half, s, pl.ds(first, rows, stride=ROW_PHASES), :]
        return out

    acc = None
    piece_start = 0
    up_project(0)
    for c in range(n_chunks):
        if c + 1 < n_chunks:
            up_project(c + 1)
        for phase in range(ROW_PHASES):
            for s in range(slabs):
                act = _gelu_tanh(conv(c, 0, s, phase)) * conv(c, 1, s, phase)
                act_ref[phase * rows:(phase + 1) * rows,
                        c * FFN_CHUNK + s * LANES:c * FFN_CHUNK + (s + 1) * LANES] = act.astype(BF16)
        if (c + 1) % DOWN_CHUNKS == 0 or c + 1 == n_chunks:
            piece = slice(piece_start * FFN_CHUNK, (c + 1) * FFN_CHUNK)
            part = _dot(act_ref[:, piece], wd_ref[piece, :])
            acc = part if acc is None else acc + part
            piece_start = c + 1

    for phase in range(ROW_PHASES):
        for s in range(D_MODEL // LANES):
            nat_ref[s, pl.ds(phase, rows, stride=ROW_PHASES), :] = acc[phase * rows:(phase + 1) * rows,
                                                                       s * LANES:(s + 1) * LANES]
    y = x_ref[...] + jnp.concatenate([nat_ref[s] for s in range(D_MODEL // LANES)], axis=1)
    o_ref[...] = _rms(y, fg_ref[...]) if final_norm else y


def conv_ffn(x, gain, w_up, conv_w, conv_b, w_down, seq_len, final_gain=None, *, tm=ROW_TILE):
    T, D = x.shape
    halo_blocks = tm // BF16_ROWS
    final_norm = final_gain is not None
    fg = (final_gain if final_norm else gain).reshape(1, D)
    conv_b = conv_b.reshape(1, -1)
    return pl.pallas_call(
        functools.partial(_ffn_kernel, tm=tm, tiles_per_seq=seq_len // tm, final_norm=final_norm),
        grid=(T // tm,),
        in_specs=[pl.BlockSpec((tm, D), lambda i: (i, 0)),
                  pl.BlockSpec((BF16_ROWS, D), lambda i: (jnp.maximum(i * halo_blocks - 1, 0), 0)),
                  _resident((1, D)), _resident(w_up.shape), _resident(conv_w.shape), _resident(conv_b.shape),
                  _resident(w_down.shape), _resident((1, D))],
        out_specs=pl.BlockSpec((tm, D), lambda i: (i, 0)),
        out_shape=jax.ShapeDtypeStruct((T, D), F32),
        scratch_shapes=[pltpu.VMEM((tm + BF16_ROWS, D), BF16),
                        pltpu.VMEM((2, 2, FFN_CHUNK // LANES, tm + BF16_ROWS, LANES), F32),
                        pltpu.VMEM((tm, D_FF), BF16),
                        pltpu.VMEM((D // LANES, tm, LANES), F32)],
        compiler_params=_params("parallel"),
        name="conv_ffn",
    )(x, x, gain.reshape(1, D), w_up, conv_w, conv_b, w_down, fg)


def _pad_lanes(a):
    return jnp.pad(a, ((0, 0), (0, LANES - a.shape[1])))


def _fox_diff_mixer(x, B, S, gain, w_in, fox_bf, lq1, lk1, lq2, lk2, subln, w_out, lambda_init):
    fw = FOX_HEADS * FOX_DIM
    g0 = 3 * fw
    dq0 = g0 + FOX_HEADS
    dqw = DIFF_HEADS * 2 * DIFF_QK
    w_main = jnp.concatenate([w_in[:, :fw] * (FOX_DIM ** -0.5 * LOG2_E), w_in[:, fw:g0],
                              w_in[:, dq0:dq0 + dqw] * (DIFF_QK ** -0.5 * LOG2_E), w_in[:, dq0 + dqw:]],
                             axis=1).astype(BF16)
    w_gates = _pad_lanes(w_in[:, g0:g0 + FOX_HEADS]).astype(BF16)
    proj, fq, fk = fox_projection(x, gain, w_main, w_gates, _pad_lanes(fox_bf.reshape(1, -1)), S)
    proj = proj.reshape(B, S, -1)
    fox = fox_attention(proj, fq.reshape(B, S, LANES), fk.reshape(B, S, LANES))
    dif = diff_attention(proj, lq1, lk1, lq2, lk2, subln, lambda_init)
    w_out = w_out.astype(BF16)
    return [fox, dif], [w_out[:fw], w_out[fw:]]


def _mlstm_mixer(x, B, S, gain, w_in, conv_qk, b_i, b_f, head_norm, w_out):
    g0 = 2 * ML_HEADS * ML_QK + ML_HEADS * ML_V
    w_main = jnp.concatenate([w_in[:, :g0], w_in[:, g0 + 2 * ML_HEADS:]], axis=1).astype(BF16)
    w_gates = _pad_lanes(w_in[:, g0:g0 + 2 * ML_HEADS]).astype(BF16)
    gate_bias = _pad_lanes(jnp.concatenate([b_i, b_f]).reshape(1, -1))
    proj, g_col, g_row = mlstm_projection(x, gain, w_main, w_gates, gate_bias, conv_qk, S)
    h = mlstm(proj.reshape(B, S, -1), g_col.reshape(B, S, LANES), g_row, head_norm)
    return [h], [w_out.astype(BF16)]


def kernel(x, mem, mix_norm, xattn_norm, mem_norm, ffn_norm, attn_w_in, attn_fox_bf, diff_lq1, diff_lk1, diff_lq2, diff_lk2, diff_subln, attn_w_out, mlstm_w_in, mlstm_conv_qk, mlstm_b_i, mlstm_b_f, mlstm_head_norm, mlstm_w_out, xattn_wq, xattn_wkv, xattn_wo, ffn_w_up, ffn_conv_w, ffn_conv_b, ffn_w_down, final_norm):
    B, S, D = x.shape
    M = mem.shape[1]
    depth = mix_norm.shape[0]
    x = x.reshape(B * S, D)
    mem2 = mem.reshape(B * M, D)
    for layer in range(depth):
        j = layer // 2
        if layer % 2 == 0:
            lambda_init = 0.8 - 0.6 * math.exp(-0.3 * layer)
            mixed, w_mix = _fox_diff_mixer(x, B, S, mix_norm[layer], attn_w_in[j], attn_fox_bf[j], diff_lq1[j],
                                           diff_lk1[j], diff_lq2[j], diff_lk2[j], diff_subln[j], attn_w_out[j],
                                           lambda_init)
        else:
            mixed, w_mix = _mlstm_mixer(x, B, S, mix_norm[layer], mlstm_w_in[j], mlstm_conv_qk[j], mlstm_b_i[j],
                                        mlstm_b_f[j], mlstm_head_norm[j], mlstm_w_out[j])
        kv = norm_matmul(mem2, mem_norm[layer], xattn_wkv[layer].astype(BF16)).reshape(B, M, 2 * D)
        x = mix_out_cross_attention(mixed, w_mix, x.reshape(B, S, D), xattn_norm[layer],
                                    xattn_wq[layer].astype(BF16), kv, xattn_wo[layer].astype(BF16)).reshape(B * S, D)
        x = conv_ffn(x, ffn_norm[layer], ffn_w_up[layer].astype(BF16), ffn_conv_w[layer], ffn_conv_b[layer],
                     ffn_w_down[layer].astype(BF16), S, final_norm if layer == depth - 1 else None)
    return x.reshape(B, S, D)
```

```python
import functools
import math

import jax
import jax.numpy as jnp
from jax import lax
from jax.experimental import pallas as pl
from jax.experimental.pallas import tpu as pltpu

F32 = jnp.float32
BF16 = jnp.bfloat16

D_MODEL = 1024
RMS_EPS = 1e-6
NEG_INF = -1e30
CHUNK = 64
FOX_HEADS, FOX_DIM = 8, 64
DIFF_HEADS, DIFF_QK, DIFF_V = 4, 64, 128
ML_HEADS, ML_QK, ML_V, ML_CONV = 4, 128, 256, 4
X_HEADS, X_DIM = 4, 256
D_FF = 2816
FFN_CONV = 3
LANES = 128
BF16_ROWS = 16
VMEM_LIMIT = 56 * 1024 * 1024

ROW_TILE = 512
ATT_Q_TILE = 512
ATT_K_TILE = 256
LOG2_E = math.log2(math.e)
ML_CHUNK = 256
ML_SEQS = 2
X_TILE = 512
PROJ_COLS = 1024
FFN_CHUNK = 256
ROW_PHASES = 4
DOWN_CHUNKS = 4
TRANSPOSE_COLS = 512
F_PIECES = 3
FOX_BIAS_K_LANE = 0
FOX_BIAS_Q_LANE = 32


def _params(*sem):
    return pltpu.CompilerParams(dimension_semantics=sem, vmem_limit_bytes=VMEM_LIMIT)


def _resident(shape):
    return pl.BlockSpec(shape, lambda *_: (0,) * len(shape), pipeline_mode=pl.Buffered(1))


def _whole(a):
    if isinstance(a, tuple):
        arr, layer = a
        tail = arr.shape[1:]
        spec = pl.BlockSpec((None,) + tail, lambda *_: (layer,) + (0,) * len(tail), pipeline_mode=pl.Buffered(1))
        return spec, arr
    return _resident(a.shape), a


def _cols(a):
    return (a[0] if isinstance(a, tuple) else a).shape[-1]


def _rms(x, gain):
    return x * lax.rsqrt(jnp.mean(x * x, axis=-1, keepdims=True) + RMS_EPS) * gain


def _sigmoid(x):
    return 1.0 / (1.0 + jnp.exp(-x))


def _dot(a, b):
    return jnp.dot(a, b, preferred_element_type=F32)


def _dot_nt(a, b):
    return lax.dot_general(a, b, (((1,), (1,)), ((), ())), preferred_element_type=F32)


def _split3(x):
    hi = x.astype(BF16)
    r1 = x - hi.astype(F32)
    mid = r1.astype(BF16)
    return hi, mid, (r1 - mid.astype(F32)).astype(BF16)


def _select_dot(sel, x):
    hi, mid, lo = _split3(x)
    return _dot(sel, hi) + _dot(sel, mid) + _dot(sel, lo)


def _block_tri(n, block):
    r = lax.broadcasted_iota(jnp.int32, (n, n), 0)
    c = lax.broadcasted_iota(jnp.int32, (n, n), 1)
    return jnp.where((c <= r) & (r // block == c // block), 1.0, 0.0).astype(BF16)


def _log_sigmoid(x):
    return jnp.minimum(x, 0.0) - jnp.log1p(jnp.exp(-jnp.abs(x)))


def _norm_matmul_kernel(x_ref, g_ref, w_ref, o_ref, *, tn):
    xn = _rms(x_ref[...], g_ref[...]).astype(BF16)
    for c0 in range(0, o_ref.shape[1], tn):
        o_ref[:, c0:c0 + tn] = _dot(xn, w_ref[:, c0:c0 + tn]).astype(o_ref.dtype)


def norm_matmul(x, gain, w, *, tm=ROW_TILE, tn=PROJ_COLS):
    T, D = x.shape
    N = _cols(w)
    specs, operands = zip(*map(_whole, (gain, w)))
    return pl.pallas_call(
        functools.partial(_norm_matmul_kernel, tn=tn),
        grid=(T // tm,),
        in_specs=[pl.BlockSpec((tm, D), lambda i: (i, 0)), *specs],
        out_specs=pl.BlockSpec((tm, N), lambda i: (i, 0)),
        out_shape=jax.ShapeDtypeStruct((T, N), BF16),
        compiler_params=_params("parallel"),
        name="norm_matmul",
    )(x, *operands)


def _fox_proj_kernel(x_ref, g_ref, w_ref, wg_ref, bf_ref, o_ref, fq_ref, fk_ref, carry_ref, *, tm, tn, tiles_per_seq):
    i = pl.program_id(0)

    @pl.when(i % tiles_per_seq == 0)
    def _():
        carry_ref[...] = jnp.zeros_like(carry_ref)

    xn = _rms(x_ref[...], g_ref[...]).astype(BF16)

    def project(part):
        width = o_ref.shape[1] // 3
        for c0 in range(part * width, (part + 1) * width, tn):
            o_ref[:, c0:c0 + tn] = _dot(xn, w_ref[:, c0:c0 + tn]).astype(o_ref.dtype)

    gates = _dot(xn, wg_ref[...])
    project(0)
    log_f = _log_sigmoid(gates + bf_ref[...]) * LOG2_E
    f = carry_ref[...] + _select_dot(_block_tri(tm, tm), log_f)
    carry_ref[...] = f[tm - 1:tm, :]
    project(1)

    src = lax.broadcasted_iota(jnp.int32, (LANES, LANES), 0)
    dst = lax.broadcasted_iota(jnp.int32, (LANES, LANES), 1)
    lane = lax.broadcasted_iota(jnp.int32, (1, LANES), 1)
    n_bias = F_PIECES * FOX_HEADS
    fq = jnp.where((lane >= FOX_BIAS_K_LANE) & (lane < FOX_BIAS_K_LANE + n_bias), 1.0, 0.0)
    fk = jnp.where((lane >= FOX_BIAS_Q_LANE) & (lane < FOX_BIAS_Q_LANE + n_bias), 1.0, 0.0)
    for c, piece in enumerate(_split3(f)):
        head = src < FOX_HEADS
        to_q = jnp.where(head & (dst == FOX_BIAS_Q_LANE + F_PIECES * src + c), 1.0, 0.0).astype(BF16)
        to_k = jnp.where(head & (dst == FOX_BIAS_K_LANE + F_PIECES * src + c), -1.0, 0.0).astype(BF16)
        fq = fq + _dot(piece, to_q)
        fk = fk + _dot(piece, to_k)
    fq_ref[...] = fq.astype(BF16)
    fk_ref[...] = fk.astype(BF16)
    project(2)


def fox_projection(x, gain, w, w_gates, gate_bias, seq_len, *, tm=ROW_TILE, tn=PROJ_COLS):
    T, D = x.shape
    N = _cols(w)
    row_block = lambda n: pl.BlockSpec((tm, n), lambda i: (i, 0))
    specs, operands = zip(*map(_whole, (gain, w, w_gates, gate_bias)))
    return pl.pallas_call(
        functools.partial(_fox_proj_kernel, tm=tm, tn=tn, tiles_per_seq=seq_len // tm),
        grid=(T // tm,),
        in_specs=[row_block(D), *specs],
        out_specs=[row_block(N), row_block(LANES), row_block(LANES)],
        out_shape=[jax.ShapeDtypeStruct((T, N), BF16), jax.ShapeDtypeStruct((T, LANES), BF16),
                   jax.ShapeDtypeStruct((T, LANES), BF16)],
        scratch_shapes=[pltpu.VMEM((1, LANES), F32)],
        compiler_params=_params("arbitrary"),
        name="fox_projection",
    )(x, *operands)


def _mlstm_proj_kernel(x_ref, halo_ref, g_ref, w_ref, wg_ref, gb_ref, cw_ref, o_ref, gc_ref, gr_ref, up_ref, nat_ref,
                       *, tm, tiles_per_seq):
    i = pl.program_id(0)
    halo = _rms(halo_ref[...], g_ref[...])
    xe = jnp.concatenate([jnp.where(i % tiles_per_seq == 0, 0.0, halo).astype(BF16),
                          _rms(x_ref[...], g_ref[...]).astype(BF16)], axis=0)
    xn = xe[BF16_ROWS:]

    qk_slabs = 2 * ML_HEADS * ML_QK // LANES
    rows = tm // ROW_PHASES
    v0 = qk_slabs * LANES
    og0 = v0 + ML_HEADS * ML_V
    for s0 in range(0, qk_slabs, 2):
        up = _dot(xe, w_ref[:, s0 * LANES:(s0 + 2) * LANES])
        up_ref[s0] = up[:, 0:LANES]
        up_ref[s0 + 1] = up[:, LANES:2 * LANES]
    pre = _dot(xn, wg_ref[...]) + gb_ref[...]
    o_ref[:, v0:og0] = _dot(xn, w_ref[:, v0:og0]).astype(o_ref.dtype)
    for s in range(qk_slabs):
        cw = cw_ref[:, s * LANES:(s + 1) * LANES]
        for phase in range(ROW_PHASES):
            y = None
            for tap in range(ML_CONV):
                first = BF16_ROWS + phase - (ML_CONV - 1 - tap)
                term = cw[tap:tap + 1] * up_ref[s, pl.ds(first, rows, stride=ROW_PHASES), :]
                y = term if y is None else y + term
            y = y * _sigmoid(y)
            if s < qk_slabs // 2:
                y = y * (ML_QK ** -0.5)
            nat_ref[s, pl.ds(phase, rows, stride=ROW_PHASES), :] = y
        o_ref[:, s * LANES:(s + 1) * LANES] = nat_ref[s].astype(o_ref.dtype)

    og = _dot(xn, w_ref[:, og0:])
    lane = lax.broadcasted_iota(jnp.int32, (1, LANES), 1)
    val = jnp.where(lane < ML_HEADS, pre, _log_sigmoid(pre)) * LOG2_E
    gc = jnp.where(lane < ML_HEADS, val, _select_dot(_block_tri(tm, ML_CHUNK), val))
    gc_ref[...] = gc
    r = lax.broadcasted_iota(jnp.int32, (2 * ML_HEADS, LANES), 0)
    c = lax.broadcasted_iota(jnp.int32, (2 * ML_HEADS, LANES), 1)
    pick = jnp.where(r == c, 1.0, 0.0).astype(BF16)
    hi, mid, lo = _split3(gc)
    gr_ref[0] = _dot_nt(pick, hi) + _dot_nt(pick, mid) + _dot_nt(pick, lo)
    o_ref[:, og0:] = _sigmoid(og).astype(o_ref.dtype)


def mlstm_projection(x, gain, w, w_gates, gate_bias, conv_w, seq_len, *, tm=ROW_TILE):
    T, D = x.shape
    N = _cols(w)
    tiles_per_seq = seq_len // tm
    halo_blocks = tm // BF16_ROWS
    row_block = lambda n: pl.BlockSpec((tm, n), lambda i: (i, 0))
    specs, operands = zip(*map(_whole, (gain, w, w_gates, gate_bias, conv_w)))
    return pl.pallas_call(
        functools.partial(_mlstm_proj_kernel, tm=tm, tiles_per_seq=tiles_per_seq),
        grid=(T // tm,),
        in_specs=[row_block(D),
                  pl.BlockSpec((BF16_ROWS, D), lambda i: (jnp.maximum(i * halo_blocks - 1, 0), 0)),
                  *specs],
        out_specs=[row_block(N), row_block(LANES),
                   pl.BlockSpec((1, 2 * ML_HEADS, tm), lambda i: (i // tiles_per_seq, 0, i % tiles_per_seq))],
        out_shape=[jax.ShapeDtypeStruct((T, N), BF16), jax.ShapeDtypeStruct((T, LANES), F32),
                   jax.ShapeDtypeStruct((T // seq_len, 2 * ML_HEADS, seq_len), F32)],
        scratch_shapes=[pltpu.VMEM((2 * ML_HEADS * ML_QK // LANES, tm + BF16_ROWS, LANES), F32),
                        pltpu.VMEM((2 * ML_HEADS * ML_QK // LANES, tm, LANES), F32)],
        compiler_params=_params("parallel"),
        name="mlstm_projection",
    )(x, x, *operands)


def _build_vt(v_ref, vt_ref, n_heads, rows):
    S = v_ref.shape[1]
    r = lax.broadcasted_iota(jnp.int32, (LANES, LANES), 0)
    c = lax.broadcasted_iota(jnp.int32, (LANES, LANES), 1)
    eye = jnp.where(r == c, 1.0, 0.0).astype(BF16)
    per_group = LANES // rows
    for g in range(n_heads // per_group):
        for c0 in range(0, S, TRANSPOSE_COLS):
            cs = slice(c0, c0 + TRANSPOSE_COLS)
            vt = _dot_nt(eye, v_ref[0, cs, g * LANES:(g + 1) * LANES]).astype(BF16)
            for k in range(per_group):
                vt_ref[g * per_group + k, 0:rows, cs] = vt[k * rows:(k + 1) * rows]
    for h in range(n_heads):
        vt_ref[h, rows:rows + BF16_ROWS, :] = jnp.ones((BF16_ROWS, S), BF16)


def _softmax_step(s_t, vt, m_ref, acc_ref, idx):
    m_prev = m_ref[idx]
    m_new = jnp.maximum(m_prev, jnp.max(s_t, axis=0, keepdims=True))
    p = jnp.exp2(s_t - m_new).astype(BF16)
    acc_ref[idx] = jnp.exp2(m_prev - m_new) * acc_ref[idx] + _dot(vt, p)
    m_ref[idx] = m_new


def _causal_steps(i, logits, attend, n_streams, buf_a, buf_b):
    def phase(j, src, dst, mask=None):
        for n in range(n_streams):
            logits(j + 1, dst, n)
            attend(j, src, n, mask)

    for n in range(n_streams):
        logits(0, buf_a, n)

    def body(jj, carry):
        phase(2 * jj, buf_a, buf_b)
        phase(2 * jj + 1, buf_b, buf_a)
        return carry

    lax.fori_loop(0, i, body, 0)
    phase(2 * i, buf_a, buf_b, 0)
    for n in range(n_streams):
        attend(2 * i + 1, buf_b, n, 1)


def _fox_kernel(q_ref, k_ref, v_ref, fq_ref, fk_ref, o_ref, vt_ref, qc_ref, acc_ref, m_ref, sa_ref, sb_ref, *, tq, tk):
    i = pl.program_id(1)

    @pl.when(i == 0)
    def _():
        _build_vt(v_ref, vt_ref, FOX_HEADS, FOX_DIM)

    lane = lax.broadcasted_iota(jnp.int32, (1, LANES), 1)
    fq = fq_ref[0]
    for h in range(FOX_HEADS):
        pair, half = divmod(h, 2)
        q = q_ref[0, :, pair * LANES:(pair + 1) * LANES]
        in_head = (lane >= half * FOX_DIM) & (lane < (half + 1) * FOX_DIM)
        lo_k, lo_q = FOX_BIAS_K_LANE + F_PIECES * h, FOX_BIAS_Q_LANE + F_PIECES * h
        mine = ((lane >= lo_k) & (lane < lo_k + F_PIECES)) | ((lane >= lo_q) & (lane < lo_q + F_PIECES))
        qc_ref[h, :, 0:LANES] = jnp.where(in_head, q, jnp.zeros_like(q))
        qc_ref[h, :, LANES:2 * LANES] = jnp.where(mine, fq, jnp.zeros_like(fq))
    m_ref[...] = jnp.full(m_ref.shape, NEG_INF, F32)
    acc_ref[...] = jnp.zeros(acc_ref.shape, F32)

    key = lax.broadcasted_iota(jnp.int32, (tk, tq), 0)
    qry = lax.broadcasted_iota(jnp.int32, (tk, tq), 1)
    causal = [key + half * tk <= qry for half in range(tq // tk)]

    def key_rows(j):
        return pl.ds(pl.multiple_of(j * tk, tk), tk)

    def logits(j, buf, h):
        pair = h // 2
        kc = jnp.concatenate([k_ref[0, key_rows(j), pair * LANES:(pair + 1) * LANES], fk_ref[0, key_rows(j), :]],
                             axis=1)
        buf[h] = _dot_nt(kc, qc_ref[h])

    def attend(j, buf, h, mask):
        s_t = buf[h]
        if mask is not None:
            s_t = jnp.where(causal[mask], s_t, NEG_INF)
        _softmax_step(s_t, vt_ref[h, :, key_rows(j)], m_ref, acc_ref, h)

    _causal_steps(i, logits, attend, FOX_HEADS, sa_ref, sb_ref)

    for pair in range(FOX_HEADS // 2):
        halves = []
        for half in range(2):
            a = acc_ref[2 * pair + half]
            halves.append(a[0:FOX_DIM] / a[FOX_DIM:FOX_DIM + 1])
        o_ref[0, :, pair * LANES:(pair + 1) * LANES] = jnp.concatenate(halves, axis=0).T.astype(o_ref.dtype)


def fox_attention(proj, fq, fk, *, tq=ATT_Q_TILE, tk=ATT_K_TILE):
    B, S, _ = proj.shape
    width = FOX_HEADS * FOX_DIM
    rows = FOX_DIM + BF16_ROWS
    return pl.pallas_call(
        functools.partial(_fox_kernel, tq=tq, tk=tk),
        grid=(B, S // tq),
        in_specs=[pl.BlockSpec((1, tq, width), lambda b, i: (b, i, 0)),
                  pl.BlockSpec((1, S, width), lambda b, i: (b, 0, 1)),
                  pl.BlockSpec((1, S, width), lambda b, i: (b, 0, 2)),
                  pl.BlockSpec((1, tq, LANES), lambda b, i: (b, i, 0)),
                  pl.BlockSpec((1, S, LANES), lambda b, i: (b, 0, 0))],
        out_specs=pl.BlockSpec((1, tq, width), lambda b, i: (b, i, 0)),
        out_shape=jax.ShapeDtypeStruct((B, S, width), BF16),
        scratch_shapes=[pltpu.VMEM((FOX_HEADS, rows, S), BF16),
                        pltpu.VMEM((FOX_HEADS, tq, 2 * LANES), BF16),
                        pltpu.VMEM((FOX_HEADS, rows, tq), F32),
                        pltpu.VMEM((FOX_HEADS, 1, tq), F32),
                        pltpu.VMEM((FOX_HEADS, tk, tq), F32),
                        pltpu.VMEM((FOX_HEADS, tk, tq), F32)],
        compiler_params=_params("parallel", "arbitrary"),
        name="fox_attention",
    )(proj, proj, proj, fq, fk)


def _diff_kernel(q_ref, k_ref, v_ref, lq1_ref, lk1_ref, lq2_ref, lk2_ref, sub_ref, o_ref, vt_ref, qc_ref, acc_ref,
                 m_ref, sa_ref, sb_ref, *, tq, tk, lambda_init):
    i = pl.program_id(1)

    @pl.when(i == 0)
    def _():
        _build_vt(v_ref, vt_ref, DIFF_HEADS, DIFF_V)

    lane = lax.broadcasted_iota(jnp.int32, (1, LANES), 1)
    for h in range(DIFF_HEADS):
        q = q_ref[0, :, h * LANES:(h + 1) * LANES]
        zero = jnp.zeros_like(q)
        qc_ref[2 * h] = jnp.where(lane < DIFF_QK, q, zero)
        qc_ref[2 * h + 1] = jnp.where(lane >= DIFF_QK, q, zero)
    m_ref[...] = jnp.full(m_ref.shape, NEG_INF, F32)
    acc_ref[...] = jnp.zeros(acc_ref.shape, F32)

    key = lax.broadcasted_iota(jnp.int32, (tk, tq), 0)
    qry = lax.broadcasted_iota(jnp.int32, (tk, tq), 1)
    visible = [(key + half * tk) // CHUNK <= qry // CHUNK for half in range(tq // tk)]

    def key_rows(j):
        return pl.ds(pl.multiple_of(j * tk, tk), tk)

    def logits(j, buf, n):
        h = n // 2
        buf[n] = _dot_nt(k_ref[0, key_rows(j), h * LANES:(h + 1) * LANES], qc_ref[n])

    def attend(j, buf, n, mask):
        s_t = buf[n]
        if mask is not None:
            s_t = jnp.where(visible[mask], s_t, NEG_INF)
        _softmax_step(s_t, vt_ref[n // 2, :, key_rows(j)], m_ref, acc_ref, n)

    _causal_steps(i, logits, attend, 2 * DIFF_HEADS, sa_ref, sb_ref)

    lam = (jnp.exp(jnp.sum(lq1_ref[...] * lk1_ref[...], axis=1, keepdims=True))
           - jnp.exp(jnp.sum(lq2_ref[...] * lk2_ref[...], axis=1, keepdims=True)) + lambda_init)
    for h in range(DIFF_HEADS):
        a1, a2 = acc_ref[2 * h], acc_ref[2 * h + 1]
        o_t = a1[0:DIFF_V] / a1[DIFF_V:DIFF_V + 1] - lam * (a2[0:DIFF_V] / a2[DIFF_V:DIFF_V + 1])
        out = _rms(o_t.T, sub_ref[...]) * (1.0 - lambda_init)
        o_ref[0, :, h * LANES:(h + 1) * LANES] = out.astype(o_ref.dtype)


def diff_attention(proj, lq1, lk1, lq2, lk2, subln, lambda_init, *, tq=ATT_Q_TILE, tk=ATT_K_TILE):
    B, S, _ = proj.shape
    width = DIFF_HEADS * DIFF_V
    rows = DIFF_V + BF16_ROWS
    specs, operands = zip(*map(_whole, (lq1, lk1, lq2, lk2, subln)))
    return pl.pallas_call(
        functools.partial(_diff_kernel, tq=tq, tk=tk, lambda_init=lambda_init),
        grid=(B, S // tq),
        in_specs=[pl.BlockSpec((1, tq, width), lambda b, i: (b, i, 3)),
                  pl.BlockSpec((1, S, width), lambda b, i: (b, 0, 4)),
                  pl.BlockSpec((1, S, width), lambda b, i: (b, 0, 5)),
                  *specs],
        out_specs=pl.BlockSpec((1, tq, width), lambda b, i: (b, i, 0)),
        out_shape=jax.ShapeDtypeStruct((B, S, width), BF16),
        scratch_shapes=[pltpu.VMEM((DIFF_HEADS, rows, S), BF16),
                        pltpu.VMEM((2 * DIFF_HEADS, tq, LANES), BF16),
                        pltpu.VMEM((2 * DIFF_HEADS, rows, tq), F32),
                        pltpu.VMEM((2 * DIFF_HEADS, 1, tq), F32),
                        pltpu.VMEM((2 * DIFF_HEADS, tk, tq), F32),
                        pltpu.VMEM((2 * DIFF_HEADS, tk, tq), F32)],
        compiler_params=_params("parallel", "arbitrary"),
        name="diff_attention",
    )(proj, proj, proj, *operands)


def _mlstm_kernel(qk_ref, v_ref, sg_ref, gc_ref, gr_ref, hn_ref, o_ref, ct_ref, m_ref, vt_ref, *, L, nb):
    c = pl.program_id(1)

    @pl.when(c == 0)
    def _():
        ct_ref[...] = jnp.zeros_like(ct_ref)
        m_ref[...] = jnp.zeros_like(m_ref)
        for n in range(nb * ML_HEADS):
            vt_ref[n, ML_V:ML_V + BF16_ROWS, :] = jnp.ones((BF16_ROWS, L), BF16)

    src = lax.broadcasted_iota(jnp.int32, (L, L), 0)
    dst = lax.broadcasted_iota(jnp.int32, (L, L), 1)
    causal = src <= dst
    eye = jnp.where(src == dst, 1.0, 0.0).astype(BF16)
    k0 = ML_HEADS * ML_QK
    for b in range(nb):
        gc = gc_ref[b]
        gr = gr_ref[b]
        for h in range(ML_HEADS):
            n = b * ML_HEADS + h
            q = qk_ref[b, :, h * ML_QK:(h + 1) * ML_QK]
            k = qk_ref[b, :, k0 + h * ML_QK:k0 + (h + 1) * ML_QK]
            vt_ref[n, 0:ML_V, :] = _dot_nt(eye, v_ref[b, :, h * ML_V:(h + 1) * ML_V]).astype(BF16)
            v_t = vt_ref[n]
            r_col = gc[:, h:h + 1] - gc[:, ML_HEADS + h:ML_HEADS + h + 1]
            b_row = gr[ML_HEADS + h:ML_HEADS + h + 1, :]
            r_row = gr[h:h + 1, :] - b_row
            g = b_row[:, L - 1:L]
            ct = ct_ref[n]
            m = m_ref[n][:, 0:1]

            dm = jnp.where(causal, r_col, NEG_INF)
            mt = jnp.maximum(m, jnp.max(dm, axis=0, keepdims=True))
            s_t = _dot_nt(k, q) * jnp.exp2(dm - mt)
            both = jnp.exp2(m - mt) * _dot_nt(ct.astype(BF16), q) + _dot(v_t, s_t.astype(BF16))
            den = both[ML_V:ML_V + 1]
            hh = both[0:ML_V] / jnp.maximum(jnp.abs(den), jnp.exp2(-(b_row + mt)))
            hh = hh * lax.rsqrt(jnp.mean(hh * hh, axis=0, keepdims=True) + RMS_EPS)

            m_next = jnp.maximum(m, jnp.max(r_row, axis=1, keepdims=True))
            kw = (k.astype(F32) * jnp.exp2(r_col - m_next)).astype(BF16)
            ct_ref[n] = jnp.exp2(m - m_next) * ct + _dot(v_t, kw)
            m_ref[n] = jnp.broadcast_to(g + m_next, (1, LANES))

            vs = slice(h * ML_V, (h + 1) * ML_V)
            o_ref[b, :, vs] = (hh.T * hn_ref[:, vs] * sg_ref[b, :, vs].astype(F32)).astype(o_ref.dtype)


def mlstm(proj, g_col, g_row, head_norm, *, L=ML_CHUNK, nb=ML_SEQS):
    B, S, _ = proj.shape
    W = D_MODEL
    assert L == ML_V, "one identity matrix serves the v transposes"
    norm_spec, head_norm = _whole(head_norm)
    return pl.pallas_call(
        functools.partial(_mlstm_kernel, L=L, nb=nb),
        grid=(B // nb, S // L),
        in_specs=[pl.BlockSpec((nb, L, W), lambda b, c: (b, c, 0)),
                  pl.BlockSpec((nb, L, W), lambda b, c: (b, c, 1)),
                  pl.BlockSpec((nb, L, W), lambda b, c: (b, c, 2)),
                  pl.BlockSpec((nb, L, LANES), lambda b, c: (b, c, 0)),
                  pl.BlockSpec((nb, 2 * ML_HEADS, L), lambda b, c: (b, 0, c)),
                  norm_spec],
        out_specs=pl.BlockSpec((nb, L, W), lambda b, c: (b, c, 0)),
        out_shape=jax.ShapeDtypeStruct((B, S, W), BF16),
        scratch_shapes=[pltpu.VMEM((nb * ML_HEADS, ML_V + BF16_ROWS, ML_QK), F32),
                        pltpu.VMEM((nb * ML_HEADS, 1, LANES), F32),
                        pltpu.VMEM((nb * ML_HEADS, ML_V + BF16_ROWS, L), BF16)],
        compiler_params=_params("parallel", "arbitrary"),
        name="mlstm",
    )(proj, proj, proj, g_col, g_row, head_norm)


def _xattn_kernel(*refs, n_in):
    x_ref, g_ref, wq_ref, k_ref, v_ref, wo_ref, o_ref = refs[2 * n_in:]
    x = x_ref[0]
    for a_ref, w_ref in zip(refs[:n_in], refs[n_in:2 * n_in]):
        x = x + _dot(a_ref[0], w_ref[...])
    xn = _rms(x, g_ref[...]).astype(BF16)
    q = (_dot(xn, wq_ref[...]) * (X_DIM ** -0.5)).astype(BF16)
    heads = []
    for h in range(X_HEADS):
        cols = slice(h * X_DIM, (h + 1) * X_DIM)
        s = _dot_nt(q[:, cols], k_ref[0, :, cols])
        p = jnp.exp(s - jnp.max(s, axis=1, keepdims=True))
        l = jnp.sum(p, axis=1, keepdims=True)
        heads.append((_dot(p.astype(BF16), v_ref[0, :, cols]) / l).astype(BF16))
    o_ref[0] = x + _dot(jnp.concatenate(heads, axis=1), wo_ref[...])


def mix_out_cross_attention(acts, w_mix, x, gain, wq, kv, wo, *, tq=X_TILE):
    B, S, D = x.shape
    M = kv.shape[1]
    (gain_spec, gain), (wq_spec, wq), (wo_spec, wo) = map(_whole, (gain, wq, wo))
    return pl.pallas_call(
        functools.partial(_xattn_kernel, n_in=len(acts)),
        grid=(B, S // tq),
        in_specs=([pl.BlockSpec((1, tq, a.shape[2]), lambda b, i: (b, i, 0)) for a in acts]
                  + [w[0] for w in w_mix]
                  + [pl.BlockSpec((1, tq, D), lambda b, i: (b, i, 0)),
                     gain_spec, wq_spec,
                     pl.BlockSpec((1, M, D), lambda b, i: (b, 0, 0)),
                     pl.BlockSpec((1, M, D), lambda b, i: (b, 0, 1)),
                     wo_spec]),
        out_specs=pl.BlockSpec((1, tq, D), lambda b, i: (b, i, 0)),
        out_shape=jax.ShapeDtypeStruct((B, S, D), F32),
        compiler_params=_params("parallel", "parallel"),
        name="mix_out_cross_attention",
    )(*acts, *[w[1] for w in w_mix], x, gain, wq, kv, kv, wo)


def _gelu_tanh(x):
    k = -2.0 * math.sqrt(2.0 / math.pi) * math.log2(math.e)
    return x / (1.0 + jnp.exp2(x * (k * 0.044715 * (x * x) + k)))


def _ffn_kernel(x_ref, halo_ref, g_ref, wup_ref, cw_ref, cb_ref, wd_ref, fg_ref, o_ref, up_ref, act_ref, nat_ref,
                *, tm, tiles_per_seq, final_norm):
    i = pl.program_id(0)
    halo = _rms(halo_ref[...], g_ref[...])
    xe = jnp.concatenate([jnp.where(i % tiles_per_seq == 0, 0.0, halo).astype(BF16),
                          _rms(x_ref[...], g_ref[...]).astype(BF16)], axis=0)

    n_chunks = D_FF // FFN_CHUNK
    slabs = FFN_CHUNK // LANES
    rows = tm // ROW_PHASES

    def up_project(c):
        for half in range(2):
            col0 = half * D_FF + c * FFN_CHUNK
            up = _dot(xe, wup_ref[:, col0:col0 + FFN_CHUNK])
            for s in range(slabs):
                up_ref[c % 2, half, s] = up[:, s * LANES:(s + 1) * LANES]

    def conv(c, half, s, phase):
        col0 = half * D_FF + c * FFN_CHUNK + s * LANES
        cw = cw_ref[:, col0:col0 + LANES]
        out = cb_ref[:, col0:col0 + LANES]
        for tap in range(FFN_CONV):
            first = BF16_ROWS + phase - (FFN_CONV - 1 - tap)
            out = out + cw[tap:tap + 1] * up_ref[c % 2, half, s, pl.ds(first, rows, stride=ROW_PHASES), :]
        return out

    acc = None
    piece_start = 0
    up_project(0)
    for c in range(n_chunks):
        if c + 1 < n_chunks:
            up_project(c + 1)
        for phase in range(ROW_PHASES):
            for s in range(slabs):
                act = _gelu_tanh(conv(c, 0, s, phase)) * conv(c, 1, s, phase)
                act_ref[phase * rows:(phase + 1) * rows,
                        c * FFN_CHUNK + s * LANES:c * FFN_CHUNK + (s + 1) * LANES] = act.astype(BF16)
        if (c + 1) % DOWN_CHUNKS == 0 or c + 1 == n_chunks:
            piece = slice(piece_start * FFN_CHUNK, (c + 1) * FFN_CHUNK)
            part = _dot(act_ref[:, piece], wd_ref[piece, :])
            acc = part if acc is None else acc + part
            piece_start = c + 1

    for phase in range(ROW_PHASES):
        for s in range(D_MODEL // LANES):
            nat_ref[s, pl.ds(phase, rows, stride=ROW_PHASES), :] = acc[phase * rows:(phase + 1) * rows,
                                                                       s * LANES:(s + 1) * LANES]
    y = x_ref[...] + jnp.concatenate([nat_ref[s] for s in range(D_MODEL // LANES)], axis=1)
    o_ref[...] = _rms(y, fg_ref[...]) if final_norm else y


def conv_ffn(x, gain, w_up, conv_w, conv_b, w_down, seq_len, final_gain=None, *, tm=ROW_TILE):
    T, D = x.shape
    halo_blocks = tm // BF16_ROWS
    final_norm = final_gain is not None
    specs, operands = zip(*map(_whole, (gain, w_up, conv_w, conv_b, w_down, final_gain if final_norm else gain)))
    return pl.pallas_call(
        functools.partial(_ffn_kernel, tm=tm, tiles_per_seq=seq_len // tm, final_norm=final_norm),
        grid=(T // tm,),
        in_specs=[pl.BlockSpec((tm, D), lambda i: (i, 0)),
                  pl.BlockSpec((BF16_ROWS, D), lambda i: (jnp.maximum(i * halo_blocks - 1, 0), 0)),
                  *specs],
        out_specs=pl.BlockSpec((tm, D), lambda i: (i, 0)),
        out_shape=jax.ShapeDtypeStruct((T, D), F32),
        scratch_shapes=[pltpu.VMEM((2, 2, FFN_CHUNK // LANES, tm + BF16_ROWS, LANES), F32),
                        pltpu.VMEM((tm, D_FF), BF16),
                        pltpu.VMEM((D // LANES, tm, LANES), F32)],
        compiler_params=_params("parallel"),
        name="conv_ffn",
    )(x, x, *operands)


def _pad_lanes(a):
    return jnp.pad(a, ((0, 0), (0, LANES - a.shape[1])))


def _fox_diff_mixer(x, B, S, j, gain, w_in, fox_bf, lq1, lk1, lq2, lk2, subln, w_out, lambda_init):
    fw = FOX_HEADS * FOX_DIM
    g0 = 3 * fw
    dq0 = g0 + FOX_HEADS
    dqw = DIFF_HEADS * 2 * DIFF_QK
    w_main = jnp.concatenate([w_in[:, :fw] * (FOX_DIM ** -0.5 * LOG2_E), w_in[:, fw:g0],
                              w_in[:, dq0:dq0 + dqw] * (DIFF_QK ** -0.5 * LOG2_E), w_in[:, dq0 + dqw:]],
                             axis=1).astype(BF16)
    w_gates = _pad_lanes(w_in[:, g0:g0 + FOX_HEADS]).astype(BF16)
    proj, fq, fk = fox_projection(x, gain, w_main, w_gates, _pad_lanes(fox_bf.reshape(1, -1)), S)
    proj = proj.reshape(B, S, -1)
    fox = fox_attention(proj, fq.reshape(B, S, LANES), fk.reshape(B, S, LANES))
    dif = diff_attention(proj, lq1, lk1, lq2, lk2, subln, lambda_init)
    row_block = lambda r: (pl.BlockSpec((None, fw, D_MODEL), lambda *_: (j, r, 0), pipeline_mode=pl.Buffered(1)), w_out)
    return [fox, dif], [row_block(0), row_block(1)]


def _mlstm_mixer(x, B, S, j, gain, w_in, conv_qk, b_i, b_f, head_norm, w_out):
    g0 = 2 * ML_HEADS * ML_QK + ML_HEADS * ML_V
    w_main = jnp.concatenate([w_in[:, :g0], w_in[:, g0 + 2 * ML_HEADS:]], axis=1).astype(BF16)
    w_gates = _pad_lanes(w_in[:, g0:g0 + 2 * ML_HEADS]).astype(BF16)
    gate_bias = _pad_lanes(jnp.concatenate([b_i, b_f]).reshape(1, -1))
    proj, g_col, g_row = mlstm_projection(x, gain, w_main, w_gates, gate_bias, conv_qk, S)
    h = mlstm(proj.reshape(B, S, -1), g_col.reshape(B, S, LANES), g_row, head_norm)
    return [h], [_whole((w_out, j))]


def kernel(x, mem, mix_norm, xattn_norm, mem_norm, ffn_norm, attn_w_in, attn_fox_bf, diff_lq1, diff_lk1, diff_lq2, diff_lk2, diff_subln, attn_w_out, mlstm_w_in, mlstm_conv_qk, mlstm_b_i, mlstm_b_f, mlstm_head_norm, mlstm_w_out, xattn_wq, xattn_wkv, xattn_wo, ffn_w_up, ffn_conv_w, ffn_conv_b, ffn_w_down, final_norm):
    B, S, D = x.shape
    M = mem.shape[1]
    depth = mix_norm.shape[0]
    x = x.reshape(B * S, D)
    mem2 = mem.reshape(B * M, D)
    rows = lambda a: a.reshape(a.shape[0], 1, -1)
    to_bf16 = lambda a: a.astype(BF16)
    mix_norm, xattn_norm, mem_norm, ffn_norm = map(rows, (mix_norm, xattn_norm, mem_norm, ffn_norm))
    diff_lq1, diff_lk1, diff_lq2, diff_lk2, diff_subln = map(rows, (diff_lq1, diff_lk1, diff_lq2, diff_lk2, diff_subln))
    mlstm_head_norm, ffn_conv_b = rows(mlstm_head_norm), rows(ffn_conv_b)
    attn_w_out, mlstm_w_out, xattn_wq, xattn_wkv, xattn_wo, ffn_w_up, ffn_w_down = map(
        to_bf16, (attn_w_out, mlstm_w_out, xattn_wq, xattn_wkv, xattn_wo, ffn_w_up, ffn_w_down))
    for layer in range(depth):
        j = layer // 2
        if layer % 2 == 0:
            lambda_init = 0.8 - 0.6 * math.exp(-0.3 * layer)
            mixed, w_mix = _fox_diff_mixer(x, B, S, j, (mix_norm, layer), attn_w_in[j], attn_fox_bf[j],
                                           (diff_lq1, j), (diff_lk1, j), (diff_lq2, j), (diff_lk2, j),
                                           (diff_subln, j), attn_w_out, lambda_init)
        else:
            mixed, w_mix = _mlstm_mixer(x, B, S, j, (mix_norm, layer), mlstm_w_in[j], (mlstm_conv_qk, j),
                                        mlstm_b_i[j], mlstm_b_f[j], (mlstm_head_norm, j), mlstm_w_out)
        kv = norm_matmul(mem2, (mem_norm, layer), (xattn_wkv, layer)).reshape(B, M, 2 * D)
        x = mix_out_cross_attention(mixed, w_mix, x.reshape(B, S, D), (xattn_norm, layer), (xattn_wq, layer), kv,
                                    (xattn_wo, layer)).reshape(B * S, D)
        x = conv_ffn(x, (ffn_norm, layer), (ffn_w_up, layer), (ffn_conv_w, layer), (ffn_conv_b, layer),
                     (ffn_w_down, layer), S, final_norm.reshape(1, D) if layer == depth - 1 else None)
    return x.reshape(B, S, D)
```

```python
import functools
import math

import jax
import jax.numpy as jnp
from jax import lax
from jax.experimental import pallas as pl
from jax.experimental.pallas import tpu as pltpu

F32 = jnp.float32
BF16 = jnp.bfloat16

D_MODEL = 1024
RMS_EPS = 1e-6
NEG_INF = -1e30
CHUNK = 64
FOX_HEADS, FOX_DIM = 8, 64
DIFF_HEADS, DIFF_QK, DIFF_V = 4, 64, 128
ML_HEADS, ML_QK, ML_V, ML_CONV = 4, 128, 256, 4
X_HEADS, X_DIM = 4, 256
D_FF = 2816
FFN_CONV = 3
LANES = 128
BF16_ROWS = 16
VMEM_LIMIT = 56 * 1024 * 1024

ROW_TILE = 512
ATT_Q_TILE = 512
ATT_K_TILE = 512
LOG2_E = math.log2(math.e)
ML_CHUNK = 256
ML_SEQS = 2
X_TILE = 512
PROJ_COLS = 1024
FFN_CHUNK = 256
ROW_PHASES = 4
DOWN_CHUNKS = 4
TRANSPOSE_COLS = 512
F_PIECES = 3
FOX_BIAS_K_LANE = 0
FOX_BIAS_Q_LANE = 32


def _params(*sem):
    return pltpu.CompilerParams(dimension_semantics=sem, vmem_limit_bytes=VMEM_LIMIT)


def _resident(shape):
    return pl.BlockSpec(shape, lambda *_: (0,) * len(shape), pipeline_mode=pl.Buffered(1))


def _whole(a):
    if isinstance(a, tuple):
        arr, layer = a
        tail = arr.shape[1:]
        spec = pl.BlockSpec((None,) + tail, lambda *_: (layer,) + (0,) * len(tail), pipeline_mode=pl.Buffered(1))
        return spec, arr
    return _resident(a.shape), a


def _cols(a):
    return (a[0] if isinstance(a, tuple) else a).shape[-1]


def _rms(x, gain):
    return x * lax.rsqrt(jnp.mean(x * x, axis=-1, keepdims=True) + RMS_EPS) * gain


def _sigmoid(x):
    return 1.0 / (1.0 + jnp.exp(-x))


def _dot(a, b):
    return jnp.dot(a, b, preferred_element_type=F32)


def _dot_nt(a, b):
    return lax.dot_general(a, b, (((1,), (1,)), ((), ())), preferred_element_type=F32)


def _split3(x):
    hi = x.astype(BF16)
    r1 = x - hi.astype(F32)
    mid = r1.astype(BF16)
    return hi, mid, (r1 - mid.astype(F32)).astype(BF16)


def _select_dot(sel, x):
    hi, mid, lo = _split3(x)
    return _dot(sel, hi) + _dot(sel, mid) + _dot(sel, lo)


def _block_tri(n, block):
    r = lax.broadcasted_iota(jnp.int32, (n, n), 0)
    c = lax.broadcasted_iota(jnp.int32, (n, n), 1)
    return jnp.where((c <= r) & (r // block == c // block), 1.0, 0.0).astype(BF16)


def _log_sigmoid(x):
    return jnp.minimum(x, 0.0) - jnp.log1p(jnp.exp(-jnp.abs(x)))


def _norm_matmul_kernel(x_ref, g_ref, w_ref, o_ref, *, tn):
    xn = _rms(x_ref[...], g_ref[...]).astype(BF16)
    for c0 in range(0, o_ref.shape[1], tn):
        o_ref[:, c0:c0 + tn] = _dot(xn, w_ref[:, c0:c0 + tn]).astype(o_ref.dtype)


def norm_matmul(x, gain, w, *, tm=ROW_TILE, tn=PROJ_COLS):
    T, D = x.shape
    N = _cols(w)
    specs, operands = zip(*map(_whole, (gain, w)))
    return pl.pallas_call(
        functools.partial(_norm_matmul_kernel, tn=tn),
        grid=(T // tm,),
        in_specs=[pl.BlockSpec((tm, D), lambda i: (i, 0)), *specs],
        out_specs=pl.BlockSpec((tm, N), lambda i: (i, 0)),
        out_shape=jax.ShapeDtypeStruct((T, N), BF16),
        compiler_params=_params("parallel"),
        name="norm_matmul",
    )(x, *operands)


def _fox_proj_kernel(x_ref, g_ref, w_ref, wg_ref, bf_ref, o_ref, fq_ref, fk_ref, carry_ref, *, tm, tn, tiles_per_seq):
    i = pl.program_id(0)

    @pl.when(i % tiles_per_seq == 0)
    def _():
        carry_ref[...] = jnp.zeros_like(carry_ref)

    xn = _rms(x_ref[...], g_ref[...]).astype(BF16)

    def project(part):
        width = o_ref.shape[1] // 3
        for c0 in range(part * width, (part + 1) * width, tn):
            o_ref[:, c0:c0 + tn] = _dot(xn, w_ref[:, c0:c0 + tn]).astype(o_ref.dtype)

    gates = _dot(xn, wg_ref[...])
    project(0)
    log_f = _log_sigmoid(gates + bf_ref[...]) * LOG2_E
    f = carry_ref[...] + _select_dot(_block_tri(tm, tm), log_f)
    carry_ref[...] = f[tm - 1:tm, :]
    project(1)

    src = lax.broadcasted_iota(jnp.int32, (LANES, LANES), 0)
    dst = lax.broadcasted_iota(jnp.int32, (LANES, LANES), 1)
    lane = lax.broadcasted_iota(jnp.int32, (1, LANES), 1)
    n_bias = F_PIECES * FOX_HEADS
    fq = jnp.where((lane >= FOX_BIAS_K_LANE) & (lane < FOX_BIAS_K_LANE + n_bias), 1.0, 0.0)
    fk = jnp.where((lane >= FOX_BIAS_Q_LANE) & (lane < FOX_BIAS_Q_LANE + n_bias), 1.0, 0.0)
    for c, piece in enumerate(_split3(f)):
        head = src < FOX_HEADS
        to_q = jnp.where(head & (dst == FOX_BIAS_Q_LANE + F_PIECES * src + c), 1.0, 0.0).astype(BF16)
        to_k = jnp.where(head & (dst == FOX_BIAS_K_LANE + F_PIECES * src + c), -1.0, 0.0).astype(BF16)
        fq = fq + _dot(piece, to_q)
        fk = fk + _dot(piece, to_k)
    fq_ref[...] = fq.astype(BF16)
    fk_ref[...] = fk.astype(BF16)
    project(2)


def fox_projection(x, gain, w, w_gates, gate_bias, seq_len, *, tm=ROW_TILE, tn=PROJ_COLS):
    T, D = x.shape
    N = _cols(w)
    row_block = lambda n: pl.BlockSpec((tm, n), lambda i: (i, 0))
    specs, operands = zip(*map(_whole, (gain, w, w_gates, gate_bias)))
    return pl.pallas_call(
        functools.partial(_fox_proj_kernel, tm=tm, tn=tn, tiles_per_seq=seq_len // tm),
        grid=(T // tm,),
        in_specs=[row_block(D), *specs],
        out_specs=[row_block(N), row_block(LANES), row_block(LANES)],
        out_shape=[jax.ShapeDtypeStruct((T, N), BF16), jax.ShapeDtypeStruct((T, LANES), BF16),
                   jax.ShapeDtypeStruct((T, LANES), BF16)],
        scratch_shapes=[pltpu.VMEM((1, LANES), F32)],
        compiler_params=_params("arbitrary"),
        name="fox_projection",
    )(x, *operands)


def _mlstm_proj_kernel(x_ref, halo_ref, g_ref, w_ref, wg_ref, gb_ref, cw_ref, o_ref, gc_ref, gr_ref, up_ref, nat_ref,
                       *, tm, tiles_per_seq):
    i = pl.program_id(0)
    halo = _rms(halo_ref[...], g_ref[...])
    xe = jnp.concatenate([jnp.where(i % tiles_per_seq == 0, 0.0, halo).astype(BF16),
                          _rms(x_ref[...], g_ref[...]).astype(BF16)], axis=0)
    xn = xe[BF16_ROWS:]

    qk_slabs = 2 * ML_HEADS * ML_QK // LANES
    rows = tm // ROW_PHASES
    v0 = qk_slabs * LANES
    og0 = v0 + ML_HEADS * ML_V
    for s0 in range(0, qk_slabs, 2):
        up = _dot(xe, w_ref[:, s0 * LANES:(s0 + 2) * LANES])
        up_ref[s0] = up[:, 0:LANES]
        up_ref[s0 + 1] = up[:, LANES:2 * LANES]
    pre = _dot(xn, wg_ref[...]) + gb_ref[...]
    o_ref[:, v0:og0] = _dot(xn, w_ref[:, v0:og0]).astype(o_ref.dtype)
    for s in range(qk_slabs):
        cw = cw_ref[:, s * LANES:(s + 1) * LANES]
        for phase in range(ROW_PHASES):
            y = None
            for tap in range(ML_CONV):
                first = BF16_ROWS + phase - (ML_CONV - 1 - tap)
                term = cw[tap:tap + 1] * up_ref[s, pl.ds(first, rows, stride=ROW_PHASES), :]
                y = term if y is None else y + term
            y = y * _sigmoid(y)
            if s < qk_slabs // 2:
                y = y * (ML_QK ** -0.5)
            nat_ref[s, pl.ds(phase, rows, stride=ROW_PHASES), :] = y
        o_ref[:, s * LANES:(s + 1) * LANES] = nat_ref[s].astype(o_ref.dtype)

    og = _dot(xn, w_ref[:, og0:])
    lane = lax.broadcasted_iota(jnp.int32, (1, LANES), 1)
    val = jnp.where(lane < ML_HEADS, pre, _log_sigmoid(pre)) * LOG2_E
    gc = jnp.where(lane < ML_HEADS, val, _select_dot(_block_tri(tm, ML_CHUNK), val))
    gc_ref[...] = gc
    r = lax.broadcasted_iota(jnp.int32, (2 * ML_HEADS, LANES), 0)
    c = lax.broadcasted_iota(jnp.int32, (2 * ML_HEADS, LANES), 1)
    pick = jnp.where(r == c, 1.0, 0.0).astype(BF16)
    hi, mid, lo = _split3(gc)
    gr_ref[0] = _dot_nt(pick, hi) + _dot_nt(pick, mid) + _dot_nt(pick, lo)
    o_ref[:, og0:] = _sigmoid(og).astype(o_ref.dtype)


def mlstm_projection(x, gain, w, w_gates, gate_bias, conv_w, seq_len, *, tm=ROW_TILE):
    T, D = x.shape
    N = _cols(w)
    tiles_per_seq = seq_len // tm
    halo_blocks = tm // BF16_ROWS
    row_block = lambda n: pl.BlockSpec((tm, n), lambda i: (i, 0))
    specs, operands = zip(*map(_whole, (gain, w, w_gates, gate_bias, conv_w)))
    return pl.pallas_call(
        functools.partial(_mlstm_proj_kernel, tm=tm, tiles_per_seq=tiles_per_seq),
        grid=(T // tm,),
        in_specs=[row_block(D),
                  pl.BlockSpec((BF16_ROWS, D), lambda i: (jnp.maximum(i * halo_blocks - 1, 0), 0)),
                  *specs],
        out_specs=[row_block(N), row_block(LANES),
                   pl.BlockSpec((1, 2 * ML_HEADS, tm), lambda i: (i // tiles_per_seq, 0, i % tiles_per_seq))],
        out_shape=[jax.ShapeDtypeStruct((T, N), BF16), jax.ShapeDtypeStruct((T, LANES), F32),
                   jax.ShapeDtypeStruct((T // seq_len, 2 * ML_HEADS, seq_len), F32)],
        scratch_shapes=[pltpu.VMEM((2 * ML_HEADS * ML_QK // LANES, tm + BF16_ROWS, LANES), F32),
                        pltpu.VMEM((2 * ML_HEADS * ML_QK // LANES, tm, LANES), F32)],
        compiler_params=_params("parallel"),
        name="mlstm_projection",
    )(x, x, *operands)


def _build_vt(v_ref, vt_ref, n_heads, rows):
    S = v_ref.shape[1]
    r = lax.broadcasted_iota(jnp.int32, (LANES, LANES), 0)
    c = lax.broadcasted_iota(jnp.int32, (LANES, LANES), 1)
    eye = jnp.where(r == c, 1.0, 0.0).astype(BF16)
    per_group = LANES // rows
    for g in range(n_heads // per_group):
        for c0 in range(0, S, TRANSPOSE_COLS):
            cs = slice(c0, c0 + TRANSPOSE_COLS)
            vt = _dot_nt(eye, v_ref[0, cs, g * LANES:(g + 1) * LANES]).astype(BF16)
            for k in range(per_group):
                vt_ref[g * per_group + k, 0:rows, cs] = vt[k * rows:(k + 1) * rows]
    for h in range(n_heads):
        vt_ref[h, rows:rows + BF16_ROWS, :] = jnp.ones((BF16_ROWS, S), BF16)


def _softmax_step(s_t, vt, m_ref, acc_ref, idx):
    m_prev = m_ref[idx]
    m_new = jnp.maximum(m_prev, jnp.max(s_t, axis=0, keepdims=True))
    p = jnp.exp2(s_t - m_new).astype(BF16)
    acc_ref[idx] = jnp.exp2(m_prev - m_new) * acc_ref[idx] + _dot(vt, p)
    m_ref[idx] = m_new


def _causal_steps(i, logits, attend, n_streams, buf_a, buf_b):
    def phase(j, src, dst):
        for n in range(n_streams):
            logits(j + 1, dst, n)
            attend(j, src, n, False)

    for n in range(n_streams):
        logits(0, buf_a, n)

    def body(jj, carry):
        phase(2 * jj, buf_a, buf_b)
        phase(2 * jj + 1, buf_b, buf_a)
        return carry

    lax.fori_loop(0, lax.shift_right_logical(i, 1), body, 0)
    odd = lax.rem(i, 2) == 1

    @pl.when(odd)
    def _():
        phase(i - 1, buf_a, buf_b)
        for n in range(n_streams):
            attend(i, buf_b, n, True)

    @pl.when(jnp.logical_not(odd))
    def _():
        for n in range(n_streams):
            attend(i, buf_a, n, True)


def _fox_kernel(q_ref, k_ref, v_ref, fq_ref, fk_ref, o_ref, vt_ref, qc_ref, acc_ref, m_ref, sa_ref, sb_ref, *, tq, tk):
    i = pl.program_id(1)

    @pl.when(i == 0)
    def _():
        _build_vt(v_ref, vt_ref, FOX_HEADS, FOX_DIM)

    lane = lax.broadcasted_iota(jnp.int32, (1, LANES), 1)
    fq = fq_ref[0]
    for h in range(FOX_HEADS):
        pair, half = divmod(h, 2)
        q = q_ref[0, :, pair * LANES:(pair + 1) * LANES]
        in_head = (lane >= half * FOX_DIM) & (lane < (half + 1) * FOX_DIM)
        lo_k, lo_q = FOX_BIAS_K_LANE + F_PIECES * h, FOX_BIAS_Q_LANE + F_PIECES * h
        mine = ((lane >= lo_k) & (lane < lo_k + F_PIECES)) | ((lane >= lo_q) & (lane < lo_q + F_PIECES))
        qc_ref[h, :, 0:LANES] = jnp.where(in_head, q, jnp.zeros_like(q))
        qc_ref[h, :, LANES:2 * LANES] = jnp.where(mine, fq, jnp.zeros_like(fq))
    m_ref[...] = jnp.full(m_ref.shape, NEG_INF, F32)
    acc_ref[...] = jnp.zeros(acc_ref.shape, F32)

    key = lax.broadcasted_iota(jnp.int32, (tk, tq), 0)
    qry = lax.broadcasted_iota(jnp.int32, (tk, tq), 1)
    causal = key <= qry

    def key_rows(j):
        return pl.ds(pl.multiple_of(j * tk, tk), tk)

    def logits(j, buf, h):
        pair = h // 2
        kc = jnp.concatenate([k_ref[0, key_rows(j), pair * LANES:(pair + 1) * LANES], fk_ref[0, key_rows(j), :]],
                             axis=1)
        buf[h] = _dot_nt(kc, qc_ref[h])

    def attend(j, buf, h, masked):
        s_t = buf[h]
        if masked:
            s_t = jnp.where(causal, s_t, NEG_INF)
        _softmax_step(s_t, vt_ref[h, :, key_rows(j)], m_ref, acc_ref, h)

    _causal_steps(i, logits, attend, FOX_HEADS, sa_ref, sb_ref)

    for pair in range(FOX_HEADS // 2):
        halves = []
        for half in range(2):
            a = acc_ref[2 * pair + half]
            halves.append(a[0:FOX_DIM] / a[FOX_DIM:FOX_DIM + 1])
        o_ref[0, :, pair * LANES:(pair + 1) * LANES] = jnp.concatenate(halves, axis=0).T.astype(o_ref.dtype)


def fox_attention(proj, fq, fk, *, tq=ATT_Q_TILE, tk=ATT_K_TILE):
    B, S, _ = proj.shape
    assert tq == tk
    width = FOX_HEADS * FOX_DIM
    rows = FOX_DIM + BF16_ROWS
    return pl.pallas_call(
        functools.partial(_fox_kernel, tq=tq, tk=tk),
        grid=(B, S // tq),
        in_specs=[pl.BlockSpec((1, tq, width), lambda b, i: (b, i, 0)),
                  pl.BlockSpec((1, S, width), lambda b, i: (b, 0, 1)),
                  pl.BlockSpec((1, S, width), lambda b, i: (b, 0, 2)),
                  pl.BlockSpec((1, tq, LANES), lambda b, i: (b, i, 0)),
                  pl.BlockSpec((1, S, LANES), lambda b, i: (b, 0, 0))],
        out_specs=pl.BlockSpec((1, tq, width), lambda b, i: (b, i, 0)),
        out_shape=jax.ShapeDtypeStruct((B, S, width), BF16),
        scratch_shapes=[pltpu.VMEM((FOX_HEADS, rows, S), BF16),
                        pltpu.VMEM((FOX_HEADS, tq, 2 * LANES), BF16),
                        pltpu.VMEM((FOX_HEADS, rows, tq), F32),
                        pltpu.VMEM((FOX_HEADS, 1, tq), F32),
                        pltpu.VMEM((FOX_HEADS, tk, tq), F32),
                        pltpu.VMEM((FOX_HEADS, tk, tq), F32)],
        compiler_params=_params("parallel", "arbitrary"),
        name="fox_attention",
    )(proj, proj, proj, fq, fk)


def _diff_kernel(q_ref, k_ref, v_ref, lq1_ref, lk1_ref, lq2_ref, lk2_ref, sub_ref, o_ref, vt_ref, qc_ref, acc_ref,
                 m_ref, sa_ref, sb_ref, *, tq, tk, lambda_init):
    i = pl.program_id(1)

    @pl.when(i == 0)
    def _():
        _build_vt(v_ref, vt_ref, DIFF_HEADS, DIFF_V)

    lane = lax.broadcasted_iota(jnp.int32, (1, LANES), 1)
    for h in range(DIFF_HEADS):
        q = q_ref[0, :, h * LANES:(h + 1) * LANES]
        zero = jnp.zeros_like(q)
        qc_ref[2 * h] = jnp.where(lane < DIFF_QK, q, zero)
        qc_ref[2 * h + 1] = jnp.where(lane >= DIFF_QK, q, zero)
    m_ref[...] = jnp.full(m_ref.shape, NEG_INF, F32)
    acc_ref[...] = jnp.zeros(acc_ref.shape, F32)

    key = lax.broadcasted_iota(jnp.int32, (tk, tq), 0)
    qry = lax.broadcasted_iota(jnp.int32, (tk, tq), 1)
    visible = key // CHUNK <= qry // CHUNK

    def key_rows(j):
        return pl.ds(pl.multiple_of(j * tk, tk), tk)

    def logits(j, buf, n):
        h = n // 2
        buf[n] = _dot_nt(k_ref[0, key_rows(j), h * LANES:(h + 1) * LANES], qc_ref[n])

    def attend(j, buf, n, masked):
        s_t = buf[n]
        if masked:
            s_t = jnp.where(visible, s_t, NEG_INF)
        _softmax_step(s_t, vt_ref[n // 2, :, key_rows(j)], m_ref, acc_ref, n)

    _causal_steps(i, logits, attend, 2 * DIFF_HEADS, sa_ref, sb_ref)

    lam = (jnp.exp(jnp.sum(lq1_ref[...] * lk1_ref[...], axis=1, keepdims=True))
           - jnp.exp(jnp.sum(lq2_ref[...] * lk2_ref[...], axis=1, keepdims=True)) + lambda_init)
    for h in range(DIFF_HEADS):
        a1, a2 = acc_ref[2 * h], acc_ref[2 * h + 1]
        o_t = a1[0:DIFF_V] / a1[DIFF_V:DIFF_V + 1] - lam * (a2[0:DIFF_V] / a2[DIFF_V:DIFF_V + 1])
        out = _rms(o_t.T, sub_ref[...]) * (1.0 - lambda_init)
        o_ref[0, :, h * LANES:(h + 1) * LANES] = out.astype(o_ref.dtype)


def diff_attention(proj, lq1, lk1, lq2, lk2, subln, lambda_init, *, tq=ATT_Q_TILE, tk=ATT_K_TILE):
    B, S, _ = proj.shape
    assert tq == tk
    width = DIFF_HEADS * DIFF_V
    rows = DIFF_V + BF16_ROWS
    specs, operands = zip(*map(_whole, (lq1, lk1, lq2, lk2, subln)))
    return pl.pallas_call(
        functools.partial(_diff_kernel, tq=tq, tk=tk, lambda_init=lambda_init),
        grid=(B, S // tq),
        in_specs=[pl.BlockSpec((1, tq, width), lambda b, i: (b, i, 3)),
                  pl.BlockSpec((1, S, width), lambda b, i: (b, 0, 4)),
                  pl.BlockSpec((1, S, width), lambda b, i: (b, 0, 5)),
                  *specs],
        out_specs=pl.BlockSpec((1, tq, width), lambda b, i: (b, i, 0)),
        out_shape=jax.ShapeDtypeStruct((B, S, width), BF16),
        scratch_shapes=[pltpu.VMEM((DIFF_HEADS, rows, S), BF16),
                        pltpu.VMEM((2 * DIFF_HEADS, tq, LANES), BF16),
                        pltpu.VMEM((2 * DIFF_HEADS, rows, tq), F32),
                        pltpu.VMEM((2 * DIFF_HEADS, 1, tq), F32),
                        pltpu.VMEM((2 * DIFF_HEADS, tk, tq), F32),
                        pltpu.VMEM((2 * DIFF_HEADS, tk, tq), F32)],
        compiler_params=_params("parallel", "arbitrary"),
        name="diff_attention",
    )(proj, proj, proj, *operands)


def _mlstm_kernel(qk_ref, v_ref, sg_ref, gc_ref, gr_ref, hn_ref, o_ref, ct_ref, m_ref, vt_ref, *, L, nb):
    c = pl.program_id(1)

    @pl.when(c == 0)
    def _():
        ct_ref[...] = jnp.zeros_like(ct_ref)
        m_ref[...] = jnp.zeros_like(m_ref)
        for n in range(nb * ML_HEADS):
            vt_ref[n, ML_V:ML_V + BF16_ROWS, :] = jnp.ones((BF16_ROWS, L), BF16)

    src = lax.broadcasted_iota(jnp.int32, (L, L), 0)
    dst = lax.broadcasted_iota(jnp.int32, (L, L), 1)
    causal = src <= dst
    eye = jnp.where(src == dst, 1.0, 0.0).astype(BF16)
    k0 = ML_HEADS * ML_QK
    for b in range(nb):
        gc = gc_ref[b]
        gr = gr_ref[b]
        for h in range(ML_HEADS):
            n = b * ML_HEADS + h
            q = qk_ref[b, :, h * ML_QK:(h + 1) * ML_QK]
            k = qk_ref[b, :, k0 + h * ML_QK:k0 + (h + 1) * ML_QK]
            vt_ref[n, 0:ML_V, :] = _dot_nt(eye, v_ref[b, :, h * ML_V:(h + 1) * ML_V]).astype(BF16)
            v_t = vt_ref[n]
            r_col = gc[:, h:h + 1] - gc[:, ML_HEADS + h:ML_HEADS + h + 1]
            b_row = gr[ML_HEADS + h:ML_HEADS + h + 1, :]
            r_row = gr[h:h + 1, :] - b_row
            g = b_row[:, L - 1:L]
            ct = ct_ref[n]
            m = m_ref[n][:, 0:1]

            dm = jnp.where(causal, r_col, NEG_INF)
            mt = jnp.maximum(m, jnp.max(dm, axis=0, keepdims=True))
            s_t = _dot_nt(k, q) * jnp.exp2(dm - mt)
            both = jnp.exp2(m - mt) * _dot_nt(ct.astype(BF16), q) + _dot(v_t, s_t.astype(BF16))
            den = both[ML_V:ML_V + 1]
            hh = both[0:ML_V] / jnp.maximum(jnp.abs(den), jnp.exp2(-(b_row + mt)))
            hh = hh * lax.rsqrt(jnp.mean(hh * hh, axis=0, keepdims=True) + RMS_EPS)

            m_next = jnp.maximum(m, jnp.max(r_row, axis=1, keepdims=True))
            kw = (k.astype(F32) * jnp.exp2(r_col - m_next)).astype(BF16)
            ct_ref[n] = jnp.exp2(m - m_next) * ct + _dot(v_t, kw)
            m_ref[n] = jnp.broadcast_to(g + m_next, (1, LANES))

            vs = slice(h * ML_V, (h + 1) * ML_V)
            o_ref[b, :, vs] = (hh.T * hn_ref[:, vs] * sg_ref[b, :, vs].astype(F32)).astype(o_ref.dtype)


def mlstm(proj, g_col, g_row, head_norm, *, L=ML_CHUNK, nb=ML_SEQS):
    B, S, _ = proj.shape
    W = D_MODEL
    assert L == ML_V, "one identity matrix serves the v transposes"
    norm_spec, head_norm = _whole(head_norm)
    return pl.pallas_call(
        functools.partial(_mlstm_kernel, L=L, nb=nb),
        grid=(B // nb, S // L),
        in_specs=[pl.BlockSpec((nb, L, W), lambda b, c: (b, c, 0)),
                  pl.BlockSpec((nb, L, W), lambda b, c: (b, c, 1)),
                  pl.BlockSpec((nb, L, W), lambda b, c: (b, c, 2)),
                  pl.BlockSpec((nb, L, LANES), lambda b, c: (b, c, 0)),
                  pl.BlockSpec((nb, 2 * ML_HEADS, L), lambda b, c: (b, 0, c)),
                  norm_spec],
        out_specs=pl.BlockSpec((nb, L, W), lambda b, c: (b, c, 0)),
        out_shape=jax.ShapeDtypeStruct((B, S, W), BF16),
        scratch_shapes=[pltpu.VMEM((nb * ML_HEADS, ML_V + BF16_ROWS, ML_QK), F32),
                        pltpu.VMEM((nb * ML_HEADS, 1, LANES), F32),
                        pltpu.VMEM((nb * ML_HEADS, ML_V + BF16_ROWS, L), BF16)],
        compiler_params=_params("parallel", "arbitrary"),
        name="mlstm",
    )(proj, proj, proj, g_col, g_row, head_norm)


def _xattn_kernel(*refs, n_in):
    x_ref, g_ref, wq_ref, k_ref, v_ref, wo_ref, o_ref = refs[2 * n_in:]
    x = x_ref[0]
    for a_ref, w_ref in zip(refs[:n_in], refs[n_in:2 * n_in]):
        x = x + _dot(a_ref[0], w_ref[...])
    xn = _rms(x, g_ref[...]).astype(BF16)
    q = (_dot(xn, wq_ref[...]) * (X_DIM ** -0.5)).astype(BF16)
    heads = []
    for h in range(X_HEADS):
        cols = slice(h * X_DIM, (h + 1) * X_DIM)
        s = _dot_nt(q[:, cols], k_ref[0, :, cols])
        p = jnp.exp(s - jnp.max(s, axis=1, keepdims=True))
        l = jnp.sum(p, axis=1, keepdims=True)
        heads.append((_dot(p.astype(BF16), v_ref[0, :, cols]) / l).astype(BF16))
    o_ref[0] = x + _dot(jnp.concatenate(heads, axis=1), wo_ref[...])


def mix_out_cross_attention(acts, w_mix, x, gain, wq, kv, wo, *, tq=X_TILE):
    B, S, D = x.shape
    M = kv.shape[1]
    (gain_spec, gain), (wq_spec, wq), (wo_spec, wo) = map(_whole, (gain, wq, wo))
    return pl.pallas_call(
        functools.partial(_xattn_kernel, n_in=len(acts)),
        grid=(B, S // tq),
        in_specs=([pl.BlockSpec((1, tq, a.shape[2]), lambda b, i: (b, i, 0)) for a in acts]
                  + [w[0] for w in w_mix]
                  + [pl.BlockSpec((1, tq, D), lambda b, i: (b, i, 0)),
                     gain_spec, wq_spec,
                     pl.BlockSpec((1, M, D), lambda b, i: (b, 0, 0)),
                     pl.BlockSpec((1, M, D), lambda b, i: (b, 0, 1)),
                     wo_spec]),
        out_specs=pl.BlockSpec((1, tq, D), lambda b, i: (b, i, 0)),
        out_shape=jax.ShapeDtypeStruct((B, S, D), F32),
        compiler_params=_params("parallel", "parallel"),
        name="mix_out_cross_attention",
    )(*acts, *[w[1] for w in w_mix], x, gain, wq, kv, kv, wo)


def _gelu_tanh(x):
    k = -2.0 * math.sqrt(2.0 / math.pi) * math.log2(math.e)
    return x / (1.0 + jnp.exp2(x * (k * 0.044715 * (x * x) + k)))


def _ffn_kernel(x_ref, halo_ref, g_ref, wup_ref, cw_ref, cb_ref, wd_ref, fg_ref, o_ref, up_ref, act_ref, nat_ref,
                *, tm, tiles_per_seq, final_norm):
    i = pl.program_id(0)
    halo = _rms(halo_ref[...], g_ref[...])
    xe = jnp.concatenate([jnp.where(i % tiles_per_seq == 0, 0.0, halo).astype(BF16),
                          _rms(x_ref[...], g_ref[...]).astype(BF16)], axis=0)

    n_chunks = D_FF // FFN_CHUNK
    slabs = FFN_CHUNK // LANES
    rows = tm // ROW_PHASES

    def up_project(c):
        for half in range(2):
            col0 = half * D_FF + c * FFN_CHUNK
            up = _dot(xe, wup_ref[:, col0:col0 + FFN_CHUNK])
            for s in range(slabs):
                up_ref[c % 2, half, s] = up[:, s * LANES:(s + 1) * LANES]

    def conv(c, half, s, phase):
        col0 = half * D_FF + c * FFN_CHUNK + s * LANES
        cw = cw_ref[:, col0:col0 + LANES]
        out = cb_ref[:, col0:col0 + LANES]
        for tap in range(FFN_CONV):
            first = BF16_ROWS + phase - (FFN_CONV - 1 - tap)
            out = out + cw[tap:tap + 1] * up_ref[c % 2, half, s, pl.ds(first, rows, stride=ROW_PHASES), :]
        return out

    acc = None
    piece_start = 0
    up_project(0)
    for c in range(n_chunks):
        if c + 1 < n_chunks:
            up_project(c + 1)
        for phase in range(ROW_PHASES):
            for s in range(slabs):
                act = _gelu_tanh(conv(c, 0, s, phase)) * conv(c, 1, s, phase)
                act_ref[phase * rows:(phase + 1) * rows,
                        c * FFN_CHUNK + s * LANES:c * FFN_CHUNK + (s + 1) * LANES] = act.astype(BF16)
        if (c + 1) % DOWN_CHUNKS == 0 or c + 1 == n_chunks:
            piece = slice(piece_start * FFN_CHUNK, (c + 1) * FFN_CHUNK)
            part = _dot(act_ref[:, piece], wd_ref[piece, :])
            acc = part if acc is None else acc + part
            piece_start = c + 1

    for phase in range(ROW_PHASES):
        for s in range(D_MODEL // LANES):
            nat_ref[s, pl.ds(phase, rows, stride=ROW_PHASES), :] = acc[phase * rows:(phase + 1) * rows,
                                                                       s * LANES:(s + 1) * LANES]
    y = x_ref[...] + jnp.concatenate([nat_ref[s] for s in range(D_MODEL // LANES)], axis=1)
    o_ref[...] = _rms(y, fg_ref[...]) if final_norm else y


def conv_ffn(x, gain, w_up, conv_w, conv_b, w_down, seq_len, final_gain=None, *, tm=ROW_TILE):
    T, D = x.shape
    halo_blocks = tm // BF16_ROWS
    final_norm = final_gain is not None
    specs, operands = zip(*map(_whole, (gain, w_up, conv_w, conv_b, w_down, final_gain if final_norm else gain)))
    return pl.pallas_call(
        functools.partial(_ffn_kernel, tm=tm, tiles_per_seq=seq_len // tm, final_norm=final_norm),
        grid=(T // tm,),
        in_specs=[pl.BlockSpec((tm, D), lambda i: (i, 0)),
                  pl.BlockSpec((BF16_ROWS, D), lambda i: (jnp.maximum(i * halo_blocks - 1, 0), 0)),
                  *specs],
        out_specs=pl.BlockSpec((tm, D), lambda i: (i, 0)),
        out_shape=jax.ShapeDtypeStruct((T, D), F32),
        scratch_shapes=[pltpu.VMEM((2, 2, FFN_CHUNK // LANES, tm + BF16_ROWS, LANES), F32),
                        pltpu.VMEM((tm, D_FF), BF16),
                        pltpu.VMEM((D // LANES, tm, LANES), F32)],
        compiler_params=_params("parallel"),
        name="conv_ffn",
    )(x, x, *operands)


def _pad_lanes(a):
    return jnp.pad(a, ((0, 0), (0, LANES - a.shape[1])))


def _fox_diff_mixer(x, B, S, j, gain, w_in, fox_bf, lq1, lk1, lq2, lk2, subln, w_out, lambda_init):
    fw = FOX_HEADS * FOX_DIM
    g0 = 3 * fw
    dq0 = g0 + FOX_HEADS
    dqw = DIFF_HEADS * 2 * DIFF_QK
    w_main = jnp.concatenate([w_in[:, :fw] * (FOX_DIM ** -0.5 * LOG2_E), w_in[:, fw:g0],
                              w_in[:, dq0:dq0 + dqw] * (DIFF_QK ** -0.5 * LOG2_E), w_in[:, dq0 + dqw:]],
                             axis=1).astype(BF16)
    w_gates = _pad_lanes(w_in[:, g0:g0 + FOX_HEADS]).astype(BF16)
    proj, fq, fk = fox_projection(x, gain, w_main, w_gates, _pad_lanes(fox_bf.reshape(1, -1)), S)
    proj = proj.reshape(B, S, -1)
    fox = fox_attention(proj, fq.reshape(B, S, LANES), fk.reshape(B, S, LANES))
    dif = diff_attention(proj, lq1, lk1, lq2, lk2, subln, lambda_init)
    row_block = lambda r: (pl.BlockSpec((None, fw, D_MODEL), lambda *_: (j, r, 0), pipeline_mode=pl.Buffered(1)), w_out)
    return [fox, dif], [row_block(0), row_block(1)]


def _mlstm_mixer(x, B, S, j, gain, w_in, conv_qk, b_i, b_f, head_norm, w_out):
    g0 = 2 * ML_HEADS * ML_QK + ML_HEADS * ML_V
    w_main = jnp.concatenate([w_in[:, :g0], w_in[:, g0 + 2 * ML_HEADS:]], axis=1).astype(BF16)
    w_gates = _pad_lanes(w_in[:, g0:g0 + 2 * ML_HEADS]).astype(BF16)
    gate_bias = _pad_lanes(jnp.concatenate([b_i, b_f]).reshape(1, -1))
    proj, g_col, g_row = mlstm_projection(x, gain, w_main, w_gates, gate_bias, conv_qk, S)
    h = mlstm(proj.reshape(B, S, -1), g_col.reshape(B, S, LANES), g_row, head_norm)
    return [h], [_whole((w_out, j))]


def kernel(x, mem, mix_norm, xattn_norm, mem_norm, ffn_norm, attn_w_in, attn_fox_bf, diff_lq1, diff_lk1, diff_lq2, diff_lk2, diff_subln, attn_w_out, mlstm_w_in, mlstm_conv_qk, mlstm_b_i, mlstm_b_f, mlstm_head_norm, mlstm_w_out, xattn_wq, xattn_wkv, xattn_wo, ffn_w_up, ffn_conv_w, ffn_conv_b, ffn_w_down, final_norm):
    B, S, D = x.shape
    M = mem.shape[1]
    depth = mix_norm.shape[0]
    x = x.reshape(B * S, D)
    mem2 = mem.reshape(B * M, D)
    rows = lambda a: a.reshape(a.shape[0], 1, -1)
    to_bf16 = lambda a: a.astype(BF16)
    mix_norm, xattn_norm, mem_norm, ffn_norm = map(rows, (mix_norm, xattn_norm, mem_norm, ffn_norm))
    diff_lq1, diff_lk1, diff_lq2, diff_lk2, diff_subln = map(rows, (diff_lq1, diff_lk1, diff_lq2, diff_lk2, diff_subln))
    mlstm_head_norm, ffn_conv_b = rows(mlstm_head_norm), rows(ffn_conv_b)
    attn_w_out, mlstm_w_out, xattn_wq, xattn_wkv, xattn_wo, ffn_w_up, ffn_w_down = map(
        to_bf16, (attn_w_out, mlstm_w_out, xattn_wq, xattn_wkv, xattn_wo, ffn_w_up, ffn_w_down))
    for layer in range(depth):
        j = layer // 2
        if layer % 2 == 0:
            lambda_init = 0.8 - 0.6 * math.exp(-0.3 * layer)
            mixed, w_mix = _fox_diff_mixer(x, B, S, j, (mix_norm, layer), attn_w_in[j], attn_fox_bf[j],
                                           (diff_lq1, j), (diff_lk1, j), (diff_lq2, j), (diff_lk2, j),
                                           (diff_subln, j), attn_w_out, lambda_init)
        else:
            mixed, w_mix = _mlstm_mixer(x, B, S, j, (mix_norm, layer), mlstm_w_in[j], (mlstm_conv_qk, j),
                                        mlstm_b_i[j], mlstm_b_f[j], (mlstm_head_norm, j), mlstm_w_out)
        kv = norm_matmul(mem2, (mem_norm, layer), (xattn_wkv, layer)).reshape(B, M, 2 * D)
        x = mix_out_cross_attention(mixed, w_mix, x.reshape(B, S, D), (xattn_norm, layer), (xattn_wq, layer), kv,
                                    (xattn_wo, layer)).reshape(B * S, D)
        x = conv_ffn(x, (ffn_norm, layer), (ffn_w_up, layer), (ffn_conv_w, layer), (ffn_conv_b, layer),
                     (ffn_w_down, layer), S, final_norm.reshape(1, D) if layer == depth - 1 else None)
    return x.reshape(B, S, D)
```

```python
import functools
import math

import jax
import jax.numpy as jnp
from jax import lax
from jax.experimental import pallas as pl
from jax.experimental.pallas import tpu as pltpu

F32 = jnp.float32
BF16 = jnp.bfloat16

D_MODEL = 1024
RMS_EPS = 1e-6
NEG_INF = -1e30
CHUNK = 64
FOX_HEADS, FOX_DIM = 8, 64
DIFF_HEADS, DIFF_QK, DIFF_V = 4, 64, 128
ML_HEADS, ML_QK, ML_V, ML_CONV = 4, 128, 256, 4
X_HEADS, X_DIM = 4, 256
D_FF = 2816
FFN_CONV = 3
LANES = 128
BF16_ROWS = 16
VMEM_LIMIT = 56 * 1024 * 1024

ROW_TILE = 512
ATT_Q_TILE = 512
ATT_K_TILE = 512
LOG2_E = math.log2(math.e)
ML_CHUNK = 256
ML_SEQS = 2
X_TILE = 512
PROJ_COLS = 1024
FFN_CHUNK = 256
ROW_PHASES = 4
DOWN_CHUNKS = 4
TRANSPOSE_COLS = 512
F_PIECES = 3
FOX_BIAS_K_LANE = 0
FOX_BIAS_Q_LANE = 32


def _params(*sem):
    return pltpu.CompilerParams(dimension_semantics=sem, vmem_limit_bytes=VMEM_LIMIT)


def _resident(shape):
    return pl.BlockSpec(shape, lambda *_: (0,) * len(shape), pipeline_mode=pl.Buffered(1))


def _whole(a):
    if isinstance(a, tuple):
        arr, layer = a
        tail = arr.shape[1:]
        spec = pl.BlockSpec((None,) + tail, lambda *_: (layer,) + (0,) * len(tail), pipeline_mode=pl.Buffered(1))
        return spec, arr
    return _resident(a.shape), a


def _cols(a):
    return (a[0] if isinstance(a, tuple) else a).shape[-1]


def _rms(x, gain):
    return x * lax.rsqrt(jnp.mean(x * x, axis=-1, keepdims=True) + RMS_EPS) * gain


def _sigmoid(x):
    return 1.0 / (1.0 + jnp.exp(-x))


def _dot(a, b):
    return jnp.dot(a, b, preferred_element_type=F32)


def _dot_nt(a, b):
    return lax.dot_general(a, b, (((1,), (1,)), ((), ())), preferred_element_type=F32)


def _split3(x):
    hi = x.astype(BF16)
    r1 = x - hi.astype(F32)
    mid = r1.astype(BF16)
    return hi, mid, (r1 - mid.astype(F32)).astype(BF16)


def _select_dot(sel, x):
    hi, mid, lo = _split3(x)
    return _dot(sel, hi) + _dot(sel, mid) + _dot(sel, lo)


def _block_tri(n, block):
    r = lax.broadcasted_iota(jnp.int32, (n, n), 0)
    c = lax.broadcasted_iota(jnp.int32, (n, n), 1)
    return jnp.where((c <= r) & (r // block == c // block), 1.0, 0.0).astype(BF16)


def _log_sigmoid(x):
    return jnp.minimum(x, 0.0) - jnp.log1p(jnp.exp(-jnp.abs(x)))


def _fox_proj_kernel(x_ref, g_ref, w_ref, wg_ref, bf_ref, o_ref, fq_ref, fk_ref, carry_ref, *, tm, tn, tiles_per_seq):
    i = pl.program_id(0)

    @pl.when(i % tiles_per_seq == 0)
    def _():
        carry_ref[...] = jnp.zeros_like(carry_ref)

    xn = _rms(x_ref[...], g_ref[...]).astype(BF16)

    def project(part):
        width = o_ref.shape[1] // 3
        for c0 in range(part * width, (part + 1) * width, tn):
            o_ref[:, c0:c0 + tn] = _dot(xn, w_ref[:, c0:c0 + tn]).astype(o_ref.dtype)

    gates = _dot(xn, wg_ref[...])
    project(0)
    log_f = _log_sigmoid(gates + bf_ref[...]) * LOG2_E
    f = carry_ref[...] + _select_dot(_block_tri(tm, tm), log_f)
    carry_ref[...] = f[tm - 1:tm, :]
    project(1)

    src = lax.broadcasted_iota(jnp.int32, (LANES, LANES), 0)
    dst = lax.broadcasted_iota(jnp.int32, (LANES, LANES), 1)
    lane = lax.broadcasted_iota(jnp.int32, (1, LANES), 1)
    n_bias = F_PIECES * FOX_HEADS
    ones_q = jnp.where((lane >= FOX_BIAS_K_LANE) & (lane < FOX_BIAS_K_LANE + n_bias), 1.0, 0.0)
    ones_k = jnp.where((lane >= FOX_BIAS_Q_LANE) & (lane < FOX_BIAS_Q_LANE + n_bias), 1.0, 0.0)
    both = jnp.concatenate([ones_q, ones_k], axis=1)
    for c, piece in enumerate(_split3(f)):
        head = src < FOX_HEADS
        to_q = jnp.where(head & (dst == FOX_BIAS_Q_LANE + F_PIECES * src + c), 1.0, 0.0)
        to_k = jnp.where(head & (dst == FOX_BIAS_K_LANE + F_PIECES * src + c), -1.0, 0.0)
        both = both + _dot(piece, jnp.concatenate([to_q, to_k], axis=1).astype(BF16))
    fq_ref[...] = both[:, 0:LANES].astype(BF16)
    fk_ref[...] = both[:, LANES:2 * LANES].astype(BF16)
    project(2)


def fox_projection(x, gain, w, w_gates, gate_bias, seq_len, *, tm=ROW_TILE, tn=PROJ_COLS):
    T, D = x.shape
    N = _cols(w)
    row_block = lambda n: pl.BlockSpec((tm, n), lambda i: (i, 0))
    specs, operands = zip(*map(_whole, (gain, w, w_gates, gate_bias)))
    return pl.pallas_call(
        functools.partial(_fox_proj_kernel, tm=tm, tn=tn, tiles_per_seq=seq_len // tm),
        grid=(T // tm,),
        in_specs=[row_block(D), *specs],
        out_specs=[row_block(N), row_block(LANES), row_block(LANES)],
        out_shape=[jax.ShapeDtypeStruct((T, N), BF16), jax.ShapeDtypeStruct((T, LANES), BF16),
                   jax.ShapeDtypeStruct((T, LANES), BF16)],
        scratch_shapes=[pltpu.VMEM((1, LANES), F32)],
        compiler_params=_params("arbitrary"),
        name="fox_projection",
    )(x, *operands)


def _mlstm_proj_kernel(x_ref, halo_ref, g_ref, w_ref, wg_ref, gb_ref, cw_ref, o_ref, gc_ref, gr_ref, up_ref, nat_ref,
                       *, tm, tiles_per_seq):
    i = pl.program_id(0)
    halo = _rms(halo_ref[...], g_ref[...])
    xe = jnp.concatenate([jnp.where(i % tiles_per_seq == 0, 0.0, halo).astype(BF16),
                          _rms(x_ref[...], g_ref[...]).astype(BF16)], axis=0)
    xn = xe[BF16_ROWS:]

    qk_slabs = 2 * ML_HEADS * ML_QK // LANES
    rows = tm // ROW_PHASES
    v0 = qk_slabs * LANES
    og0 = v0 + ML_HEADS * ML_V
    for s0 in range(0, qk_slabs, 2):
        up = _dot(xe, w_ref[:, s0 * LANES:(s0 + 2) * LANES])
        up_ref[s0] = up[:, 0:LANES]
        up_ref[s0 + 1] = up[:, LANES:2 * LANES]
    pre = _dot(xn, wg_ref[...]) + gb_ref[...]
    o_ref[:, v0:og0] = _dot(xn, w_ref[:, v0:og0]).astype(o_ref.dtype)
    for s in range(qk_slabs):
        cw = cw_ref[:, s * LANES:(s + 1) * LANES]
        for phase in range(ROW_PHASES):
            y = None
            for tap in range(ML_CONV):
                first = BF16_ROWS + phase - (ML_CONV - 1 - tap)
                term = cw[tap:tap + 1] * up_ref[s, pl.ds(first, rows, stride=ROW_PHASES), :]
                y = term if y is None else y + term
            y = y * _sigmoid(y)
            if s < qk_slabs // 2:
                y = y * (ML_QK ** -0.5)
            nat_ref[s, pl.ds(phase, rows, stride=ROW_PHASES), :] = y
        o_ref[:, s * LANES:(s + 1) * LANES] = nat_ref[s].astype(o_ref.dtype)

    og = _dot(xn, w_ref[:, og0:])
    lane = lax.broadcasted_iota(jnp.int32, (1, LANES), 1)
    val = jnp.where(lane < ML_HEADS, pre, _log_sigmoid(pre)) * LOG2_E
    gc = jnp.where(lane < ML_HEADS, val, _select_dot(_block_tri(tm, ML_CHUNK), val))
    gc_ref[...] = gc
    r = lax.broadcasted_iota(jnp.int32, (2 * ML_HEADS, LANES), 0)
    c = lax.broadcasted_iota(jnp.int32, (2 * ML_HEADS, LANES), 1)
    pick = jnp.where(r == c, 1.0, 0.0).astype(BF16)
    hi, mid, lo = _split3(gc)
    gr_ref[0] = _dot_nt(pick, hi) + _dot_nt(pick, mid) + _dot_nt(pick, lo)
    o_ref[:, og0:] = _sigmoid(og).astype(o_ref.dtype)


def mlstm_projection(x, gain, w, w_gates, gate_bias, conv_w, seq_len, *, tm=ROW_TILE):
    T, D = x.shape
    N = _cols(w)
    tiles_per_seq = seq_len // tm
    halo_blocks = tm // BF16_ROWS
    row_block = lambda n: pl.BlockSpec((tm, n), lambda i: (i, 0))
    specs, operands = zip(*map(_whole, (gain, w, w_gates, gate_bias, conv_w)))
    return pl.pallas_call(
        functools.partial(_mlstm_proj_kernel, tm=tm, tiles_per_seq=tiles_per_seq),
        grid=(T // tm,),
        in_specs=[row_block(D),
                  pl.BlockSpec((BF16_ROWS, D), lambda i: (jnp.maximum(i * halo_blocks - 1, 0), 0)),
                  *specs],
        out_specs=[row_block(N), row_block(LANES),
                   pl.BlockSpec((1, 2 * ML_HEADS, tm), lambda i: (i // tiles_per_seq, 0, i % tiles_per_seq))],
        out_shape=[jax.ShapeDtypeStruct((T, N), BF16), jax.ShapeDtypeStruct((T, LANES), F32),
                   jax.ShapeDtypeStruct((T // seq_len, 2 * ML_HEADS, seq_len), F32)],
        scratch_shapes=[pltpu.VMEM((2 * ML_HEADS * ML_QK // LANES, tm + BF16_ROWS, LANES), F32),
                        pltpu.VMEM((2 * ML_HEADS * ML_QK // LANES, tm, LANES), F32)],
        compiler_params=_params("parallel"),
        name="mlstm_projection",
    )(x, x, *operands)


def _build_vt(v_ref, vt_ref, n_heads, rows):
    S = v_ref.shape[1]
    r = lax.broadcasted_iota(jnp.int32, (LANES, LANES), 0)
    c = lax.broadcasted_iota(jnp.int32, (LANES, LANES), 1)
    eye = jnp.where(r == c, 1.0, 0.0).astype(BF16)
    per_group = LANES // rows
    for g in range(n_heads // per_group):
        for c0 in range(0, S, TRANSPOSE_COLS):
            cs = slice(c0, c0 + TRANSPOSE_COLS)
            vt = _dot_nt(eye, v_ref[0, cs, g * LANES:(g + 1) * LANES]).astype(BF16)
            for k in range(per_group):
                vt_ref[g * per_group + k, 0:rows, cs] = vt[k * rows:(k + 1) * rows]
    for h in range(n_heads):
        vt_ref[h, rows:rows + BF16_ROWS, :] = jnp.ones((BF16_ROWS, S), BF16)


def _softmax_step(s_t, vt, m_ref, acc_ref, idx, queries=slice(None)):
    m_prev = m_ref[idx, :, queries]
    m_new = jnp.maximum(m_prev, jnp.max(s_t, axis=0, keepdims=True))
    p = jnp.exp2(s_t - m_new).astype(BF16)
    acc_ref[idx, :, queries] = jnp.exp2(m_prev - m_new) * acc_ref[idx, :, queries] + _dot(vt, p)
    m_ref[idx, :, queries] = m_new


def _causal_steps(i, logits, attend, n_streams, buf_a, buf_b):
    def phase(j, src, dst, next_is_diagonal=False):
        for n in range(n_streams):
            logits(j + 1, dst, n, next_is_diagonal)
            attend(j, src, n, False)

    for n in range(n_streams):
        logits(0, buf_a, n, False)

    def body(jj, carry):
        phase(2 * jj, buf_a, buf_b)
        phase(2 * jj + 1, buf_b, buf_a)
        return carry

    lax.fori_loop(0, lax.shift_right_logical(i, 1), body, 0)
    odd = lax.rem(i, 2) == 1

    @pl.when(odd)
    def _():
        phase(i - 1, buf_a, buf_b, True)
        for n in range(n_streams):
            attend(i, buf_b, n, True)

    @pl.when(jnp.logical_not(odd))
    def _():
        for n in range(n_streams):
            attend(i, buf_a, n, True)


def _fox_kernel(q_ref, k_ref, v_ref, fq_ref, fk_ref, o_ref, vt_ref, qc_ref, acc_ref, m_ref, sa_ref, sb_ref, *, tq, tk):
    i = pl.program_id(1)

    @pl.when(i == 0)
    def _():
        _build_vt(v_ref, vt_ref, FOX_HEADS, FOX_DIM)

    lane = lax.broadcasted_iota(jnp.int32, (1, LANES), 1)
    fq = fq_ref[0]
    for h in range(FOX_HEADS):
        pair, half = divmod(h, 2)
        q = q_ref[0, :, pair * LANES:(pair + 1) * LANES]
        in_head = (lane >= half * FOX_DIM) & (lane < (half + 1) * FOX_DIM)
        lo_k, lo_q = FOX_BIAS_K_LANE + F_PIECES * h, FOX_BIAS_Q_LANE + F_PIECES * h
        mine = ((lane >= lo_k) & (lane < lo_k + F_PIECES)) | ((lane >= lo_q) & (lane < lo_q + F_PIECES))
        qc_ref[h, :, 0:LANES] = jnp.where(in_head, q, jnp.zeros_like(q))
        qc_ref[h, :, LANES:2 * LANES] = jnp.where(mine, fq, jnp.zeros_like(fq))
    m_ref[...] = jnp.full(m_ref.shape, NEG_INF, F32)
    acc_ref[...] = jnp.zeros(acc_ref.shape, F32)

    half = tk // 2
    key = lax.broadcasted_iota(jnp.int32, (half, tq), 0)
    qry = lax.broadcasted_iota(jnp.int32, (half, tq), 1)
    causal = key <= qry

    def key_rows(j, part=None):
        if part is None:
            return pl.ds(pl.multiple_of(j * tk, tk), tk)
        return pl.ds(pl.multiple_of(j * tk + part * half, half), half)

    def logits(j, buf, h, diagonal):
        pair = h // 2
        kc = jnp.concatenate([k_ref[0, key_rows(j), pair * LANES:(pair + 1) * LANES], fk_ref[0, key_rows(j), :]],
                             axis=1)
        if diagonal:
            buf[h, 0:half, :] = _dot_nt(kc[0:half], qc_ref[h])
            buf[h, half:tk, half:tq] = _dot_nt(kc[half:tk], qc_ref[h, half:tq, :])
        else:
            buf[h] = _dot_nt(kc, qc_ref[h])

    def attend(j, buf, h, diagonal):
        if diagonal:
            _softmax_step(jnp.where(causal, buf[h, 0:half, :], NEG_INF), vt_ref[h, :, key_rows(j, 0)],
                          m_ref, acc_ref, h)
            _softmax_step(jnp.where(causal[:, 0:tq - half], buf[h, half:tk, half:tq], NEG_INF),
                          vt_ref[h, :, key_rows(j, 1)], m_ref, acc_ref, h, slice(half, tq))
        else:
            _softmax_step(buf[h], vt_ref[h, :, key_rows(j)], m_ref, acc_ref, h)

    _causal_steps(i, logits, attend, FOX_HEADS, sa_ref, sb_ref)

    for pair in range(FOX_HEADS // 2):
        halves = []
        for half in range(2):
            a = acc_ref[2 * pair + half]
            halves.append(a[0:FOX_DIM] / a[FOX_DIM:FOX_DIM + 1])
        o_ref[0, :, pair * LANES:(pair + 1) * LANES] = jnp.concatenate(halves, axis=0).T.astype(o_ref.dtype)


def fox_attention(proj, fq, fk, *, tq=ATT_Q_TILE, tk=ATT_K_TILE):
    B, S, _ = proj.shape
    assert tq == tk
    width = FOX_HEADS * FOX_DIM
    rows = FOX_DIM + BF16_ROWS
    return pl.pallas_call(
        functools.partial(_fox_kernel, tq=tq, tk=tk),
        grid=(B, S // tq),
        in_specs=[pl.BlockSpec((1, tq, width), lambda b, i: (b, i, 0)),
                  pl.BlockSpec((1, S, width), lambda b, i: (b, 0, 1)),
                  pl.BlockSpec((1, S, width), lambda b, i: (b, 0, 2)),
                  pl.BlockSpec((1, tq, LANES), lambda b, i: (b, i, 0)),
                  pl.BlockSpec((1, S, LANES), lambda b, i: (b, 0, 0))],
        out_specs=pl.BlockSpec((1, tq, width), lambda b, i: (b, i, 0)),
        out_shape=jax.ShapeDtypeStruct((B, S, width), BF16),
        scratch_shapes=[pltpu.VMEM((FOX_HEADS, rows, S), BF16),
                        pltpu.VMEM((FOX_HEADS, tq, 2 * LANES), BF16),
                        pltpu.VMEM((FOX_HEADS, rows, tq), F32),
                        pltpu.VMEM((FOX_HEADS, 1, tq), F32),
                        pltpu.VMEM((FOX_HEADS, tk, tq), F32),
                        pltpu.VMEM((FOX_HEADS, tk, tq), F32)],
        compiler_params=_params("parallel", "arbitrary"),
        name="fox_attention",
    )(proj, proj, proj, fq, fk)


def _diff_kernel(q_ref, k_ref, v_ref, lq1_ref, lk1_ref, lq2_ref, lk2_ref, sub_ref, o_ref, vt_ref, qc_ref, acc_ref,
                 m_ref, sa_ref, sb_ref, *, tq, tk, lambda_init):
    i = pl.program_id(1)

    @pl.when(i == 0)
    def _():
        _build_vt(v_ref, vt_ref, DIFF_HEADS, DIFF_V)

    lane = lax.broadcasted_iota(jnp.int32, (1, LANES), 1)
    for h in range(DIFF_HEADS):
        q = q_ref[0, :, h * LANES:(h + 1) * LANES]
        zero = jnp.zeros_like(q)
        qc_ref[2 * h] = jnp.where(lane < DIFF_QK, q, zero)
        qc_ref[2 * h + 1] = jnp.where(lane >= DIFF_QK, q, zero)
    m_ref[...] = jnp.full(m_ref.shape, NEG_INF, F32)
    acc_ref[...] = jnp.zeros(acc_ref.shape, F32)

    half = tk // 2
    key = lax.broadcasted_iota(jnp.int32, (half, tq), 0)
    qry = lax.broadcasted_iota(jnp.int32, (half, tq), 1)
    visible = key // CHUNK <= qry // CHUNK

    def key_rows(j, part=None):
        if part is None:
            return pl.ds(pl.multiple_of(j * tk, tk), tk)
        return pl.ds(pl.multiple_of(j * tk + part * half, half), half)

    def logits(j, buf, n, diagonal):
        k = k_ref[0, key_rows(j), (n // 2) * LANES:(n // 2 + 1) * LANES]
        if diagonal:
            buf[n, 0:half, :] = _dot_nt(k[0:half], qc_ref[n])
            buf[n, half:tk, half:tq] = _dot_nt(k[half:tk], qc_ref[n, half:tq, :])
        else:
            buf[n] = _dot_nt(k, qc_ref[n])

    def attend(j, buf, n, diagonal):
        vt = vt_ref.at[n // 2]
        if diagonal:
            _softmax_step(jnp.where(visible, buf[n, 0:half, :], NEG_INF), vt[:, key_rows(j, 0)], m_ref, acc_ref, n)
            _softmax_step(jnp.where(visible[:, 0:tq - half], buf[n, half:tk, half:tq], NEG_INF),
                          vt[:, key_rows(j, 1)], m_ref, acc_ref, n, slice(half, tq))
        else:
            _softmax_step(buf[n], vt[:, key_rows(j)], m_ref, acc_ref, n)

    _causal_steps(i, logits, attend, 2 * DIFF_HEADS, sa_ref, sb_ref)

    lam = (jnp.exp(jnp.sum(lq1_ref[...] * lk1_ref[...], axis=1, keepdims=True))
           - jnp.exp(jnp.sum(lq2_ref[...] * lk2_ref[...], axis=1, keepdims=True)) + lambda_init)
    for h in range(DIFF_HEADS):
        a1, a2 = acc_ref[2 * h], acc_ref[2 * h + 1]
        o_t = a1[0:DIFF_V] / a1[DIFF_V:DIFF_V + 1] - lam * (a2[0:DIFF_V] / a2[DIFF_V:DIFF_V + 1])
        out = _rms(o_t.T, sub_ref[...]) * (1.0 - lambda_init)
        o_ref[0, :, h * LANES:(h + 1) * LANES] = out.astype(o_ref.dtype)


def diff_attention(proj, lq1, lk1, lq2, lk2, subln, lambda_init, *, tq=ATT_Q_TILE, tk=ATT_K_TILE):
    B, S, _ = proj.shape
    assert tq == tk
    width = DIFF_HEADS * DIFF_V
    rows = DIFF_V + BF16_ROWS
    specs, operands = zip(*map(_whole, (lq1, lk1, lq2, lk2, subln)))
    return pl.pallas_call(
        functools.partial(_diff_kernel, tq=tq, tk=tk, lambda_init=lambda_init),
        grid=(B, S // tq),
        in_specs=[pl.BlockSpec((1, tq, width), lambda b, i: (b, i, 3)),
                  pl.BlockSpec((1, S, width), lambda b, i: (b, 0, 4)),
                  pl.BlockSpec((1, S, width), lambda b, i: (b, 0, 5)),
                  *specs],
        out_specs=pl.BlockSpec((1, tq, width), lambda b, i: (b, i, 0)),
        out_shape=jax.ShapeDtypeStruct((B, S, width), BF16),
        scratch_shapes=[pltpu.VMEM((DIFF_HEADS, rows, S), BF16),
                        pltpu.VMEM((2 * DIFF_HEADS, tq, LANES), BF16),
                        pltpu.VMEM((2 * DIFF_HEADS, rows, tq), F32),
                        pltpu.VMEM((2 * DIFF_HEADS, 1, tq), F32),
                        pltpu.VMEM((2 * DIFF_HEADS, tk, tq), F32),
                        pltpu.VMEM((2 * DIFF_HEADS, tk, tq), F32)],
        compiler_params=_params("parallel", "arbitrary"),
        name="diff_attention",
    )(proj, proj, proj, *operands)


def _mlstm_kernel(qk_ref, v_ref, sg_ref, gc_ref, gr_ref, hn_ref, o_ref, ct_ref, m_ref, vt_ref, *, L, nb):
    c = pl.program_id(1)

    @pl.when(c == 0)
    def _():
        ct_ref[...] = jnp.zeros_like(ct_ref)
        m_ref[...] = jnp.zeros_like(m_ref)
        for n in range(nb * ML_HEADS):
            vt_ref[n, ML_V:ML_V + BF16_ROWS, :] = jnp.ones((BF16_ROWS, L), BF16)

    src = lax.broadcasted_iota(jnp.int32, (L, L), 0)
    dst = lax.broadcasted_iota(jnp.int32, (L, L), 1)
    causal = src <= dst
    eye = jnp.where(src == dst, 1.0, 0.0).astype(BF16)
    k0 = ML_HEADS * ML_QK
    for b in range(nb):
        gc = gc_ref[b]
        gr = gr_ref[b]
        for h in range(ML_HEADS):
            n = b * ML_HEADS + h
            q = qk_ref[b, :, h * ML_QK:(h + 1) * ML_QK]
            k = qk_ref[b, :, k0 + h * ML_QK:k0 + (h + 1) * ML_QK]
            vt_ref[n, 0:ML_V, :] = _dot_nt(eye, v_ref[b, :, h * ML_V:(h + 1) * ML_V]).astype(BF16)
            v_t = vt_ref[n]
            r_col = gc[:, h:h + 1] - gc[:, ML_HEADS + h:ML_HEADS + h + 1]
            b_row = gr[ML_HEADS + h:ML_HEADS + h + 1, :]
            r_row = gr[h:h + 1, :] - b_row
            g = b_row[:, L - 1:L]
            ct = ct_ref[n]
            m = m_ref[n][:, 0:1]

            dm = jnp.where(causal, r_col, NEG_INF)
            mt = jnp.maximum(m, jnp.max(dm, axis=0, keepdims=True))
            s_t = _dot_nt(k, q) * jnp.exp2(dm - mt)
            both = jnp.exp2(m - mt) * _dot_nt(ct.astype(BF16), q) + _dot(v_t, s_t.astype(BF16))
            den = both[ML_V:ML_V + 1]
            hh = both[0:ML_V] / jnp.maximum(jnp.abs(den), jnp.exp2(-(b_row + mt)))
            hh = hh * lax.rsqrt(jnp.mean(hh * hh, axis=0, keepdims=True) + RMS_EPS)

            m_next = jnp.maximum(m, jnp.max(r_row, axis=1, keepdims=True))
            kw = (k.astype(F32) * jnp.exp2(r_col - m_next)).astype(BF16)
            ct_ref[n] = jnp.exp2(m - m_next) * ct + _dot(v_t, kw)
            m_ref[n] = jnp.broadcast_to(g + m_next, (1, LANES))

            vs = slice(h * ML_V, (h + 1) * ML_V)
            o_ref[b, :, vs] = (hh.T * hn_ref[:, vs] * sg_ref[b, :, vs].astype(F32)).astype(o_ref.dtype)


def mlstm(proj, g_col, g_row, head_norm, *, L=ML_CHUNK, nb=ML_SEQS):
    B, S, _ = proj.shape
    W = D_MODEL
    assert L == ML_V, "one identity matrix serves the v transposes"
    norm_spec, head_norm = _whole(head_norm)
    return pl.pallas_call(
        functools.partial(_mlstm_kernel, L=L, nb=nb),
        grid=(B // nb, S // L),
        in_specs=[pl.BlockSpec((nb, L, W), lambda b, c: (b, c, 0)),
                  pl.BlockSpec((nb, L, W), lambda b, c: (b, c, 1)),
                  pl.BlockSpec((nb, L, W), lambda b, c: (b, c, 2)),
                  pl.BlockSpec((nb, L, LANES), lambda b, c: (b, c, 0)),
                  pl.BlockSpec((nb, 2 * ML_HEADS, L), lambda b, c: (b, 0, c)),
                  norm_spec],
        out_specs=pl.BlockSpec((nb, L, W), lambda b, c: (b, c, 0)),
        out_shape=jax.ShapeDtypeStruct((B, S, W), BF16),
        scratch_shapes=[pltpu.VMEM((nb * ML_HEADS, ML_V + BF16_ROWS, ML_QK), F32),
                        pltpu.VMEM((nb * ML_HEADS, 1, LANES), F32),
                        pltpu.VMEM((nb * ML_HEADS, ML_V + BF16_ROWS, L), BF16)],
        compiler_params=_params("parallel", "arbitrary"),
        name="mlstm",
    )(proj, proj, proj, g_col, g_row, head_norm)


def _xattn_kernel(*refs, n_in):
    x_ref, g_ref, wq_ref, mem_ref, gm_ref, wkv_ref, wo_ref, o_ref, kv_ref = refs[2 * n_in:]

    @pl.when(pl.program_id(1) == 0)
    def _():
        memn = _rms(mem_ref[0], gm_ref[...]).astype(BF16)
        for c0 in range(0, kv_ref.shape[1], PROJ_COLS):
            kv_ref[:, c0:c0 + PROJ_COLS] = _dot(memn, wkv_ref[:, c0:c0 + PROJ_COLS]).astype(BF16)

    x = x_ref[0]
    for a_ref, w_ref in zip(refs[:n_in], refs[n_in:2 * n_in]):
        x = x + _dot(a_ref[0], w_ref[...])
    xn = _rms(x, g_ref[...]).astype(BF16)
    q = (_dot(xn, wq_ref[...]) * (X_DIM ** -0.5)).astype(BF16)
    heads = []
    for h in range(X_HEADS):
        cols = slice(h * X_DIM, (h + 1) * X_DIM)
        s = _dot_nt(q[:, cols], kv_ref[:, cols])
        p = jnp.exp(s - jnp.max(s, axis=1, keepdims=True))
        l = jnp.sum(p, axis=1, keepdims=True)
        v = kv_ref[:, D_MODEL + h * X_DIM:D_MODEL + (h + 1) * X_DIM]
        heads.append((_dot(p.astype(BF16), v) / l).astype(BF16))
    o_ref[0] = x + _dot(jnp.concatenate(heads, axis=1), wo_ref[...])


def mix_out_cross_attention(acts, w_mix, x, gain, wq, mem, mem_gain, wkv, wo, *, tq=X_TILE):
    B, S, D = x.shape
    M = mem.shape[1]
    (gain_spec, gain), (wq_spec, wq), (gm_spec, mem_gain), (wkv_spec, wkv), (wo_spec, wo) = map(
        _whole, (gain, wq, mem_gain, wkv, wo))
    return pl.pallas_call(
        functools.partial(_xattn_kernel, n_in=len(acts)),
        grid=(B, S // tq),
        in_specs=([pl.BlockSpec((1, tq, a.shape[2]), lambda b, i: (b, i, 0)) for a in acts]
                  + [w[0] for w in w_mix]
                  + [pl.BlockSpec((1, tq, D), lambda b, i: (b, i, 0)),
                     gain_spec, wq_spec,
                     pl.BlockSpec((1, M, D), lambda b, i: (b, 0, 0)),
                     gm_spec, wkv_spec, wo_spec]),
        out_specs=pl.BlockSpec((1, tq, D), lambda b, i: (b, i, 0)),
        out_shape=jax.ShapeDtypeStruct((B, S, D), F32),
        scratch_shapes=[pltpu.VMEM((M, 2 * D), BF16)],
        compiler_params=_params("parallel", "arbitrary"),
        name="mix_out_cross_attention",
    )(*acts, *[w[1] for w in w_mix], x, gain, wq, mem, mem_gain, wkv, wo)


def _gelu_tanh(x):
    k = -2.0 * math.sqrt(2.0 / math.pi) * math.log2(math.e)
    return x / (1.0 + jnp.exp2(x * (k * 0.044715 * (x * x) + k)))


def _ffn_kernel(x_ref, halo_ref, g_ref, wup_ref, cw_ref, cb_ref, wd_ref, fg_ref, o_ref, up_ref, act_ref, nat_ref,
                *, tm, tiles_per_seq, final_norm):
    i = pl.program_id(0)
    halo = _rms(halo_ref[...], g_ref[...])
    xe = jnp.concatenate([jnp.where(i % tiles_per_seq == 0, 0.0, halo).astype(BF16),
                          _rms(x_ref[...], g_ref[...]).astype(BF16)], axis=0)

    n_chunks = D_FF // FFN_CHUNK
    slabs = FFN_CHUNK // LANES
    rows = tm // ROW_PHASES

    def up_project(c):
        for half in range(2):
            col0 = half * D_FF + c * FFN_CHUNK
            up = _dot(xe, wup_ref[:, col0:col0 + FFN_CHUNK])
            for s in range(slabs):
                up_ref[c % 2, half, s] = up[:, s * LANES:(s + 1) * LANES]

    def conv(c, half, s, phase):
        col0 = half * D_FF + c * FFN_CHUNK + s * LANES
        cw = cw_ref[:, col0:col0 + LANES]
        out = cb_ref[:, col0:col0 + LANES]
        for tap in range(FFN_CONV):
            first = BF16_ROWS + phase - (FFN_CONV - 1 - tap)
            out = out + cw[tap:tap + 1] * up_ref[c % 2, half, s, pl.ds(first, rows, stride=ROW_PHASES), :]
        return out

    acc = None
    piece_start = 0
    up_project(0)
    for c in range(n_chunks):
        if c + 1 < n_chunks:
            up_project(c + 1)
        for phase in range(ROW_PHASES):
            for s in range(slabs):
                act = _gelu_tanh(conv(c, 0, s, phase)) * conv(c, 1, s, phase)
                act_ref[phase * rows:(phase + 1) * rows,
                        c * FFN_CHUNK + s * LANES:c * FFN_CHUNK + (s + 1) * LANES] = act.astype(BF16)
        if (c + 1) % DOWN_CHUNKS == 0 or c + 1 == n_chunks:
            piece = slice(piece_start * FFN_CHUNK, (c + 1) * FFN_CHUNK)
            part = _dot(act_ref[:, piece], wd_ref[piece, :])
            acc = part if acc is None else acc + part
            piece_start = c + 1

    for phase in range(ROW_PHASES):
        for s in range(D_MODEL // LANES):
            nat_ref[s, pl.ds(phase, rows, stride=ROW_PHASES), :] = acc[phase * rows:(phase + 1) * rows,
                                                                       s * LANES:(s + 1) * LANES]
    y = x_ref[...] + jnp.concatenate([nat_ref[s] for s in range(D_MODEL // LANES)], axis=1)
    o_ref[...] = _rms(y, fg_ref[...]) if final_norm else y


def conv_ffn(x, gain, w_up, conv_w, conv_b, w_down, seq_len, final_gain=None, *, tm=ROW_TILE):
    T, D = x.shape
    halo_blocks = tm // BF16_ROWS
    final_norm = final_gain is not None
    specs, operands = zip(*map(_whole, (gain, w_up, conv_w, conv_b, w_down, final_gain if final_norm else gain)))
    return pl.pallas_call(
        functools.partial(_ffn_kernel, tm=tm, tiles_per_seq=seq_len // tm, final_norm=final_norm),
        grid=(T // tm,),
        in_specs=[pl.BlockSpec((tm, D), lambda i: (i, 0)),
                  pl.BlockSpec((BF16_ROWS, D), lambda i: (jnp.maximum(i * halo_blocks - 1, 0), 0)),
                  *specs],
        out_specs=pl.BlockSpec((tm, D), lambda i: (i, 0)),
        out_shape=jax.ShapeDtypeStruct((T, D), F32),
        scratch_shapes=[pltpu.VMEM((2, 2, FFN_CHUNK // LANES, tm + BF16_ROWS, LANES), F32),
                        pltpu.VMEM((tm, D_FF), BF16),
                        pltpu.VMEM((D // LANES, tm, LANES), F32)],
        compiler_params=_params("parallel"),
        name="conv_ffn",
    )(x, x, *operands)


def _pad_lanes(a):
    return jnp.pad(a, ((0, 0), (0, LANES - a.shape[1])))


def _fox_diff_mixer(x, B, S, j, gain, w_in, fox_bf, lq1, lk1, lq2, lk2, subln, w_out, lambda_init):
    fw = FOX_HEADS * FOX_DIM
    g0 = 3 * fw
    dq0 = g0 + FOX_HEADS
    dqw = DIFF_HEADS * 2 * DIFF_QK
    w_main = jnp.concatenate([w_in[:, :fw] * (FOX_DIM ** -0.5 * LOG2_E), w_in[:, fw:g0],
                              w_in[:, dq0:dq0 + dqw] * (DIFF_QK ** -0.5 * LOG2_E), w_in[:, dq0 + dqw:]],
                             axis=1).astype(BF16)
    w_gates = _pad_lanes(w_in[:, g0:g0 + FOX_HEADS]).astype(BF16)
    proj, fq, fk = fox_projection(x, gain, w_main, w_gates, _pad_lanes(fox_bf.reshape(1, -1)), S)
    proj = proj.reshape(B, S, -1)
    fox = fox_attention(proj, fq.reshape(B, S, LANES), fk.reshape(B, S, LANES))
    dif = diff_attention(proj, lq1, lk1, lq2, lk2, subln, lambda_init)
    row_block = lambda r: (pl.BlockSpec((None, fw, D_MODEL), lambda *_: (j, r, 0), pipeline_mode=pl.Buffered(1)), w_out)
    return [fox, dif], [row_block(0), row_block(1)]


def _mlstm_mixer(x, B, S, j, gain, w_in, conv_qk, b_i, b_f, head_norm, w_out):
    g0 = 2 * ML_HEADS * ML_QK + ML_HEADS * ML_V
    w_main = jnp.concatenate([w_in[:, :g0], w_in[:, g0 + 2 * ML_HEADS:]], axis=1).astype(BF16)
    w_gates = _pad_lanes(w_in[:, g0:g0 + 2 * ML_HEADS]).astype(BF16)
    gate_bias = _pad_lanes(jnp.concatenate([b_i, b_f]).reshape(1, -1))
    proj, g_col, g_row = mlstm_projection(x, gain, w_main, w_gates, gate_bias, conv_qk, S)
    h = mlstm(proj.reshape(B, S, -1), g_col.reshape(B, S, LANES), g_row, head_norm)
    return [h], [_whole((w_out, j))]


def kernel(x, mem, mix_norm, xattn_norm, mem_norm, ffn_norm, attn_w_in, attn_fox_bf, diff_lq1, diff_lk1, diff_lq2, diff_lk2, diff_subln, attn_w_out, mlstm_w_in, mlstm_conv_qk, mlstm_b_i, mlstm_b_f, mlstm_head_norm, mlstm_w_out, xattn_wq, xattn_wkv, xattn_wo, ffn_w_up, ffn_conv_w, ffn_conv_b, ffn_w_down, final_norm):
    B, S, D = x.shape
    depth = mix_norm.shape[0]
    x = x.reshape(B * S, D)
    rows = lambda a: a.reshape(a.shape[0], 1, -1)
    to_bf16 = lambda a: a.astype(BF16)
    mix_norm, xattn_norm, mem_norm, ffn_norm = map(rows, (mix_norm, xattn_norm, mem_norm, ffn_norm))
    diff_lq1, diff_lk1, diff_lq2, diff_lk2, diff_subln = map(rows, (diff_lq1, diff_lk1, diff_lq2, diff_lk2, diff_subln))
    mlstm_head_norm, ffn_conv_b = rows(mlstm_head_norm), rows(ffn_conv_b)
    attn_w_out, mlstm_w_out, xattn_wq, xattn_wkv, xattn_wo, ffn_w_up, ffn_w_down = map(
        to_bf16, (attn_w_out, mlstm_w_out, xattn_wq, xattn_wkv, xattn_wo, ffn_w_up, ffn_w_down))
    for layer in range(depth):
        j = layer // 2
        if layer % 2 == 0:
            lambda_init = 0.8 - 0.6 * math.exp(-0.3 * layer)
            mixed, w_mix = _fox_diff_mixer(x, B, S, j, (mix_norm, layer), attn_w_in[j], attn_fox_bf[j],
                                           (diff_lq1, j), (diff_lk1, j), (diff_lq2, j), (diff_lk2, j),
                                           (diff_subln, j), attn_w_out, lambda_init)
        else:
            mixed, w_mix = _mlstm_mixer(x, B, S, j, (mix_norm, layer), mlstm_w_in[j], (mlstm_conv_qk, j),
                                        mlstm_b_i[j], mlstm_b_f[j], (mlstm_head_norm, j), mlstm_w_out)
        x = mix_out_cross_attention(mixed, w_mix, x.reshape(B, S, D), (xattn_norm, layer), (xattn_wq, layer), mem,
                                    (mem_norm, layer), (xattn_wkv, layer), (xattn_wo, layer)).reshape(B * S, D)
        x = conv_ffn(x, (ffn_norm, layer), (ffn_w_up, layer), (ffn_conv_w, layer), (ffn_conv_b, layer),
                     (ffn_w_down, layer), S, final_norm.reshape(1, D) if layer == depth - 1 else None)
    return x.reshape(B, S, D)
```

```python
import functools
import math

import jax
import jax.numpy as jnp
from jax import lax
from jax.experimental import pallas as pl
from jax.experimental.pallas import tpu as pltpu

F32 = jnp.float32
BF16 = jnp.bfloat16

D_MODEL = 1024
RMS_EPS = 1e-6
NEG_INF = -1e30
CHUNK = 64
FOX_HEADS, FOX_DIM = 8, 64
DIFF_HEADS, DIFF_QK, DIFF_V = 4, 64, 128
ML_HEADS, ML_QK, ML_V, ML_CONV = 4, 128, 256, 4
X_HEADS, X_DIM = 4, 256
D_FF = 2816
FFN_CONV = 3
LANES = 128
BF16_ROWS = 16
VMEM_LIMIT = 56 * 1024 * 1024

ROW_TILE = 512
ATT_Q_TILE = 512
ATT_K_TILE = 512
LOG2_E = math.log2(math.e)
ML_CHUNK = 256
ML_SEQS = 2
X_TILE = 1024
PROJ_COLS = 1024
FFN_CHUNK = 256
ROW_PHASES = 4
DOWN_CHUNKS = 4
TRANSPOSE_COLS = 512
F_PIECES = 3
FOX_BIAS_K_LANE = 0
FOX_BIAS_Q_LANE = 32


def _params(*sem):
    return pltpu.CompilerParams(dimension_semantics=sem, vmem_limit_bytes=VMEM_LIMIT)


def _resident(shape):
    return pl.BlockSpec(shape, lambda *_: (0,) * len(shape), pipeline_mode=pl.Buffered(1))


def _whole(a):
    if isinstance(a, tuple):
        arr, layer = a
        tail = arr.shape[1:]
        spec = pl.BlockSpec((None,) + tail, lambda *_: (layer,) + (0,) * len(tail), pipeline_mode=pl.Buffered(1))
        return spec, arr
    return _resident(a.shape), a


def _cols(a):
    return (a[0] if isinstance(a, tuple) else a).shape[-1]


def _rms(x, gain):
    return x * lax.rsqrt(jnp.mean(x * x, axis=-1, keepdims=True) + RMS_EPS) * gain


def _sigmoid(x):
    return 1.0 / (1.0 + jnp.exp(-x))


def _dot(a, b):
    return jnp.dot(a, b, preferred_element_type=F32)


def _dot_nt(a, b):
    return lax.dot_general(a, b, (((1,), (1,)), ((), ())), preferred_element_type=F32)


def _split3(x):
    hi = x.astype(BF16)
    r1 = x - hi.astype(F32)
    mid = r1.astype(BF16)
    return hi, mid, (r1 - mid.astype(F32)).astype(BF16)


def _dot_select(x, sel):
    hi, mid, lo = _split3(x)
    return _dot(hi, sel) + _dot(mid, sel) + _dot(lo, sel)


def _prefix_matrix(n, block):
    r = lax.broadcasted_iota(jnp.int32, (n, n), 0)
    c = lax.broadcasted_iota(jnp.int32, (n, n), 1)
    return jnp.where((r <= c) & (r // block == c // block), 1.0, 0.0).astype(BF16)


def _tile_lanes(a, n):
    return jnp.concatenate([a] * (n // LANES), axis=1)


def _log_sigmoid(x):
    return jnp.minimum(x, 0.0) - jnp.log1p(jnp.exp(-jnp.abs(x)))


def _fox_proj_kernel(x_ref, g_ref, w_ref, wg_ref, bf_ref, o_ref, fq_ref, fk_ref, carry_ref, *, tm, tn, tiles_per_seq):
    i = pl.program_id(0)

    @pl.when(i % tiles_per_seq == 0)
    def _():
        carry_ref[...] = jnp.zeros_like(carry_ref)

    xn = _rms(x_ref[...], g_ref[...]).astype(BF16)

    def project(part):
        width = o_ref.shape[1] // 3
        for c0 in range(part * width, (part + 1) * width, tn):
            o_ref[:, c0:c0 + tn] = _dot(xn, w_ref[:, c0:c0 + tn]).astype(o_ref.dtype)

    gates_t = _dot_nt(wg_ref[...], xn)
    project(0)
    log_f = _log_sigmoid(gates_t + _tile_lanes(bf_ref[...], tm)) * LOG2_E
    f_t = _tile_lanes(carry_ref[...], tm) + _dot_select(log_f, _prefix_matrix(tm, tm))
    carry_ref[...] = jnp.broadcast_to(f_t[:, tm - 1:tm], carry_ref.shape)
    f = jnp.concatenate([f_t, jnp.zeros((LANES - BF16_ROWS, tm), F32)], axis=0).T
    project(1)

    src = lax.broadcasted_iota(jnp.int32, (LANES, LANES), 0)
    dst = lax.broadcasted_iota(jnp.int32, (LANES, LANES), 1)
    lane = lax.broadcasted_iota(jnp.int32, (1, LANES), 1)
    n_bias = F_PIECES * FOX_HEADS
    ones_q = jnp.where((lane >= FOX_BIAS_K_LANE) & (lane < FOX_BIAS_K_LANE + n_bias), 1.0, 0.0)
    ones_k = jnp.where((lane >= FOX_BIAS_Q_LANE) & (lane < FOX_BIAS_Q_LANE + n_bias), 1.0, 0.0)
    both = jnp.concatenate([ones_q, ones_k], axis=1)
    for c, piece in enumerate(_split3(f)):
        head = src < FOX_HEADS
        to_q = jnp.where(head & (dst == FOX_BIAS_Q_LANE + F_PIECES * src + c), 1.0, 0.0)
        to_k = jnp.where(head & (dst == FOX_BIAS_K_LANE + F_PIECES * src + c), -1.0, 0.0)
        both = both + _dot(piece, jnp.concatenate([to_q, to_k], axis=1).astype(BF16))
    fq_ref[...] = both[:, 0:LANES].astype(BF16)
    fk_ref[...] = both[:, LANES:2 * LANES].astype(BF16)
    project(2)


def fox_projection(x, gain, w, w_gates, gate_bias, seq_len, *, tm=ROW_TILE, tn=PROJ_COLS):
    T, D = x.shape
    N = _cols(w)
    row_block = lambda n: pl.BlockSpec((tm, n), lambda i: (i, 0))
    specs, operands = zip(*map(_whole, (gain, w, w_gates, gate_bias)))
    return pl.pallas_call(
        functools.partial(_fox_proj_kernel, tm=tm, tn=tn, tiles_per_seq=seq_len // tm),
        grid=(T // tm,),
        in_specs=[row_block(D), *specs],
        out_specs=[row_block(N), row_block(LANES), row_block(LANES)],
        out_shape=[jax.ShapeDtypeStruct((T, N), BF16), jax.ShapeDtypeStruct((T, LANES), BF16),
                   jax.ShapeDtypeStruct((T, LANES), BF16)],
        scratch_shapes=[pltpu.VMEM((BF16_ROWS, LANES), F32)],
        compiler_params=_params("arbitrary"),
        name="fox_projection",
    )(x, *operands)


def _mlstm_proj_kernel(x_ref, halo_ref, g_ref, w_ref, wg_ref, gb_ref, cw_ref, o_ref, gr_ref, up_ref, nat_ref,
                       *, tm, tiles_per_seq):
    i = pl.program_id(0)
    halo = _rms(halo_ref[...], g_ref[...])
    xe = jnp.concatenate([jnp.where(i % tiles_per_seq == 0, 0.0, halo).astype(BF16),
                          _rms(x_ref[...], g_ref[...]).astype(BF16)], axis=0)
    xn = xe[BF16_ROWS:]

    qk_slabs = 2 * ML_HEADS * ML_QK // LANES
    rows = tm // ROW_PHASES
    v0 = qk_slabs * LANES
    og0 = v0 + ML_HEADS * ML_V
    for s0 in range(0, qk_slabs, 2):
        up = _dot(xe, w_ref[:, s0 * LANES:(s0 + 2) * LANES])
        up_ref[s0] = up[:, 0:LANES]
        up_ref[s0 + 1] = up[:, LANES:2 * LANES]
    pre_t = _dot_nt(wg_ref[...], xn) + _tile_lanes(gb_ref[...], tm)
    o_ref[:, v0:og0] = _dot(xn, w_ref[:, v0:og0]).astype(o_ref.dtype)
    for s in range(qk_slabs):
        cw = cw_ref[:, s * LANES:(s + 1) * LANES]
        for phase in range(ROW_PHASES):
            y = None
            for tap in range(ML_CONV):
                first = BF16_ROWS + phase - (ML_CONV - 1 - tap)
                term = cw[tap:tap + 1] * up_ref[s, pl.ds(first, rows, stride=ROW_PHASES), :]
                y = term if y is None else y + term
            y = y * _sigmoid(y)
            if s < qk_slabs // 2:
                y = y * (ML_QK ** -0.5)
            nat_ref[s, pl.ds(phase, rows, stride=ROW_PHASES), :] = y
        o_ref[:, s * LANES:(s + 1) * LANES] = nat_ref[s].astype(o_ref.dtype)

    og = _dot(xn, w_ref[:, og0:])
    row = lax.broadcasted_iota(jnp.int32, (BF16_ROWS, 1), 0)
    val_t = jnp.where(row < ML_HEADS, pre_t, _log_sigmoid(pre_t)) * LOG2_E
    g_t = jnp.where(row < ML_HEADS, val_t, _dot_select(val_t, _prefix_matrix(tm, ML_CHUNK)))
    gr_ref[0] = g_t[0:2 * ML_HEADS]
    o_ref[:, og0:] = _sigmoid(og).astype(o_ref.dtype)


def mlstm_projection(x, gain, w, w_gates, gate_bias, conv_w, seq_len, *, tm=ROW_TILE):
    T, D = x.shape
    N = _cols(w)
    tiles_per_seq = seq_len // tm
    halo_blocks = tm // BF16_ROWS
    row_block = lambda n: pl.BlockSpec((tm, n), lambda i: (i, 0))
    specs, operands = zip(*map(_whole, (gain, w, w_gates, gate_bias, conv_w)))
    return pl.pallas_call(
        functools.partial(_mlstm_proj_kernel, tm=tm, tiles_per_seq=tiles_per_seq),
        grid=(T // tm,),
        in_specs=[row_block(D),
                  pl.BlockSpec((BF16_ROWS, D), lambda i: (jnp.maximum(i * halo_blocks - 1, 0), 0)),
                  *specs],
        out_specs=[row_block(N),
                   pl.BlockSpec((1, 2 * ML_HEADS, tm), lambda i: (i // tiles_per_seq, 0, i % tiles_per_seq))],
        out_shape=[jax.ShapeDtypeStruct((T, N), BF16),
                   jax.ShapeDtypeStruct((T // seq_len, 2 * ML_HEADS, seq_len), F32)],
        scratch_shapes=[pltpu.VMEM((2 * ML_HEADS * ML_QK // LANES, tm + BF16_ROWS, LANES), F32),
                        pltpu.VMEM((2 * ML_HEADS * ML_QK // LANES, tm, LANES), F32)],
        compiler_params=_params("parallel"),
        name="mlstm_projection",
    )(x, x, *operands)


def _build_vt(v_ref, vt_ref, n_heads, rows):
    S = v_ref.shape[1]
    r = lax.broadcasted_iota(jnp.int32, (LANES, LANES), 0)
    c = lax.broadcasted_iota(jnp.int32, (LANES, LANES), 1)
    eye = jnp.where(r == c, 1.0, 0.0).astype(BF16)
    per_group = LANES // rows
    for g in range(n_heads // per_group):
        for c0 in range(0, S, TRANSPOSE_COLS):
            cs = slice(c0, c0 + TRANSPOSE_COLS)
            vt = _dot_nt(eye, v_ref[0, cs, g * LANES:(g + 1) * LANES]).astype(BF16)
            for k in range(per_group):
                vt_ref[g * per_group + k, 0:rows, cs] = vt[k * rows:(k + 1) * rows]
    for h in range(n_heads):
        vt_ref[h, rows:rows + BF16_ROWS, :] = jnp.ones((BF16_ROWS, S), BF16)


def _softmax_step(s_t, vt, m_ref, acc_ref, idx, queries=slice(None)):
    m_prev = m_ref[idx, :, queries]
    m_new = jnp.maximum(m_prev, jnp.max(s_t, axis=0, keepdims=True))
    p = jnp.exp2(s_t - m_new).astype(BF16)
    acc_ref[idx, :, queries] = jnp.exp2(m_prev - m_new) * acc_ref[idx, :, queries] + _dot(vt, p)
    m_ref[idx, :, queries] = m_new


def _causal_steps(i, logits, attend, n_streams, buf_a, buf_b):
    def phase(j, src, dst, next_is_diagonal=False):
        for n in range(n_streams):
            logits(j + 1, dst, n, next_is_diagonal)
            attend(j, src, n, False)

    for n in range(n_streams):
        logits(0, buf_a, n, False)

    def body(jj, carry):
        phase(2 * jj, buf_a, buf_b)
        phase(2 * jj + 1, buf_b, buf_a)
        return carry

    lax.fori_loop(0, lax.shift_right_logical(i, 1), body, 0)
    odd = lax.rem(i, 2) == 1

    @pl.when(odd)
    def _():
        phase(i - 1, buf_a, buf_b, True)
        for n in range(n_streams):
            attend(i, buf_b, n, True)

    @pl.when(jnp.logical_not(odd))
    def _():
        for n in range(n_streams):
            attend(i, buf_a, n, True)


def _fox_kernel(q_ref, k_ref, v_ref, fq_ref, fk_ref, o_ref, vt_ref, qc_ref, acc_ref, m_ref, sa_ref, sb_ref, *, tq, tk):
    i = pl.program_id(1)

    @pl.when(i == 0)
    def _():
        _build_vt(v_ref, vt_ref, FOX_HEADS, FOX_DIM)

    lane = lax.broadcasted_iota(jnp.int32, (1, LANES), 1)
    fq = fq_ref[0]
    for h in range(FOX_HEADS):
        pair, half = divmod(h, 2)
        q = q_ref[0, :, pair * LANES:(pair + 1) * LANES]
        in_head = (lane >= half * FOX_DIM) & (lane < (half + 1) * FOX_DIM)
        lo_k, lo_q = FOX_BIAS_K_LANE + F_PIECES * h, FOX_BIAS_Q_LANE + F_PIECES * h
        mine = ((lane >= lo_k) & (lane < lo_k + F_PIECES)) | ((lane >= lo_q) & (lane < lo_q + F_PIECES))
        qc_ref[h, :, 0:LANES] = jnp.where(in_head, q, jnp.zeros_like(q))
        qc_ref[h, :, LANES:2 * LANES] = jnp.where(mine, fq, jnp.zeros_like(fq))
    m_ref[...] = jnp.full(m_ref.shape, NEG_INF, F32)
    acc_ref[...] = jnp.zeros(acc_ref.shape, F32)

    half = tk // 2
    key = lax.broadcasted_iota(jnp.int32, (half, tq), 0)
    qry = lax.broadcasted_iota(jnp.int32, (half, tq), 1)
    causal = key <= qry

    def key_rows(j, part=None):
        if part is None:
            return pl.ds(pl.multiple_of(j * tk, tk), tk)
        return pl.ds(pl.multiple_of(j * tk + part * half, half), half)

    def logits(j, buf, h, diagonal):
        pair = h // 2
        kc = jnp.concatenate([k_ref[0, key_rows(j), pair * LANES:(pair + 1) * LANES], fk_ref[0, key_rows(j), :]],
                             axis=1)
        if diagonal:
            buf[h, 0:half, :] = _dot_nt(kc[0:half], qc_ref[h])
            buf[h, half:tk, half:tq] = _dot_nt(kc[half:tk], qc_ref[h, half:tq, :])
        else:
            buf[h] = _dot_nt(kc, qc_ref[h])

    def attend(j, buf, h, diagonal):
        if diagonal:
            _softmax_step(jnp.where(causal, buf[h, 0:half, :], NEG_INF), vt_ref[h, :, key_rows(j, 0)],
                          m_ref, acc_ref, h)
            _softmax_step(jnp.where(causal[:, 0:tq - half], buf[h, half:tk, half:tq], NEG_INF),
                          vt_ref[h, :, key_rows(j, 1)], m_ref, acc_ref, h, slice(half, tq))
        else:
            _softmax_step(buf[h], vt_ref[h, :, key_rows(j)], m_ref, acc_ref, h)

    _causal_steps(i, logits, attend, FOX_HEADS, sa_ref, sb_ref)

    for pair in range(FOX_HEADS // 2):
        halves = []
        for half in range(2):
            a = acc_ref[2 * pair + half]
            halves.append(a[0:FOX_DIM] / a[FOX_DIM:FOX_DIM + 1])
        o_ref[0, :, pair * LANES:(pair + 1) * LANES] = jnp.concatenate(halves, axis=0).T.astype(o_ref.dtype)


def fox_attention(proj, fq, fk, *, tq=ATT_Q_TILE, tk=ATT_K_TILE):
    B, S, _ = proj.shape
    assert tq == tk
    width = FOX_HEADS * FOX_DIM
    rows = FOX_DIM + BF16_ROWS
    return pl.pallas_call(
        functools.partial(_fox_kernel, tq=tq, tk=tk),
        grid=(B, S // tq),
        in_specs=[pl.BlockSpec((1, tq, width), lambda b, i: (b, i, 0)),
                  pl.BlockSpec((1, S, width), lambda b, i: (b, 0, 1)),
                  pl.BlockSpec((1, S, width), lambda b, i: (b, 0, 2)),
                  pl.BlockSpec((1, tq, LANES), lambda b, i: (b, i, 0)),
                  pl.BlockSpec((1, S, LANES), lambda b, i: (b, 0, 0))],
        out_specs=pl.BlockSpec((1, tq, width), lambda b, i: (b, i, 0)),
        out_shape=jax.ShapeDtypeStruct((B, S, width), BF16),
        scratch_shapes=[pltpu.VMEM((FOX_HEADS, rows, S), BF16),
                        pltpu.VMEM((FOX_HEADS, tq, 2 * LANES), BF16),
                        pltpu.VMEM((FOX_HEADS, rows, tq), F32),
                        pltpu.VMEM((FOX_HEADS, 1, tq), F32),
                        pltpu.VMEM((FOX_HEADS, tk, tq), F32),
                        pltpu.VMEM((FOX_HEADS, tk, tq), F32)],
        compiler_params=_params("parallel", "arbitrary"),
        name="fox_attention",
    )(proj, proj, proj, fq, fk)


def _diff_kernel(q_ref, k_ref, v_ref, lq1_ref, lk1_ref, lq2_ref, lk2_ref, sub_ref, o_ref, vt_ref, qc_ref, acc_ref,
                 m_ref, sa_ref, sb_ref, *, tq, tk, lambda_init):
    i = pl.program_id(1)

    @pl.when(i == 0)
    def _():
        _build_vt(v_ref, vt_ref, DIFF_HEADS, DIFF_V)

    lane = lax.broadcasted_iota(jnp.int32, (1, LANES), 1)
    for h in range(DIFF_HEADS):
        q = q_ref[0, :, h * LANES:(h + 1) * LANES]
        zero = jnp.zeros_like(q)
        qc_ref[2 * h] = jnp.where(lane < DIFF_QK, q, zero)
        qc_ref[2 * h + 1] = jnp.where(lane >= DIFF_QK, q, zero)
    m_ref[...] = jnp.full(m_ref.shape, NEG_INF, F32)
    acc_ref[...] = jnp.zeros(acc_ref.shape, F32)

    half = tk // 2
    key = lax.broadcasted_iota(jnp.int32, (half, tq), 0)
    qry = lax.broadcasted_iota(jnp.int32, (half, tq), 1)
    visible = key // CHUNK <= qry // CHUNK

    def key_rows(j, part=None):
        if part is None:
            return pl.ds(pl.multiple_of(j * tk, tk), tk)
        return pl.ds(pl.multiple_of(j * tk + part * half, half), half)

    def logits(j, buf, n, diagonal):
        k = k_ref[0, key_rows(j), (n // 2) * LANES:(n // 2 + 1) * LANES]
        if diagonal:
            buf[n, 0:half, :] = _dot_nt(k[0:half], qc_ref[n])
            buf[n, half:tk, half:tq] = _dot_nt(k[half:tk], qc_ref[n, half:tq, :])
        else:
            buf[n] = _dot_nt(k, qc_ref[n])

    def attend(j, buf, n, diagonal):
        vt = vt_ref.at[n // 2]
        if diagonal:
            _softmax_step(jnp.where(visible, buf[n, 0:half, :], NEG_INF), vt[:, key_rows(j, 0)], m_ref, acc_ref, n)
            _softmax_step(jnp.where(visible[:, 0:tq - half], buf[n, half:tk, half:tq], NEG_INF),
                          vt[:, key_rows(j, 1)], m_ref, acc_ref, n, slice(half, tq))
        else:
            _softmax_step(buf[n], vt[:, key_rows(j)], m_ref, acc_ref, n)

    _causal_steps(i, logits, attend, 2 * DIFF_HEADS, sa_ref, sb_ref)

    lam = (jnp.exp(jnp.sum(lq1_ref[...] * lk1_ref[...], axis=1, keepdims=True))
           - jnp.exp(jnp.sum(lq2_ref[...] * lk2_ref[...], axis=1, keepdims=True)) + lambda_init)
    for h in range(DIFF_HEADS):
        a1, a2 = acc_ref[2 * h], acc_ref[2 * h + 1]
        o_t = a1[0:DIFF_V] / a1[DIFF_V:DIFF_V + 1] - lam * (a2[0:DIFF_V] / a2[DIFF_V:DIFF_V + 1])
        out = _rms(o_t.T, sub_ref[...]) * (1.0 - lambda_init)
        o_ref[0, :, h * LANES:(h + 1) * LANES] = out.astype(o_ref.dtype)


def diff_attention(proj, lq1, lk1, lq2, lk2, subln, lambda_init, *, tq=ATT_Q_TILE, tk=ATT_K_TILE):
    B, S, _ = proj.shape
    assert tq == tk
    width = DIFF_HEADS * DIFF_V
    rows = DIFF_V + BF16_ROWS
    specs, operands = zip(*map(_whole, (lq1, lk1, lq2, lk2, subln)))
    return pl.pallas_call(
        functools.partial(_diff_kernel, tq=tq, tk=tk, lambda_init=lambda_init),
        grid=(B, S // tq),
        in_specs=[pl.BlockSpec((1, tq, width), lambda b, i: (b, i, 3)),
                  pl.BlockSpec((1, S, width), lambda b, i: (b, 0, 4)),
                  pl.BlockSpec((1, S, width), lambda b, i: (b, 0, 5)),
                  *specs],
        out_specs=pl.BlockSpec((1, tq, width), lambda b, i: (b, i, 0)),
        out_shape=jax.ShapeDtypeStruct((B, S, width), BF16),
        scratch_shapes=[pltpu.VMEM((DIFF_HEADS, rows, S), BF16),
                        pltpu.VMEM((2 * DIFF_HEADS, tq, LANES), BF16),
                        pltpu.VMEM((2 * DIFF_HEADS, rows, tq), F32),
                        pltpu.VMEM((2 * DIFF_HEADS, 1, tq), F32),
                        pltpu.VMEM((2 * DIFF_HEADS, tk, tq), F32),
                        pltpu.VMEM((2 * DIFF_HEADS, tk, tq), F32)],
        compiler_params=_params("parallel", "arbitrary"),
        name="diff_attention",
    )(proj, proj, proj, *operands)


def _mlstm_kernel(qk_ref, v_ref, sg_ref, gr_ref, hn_ref, o_ref, ct_ref, m_ref, vt_ref, *, L, nb):
    c = pl.program_id(1)

    @pl.when(c == 0)
    def _():
        ct_ref[...] = jnp.zeros_like(ct_ref)
        m_ref[...] = jnp.zeros_like(m_ref)
        for n in range(nb * ML_HEADS):
            vt_ref[n, ML_V:ML_V + BF16_ROWS, :] = jnp.ones((BF16_ROWS, L), BF16)

    src = lax.broadcasted_iota(jnp.int32, (L, L), 0)
    dst = lax.broadcasted_iota(jnp.int32, (L, L), 1)
    causal = src <= dst
    eye = jnp.where(src == dst, 1.0, 0.0).astype(BF16)
    k0 = ML_HEADS * ML_QK
    for b in range(nb):
        gr = gr_ref[b]
        gc = jnp.concatenate([gr, jnp.zeros((LANES - gr.shape[0], L), F32)], axis=0).T
        for h in range(ML_HEADS):
            n = b * ML_HEADS + h
            q = qk_ref[b, :, h * ML_QK:(h + 1) * ML_QK]
            k = qk_ref[b, :, k0 + h * ML_QK:k0 + (h + 1) * ML_QK]
            vt_ref[n, 0:ML_V, :] = _dot_nt(eye, v_ref[b, :, h * ML_V:(h + 1) * ML_V]).astype(BF16)
            v_t = vt_ref[n]
            r_col = gc[:, h:h + 1] - gc[:, ML_HEADS + h:ML_HEADS + h + 1]
            b_row = gr[ML_HEADS + h:ML_HEADS + h + 1, :]
            r_row = gr[h:h + 1, :] - b_row
            g = b_row[:, L - 1:L]
            ct = ct_ref[n]
            m = m_ref[n][:, 0:1]

            dm = jnp.where(causal, r_col, NEG_INF)
            mt = jnp.maximum(m, jnp.max(dm, axis=0, keepdims=True))
            s_t = _dot_nt(k, q) * jnp.exp2(dm - mt)
            both = jnp.exp2(m - mt) * _dot_nt(ct.astype(BF16), q) + _dot(v_t, s_t.astype(BF16))
            den = both[ML_V:ML_V + 1]
            hh = both[0:ML_V] / jnp.maximum(jnp.abs(den), jnp.exp2(-(b_row + mt)))
            hh = hh * lax.rsqrt(jnp.mean(hh * hh, axis=0, keepdims=True) + RMS_EPS)

            m_next = jnp.maximum(m, jnp.max(r_row, axis=1, keepdims=True))
            kw = (k.astype(F32) * jnp.exp2(r_col - m_next)).astype(BF16)
            ct_ref[n] = jnp.exp2(m - m_next) * ct + _dot(v_t, kw)
            m_ref[n] = jnp.broadcast_to(g + m_next, (1, LANES))

            vs = slice(h * ML_V, (h + 1) * ML_V)
            o_ref[b, :, vs] = (hh.T * hn_ref[:, vs] * sg_ref[b, :, vs].astype(F32)).astype(o_ref.dtype)


def mlstm(proj, g_row, head_norm, *, L=ML_CHUNK, nb=ML_SEQS):
    B, S, _ = proj.shape
    W = D_MODEL
    assert L == ML_V, "one identity matrix serves the v transposes"
    norm_spec, head_norm = _whole(head_norm)
    return pl.pallas_call(
        functools.partial(_mlstm_kernel, L=L, nb=nb),
        grid=(B // nb, S // L),
        in_specs=[pl.BlockSpec((nb, L, W), lambda b, c: (b, c, 0)),
                  pl.BlockSpec((nb, L, W), lambda b, c: (b, c, 1)),
                  pl.BlockSpec((nb, L, W), lambda b, c: (b, c, 2)),
                  pl.BlockSpec((nb, 2 * ML_HEADS, L), lambda b, c: (b, 0, c)),
                  norm_spec],
        out_specs=pl.BlockSpec((nb, L, W), lambda b, c: (b, c, 0)),
        out_shape=jax.ShapeDtypeStruct((B, S, W), BF16),
        scratch_shapes=[pltpu.VMEM((nb * ML_HEADS, ML_V + BF16_ROWS, ML_QK), F32),
                        pltpu.VMEM((nb * ML_HEADS, 1, LANES), F32),
                        pltpu.VMEM((nb * ML_HEADS, ML_V + BF16_ROWS, L), BF16)],
        compiler_params=_params("parallel", "arbitrary"),
        name="mlstm",
    )(proj, proj, proj, g_row, head_norm)


def _xattn_kernel(*refs, n_in):
    x_ref, g_ref, wq_ref, mem_ref, gm_ref, wkv_ref, wo_ref, o_ref, kv_ref = refs[2 * n_in:]

    @pl.when(pl.program_id(1) == 0)
    def _():
        memn = _rms(mem_ref[0], gm_ref[...]).astype(BF16)
        for c0 in range(0, kv_ref.shape[1], PROJ_COLS):
            kv_ref[:, c0:c0 + PROJ_COLS] = _dot(memn, wkv_ref[:, c0:c0 + PROJ_COLS]).astype(BF16)

    x = x_ref[0]
    for a_ref, w_ref in zip(refs[:n_in], refs[n_in:2 * n_in]):
        x = x + _dot(a_ref[0], w_ref[...])
    xn = _rms(x, g_ref[...]).astype(BF16)
    q = (_dot(xn, wq_ref[...]) * (X_DIM ** -0.5)).astype(BF16)
    heads = []
    for h in range(X_HEADS):
        cols = slice(h * X_DIM, (h + 1) * X_DIM)
        s = _dot_nt(q[:, cols], kv_ref[:, cols])
        p = jnp.exp(s - jnp.max(s, axis=1, keepdims=True))
        l = jnp.sum(p, axis=1, keepdims=True)
        v = kv_ref[:, D_MODEL + h * X_DIM:D_MODEL + (h + 1) * X_DIM]
        heads.append((_dot(p.astype(BF16), v) / l).astype(BF16))
    o_ref[0] = x + _dot(jnp.concatenate(heads, axis=1), wo_ref[...])


def mix_out_cross_attention(acts, w_mix, x, gain, wq, mem, mem_gain, wkv, wo, *, tq=X_TILE):
    B, S, D = x.shape
    M = mem.shape[1]
    (gain_spec, gain), (wq_spec, wq), (gm_spec, mem_gain), (wkv_spec, wkv), (wo_spec, wo) = map(
        _whole, (gain, wq, mem_gain, wkv, wo))
    return pl.pallas_call(
        functools.partial(_xattn_kernel, n_in=len(acts)),
        grid=(B, S // tq),
        in_specs=([pl.BlockSpec((1, tq, a.shape[2]), lambda b, i: (b, i, 0)) for a in acts]
                  + [w[0] for w in w_mix]
                  + [pl.BlockSpec((1, tq, D), lambda b, i: (b, i, 0)),
                     gain_spec, wq_spec,
                     pl.BlockSpec((1, M, D), lambda b, i: (b, 0, 0)),
                     gm_spec, wkv_spec, wo_spec]),
        out_specs=pl.BlockSpec((1, tq, D), lambda b, i: (b, i, 0)),
        out_shape=jax.ShapeDtypeStruct((B, S, D), F32),
        scratch_shapes=[pltpu.VMEM((M, 2 * D), BF16)],
        compiler_params=_params("parallel", "arbitrary"),
        name="mix_out_cross_attention",
    )(*acts, *[w[1] for w in w_mix], x, gain, wq, mem, mem_gain, wkv, wo)


def _gelu_tanh(x):
    k = -2.0 * math.sqrt(2.0 / math.pi) * math.log2(math.e)
    return x / (1.0 + jnp.exp2(x * (k * 0.044715 * (x * x) + k)))


def _ffn_kernel(x_ref, halo_ref, g_ref, wup_ref, cw_ref, cb_ref, wd_ref, fg_ref, o_ref, up_ref, act_ref, nat_ref,
                *, tm, tiles_per_seq, final_norm):
    i = pl.program_id(0)
    halo = _rms(halo_ref[...], g_ref[...])
    xe = jnp.concatenate([jnp.where(i % tiles_per_seq == 0, 0.0, halo).astype(BF16),
                          _rms(x_ref[...], g_ref[...]).astype(BF16)], axis=0)

    n_chunks = D_FF // FFN_CHUNK
    slabs = FFN_CHUNK // LANES
    rows = tm // ROW_PHASES

    def up_project(c):
        for half in range(2):
            col0 = half * D_FF + c * FFN_CHUNK
            up = _dot(xe, wup_ref[:, col0:col0 + FFN_CHUNK])
            for s in range(slabs):
                up_ref[c % 2, half, s] = up[:, s * LANES:(s + 1) * LANES]

    def conv(c, half, s, phase):
        col0 = half * D_FF + c * FFN_CHUNK + s * LANES
        cw = cw_ref[:, col0:col0 + LANES]
        out = cb_ref[:, col0:col0 + LANES]
        for tap in range(FFN_CONV):
            first = BF16_ROWS + phase - (FFN_CONV - 1 - tap)
            out = out + cw[tap:tap + 1] * up_ref[c % 2, half, s, pl.ds(first, rows, stride=ROW_PHASES), :]
        return out

    acc = None
    piece_start = 0
    up_project(0)
    for c in range(n_chunks):
        if c + 1 < n_chunks:
            up_project(c + 1)
        for phase in range(ROW_PHASES):
            for s in range(slabs):
                act = _gelu_tanh(conv(c, 0, s, phase)) * conv(c, 1, s, phase)
                act_ref[phase * rows:(phase + 1) * rows,
                        c * FFN_CHUNK + s * LANES:c * FFN_CHUNK + (s + 1) * LANES] = act.astype(BF16)
        if (c + 1) % DOWN_CHUNKS == 0 or c + 1 == n_chunks:
            piece = slice(piece_start * FFN_CHUNK, (c + 1) * FFN_CHUNK)
            part = _dot(act_ref[:, piece], wd_ref[piece, :])
            acc = part if acc is None else acc + part
            piece_start = c + 1

    for phase in range(ROW_PHASES):
        for s in range(D_MODEL // LANES):
            nat_ref[s, pl.ds(phase, rows, stride=ROW_PHASES), :] = acc[phase * rows:(phase + 1) * rows,
                                                                       s * LANES:(s + 1) * LANES]
    y = x_ref[...] + jnp.concatenate([nat_ref[s] for s in range(D_MODEL // LANES)], axis=1)
    o_ref[...] = _rms(y, fg_ref[...]) if final_norm else y


def conv_ffn(x, gain, w_up, conv_w, conv_b, w_down, seq_len, final_gain=None, *, tm=ROW_TILE):
    T, D = x.shape
    halo_blocks = tm // BF16_ROWS
    final_norm = final_gain is not None
    specs, operands = zip(*map(_whole, (gain, w_up, conv_w, conv_b, w_down, final_gain if final_norm else gain)))
    return pl.pallas_call(
        functools.partial(_ffn_kernel, tm=tm, tiles_per_seq=seq_len // tm, final_norm=final_norm),
        grid=(T // tm,),
        in_specs=[pl.BlockSpec((tm, D), lambda i: (i, 0)),
                  pl.BlockSpec((BF16_ROWS, D), lambda i: (jnp.maximum(i * halo_blocks - 1, 0), 0)),
                  *specs],
        out_specs=pl.BlockSpec((tm, D), lambda i: (i, 0)),
        out_shape=jax.ShapeDtypeStruct((T, D), F32),
        scratch_shapes=[pltpu.VMEM((2, 2, FFN_CHUNK // LANES, tm + BF16_ROWS, LANES), F32),
                        pltpu.VMEM((tm, D_FF), BF16),
                        pltpu.VMEM((D // LANES, tm, LANES), F32)],
        compiler_params=_params("parallel"),
        name="conv_ffn",
    )(x, x, *operands)


def _gate_rows(w_cols, bias):
    pad = BF16_ROWS - w_cols.shape[1]
    w_t = jnp.pad(w_cols.T, ((0, pad), (0, 0))).astype(BF16)
    return w_t, jnp.broadcast_to(jnp.pad(bias, (0, pad))[:, None], (BF16_ROWS, LANES))


def _fox_diff_mixer(x, B, S, j, gain, w_in, fox_bf, lq1, lk1, lq2, lk2, subln, w_out, lambda_init):
    fw = FOX_HEADS * FOX_DIM
    g0 = 3 * fw
    dq0 = g0 + FOX_HEADS
    dqw = DIFF_HEADS * 2 * DIFF_QK
    w_main = jnp.concatenate([w_in[:, :fw] * (FOX_DIM ** -0.5 * LOG2_E), w_in[:, fw:g0],
                              w_in[:, dq0:dq0 + dqw] * (DIFF_QK ** -0.5 * LOG2_E), w_in[:, dq0 + dqw:]],
                             axis=1).astype(BF16)
    w_gates, gate_bias = _gate_rows(w_in[:, g0:g0 + FOX_HEADS], fox_bf)
    proj, fq, fk = fox_projection(x, gain, w_main, w_gates, gate_bias, S)
    proj = proj.reshape(B, S, -1)
    fox = fox_attention(proj, fq.reshape(B, S, LANES), fk.reshape(B, S, LANES))
    dif = diff_attention(proj, lq1, lk1, lq2, lk2, subln, lambda_init)
    row_block = lambda r: (pl.BlockSpec((None, fw, D_MODEL), lambda *_: (j, r, 0), pipeline_mode=pl.Buffered(1)), w_out)
    return [fox, dif], [row_block(0), row_block(1)]


def _mlstm_mixer(x, B, S, j, gain, w_in, conv_qk, b_i, b_f, head_norm, w_out):
    g0 = 2 * ML_HEADS * ML_QK + ML_HEADS * ML_V
    w_main = jnp.concatenate([w_in[:, :g0], w_in[:, g0 + 2 * ML_HEADS:]], axis=1).astype(BF16)
    w_gates, gate_bias = _gate_rows(w_in[:, g0:g0 + 2 * ML_HEADS], jnp.concatenate([b_i, b_f]))
    proj, g_row = mlstm_projection(x, gain, w_main, w_gates, gate_bias, conv_qk, S)
    h = mlstm(proj.reshape(B, S, -1), g_row, head_norm)
    return [h], [_whole((w_out, j))]


def kernel(x, mem, mix_norm, xattn_norm, mem_norm, ffn_norm, attn_w_in, attn_fox_bf, diff_lq1, diff_lk1, diff_lq2, diff_lk2, diff_subln, attn_w_out, mlstm_w_in, mlstm_conv_qk, mlstm_b_i, mlstm_b_f, mlstm_head_norm, mlstm_w_out, xattn_wq, xattn_wkv, xattn_wo, ffn_w_up, ffn_conv_w, ffn_conv_b, ffn_w_down, final_norm):
    B, S, D = x.shape
    depth = mix_norm.shape[0]
    x = x.reshape(B * S, D)
    rows = lambda a: a.reshape(a.shape[0], 1, -1)
    to_bf16 = lambda a: a.astype(BF16)
    mix_norm, xattn_norm, mem_norm, ffn_norm = map(rows, (mix_norm, xattn_norm, mem_norm, ffn_norm))
    diff_lq1, diff_lk1, diff_lq2, diff_lk2, diff_subln = map(rows, (diff_lq1, diff_lk1, diff_lq2, diff_lk2, diff_subln))
    mlstm_head_norm, ffn_conv_b = rows(mlstm_head_norm), rows(ffn_conv_b)
    attn_w_out, mlstm_w_out, xattn_wq, xattn_wkv, xattn_wo, ffn_w_up, ffn_w_down = map(
        to_bf16, (attn_w_out, mlstm_w_out, xattn_wq, xattn_wkv, xattn_wo, ffn_w_up, ffn_w_down))
    for layer in range(depth):
        j = layer // 2
        if layer % 2 == 0:
            lambda_init = 0.8 - 0.6 * math.exp(-0.3 * layer)
            mixed, w_mix = _fox_diff_mixer(x, B, S, j, (mix_norm, layer), attn_w_in[j], attn_fox_bf[j],
                                           (diff_lq1, j), (diff_lk1, j), (diff_lq2, j), (diff_lk2, j),
                                           (diff_subln, j), attn_w_out, lambda_init)
        else:
            mixed, w_mix = _mlstm_mixer(x, B, S, j, (mix_norm, layer), mlstm_w_in[j], (mlstm_conv_qk, j),
                                        mlstm_b_i[j], mlstm_b_f[j], (mlstm_head_norm, j), mlstm_w_out)
        x = mix_out_cross_attention(mixed, w_mix, x.reshape(B, S, D), (xattn_norm, layer), (xattn_wq, layer), mem,
                                    (mem_norm, layer), (xattn_wkv, layer), (xattn_wo, layer)).reshape(B * S, D)
        x = conv_ffn(x, (ffn_norm, layer), (ffn_w_up, layer), (ffn_conv_w, layer), (ffn_conv_b, layer),
                     (ffn_w_down, layer), S, final_norm.reshape(1, D) if layer == depth - 1 else None)
    return x.reshape(B, S, D)
```

```python
import functools
import math

import jax
import jax.numpy as jnp
from jax import lax
from jax.experimental import pallas as pl
from jax.experimental.pallas import tpu as pltpu

F32 = jnp.float32
BF16 = jnp.bfloat16

D_MODEL = 1024
RMS_EPS = 1e-6
NEG_INF = -1e30
CHUNK = 64
FOX_HEADS, FOX_DIM = 8, 64
DIFF_HEADS, DIFF_QK, DIFF_V = 4, 64, 128
ML_HEADS, ML_QK, ML_V, ML_CONV = 4, 128, 256, 4
X_HEADS, X_DIM = 4, 256
D_FF = 2816
FFN_CONV = 3
LANES = 128
BF16_ROWS = 16
VMEM_LIMIT = 56 * 1024 * 1024

ROW_TILE = 512
ATT_Q_TILE = 512
ATT_K_TILE = 512
LOG2_E = math.log2(math.e)
ML_CHUNK = 256
ML_SEQS = 2
X_TILE = 1024
PROJ_COLS = 1024
FFN_CHUNK = 256
ROW_PHASES = 4
DOWN_CHUNKS = 4
TRANSPOSE_COLS = 512
F_PIECES = 3
FOX_BIAS_K_LANE = 0
FOX_BIAS_Q_LANE = 32


def _params(*sem):
    return pltpu.CompilerParams(dimension_semantics=sem, vmem_limit_bytes=VMEM_LIMIT)


def _resident(shape):
    return pl.BlockSpec(shape, lambda *_: (0,) * len(shape), pipeline_mode=pl.Buffered(1))


def _whole(a):
    if isinstance(a, tuple):
        arr, layer = a
        tail = arr.shape[1:]
        spec = pl.BlockSpec((None,) + tail, lambda *_: (layer,) + (0,) * len(tail), pipeline_mode=pl.Buffered(1))
        return spec, arr
    return _resident(a.shape), a


def _cols(a):
    return (a[0] if isinstance(a, tuple) else a).shape[-1]


def _rms(x, gain):
    return x * lax.rsqrt(jnp.mean(x * x, axis=-1, keepdims=True) + RMS_EPS) * gain


def _sigmoid(x):
    return 1.0 / (1.0 + jnp.exp(-x))


def _dot(a, b):
    return jnp.dot(a, b, preferred_element_type=F32)


def _dot_nt(a, b):
    return lax.dot_general(a, b, (((1,), (1,)), ((), ())), preferred_element_type=F32)


def _split3(x):
    hi = x.astype(BF16)
    r1 = x - hi.astype(F32)
    mid = r1.astype(BF16)
    return hi, mid, (r1 - mid.astype(F32)).astype(BF16)


def _dot_select(x, sel):
    hi, mid, lo = _split3(x)
    return _dot(hi, sel) + _dot(mid, sel) + _dot(lo, sel)


def _prefix_matrix(n, block):
    r = lax.broadcasted_iota(jnp.int32, (n, n), 0)
    c = lax.broadcasted_iota(jnp.int32, (n, n), 1)
    return jnp.where((r <= c) & (r // block == c // block), 1.0, 0.0).astype(BF16)


def _tile_lanes(a, n):
    return jnp.concatenate([a] * (n // LANES), axis=1)


def _log_sigmoid(x):
    return jnp.minimum(x, 0.0) - jnp.log1p(jnp.exp(-jnp.abs(x)))


def _prepare_weights(w_ref, wb_ref, wgt_ref, pieces, gate_col, n_gates):
    dst = 0
    for src, width, scale in pieces:
        for c0 in range(0, width, PROJ_COLS):
            n = min(PROJ_COLS, width - c0)
            wb_ref[:, dst + c0:dst + c0 + n] = (w_ref[:, src + c0:src + c0 + n] * scale).astype(BF16)
        dst += width
    slab0 = gate_col // LANES * LANES
    slab_t = w_ref[:, slab0:slab0 + LANES].T
    row = lax.broadcasted_iota(jnp.int32, (LANES, 1), 0)
    first = gate_col - slab0
    gates = jnp.where((row >= first) & (row < first + n_gates), slab_t, 0.0)
    wgt_ref[...] = gates[first:first + BF16_ROWS].astype(BF16)


def _fox_proj_kernel(x_ref, g_ref, w_ref, bf_ref, o_ref, fq_ref, fk_ref, carry_ref, wb_ref, wg_ref,
                     *, tm, tn, tiles_per_seq):
    i = pl.program_id(0)

    @pl.when(i == 0)
    def _():
        fw = FOX_HEADS * FOX_DIM
        dq0 = 3 * fw + FOX_HEADS
        dqw = DIFF_HEADS * 2 * DIFF_QK
        _prepare_weights(w_ref, wb_ref, wg_ref,
                         ((0, fw, FOX_DIM ** -0.5 * LOG2_E), (fw, 2 * fw, 1.0),
                          (dq0, dqw, DIFF_QK ** -0.5 * LOG2_E), (dq0 + dqw, w_ref.shape[1] - dq0 - dqw, 1.0)),
                         3 * fw, FOX_HEADS)

    @pl.when(i % tiles_per_seq == 0)
    def _():
        carry_ref[...] = jnp.zeros_like(carry_ref)

    xn = _rms(x_ref[...], g_ref[...]).astype(BF16)

    def project(part):
        width = o_ref.shape[1] // 3
        for c0 in range(part * width, (part + 1) * width, tn):
            o_ref[:, c0:c0 + tn] = _dot(xn, wb_ref[:, c0:c0 + tn]).astype(o_ref.dtype)

    gates_t = _dot_nt(wg_ref[...], xn)
    project(0)
    log_f = _log_sigmoid(gates_t + _tile_lanes(bf_ref[...], tm)) * LOG2_E
    f_t = _tile_lanes(carry_ref[...], tm) + _dot_select(log_f, _prefix_matrix(tm, tm))
    carry_ref[...] = jnp.broadcast_to(f_t[:, tm - 1:tm], carry_ref.shape)
    f = jnp.concatenate([f_t, jnp.zeros((LANES - BF16_ROWS, tm), F32)], axis=0).T
    project(1)

    src = lax.broadcasted_iota(jnp.int32, (LANES, LANES), 0)
    dst = lax.broadcasted_iota(jnp.int32, (LANES, LANES), 1)
    lane = lax.broadcasted_iota(jnp.int32, (1, LANES), 1)
    n_bias = F_PIECES * FOX_HEADS
    ones_q = jnp.where((lane >= FOX_BIAS_K_LANE) & (lane < FOX_BIAS_K_LANE + n_bias), 1.0, 0.0)
    ones_k = jnp.where((lane >= FOX_BIAS_Q_LANE) & (lane < FOX_BIAS_Q_LANE + n_bias), 1.0, 0.0)
    both = jnp.concatenate([ones_q, ones_k], axis=1)
    for c, piece in enumerate(_split3(f)):
        head = src < FOX_HEADS
        to_q = jnp.where(head & (dst == FOX_BIAS_Q_LANE + F_PIECES * src + c), 1.0, 0.0)
        to_k = jnp.where(head & (dst == FOX_BIAS_K_LANE + F_PIECES * src + c), -1.0, 0.0)
        both = both + _dot(piece, jnp.concatenate([to_q, to_k], axis=1).astype(BF16))
    fq_ref[...] = both[:, 0:LANES].astype(BF16)
    fk_ref[...] = both[:, LANES:2 * LANES].astype(BF16)
    project(2)


def fox_projection(x, gain, w, gate_bias, seq_len, *, tm=ROW_TILE, tn=PROJ_COLS):
    T, D = x.shape
    N = _cols(w) - FOX_HEADS
    row_block = lambda n: pl.BlockSpec((tm, n), lambda i: (i, 0))
    specs, operands = zip(*map(_whole, (gain, w, gate_bias)))
    return pl.pallas_call(
        functools.partial(_fox_proj_kernel, tm=tm, tn=tn, tiles_per_seq=seq_len // tm),
        grid=(T // tm,),
        in_specs=[row_block(D), *specs],
        out_specs=[row_block(N), row_block(LANES), row_block(LANES)],
        out_shape=[jax.ShapeDtypeStruct((T, N), BF16), jax.ShapeDtypeStruct((T, LANES), BF16),
                   jax.ShapeDtypeStruct((T, LANES), BF16)],
        scratch_shapes=[pltpu.VMEM((BF16_ROWS, LANES), F32), pltpu.VMEM((D, N), BF16),
                        pltpu.VMEM((BF16_ROWS, D), BF16)],
        compiler_params=_params("arbitrary"),
        name="fox_projection",
    )(x, *operands)


def _mlstm_proj_kernel(x_ref, halo_ref, g_ref, w_raw_ref, gb_ref, cw_ref, o_ref, gr_ref, up_ref, nat_ref, w_ref,
                       wg_ref, *, tm, tiles_per_seq):
    i = pl.program_id(0)

    @pl.when(i == 0)
    def _():
        g0 = 2 * ML_HEADS * ML_QK + ML_HEADS * ML_V
        og_col = g0 + 2 * ML_HEADS
        _prepare_weights(w_raw_ref, w_ref, wg_ref, ((0, g0, 1.0), (og_col, w_raw_ref.shape[1] - og_col, 1.0)),
                         g0, 2 * ML_HEADS)

    halo = _rms(halo_ref[...], g_ref[...])
    xe = jnp.concatenate([jnp.where(i % tiles_per_seq == 0, 0.0, halo).astype(BF16),
                          _rms(x_ref[...], g_ref[...]).astype(BF16)], axis=0)
    xn = xe[BF16_ROWS:]

    qk_slabs = 2 * ML_HEADS * ML_QK // LANES
    rows = tm // ROW_PHASES
    v0 = qk_slabs * LANES
    og0 = v0 + ML_HEADS * ML_V
    for s0 in range(0, qk_slabs, 2):
        up = _dot(xe, w_ref[:, s0 * LANES:(s0 + 2) * LANES])
        up_ref[s0] = up[:, 0:LANES]
        up_ref[s0 + 1] = up[:, LANES:2 * LANES]
    pre_t = _dot_nt(wg_ref[...], xn) + _tile_lanes(gb_ref[...], tm)
    o_ref[:, v0:og0] = _dot(xn, w_ref[:, v0:og0]).astype(o_ref.dtype)
    for s in range(qk_slabs):
        cw = cw_ref[:, s * LANES:(s + 1) * LANES]
        for phase in range(ROW_PHASES):
            y = None
            for tap in range(ML_CONV):
                first = BF16_ROWS + phase - (ML_CONV - 1 - tap)
                term = cw[tap:tap + 1] * up_ref[s, pl.ds(first, rows, stride=ROW_PHASES), :]
                y = term if y is None else y + term
            y = y * _sigmoid(y)
            if s < qk_slabs // 2:
                y = y * (ML_QK ** -0.5)
            nat_ref[s, pl.ds(phase, rows, stride=ROW_PHASES), :] = y
        o_ref[:, s * LANES:(s + 1) * LANES] = nat_ref[s].astype(o_ref.dtype)

    og = _dot(xn, w_ref[:, og0:])
    row = lax.broadcasted_iota(jnp.int32, (BF16_ROWS, 1), 0)
    val_t = jnp.where(row < ML_HEADS, pre_t, _log_sigmoid(pre_t)) * LOG2_E
    g_t = jnp.where(row < ML_HEADS, val_t, _dot_select(val_t, _prefix_matrix(tm, ML_CHUNK)))
    gr_ref[0] = g_t[0:2 * ML_HEADS]
    o_ref[:, og0:] = _sigmoid(og).astype(o_ref.dtype)


def mlstm_projection(x, gain, w, gate_bias, conv_w, seq_len, *, tm=ROW_TILE):
    T, D = x.shape
    N = _cols(w) - 2 * ML_HEADS
    tiles_per_seq = seq_len // tm
    halo_blocks = tm // BF16_ROWS
    row_block = lambda n: pl.BlockSpec((tm, n), lambda i: (i, 0))
    specs, operands = zip(*map(_whole, (gain, w, gate_bias, conv_w)))
    return pl.pallas_call(
        functools.partial(_mlstm_proj_kernel, tm=tm, tiles_per_seq=tiles_per_seq),
        grid=(T // tm,),
        in_specs=[row_block(D),
                  pl.BlockSpec((BF16_ROWS, D), lambda i: (jnp.maximum(i * halo_blocks - 1, 0), 0)),
                  *specs],
        out_specs=[row_block(N),
                   pl.BlockSpec((1, 2 * ML_HEADS, tm), lambda i: (i // tiles_per_seq, 0, i % tiles_per_seq))],
        out_shape=[jax.ShapeDtypeStruct((T, N), BF16),
                   jax.ShapeDtypeStruct((T // seq_len, 2 * ML_HEADS, seq_len), F32)],
        scratch_shapes=[pltpu.VMEM((2 * ML_HEADS * ML_QK // LANES, tm + BF16_ROWS, LANES), F32),
                        pltpu.VMEM((2 * ML_HEADS * ML_QK // LANES, tm, LANES), F32),
                        pltpu.VMEM((D, N), BF16), pltpu.VMEM((BF16_ROWS, D), BF16)],
        compiler_params=_params("arbitrary"),
        name="mlstm_projection",
    )(x, x, *operands)


def _build_vt(v_ref, vt_ref, n_heads, rows):
    S = v_ref.shape[1]
    r = lax.broadcasted_iota(jnp.int32, (LANES, LANES), 0)
    c = lax.broadcasted_iota(jnp.int32, (LANES, LANES), 1)
    eye = jnp.where(r == c, 1.0, 0.0).astype(BF16)
    per_group = LANES // rows
    for g in range(n_heads // per_group):
        for c0 in range(0, S, TRANSPOSE_COLS):
            cs = slice(c0, c0 + TRANSPOSE_COLS)
            vt = _dot_nt(eye, v_ref[0, cs, g * LANES:(g + 1) * LANES]).astype(BF16)
            for k in range(per_group):
                vt_ref[g * per_group + k, 0:rows, cs] = vt[k * rows:(k + 1) * rows]
    for h in range(n_heads):
        vt_ref[h, rows:rows + BF16_ROWS, :] = jnp.ones((BF16_ROWS, S), BF16)


def _softmax_step(s_t, vt, m_ref, acc_ref, idx, queries=slice(None)):
    m_prev = m_ref[idx, :, queries]
    m_new = jnp.maximum(m_prev, jnp.max(s_t, axis=0, keepdims=True))
    p = jnp.exp2(s_t - m_new).astype(BF16)
    acc_ref[idx, :, queries] = jnp.exp2(m_prev - m_new) * acc_ref[idx, :, queries] + _dot(vt, p)
    m_ref[idx, :, queries] = m_new


def _causal_steps(i, logits, attend, n_streams, buf_a, buf_b):
    def phase(j, src, dst, next_is_diagonal=False):
        for n in range(n_streams):
            logits(j + 1, dst, n, next_is_diagonal)
            attend(j, src, n, False)

    for n in range(n_streams):
        logits(0, buf_a, n, False)

    def body(jj, carry):
        phase(2 * jj, buf_a, buf_b)
        phase(2 * jj + 1, buf_b, buf_a)
        return carry

    lax.fori_loop(0, lax.shift_right_logical(i, 1), body, 0)
    odd = lax.rem(i, 2) == 1

    @pl.when(odd)
    def _():
        phase(i - 1, buf_a, buf_b, True)
        for n in range(n_streams):
            attend(i, buf_b, n, True)

    @pl.when(jnp.logical_not(odd))
    def _():
        for n in range(n_streams):
            attend(i, buf_a, n, True)


def _fox_kernel(q_ref, k_ref, v_ref, fq_ref, fk_ref, o_ref, vt_ref, qc_ref, acc_ref, m_ref, sa_ref, sb_ref, *, tq, tk):
    i = pl.program_id(1)

    @pl.when(i == 0)
    def _():
        _build_vt(v_ref, vt_ref, FOX_HEADS, FOX_DIM)

    lane = lax.broadcasted_iota(jnp.int32, (1, LANES), 1)
    fq = fq_ref[0]
    for h in range(FOX_HEADS):
        pair, half = divmod(h, 2)
        q = q_ref[0, :, pair * LANES:(pair + 1) * LANES]
        in_head = (lane >= half * FOX_DIM) & (lane < (half + 1) * FOX_DIM)
        lo_k, lo_q = FOX_BIAS_K_LANE + F_PIECES * h, FOX_BIAS_Q_LANE + F_PIECES * h
        mine = ((lane >= lo_k) & (lane < lo_k + F_PIECES)) | ((lane >= lo_q) & (lane < lo_q + F_PIECES))
        qc_ref[h, :, 0:LANES] = jnp.where(in_head, q, jnp.zeros_like(q))
        qc_ref[h, :, LANES:2 * LANES] = jnp.where(mine, fq, jnp.zeros_like(fq))
    m_ref[...] = jnp.full(m_ref.shape, NEG_INF, F32)
    acc_ref[...] = jnp.zeros(acc_ref.shape, F32)

    half = tk // 2
    key = lax.broadcasted_iota(jnp.int32, (half, tq), 0)
    qry = lax.broadcasted_iota(jnp.int32, (half, tq), 1)
    causal = key <= qry

    def key_rows(j, part=None):
        if part is None:
            return pl.ds(pl.multiple_of(j * tk, tk), tk)
        return pl.ds(pl.multiple_of(j * tk + part * half, half), half)

    def logits(j, buf, h, diagonal):
        pair = h // 2
        kc = jnp.concatenate([k_ref[0, key_rows(j), pair * LANES:(pair + 1) * LANES], fk_ref[0, key_rows(j), :]],
                             axis=1)
        if diagonal:
            buf[h, 0:half, :] = _dot_nt(kc[0:half], qc_ref[h])
            buf[h, half:tk, half:tq] = _dot_nt(kc[half:tk], qc_ref[h, half:tq, :])
        else:
            buf[h] = _dot_nt(kc, qc_ref[h])

    def attend(j, buf, h, diagonal):
        if diagonal:
            _softmax_step(jnp.where(causal, buf[h, 0:half, :], NEG_INF), vt_ref[h, :, key_rows(j, 0)],
                          m_ref, acc_ref, h)
            _softmax_step(jnp.where(causal[:, 0:tq - half], buf[h, half:tk, half:tq], NEG_INF),
                          vt_ref[h, :, key_rows(j, 1)], m_ref, acc_ref, h, slice(half, tq))
        else:
            _softmax_step(buf[h], vt_ref[h, :, key_rows(j)], m_ref, acc_ref, h)

    _causal_steps(i, logits, attend, FOX_HEADS, sa_ref, sb_ref)

    for pair in range(FOX_HEADS // 2):
        halves = []
        for half in range(2):
            a = acc_ref[2 * pair + half]
            halves.append(a[0:FOX_DIM] / a[FOX_DIM:FOX_DIM + 1])
        o_ref[0, :, pair * LANES:(pair + 1) * LANES] = jnp.concatenate(halves, axis=0).T.astype(o_ref.dtype)


def fox_attention(proj, fq, fk, *, tq=ATT_Q_TILE, tk=ATT_K_TILE):
    B, S, _ = proj.shape
    assert tq == tk
    width = FOX_HEADS * FOX_DIM
    rows = FOX_DIM + BF16_ROWS
    return pl.pallas_call(
        functools.partial(_fox_kernel, tq=tq, tk=tk),
        grid=(B, S // tq),
        in_specs=[pl.BlockSpec((1, tq, width), lambda b, i: (b, i, 0)),
                  pl.BlockSpec((1, S, width), lambda b, i: (b, 0, 1)),
                  pl.BlockSpec((1, S, width), lambda b, i: (b, 0, 2)),
                  pl.BlockSpec((1, tq, LANES), lambda b, i: (b, i, 0)),
                  pl.BlockSpec((1, S, LANES), lambda b, i: (b, 0, 0))],
        out_specs=pl.BlockSpec((1, tq, width), lambda b, i: (b, i, 0)),
        out_shape=jax.ShapeDtypeStruct((B, S, width), BF16),
        scratch_shapes=[pltpu.VMEM((FOX_HEADS, rows, S), BF16),
                        pltpu.VMEM((FOX_HEADS, tq, 2 * LANES), BF16),
                        pltpu.VMEM((FOX_HEADS, rows, tq), F32),
                        pltpu.VMEM((FOX_HEADS, 1, tq), F32),
                        pltpu.VMEM((FOX_HEADS, tk, tq), F32),
                        pltpu.VMEM((FOX_HEADS, tk, tq), F32)],
        compiler_params=_params("parallel", "arbitrary"),
        name="fox_attention",
    )(proj, proj, proj, fq, fk)


def _diff_kernel(q_ref, k_ref, v_ref, lq1_ref, lk1_ref, lq2_ref, lk2_ref, sub_ref, o_ref, vt_ref, qc_ref, acc_ref,
                 m_ref, sa_ref, sb_ref, *, tq, tk, lambda_init):
    i = pl.program_id(1)

    @pl.when(i == 0)
    def _():
        _build_vt(v_ref, vt_ref, DIFF_HEADS, DIFF_V)

    lane = lax.broadcasted_iota(jnp.int32, (1, LANES), 1)
    for h in range(DIFF_HEADS):
        q = q_ref[0, :, h * LANES:(h + 1) * LANES]
        zero = jnp.zeros_like(q)
        qc_ref[2 * h] = jnp.where(lane < DIFF_QK, q, zero)
        qc_ref[2 * h + 1] = jnp.where(lane >= DIFF_QK, q, zero)
    m_ref[...] = jnp.full(m_ref.shape, NEG_INF, F32)
    acc_ref[...] = jnp.zeros(acc_ref.shape, F32)

    half = tk // 2
    key = lax.broadcasted_iota(jnp.int32, (half, tq), 0)
    qry = lax.broadcasted_iota(jnp.int32, (half, tq), 1)
    visible = key // CHUNK <= qry // CHUNK

    def key_rows(j, part=None):
        if part is None:
            return pl.ds(pl.multiple_of(j * tk, tk), tk)
        return pl.ds(pl.multiple_of(j * tk + part * half, half), half)

    def logits(j, buf, n, diagonal):
        k = k_ref[0, key_rows(j), (n // 2) * LANES:(n // 2 + 1) * LANES]
        if diagonal:
            buf[n, 0:half, :] = _dot_nt(k[0:half], qc_ref[n])
            buf[n, half:tk, half:tq] = _dot_nt(k[half:tk], qc_ref[n, half:tq, :])
        else:
            buf[n] = _dot_nt(k, qc_ref[n])

    def attend(j, buf, n, diagonal):
        vt = vt_ref.at[n // 2]
        if diagonal:
            _softmax_step(jnp.where(visible, buf[n, 0:half, :], NEG_INF), vt[:, key_rows(j, 0)], m_ref, acc_ref, n)
            _softmax_step(jnp.where(visible[:, 0:tq - half], buf[n, half:tk, half:tq], NEG_INF),
                          vt[:, key_rows(j, 1)], m_ref, acc_ref, n, slice(half, tq))
        else:
            _softmax_step(buf[n], vt[:, key_rows(j)], m_ref, acc_ref, n)

    _causal_steps(i, logits, attend, 2 * DIFF_HEADS, sa_ref, sb_ref)

    lam = (jnp.exp(jnp.sum(lq1_ref[...] * lk1_ref[...], axis=1, keepdims=True))
           - jnp.exp(jnp.sum(lq2_ref[...] * lk2_ref[...], axis=1, keepdims=True)) + lambda_init)
    for h in range(DIFF_HEADS):
        a1, a2 = acc_ref[2 * h], acc_ref[2 * h + 1]
        o_t = a1[0:DIFF_V] / a1[DIFF_V:DIFF_V + 1] - lam * (a2[0:DIFF_V] / a2[DIFF_V:DIFF_V + 1])
        out = _rms(o_t.T, sub_ref[...]) * (1.0 - lambda_init)
        o_ref[0, :, h * LANES:(h + 1) * LANES] = out.astype(o_ref.dtype)


def diff_attention(proj, lq1, lk1, lq2, lk2, subln, lambda_init, *, tq=ATT_Q_TILE, tk=ATT_K_TILE):
    B, S, _ = proj.shape
    assert tq == tk
    width = DIFF_HEADS * DIFF_V
    rows = DIFF_V + BF16_ROWS
    specs, operands = zip(*map(_whole, (lq1, lk1, lq2, lk2, subln)))
    return pl.pallas_call(
        functools.partial(_diff_kernel, tq=tq, tk=tk, lambda_init=lambda_init),
        grid=(B, S // tq),
        in_specs=[pl.BlockSpec((1, tq, width), lambda b, i: (b, i, 3)),
                  pl.BlockSpec((1, S, width), lambda b, i: (b, 0, 4)),
                  pl.BlockSpec((1, S, width), lambda b, i: (b, 0, 5)),
                  *specs],
        out_specs=pl.BlockSpec((1, tq, width), lambda b, i: (b, i, 0)),
        out_shape=jax.ShapeDtypeStruct((B, S, width), BF16),
        scratch_shapes=[pltpu.VMEM((DIFF_HEADS, rows, S), BF16),
                        pltpu.VMEM((2 * DIFF_HEADS, tq, LANES), BF16),
                        pltpu.VMEM((2 * DIFF_HEADS, rows, tq), F32),
                        pltpu.VMEM((2 * DIFF_HEADS, 1, tq), F32),
                        pltpu.VMEM((2 * DIFF_HEADS, tk, tq), F32),
                        pltpu.VMEM((2 * DIFF_HEADS, tk, tq), F32)],
        compiler_params=_params("parallel", "arbitrary"),
        name="diff_attention",
    )(proj, proj, proj, *operands)


def _mlstm_kernel(qk_ref, v_ref, sg_ref, gr_ref, hn_ref, o_ref, ct_ref, m_ref, vt_ref, *, L, nb):
    c = pl.program_id(1)

    @pl.when(c == 0)
    def _():
        ct_ref[...] = jnp.zeros_like(ct_ref)
        m_ref[...] = jnp.zeros_like(m_ref)
        for n in range(nb * ML_HEADS):
            vt_ref[n, ML_V:ML_V + BF16_ROWS, :] = jnp.ones((BF16_ROWS, L), BF16)

    src = lax.broadcasted_iota(jnp.int32, (L, L), 0)
    dst = lax.broadcasted_iota(jnp.int32, (L, L), 1)
    causal = src <= dst
    eye = jnp.where(src == dst, 1.0, 0.0).astype(BF16)
    k0 = ML_HEADS * ML_QK
    for b in range(nb):
        gr = gr_ref[b]
        gc = jnp.concatenate([gr, jnp.zeros((LANES - gr.shape[0], L), F32)], axis=0).T
        for h in range(ML_HEADS):
            n = b * ML_HEADS + h
            q = qk_ref[b, :, h * ML_QK:(h + 1) * ML_QK]
            k = qk_ref[b, :, k0 + h * ML_QK:k0 + (h + 1) * ML_QK]
            vt_ref[n, 0:ML_V, :] = _dot_nt(eye, v_ref[b, :, h * ML_V:(h + 1) * ML_V]).astype(BF16)
            v_t = vt_ref[n]
            r_col = gc[:, h:h + 1] - gc[:, ML_HEADS + h:ML_HEADS + h + 1]
            b_row = gr[ML_HEADS + h:ML_HEADS + h + 1, :]
            r_row = gr[h:h + 1, :] - b_row
            g = b_row[:, L - 1:L]
            ct = ct_ref[n]
            m = m_ref[n][:, 0:1]

            dm = jnp.where(causal, r_col, NEG_INF)
            mt = jnp.maximum(m, jnp.max(dm, axis=0, keepdims=True))
            s_t = _dot_nt(k, q) * jnp.exp2(dm - mt)
            both = jnp.exp2(m - mt) * _dot_nt(ct.astype(BF16), q) + _dot(v_t, s_t.astype(BF16))
            den = both[ML_V:ML_V + 1]
            hh = both[0:ML_V] / jnp.maximum(jnp.abs(den), jnp.exp2(-(b_row + mt)))
            hh = hh * lax.rsqrt(jnp.mean(hh * hh, axis=0, keepdims=True) + RMS_EPS)

            m_next = jnp.maximum(m, jnp.max(r_row, axis=1, keepdims=True))
            kw = (k.astype(F32) * jnp.exp2(r_col - m_next)).astype(BF16)
            ct_ref[n] = jnp.exp2(m - m_next) * ct + _dot(v_t, kw)
            m_ref[n] = jnp.broadcast_to(g + m_next, (1, LANES))

            vs = slice(h * ML_V, (h + 1) * ML_V)
            o_ref[b, :, vs] = (hh.T * hn_ref[:, vs] * sg_ref[b, :, vs].astype(F32)).astype(o_ref.dtype)


def mlstm(proj, g_row, head_norm, *, L=ML_CHUNK, nb=ML_SEQS):
    B, S, _ = proj.shape
    W = D_MODEL
    assert L == ML_V, "one identity matrix serves the v transposes"
    norm_spec, head_norm = _whole(head_norm)
    return pl.pallas_call(
        functools.partial(_mlstm_kernel, L=L, nb=nb),
        grid=(B // nb, S // L),
        in_specs=[pl.BlockSpec((nb, L, W), lambda b, c: (b, c, 0)),
                  pl.BlockSpec((nb, L, W), lambda b, c: (b, c, 1)),
                  pl.BlockSpec((nb, L, W), lambda b, c: (b, c, 2)),
                  pl.BlockSpec((nb, 2 * ML_HEADS, L), lambda b, c: (b, 0, c)),
                  norm_spec],
        out_specs=pl.BlockSpec((nb, L, W), lambda b, c: (b, c, 0)),
        out_shape=jax.ShapeDtypeStruct((B, S, W), BF16),
        scratch_shapes=[pltpu.VMEM((nb * ML_HEADS, ML_V + BF16_ROWS, ML_QK), F32),
                        pltpu.VMEM((nb * ML_HEADS, 1, LANES), F32),
                        pltpu.VMEM((nb * ML_HEADS, ML_V + BF16_ROWS, L), BF16)],
        compiler_params=_params("parallel", "arbitrary"),
        name="mlstm",
    )(proj, proj, proj, g_row, head_norm)


def _xattn_kernel(*refs, n_in):
    x_ref, g_ref, wq_ref, mem_ref, gm_ref, wkv_ref, wo_ref, o_ref, kv_ref = refs[2 * n_in:]

    @pl.when(pl.program_id(1) == 0)
    def _():
        memn = _rms(mem_ref[0], gm_ref[...]).astype(BF16)
        for c0 in range(0, kv_ref.shape[1], PROJ_COLS):
            kv_ref[:, c0:c0 + PROJ_COLS] = _dot(memn, wkv_ref[:, c0:c0 + PROJ_COLS]).astype(BF16)

    x = x_ref[0]
    for a_ref, w_ref in zip(refs[:n_in], refs[n_in:2 * n_in]):
        x = x + _dot(a_ref[0], w_ref[...])
    xn = _rms(x, g_ref[...]).astype(BF16)
    q = (_dot(xn, wq_ref[...]) * (X_DIM ** -0.5)).astype(BF16)
    heads = []
    for h in range(X_HEADS):
        cols = slice(h * X_DIM, (h + 1) * X_DIM)
        s = _dot_nt(q[:, cols], kv_ref[:, cols])
        p = jnp.exp(s - jnp.max(s, axis=1, keepdims=True))
        l = jnp.sum(p, axis=1, keepdims=True)
        v = kv_ref[:, D_MODEL + h * X_DIM:D_MODEL + (h + 1) * X_DIM]
        heads.append((_dot(p.astype(BF16), v) / l).astype(BF16))
    o_ref[0] = x + _dot(jnp.concatenate(heads, axis=1), wo_ref[...])


def mix_out_cross_attention(acts, w_mix, x, gain, wq, mem, mem_gain, wkv, wo, *, tq=X_TILE):
    B, S, D = x.shape
    M = mem.shape[1]
    (gain_spec, gain), (wq_spec, wq), (gm_spec, mem_gain), (wkv_spec, wkv), (wo_spec, wo) = map(
        _whole, (gain, wq, mem_gain, wkv, wo))
    return pl.pallas_call(
        functools.partial(_xattn_kernel, n_in=len(acts)),
        grid=(B, S // tq),
        in_specs=([pl.BlockSpec((1, tq, a.shape[2]), lambda b, i: (b, i, 0)) for a in acts]
                  + [w[0] for w in w_mix]
                  + [pl.BlockSpec((1, tq, D), lambda b, i: (b, i, 0)),
                     gain_spec, wq_spec,
                     pl.BlockSpec((1, M, D), lambda b, i: (b, 0, 0)),
                     gm_spec, wkv_spec, wo_spec]),
        out_specs=pl.BlockSpec((1, tq, D), lambda b, i: (b, i, 0)),
        out_shape=jax.ShapeDtypeStruct((B, S, D), F32),
        scratch_shapes=[pltpu.VMEM((M, 2 * D), BF16)],
        compiler_params=_params("parallel", "arbitrary"),
        name="mix_out_cross_attention",
    )(*acts, *[w[1] for w in w_mix], x, gain, wq, mem, mem_gain, wkv, wo)


def _gelu_tanh(x):
    k = -2.0 * math.sqrt(2.0 / math.pi) * math.log2(math.e)
    return x / (1.0 + jnp.exp2(x * (k * 0.044715 * (x * x) + k)))


def _ffn_kernel(x_ref, halo_ref, g_ref, wup_ref, cw_ref, cb_ref, wd_ref, fg_ref, o_ref, up_ref, act_ref, nat_ref,
                *, tm, tiles_per_seq, final_norm):
    i = pl.program_id(0)
    halo = _rms(halo_ref[...], g_ref[...])
    xe = jnp.concatenate([jnp.where(i % tiles_per_seq == 0, 0.0, halo).astype(BF16),
                          _rms(x_ref[...], g_ref[...]).astype(BF16)], axis=0)

    n_chunks = D_FF // FFN_CHUNK
    slabs = FFN_CHUNK // LANES
    rows = tm // ROW_PHASES

    def up_project(c):
        for half in range(2):
            col0 = half * D_FF + c * FFN_CHUNK
            up = _dot(xe, wup_ref[:, col0:col0 + FFN_CHUNK])
            for s in range(slabs):
                up_ref[c % 2, half, s] = up[:, s * LANES:(s + 1) * LANES]

    def conv(c, half, s, phase):
        col0 = half * D_FF + c * FFN_CHUNK + s * LANES
        cw = cw_ref[:, col0:col0 + LANES]
        out = cb_ref[:, col0:col0 + LANES]
        for tap in range(FFN_CONV):
            first = BF16_ROWS + phase - (FFN_CONV - 1 - tap)
            out = out + cw[tap:tap + 1] * up_ref[c % 2, half, s, pl.ds(first, rows, stride=ROW_PHASES), :]
        return out

    acc = None
    piece_start = 0
    up_project(0)
    for c in range(n_chunks):
        if c + 1 < n_chunks:
            up_project(c + 1)
        for phase in range(ROW_PHASES):
            for s in range(slabs):
                act = _gelu_tanh(conv(c, 0, s, phase)) * conv(c, 1, s, phase)
                act_ref[phase * rows:(phase + 1) * rows,
                        c * FFN_CHUNK + s * LANES:c * FFN_CHUNK + (s + 1) * LANES] = act.astype(BF16)
        if (c + 1) % DOWN_CHUNKS == 0 or c + 1 == n_chunks:
            piece = slice(piece_start * FFN_CHUNK, (c + 1) * FFN_CHUNK)
            part = _dot(act_ref[:, piece], wd_ref[piece, :])
            acc = part if acc is None else acc + part
            piece_start = c + 1

    for phase in range(ROW_PHASES):
        for s in range(D_MODEL // LANES):
            nat_ref[s, pl.ds(phase, rows, stride=ROW_PHASES), :] = acc[phase * rows:(phase + 1) * rows,
                                                                       s * LANES:(s + 1) * LANES]
    y = x_ref[...] + jnp.concatenate([nat_ref[s] for s in range(D_MODEL // LANES)], axis=1)
    o_ref[...] = _rms(y, fg_ref[...]) if final_norm else y


def conv_ffn(x, gain, w_up, conv_w, conv_b, w_down, seq_len, final_gain=None, *, tm=ROW_TILE):
    T, D = x.shape
    halo_blocks = tm // BF16_ROWS
    final_norm = final_gain is not None
    specs, operands = zip(*map(_whole, (gain, w_up, conv_w, conv_b, w_down, final_gain if final_norm else gain)))
    return pl.pallas_call(
        functools.partial(_ffn_kernel, tm=tm, tiles_per_seq=seq_len // tm, final_norm=final_norm),
        grid=(T // tm,),
        in_specs=[pl.BlockSpec((tm, D), lambda i: (i, 0)),
                  pl.BlockSpec((BF16_ROWS, D), lambda i: (jnp.maximum(i * halo_blocks - 1, 0), 0)),
                  *specs],
        out_specs=pl.BlockSpec((tm, D), lambda i: (i, 0)),
        out_shape=jax.ShapeDtypeStruct((T, D), F32),
        scratch_shapes=[pltpu.VMEM((2, 2, FFN_CHUNK // LANES, tm + BF16_ROWS, LANES), F32),
                        pltpu.VMEM((tm, D_FF), BF16),
                        pltpu.VMEM((D // LANES, tm, LANES), F32)],
        compiler_params=_params("parallel"),
        name="conv_ffn",
    )(x, x, *operands)


def _gate_bias_rows(bias):
    return jnp.broadcast_to(jnp.pad(bias, (0, BF16_ROWS - bias.shape[0]))[:, None], (BF16_ROWS, LANES))


def _fox_diff_mixer(x, B, S, j, gain, w_in, fox_bf, lq1, lk1, lq2, lk2, subln, w_out, lambda_init):
    fw = FOX_HEADS * FOX_DIM
    proj, fq, fk = fox_projection(x, gain, w_in, _gate_bias_rows(fox_bf), S)
    proj = proj.reshape(B, S, -1)
    fox = fox_attention(proj, fq.reshape(B, S, LANES), fk.reshape(B, S, LANES))
    dif = diff_attention(proj, lq1, lk1, lq2, lk2, subln, lambda_init)
    row_block = lambda r: (pl.BlockSpec((None, fw, D_MODEL), lambda *_: (j, r, 0), pipeline_mode=pl.Buffered(1)), w_out)
    return [fox, dif], [row_block(0), row_block(1)]


def _mlstm_mixer(x, B, S, j, gain, w_in, conv_qk, b_i, b_f, head_norm, w_out):
    proj, g_row = mlstm_projection(x, gain, w_in, _gate_bias_rows(jnp.concatenate([b_i, b_f])), conv_qk, S)
    h = mlstm(proj.reshape(B, S, -1), g_row, head_norm)
    return [h], [_whole((w_out, j))]


def kernel(x, mem, mix_norm, xattn_norm, mem_norm, ffn_norm, attn_w_in, attn_fox_bf, diff_lq1, diff_lk1, diff_lq2, diff_lk2, diff_subln, attn_w_out, mlstm_w_in, mlstm_conv_qk, mlstm_b_i, mlstm_b_f, mlstm_head_norm, mlstm_w_out, xattn_wq, xattn_wkv, xattn_wo, ffn_w_up, ffn_conv_w, ffn_conv_b, ffn_w_down, final_norm):
    B, S, D = x.shape
    depth = mix_norm.shape[0]
    x = x.reshape(B * S, D)
    rows = lambda a: a.reshape(a.shape[0], 1, -1)
    to_bf16 = lambda a: a.astype(BF16)
    mix_norm, xattn_norm, mem_norm, ffn_norm = map(rows, (mix_norm, xattn_norm, mem_norm, ffn_norm))
    diff_lq1, diff_lk1, diff_lq2, diff_lk2, diff_subln = map(rows, (diff_lq1, diff_lk1, diff_lq2, diff_lk2, diff_subln))
    mlstm_head_norm, ffn_conv_b = rows(mlstm_head_norm), rows(ffn_conv_b)
    attn_w_out, mlstm_w_out, xattn_wq, xattn_wkv, xattn_wo, ffn_w_up, ffn_w_down = map(
        to_bf16, (attn_w_out, mlstm_w_out, xattn_wq, xattn_wkv, xattn_wo, ffn_w_up, ffn_w_down))
    for layer in range(depth):
        j = layer // 2
        if layer % 2 == 0:
            lambda_init = 0.8 - 0.6 * math.exp(-0.3 * layer)
            mixed, w_mix = _fox_diff_mixer(x, B, S, j, (mix_norm, layer), (attn_w_in, j), attn_fox_bf[j],
                                           (diff_lq1, j), (diff_lk1, j), (diff_lq2, j), (diff_lk2, j),
                                           (diff_subln, j), attn_w_out, lambda_init)
        else:
            mixed, w_mix = _mlstm_mixer(x, B, S, j, (mix_norm, layer), (mlstm_w_in, j), (mlstm_conv_qk, j),
                                        mlstm_b_i[j], mlstm_b_f[j], (mlstm_head_norm, j), mlstm_w_out)
        x = mix_out_cross_attention(mixed, w_mix, x.reshape(B, S, D), (xattn_norm, layer), (xattn_wq, layer), mem,
                                    (mem_norm, layer), (xattn_wkv, layer), (xattn_wo, layer)).reshape(B * S, D)
        x = conv_ffn(x, (ffn_norm, layer), (ffn_w_up, layer), (ffn_conv_w, layer), (ffn_conv_b, layer),
                     (ffn_w_down, layer), S, final_norm.reshape(1, D) if layer == depth - 1 else None)
    return x.reshape(B, S, D)
```

```python
import functools
import math

import jax
import jax.numpy as jnp
from jax import lax
from jax.experimental import pallas as pl
from jax.experimental.pallas import tpu as pltpu

F32 = jnp.float32
BF16 = jnp.bfloat16

D_MODEL = 1024
RMS_EPS = 1e-6
NEG_INF = -1e30
CHUNK = 64
FOX_HEADS, FOX_DIM = 8, 64
DIFF_HEADS, DIFF_QK, DIFF_V = 4, 64, 128
ML_HEADS, ML_QK, ML_V, ML_CONV = 4, 128, 256, 4
X_HEADS, X_DIM = 4, 256
D_FF = 2816
FFN_CONV = 3
LANES = 128
BF16_ROWS = 16
VMEM_LIMIT = 56 * 1024 * 1024

ROW_TILE = 512
ATT_Q_TILE = 512
ATT_K_TILE = 512
LOG2_E = math.log2(math.e)
ML_CHUNK = 256
ML_SEQS = 2
X_TILE = 1024
PROJ_COLS = 1024
FFN_CHUNK = 256
ROW_PHASES = 4
DOWN_CHUNKS = 4
UP_BUFFERS = 2
TRANSPOSE_COLS = 512
F_PIECES = 3
FOX_BIAS_K_LANE = 0
FOX_BIAS_Q_LANE = 32


def _params(*sem):
    return pltpu.CompilerParams(dimension_semantics=sem, vmem_limit_bytes=VMEM_LIMIT)


def _resident(shape):
    return pl.BlockSpec(shape, lambda *_: (0,) * len(shape), pipeline_mode=pl.Buffered(1))


def _whole(a):
    if isinstance(a, tuple):
        arr, layer = a
        tail = arr.shape[1:]
        spec = pl.BlockSpec((None,) + tail, lambda *_: (layer,) + (0,) * len(tail), pipeline_mode=pl.Buffered(1))
        return spec, arr
    return _resident(a.shape), a


def _cols(a):
    return (a[0] if isinstance(a, tuple) else a).shape[-1]


def _rms(x, gain):
    return x * lax.rsqrt(jnp.mean(x * x, axis=-1, keepdims=True) + RMS_EPS) * gain


def _sigmoid(x):
    return 1.0 / (1.0 + jnp.exp(-x))


def _dot(a, b):
    return jnp.dot(a, b, preferred_element_type=F32)


def _dot_nt(a, b):
    return lax.dot_general(a, b, (((1,), (1,)), ((), ())), preferred_element_type=F32)


def _split3(x):
    hi = x.astype(BF16)
    r1 = x - hi.astype(F32)
    mid = r1.astype(BF16)
    return hi, mid, (r1 - mid.astype(F32)).astype(BF16)


def _dot_select(x, sel):
    hi, mid, lo = _split3(x)
    return _dot(hi, sel) + _dot(mid, sel) + _dot(lo, sel)


def _prefix_matrix(n, block):
    r = lax.broadcasted_iota(jnp.int32, (n, n), 0)
    c = lax.broadcasted_iota(jnp.int32, (n, n), 1)
    return jnp.where((r <= c) & (r // block == c // block), 1.0, 0.0).astype(BF16)


def _tile_lanes(a, n):
    return jnp.concatenate([a] * (n // LANES), axis=1)


def _log_sigmoid(x):
    return jnp.minimum(x, 0.0) - jnp.log1p(jnp.exp(-jnp.abs(x)))


def _prepare_weights(w_ref, wb_ref, wgt_ref, pieces, gate_col, n_gates):
    dst = 0
    for src, width, scale in pieces:
        for c0 in range(0, width, PROJ_COLS):
            n = min(PROJ_COLS, width - c0)
            wb_ref[:, dst + c0:dst + c0 + n] = (w_ref[:, src + c0:src + c0 + n] * scale).astype(BF16)
        dst += width
    slab0 = gate_col // LANES * LANES
    slab_t = w_ref[:, slab0:slab0 + LANES].T
    row = lax.broadcasted_iota(jnp.int32, (LANES, 1), 0)
    first = gate_col - slab0
    gates = jnp.where((row >= first) & (row < first + n_gates), slab_t, 0.0)
    wgt_ref[...] = gates[first:first + BF16_ROWS].astype(BF16)


def _fox_proj_kernel(x_ref, g_ref, w_ref, bf_ref, o_ref, fq_ref, fk_ref, carry_ref, wb_ref, wg_ref,
                     *, tm, tn, tiles_per_seq):
    i = pl.program_id(0)

    @pl.when(i == 0)
    def _():
        fw = FOX_HEADS * FOX_DIM
        dq0 = 3 * fw + FOX_HEADS
        dqw = DIFF_HEADS * 2 * DIFF_QK
        _prepare_weights(w_ref, wb_ref, wg_ref,
                         ((0, fw, FOX_DIM ** -0.5 * LOG2_E), (fw, 2 * fw, 1.0),
                          (dq0, dqw, DIFF_QK ** -0.5 * LOG2_E), (dq0 + dqw, w_ref.shape[1] - dq0 - dqw, 1.0)),
                         3 * fw, FOX_HEADS)

    @pl.when(i % tiles_per_seq == 0)
    def _():
        carry_ref[...] = jnp.zeros_like(carry_ref)

    xn = _rms(x_ref[...], g_ref[...]).astype(BF16)

    def project(part):
        width = o_ref.shape[1] // 3
        for c0 in range(part * width, (part + 1) * width, tn):
            o_ref[:, c0:c0 + tn] = _dot(xn, wb_ref[:, c0:c0 + tn]).astype(o_ref.dtype)

    gates_t = _dot_nt(wg_ref[...], xn)
    project(0)
    log_f = _log_sigmoid(gates_t + _tile_lanes(bf_ref[...], tm)) * LOG2_E
    f_t = _tile_lanes(carry_ref[...], tm) + _dot_select(log_f, _prefix_matrix(tm, tm))
    carry_ref[...] = jnp.broadcast_to(f_t[:, tm - 1:tm], carry_ref.shape)
    f = jnp.concatenate([f_t, jnp.zeros((LANES - BF16_ROWS, tm), F32)], axis=0).T
    project(1)

    src = lax.broadcasted_iota(jnp.int32, (LANES, LANES), 0)
    dst = lax.broadcasted_iota(jnp.int32, (LANES, LANES), 1)
    lane = lax.broadcasted_iota(jnp.int32, (1, LANES), 1)
    n_bias = F_PIECES * FOX_HEADS
    ones_q = jnp.where((lane >= FOX_BIAS_K_LANE) & (lane < FOX_BIAS_K_LANE + n_bias), 1.0, 0.0)
    ones_k = jnp.where((lane >= FOX_BIAS_Q_LANE) & (lane < FOX_BIAS_Q_LANE + n_bias), 1.0, 0.0)
    both = jnp.concatenate([ones_q, ones_k], axis=1)
    for c, piece in enumerate(_split3(f)):
        head = src < FOX_HEADS
        to_q = jnp.where(head & (dst == FOX_BIAS_Q_LANE + F_PIECES * src + c), 1.0, 0.0)
        to_k = jnp.where(head & (dst == FOX_BIAS_K_LANE + F_PIECES * src + c), -1.0, 0.0)
        both = both + _dot(piece, jnp.concatenate([to_q, to_k], axis=1).astype(BF16))
    fq_ref[...] = both[:, 0:LANES].astype(BF16)
    fk_ref[...] = both[:, LANES:2 * LANES].astype(BF16)
    project(2)


def fox_projection(x, gain, w, gate_bias, seq_len, *, tm=ROW_TILE, tn=PROJ_COLS):
    T, D = x.shape
    N = _cols(w) - FOX_HEADS
    row_block = lambda n: pl.BlockSpec((tm, n), lambda i: (i, 0))
    specs, operands = zip(*map(_whole, (gain, w, gate_bias)))
    return pl.pallas_call(
        functools.partial(_fox_proj_kernel, tm=tm, tn=tn, tiles_per_seq=seq_len // tm),
        grid=(T // tm,),
        in_specs=[row_block(D), *specs],
        out_specs=[row_block(N), row_block(LANES), row_block(LANES)],
        out_shape=[jax.ShapeDtypeStruct((T, N), BF16), jax.ShapeDtypeStruct((T, LANES), BF16),
                   jax.ShapeDtypeStruct((T, LANES), BF16)],
        scratch_shapes=[pltpu.VMEM((BF16_ROWS, LANES), F32), pltpu.VMEM((D, N), BF16),
                        pltpu.VMEM((BF16_ROWS, D), BF16)],
        compiler_params=_params("arbitrary"),
        name="fox_projection",
    )(x, *operands)


def _mlstm_proj_kernel(x_ref, halo_ref, g_ref, w_raw_ref, gb_ref, cw_ref, o_ref, gr_ref, up_ref, nat_ref, w_ref,
                       wg_ref, *, tm, tiles_per_seq):
    i = pl.program_id(0)

    @pl.when(i == 0)
    def _():
        g0 = 2 * ML_HEADS * ML_QK + ML_HEADS * ML_V
        og_col = g0 + 2 * ML_HEADS
        _prepare_weights(w_raw_ref, w_ref, wg_ref, ((0, g0, 1.0), (og_col, w_raw_ref.shape[1] - og_col, 1.0)),
                         g0, 2 * ML_HEADS)

    halo = _rms(halo_ref[...], g_ref[...])
    xe = jnp.concatenate([jnp.where(i % tiles_per_seq == 0, 0.0, halo).astype(BF16),
                          _rms(x_ref[...], g_ref[...]).astype(BF16)], axis=0)
    xn = xe[BF16_ROWS:]

    qk_slabs = 2 * ML_HEADS * ML_QK // LANES
    rows = tm // ROW_PHASES
    v0 = qk_slabs * LANES
    og0 = v0 + ML_HEADS * ML_V
    for s0 in range(0, qk_slabs, 2):
        up = _dot(xe, w_ref[:, s0 * LANES:(s0 + 2) * LANES])
        up_ref[s0] = up[:, 0:LANES]
        up_ref[s0 + 1] = up[:, LANES:2 * LANES]
    pre_t = _dot_nt(wg_ref[...], xn) + _tile_lanes(gb_ref[...], tm)
    o_ref[:, v0:og0] = _dot(xn, w_ref[:, v0:og0]).astype(o_ref.dtype)
    for s in range(qk_slabs):
        cw = cw_ref[:, s * LANES:(s + 1) * LANES]
        for phase in range(ROW_PHASES):
            y = None
            for tap in range(ML_CONV):
                first = BF16_ROWS + phase - (ML_CONV - 1 - tap)
                term = cw[tap:tap + 1] * up_ref[s, pl.ds(first, rows, stride=ROW_PHASES), :]
                y = term if y is None else y + term
            y = y * _sigmoid(y)
            if s < qk_slabs // 2:
                y = y * (ML_QK ** -0.5)
            nat_ref[s, pl.ds(phase, rows, stride=ROW_PHASES), :] = y
        o_ref[:, s * LANES:(s + 1) * LANES] = nat_ref[s].astype(o_ref.dtype)

    og = _dot(xn, w_ref[:, og0:])
    row = lax.broadcasted_iota(jnp.int32, (BF16_ROWS, 1), 0)
    val_t = jnp.where(row < ML_HEADS, pre_t, _log_sigmoid(pre_t)) * LOG2_E
    g_t = jnp.where(row < ML_HEADS, val_t, _dot_select(val_t, _prefix_matrix(tm, ML_CHUNK)))
    gr_ref[0] = g_t[0:2 * ML_HEADS]
    o_ref[:, og0:] = _sigmoid(og).astype(o_ref.dtype)


def mlstm_projection(x, gain, w, gate_bias, conv_w, seq_len, *, tm=ROW_TILE):
    T, D = x.shape
    N = _cols(w) - 2 * ML_HEADS
    tiles_per_seq = seq_len // tm
    halo_blocks = tm // BF16_ROWS
    row_block = lambda n: pl.BlockSpec((tm, n), lambda i: (i, 0))
    specs, operands = zip(*map(_whole, (gain, w, gate_bias, conv_w)))
    return pl.pallas_call(
        functools.partial(_mlstm_proj_kernel, tm=tm, tiles_per_seq=tiles_per_seq),
        grid=(T // tm,),
        in_specs=[row_block(D),
                  pl.BlockSpec((BF16_ROWS, D), lambda i: (jnp.maximum(i * halo_blocks - 1, 0), 0)),
                  *specs],
        out_specs=[row_block(N),
                   pl.BlockSpec((1, 2 * ML_HEADS, tm), lambda i: (i // tiles_per_seq, 0, i % tiles_per_seq))],
        out_shape=[jax.ShapeDtypeStruct((T, N), BF16),
                   jax.ShapeDtypeStruct((T // seq_len, 2 * ML_HEADS, seq_len), F32)],
        scratch_shapes=[pltpu.VMEM((2 * ML_HEADS * ML_QK // LANES, tm + BF16_ROWS, LANES), F32),
                        pltpu.VMEM((2 * ML_HEADS * ML_QK // LANES, tm, LANES), F32),
                        pltpu.VMEM((D, N), BF16), pltpu.VMEM((BF16_ROWS, D), BF16)],
        compiler_params=_params("arbitrary"),
        name="mlstm_projection",
    )(x, x, *operands)


def _build_vt(v_ref, vt_ref, n_heads, rows):
    S = v_ref.shape[1]
    r = lax.broadcasted_iota(jnp.int32, (LANES, LANES), 0)
    c = lax.broadcasted_iota(jnp.int32, (LANES, LANES), 1)
    eye = jnp.where(r == c, 1.0, 0.0).astype(BF16)
    per_group = LANES // rows
    for g in range(n_heads // per_group):
        for c0 in range(0, S, TRANSPOSE_COLS):
            cs = slice(c0, c0 + TRANSPOSE_COLS)
            vt = _dot_nt(eye, v_ref[0, cs, g * LANES:(g + 1) * LANES]).astype(BF16)
            for k in range(per_group):
                vt_ref[g * per_group + k, 0:rows, cs] = vt[k * rows:(k + 1) * rows]
    for h in range(n_heads):
        vt_ref[h, rows:rows + BF16_ROWS, :] = jnp.ones((BF16_ROWS, S), BF16)


def _softmax_step(s_t, vt, m_ref, acc_ref, idx, queries=slice(None)):
    m_prev = m_ref[idx, :, queries]
    m_new = jnp.maximum(m_prev, jnp.max(s_t, axis=0, keepdims=True))
    p = jnp.exp2(s_t - m_new).astype(BF16)
    acc_ref[idx, :, queries] = jnp.exp2(m_prev - m_new) * acc_ref[idx, :, queries] + _dot(vt, p)
    m_ref[idx, :, queries] = m_new


def _causal_steps(i, logits, attend, next_logits, n_streams, buf_a, buf_b, buf_c):
    def phase(j, src, dst, next_is_diagonal=False):
        for n in range(n_streams):
            logits(j + 1, dst, n, next_is_diagonal)
            attend(j, src, n, False)

    def last(src):
        for n in range(n_streams):
            next_logits(buf_c, n)
            attend(i, src, n, True)

    @pl.when(i == 0)
    def _():
        for n in range(n_streams):
            logits(0, buf_a, n, False)
        last(buf_a)

    @pl.when(i > 0)
    def _():
        phase(0, buf_c, buf_a)

    def body(jj, carry):
        phase(2 * jj + 1, buf_a, buf_b)
        phase(2 * jj + 2, buf_b, buf_a)
        return carry

    lax.fori_loop(0, lax.shift_right_logical(jnp.maximum(i - 1, 0), 1), body, 0)

    @pl.when((i > 0) & (lax.rem(i, 2) == 1))
    def _():
        last(buf_a)

    @pl.when((i > 0) & (lax.rem(i, 2) == 0))
    def _():
        phase(i - 1, buf_a, buf_b, True)
        last(buf_b)


def _fox_kernel(q_ref, qn_ref, k_ref, v_ref, fq_ref, fqn_ref, fk_ref, o_ref, vt_ref, qc_ref, qcn_ref, acc_ref, m_ref,
                sa_ref, sb_ref, sc_ref, *, tq, tk):
    i = pl.program_id(1)

    @pl.when(i == 0)
    def _():
        _build_vt(v_ref, vt_ref, FOX_HEADS, FOX_DIM)

    lane = lax.broadcasted_iota(jnp.int32, (1, LANES), 1)

    def query_operand(q_blk, fq_blk, dst, h):
        pair, side = divmod(h, 2)
        q = q_blk[0, :, pair * LANES:(pair + 1) * LANES]
        fq = fq_blk[0]
        in_head = (lane >= side * FOX_DIM) & (lane < (side + 1) * FOX_DIM)
        lo_k, lo_q = FOX_BIAS_K_LANE + F_PIECES * h, FOX_BIAS_Q_LANE + F_PIECES * h
        mine = ((lane >= lo_k) & (lane < lo_k + F_PIECES)) | ((lane >= lo_q) & (lane < lo_q + F_PIECES))
        dst[h, :, 0:LANES] = jnp.where(in_head, q, jnp.zeros_like(q))
        dst[h, :, LANES:2 * LANES] = jnp.where(mine, fq, jnp.zeros_like(fq))

    for h in range(FOX_HEADS):
        query_operand(q_ref, fq_ref, qc_ref, h)
    m_ref[...] = jnp.full(m_ref.shape, NEG_INF, F32)
    acc_ref[...] = jnp.zeros(acc_ref.shape, F32)

    half = tk // 2
    key = lax.broadcasted_iota(jnp.int32, (half, tq), 0)
    qry = lax.broadcasted_iota(jnp.int32, (half, tq), 1)
    causal = key <= qry

    def key_rows(j, part=None):
        if part is None:
            return pl.ds(pl.multiple_of(j * tk, tk), tk)
        return pl.ds(pl.multiple_of(j * tk + part * half, half), half)

    def key_operand(j, h):
        pair = h // 2
        return jnp.concatenate([k_ref[0, key_rows(j), pair * LANES:(pair + 1) * LANES], fk_ref[0, key_rows(j), :]],
                               axis=1)

    def next_logits(buf, h):
        query_operand(qn_ref, fqn_ref, qcn_ref, h)
        buf[h] = _dot_nt(key_operand(0, h), qcn_ref[h])

    def logits(j, buf, h, diagonal):
        kc = key_operand(j, h)
        if diagonal:
            buf[h, 0:half, :] = _dot_nt(kc[0:half], qc_ref[h])
            buf[h, half:tk, half:tq] = _dot_nt(kc[half:tk], qc_ref[h, half:tq, :])
        else:
            buf[h] = _dot_nt(kc, qc_ref[h])

    def attend(j, buf, h, diagonal):
        if diagonal:
            _softmax_step(jnp.where(causal, buf[h, 0:half, :], NEG_INF), vt_ref[h, :, key_rows(j, 0)],
                          m_ref, acc_ref, h)
            _softmax_step(jnp.where(causal[:, 0:tq - half], buf[h, half:tk, half:tq], NEG_INF),
                          vt_ref[h, :, key_rows(j, 1)], m_ref, acc_ref, h, slice(half, tq))
        else:
            _softmax_step(buf[h], vt_ref[h, :, key_rows(j)], m_ref, acc_ref, h)

    _causal_steps(i, logits, attend, next_logits, FOX_HEADS, sa_ref, sb_ref, sc_ref)

    for pair in range(FOX_HEADS // 2):
        sides = []
        for side in range(2):
            a = acc_ref[2 * pair + side]
            sides.append(a[0:FOX_DIM] / a[FOX_DIM:FOX_DIM + 1])
        o_ref[0, :, pair * LANES:(pair + 1) * LANES] = jnp.concatenate(sides, axis=0).T.astype(o_ref.dtype)


def fox_attention(proj, fq, fk, *, tq=ATT_Q_TILE, tk=ATT_K_TILE):
    B, S, _ = proj.shape
    assert tq == tk
    width = FOX_HEADS * FOX_DIM
    rows = FOX_DIM + BF16_ROWS
    last = S // tq - 1
    per_sequence = lambda n, col: pl.BlockSpec((1, S, n), lambda b, i: (b, 0, col), pipeline_mode=pl.Buffered(1))
    return pl.pallas_call(
        functools.partial(_fox_kernel, tq=tq, tk=tk),
        grid=(B, S // tq),
        in_specs=[pl.BlockSpec((1, tq, width), lambda b, i: (b, i, 0)),
                  pl.BlockSpec((1, tq, width), lambda b, i: (b, jnp.minimum(i + 1, last), 0)),
                  per_sequence(width, 1), per_sequence(width, 2),
                  pl.BlockSpec((1, tq, LANES), lambda b, i: (b, i, 0)),
                  pl.BlockSpec((1, tq, LANES), lambda b, i: (b, jnp.minimum(i + 1, last), 0)),
                  per_sequence(LANES, 0)],
        out_specs=pl.BlockSpec((1, tq, width), lambda b, i: (b, i, 0)),
        out_shape=jax.ShapeDtypeStruct((B, S, width), BF16),
        scratch_shapes=[pltpu.VMEM((FOX_HEADS, rows, S), BF16),
                        pltpu.VMEM((FOX_HEADS, tq, 2 * LANES), BF16),
                        pltpu.VMEM((FOX_HEADS, tq, 2 * LANES), BF16),
                        pltpu.VMEM((FOX_HEADS, rows, tq), F32),
                        pltpu.VMEM((FOX_HEADS, 1, tq), F32),
                        pltpu.VMEM((FOX_HEADS, tk, tq), F32),
                        pltpu.VMEM((FOX_HEADS, tk, tq), F32),
                        pltpu.VMEM((FOX_HEADS, tk, tq), F32)],
        compiler_params=_params("parallel", "arbitrary"),
        name="fox_attention",
    )(proj, proj, proj, proj, fq, fq, fk)


def _diff_kernel(q_ref, qn_ref, k_ref, v_ref, lq1_ref, lk1_ref, lq2_ref, lk2_ref, sub_ref, o_ref, vt_ref, qc_ref,
                 qcn_ref, acc_ref, m_ref, sa_ref, sb_ref, sc_ref, *, tq, tk, lambda_init):
    i = pl.program_id(1)

    @pl.when(i == 0)
    def _():
        _build_vt(v_ref, vt_ref, DIFF_HEADS, DIFF_V)

    lane = lax.broadcasted_iota(jnp.int32, (1, LANES), 1)

    def query_operand(q_blk, dst, n):
        q = q_blk[0, :, (n // 2) * LANES:(n // 2 + 1) * LANES]
        mine = (lane < DIFF_QK) if n % 2 == 0 else (lane >= DIFF_QK)
        dst[n] = jnp.where(mine, q, jnp.zeros_like(q))

    for n in range(2 * DIFF_HEADS):
        query_operand(q_ref, qc_ref, n)
    m_ref[...] = jnp.full(m_ref.shape, NEG_INF, F32)
    acc_ref[...] = jnp.zeros(acc_ref.shape, F32)

    half = tk // 2
    key = lax.broadcasted_iota(jnp.int32, (half, tq), 0)
    qry = lax.broadcasted_iota(jnp.int32, (half, tq), 1)
    visible = key // CHUNK <= qry // CHUNK

    def key_rows(j, part=None):
        if part is None:
            return pl.ds(pl.multiple_of(j * tk, tk), tk)
        return pl.ds(pl.multiple_of(j * tk + part * half, half), half)

    def next_logits(buf, n):
        query_operand(qn_ref, qcn_ref, n)
        buf[n] = _dot_nt(k_ref[0, key_rows(0), (n // 2) * LANES:(n // 2 + 1) * LANES], qcn_ref[n])

    def logits(j, buf, n, diagonal):
        k = k_ref[0, key_rows(j), (n // 2) * LANES:(n // 2 + 1) * LANES]
        if diagonal:
            buf[n, 0:half, :] = _dot_nt(k[0:half], qc_ref[n])
            buf[n, half:tk, half:tq] = _dot_nt(k[half:tk], qc_ref[n, half:tq, :])
        else:
            buf[n] = _dot_nt(k, qc_ref[n])

    def attend(j, buf, n, diagonal):
        vt = vt_ref.at[n // 2]
        if diagonal:
            _softmax_step(jnp.where(visible, buf[n, 0:half, :], NEG_INF), vt[:, key_rows(j, 0)], m_ref, acc_ref, n)
            _softmax_step(jnp.where(visible[:, 0:tq - half], buf[n, half:tk, half:tq], NEG_INF),
                          vt[:, key_rows(j, 1)], m_ref, acc_ref, n, slice(half, tq))
        else:
            _softmax_step(buf[n], vt[:, key_rows(j)], m_ref, acc_ref, n)

    _causal_steps(i, logits, attend, next_logits, 2 * DIFF_HEADS, sa_ref, sb_ref, sc_ref)

    lam = (jnp.exp(jnp.sum(lq1_ref[...] * lk1_ref[...], axis=1, keepdims=True))
           - jnp.exp(jnp.sum(lq2_ref[...] * lk2_ref[...], axis=1, keepdims=True)) + lambda_init)
    for h in range(DIFF_HEADS):
        a1, a2 = acc_ref[2 * h], acc_ref[2 * h + 1]
        o_t = a1[0:DIFF_V] / a1[DIFF_V:DIFF_V + 1] - lam * (a2[0:DIFF_V] / a2[DIFF_V:DIFF_V + 1])
        out = _rms(o_t.T, sub_ref[...]) * (1.0 - lambda_init)
        o_ref[0, :, h * LANES:(h + 1) * LANES] = out.astype(o_ref.dtype)


def diff_attention(proj, lq1, lk1, lq2, lk2, subln, lambda_init, *, tq=ATT_Q_TILE, tk=ATT_K_TILE):
    B, S, _ = proj.shape
    assert tq == tk
    width = DIFF_HEADS * DIFF_V
    rows = DIFF_V + BF16_ROWS
    last = S // tq - 1
    per_sequence = lambda col: pl.BlockSpec((1, S, width), lambda b, i: (b, 0, col), pipeline_mode=pl.Buffered(1))
    specs, operands = zip(*map(_whole, (lq1, lk1, lq2, lk2, subln)))
    return pl.pallas_call(
        functools.partial(_diff_kernel, tq=tq, tk=tk, lambda_init=lambda_init),
        grid=(B, S // tq),
        in_specs=[pl.BlockSpec((1, tq, width), lambda b, i: (b, i, 3)),
                  pl.BlockSpec((1, tq, width), lambda b, i: (b, jnp.minimum(i + 1, last), 3)),
                  per_sequence(4), per_sequence(5),
                  *specs],
        out_specs=pl.BlockSpec((1, tq, width), lambda b, i: (b, i, 0)),
        out_shape=jax.ShapeDtypeStruct((B, S, width), BF16),
        scratch_shapes=[pltpu.VMEM((DIFF_HEADS, rows, S), BF16),
                        pltpu.VMEM((2 * DIFF_HEADS, tq, LANES), BF16),
                        pltpu.VMEM((2 * DIFF_HEADS, tq, LANES), BF16),
                        pltpu.VMEM((2 * DIFF_HEADS, rows, tq), F32),
                        pltpu.VMEM((2 * DIFF_HEADS, 1, tq), F32),
                        pltpu.VMEM((2 * DIFF_HEADS, tk, tq), F32),
                        pltpu.VMEM((2 * DIFF_HEADS, tk, tq), F32),
                        pltpu.VMEM((2 * DIFF_HEADS, tk, tq), F32)],
        compiler_params=_params("parallel", "arbitrary"),
        name="diff_attention",
    )(proj, proj, proj, proj, *operands)


def _mlstm_kernel(qk_ref, v_ref, sg_ref, gr_ref, hn_ref, o_ref, ct_ref, m_ref, vt_ref, *, L, nb):
    c = pl.program_id(1)

    @pl.when(c == 0)
    def _():
        ct_ref[...] = jnp.zeros_like(ct_ref)
        m_ref[...] = jnp.zeros_like(m_ref)
        for n in range(nb * ML_HEADS):
            vt_ref[n, ML_V:ML_V + BF16_ROWS, :] = jnp.ones((BF16_ROWS, L), BF16)

    src = lax.broadcasted_iota(jnp.int32, (L, L), 0)
    dst = lax.broadcasted_iota(jnp.int32, (L, L), 1)
    causal = src <= dst
    eye = jnp.where(src == dst, 1.0, 0.0).astype(BF16)
    k0 = ML_HEADS * ML_QK
    for b in range(nb):
        gr = gr_ref[b]
        gc = jnp.concatenate([gr, jnp.zeros((LANES - gr.shape[0], L), F32)], axis=0).T
        for h in range(ML_HEADS):
            n = b * ML_HEADS + h
            q = qk_ref[b, :, h * ML_QK:(h + 1) * ML_QK]
            k = qk_ref[b, :, k0 + h * ML_QK:k0 + (h + 1) * ML_QK]
            vt_ref[n, 0:ML_V, :] = _dot_nt(eye, v_ref[b, :, h * ML_V:(h + 1) * ML_V]).astype(BF16)
            v_t = vt_ref[n]
            r_col = gc[:, h:h + 1] - gc[:, ML_HEADS + h:ML_HEADS + h + 1]
            b_row = gr[ML_HEADS + h:ML_HEADS + h + 1, :]
            r_row = gr[h:h + 1, :] - b_row
            g = b_row[:, L - 1:L]
            ct = ct_ref[n]
            m = m_ref[n][:, 0:1]

            dm = jnp.where(causal, r_col, NEG_INF)
            mt = jnp.maximum(m, jnp.max(dm, axis=0, keepdims=True))
            s_t = _dot_nt(k, q) * jnp.exp2(dm - mt)
            both = jnp.exp2(m - mt) * _dot_nt(ct.astype(BF16), q) + _dot(v_t, s_t.astype(BF16))
            den = both[ML_V:ML_V + 1]
            hh = both[0:ML_V] / jnp.maximum(jnp.abs(den), jnp.exp2(-(b_row + mt)))
            hh = hh * lax.rsqrt(jnp.mean(hh * hh, axis=0, keepdims=True) + RMS_EPS)

            m_next = jnp.maximum(m, jnp.max(r_row, axis=1, keepdims=True))
            kw = (k.astype(F32) * jnp.exp2(r_col - m_next)).astype(BF16)
            ct_ref[n] = jnp.exp2(m - m_next) * ct + _dot(v_t, kw)
            m_ref[n] = jnp.broadcast_to(g + m_next, (1, LANES))

            vs = slice(h * ML_V, (h + 1) * ML_V)
            o_ref[b, :, vs] = (hh.T * hn_ref[:, vs] * sg_ref[b, :, vs].astype(F32)).astype(o_ref.dtype)


def mlstm(proj, g_row, head_norm, *, L=ML_CHUNK, nb=ML_SEQS):
    B, S, _ = proj.shape
    W = D_MODEL
    assert L == ML_V, "one identity matrix serves the v transposes"
    norm_spec, head_norm = _whole(head_norm)
    return pl.pallas_call(
        functools.partial(_mlstm_kernel, L=L, nb=nb),
        grid=(B // nb, S // L),
        in_specs=[pl.BlockSpec((nb, L, W), lambda b, c: (b, c, 0)),
                  pl.BlockSpec((nb, L, W), lambda b, c: (b, c, 1)),
                  pl.BlockSpec((nb, L, W), lambda b, c: (b, c, 2)),
                  pl.BlockSpec((nb, 2 * ML_HEADS, L), lambda b, c: (b, 0, c)),
                  norm_spec],
        out_specs=pl.BlockSpec((nb, L, W), lambda b, c: (b, c, 0)),
        out_shape=jax.ShapeDtypeStruct((B, S, W), BF16),
        scratch_shapes=[pltpu.VMEM((nb * ML_HEADS, ML_V + BF16_ROWS, ML_QK), F32),
                        pltpu.VMEM((nb * ML_HEADS, 1, LANES), F32),
                        pltpu.VMEM((nb * ML_HEADS, ML_V + BF16_ROWS, L), BF16)],
        compiler_params=_params("parallel", "arbitrary"),
        name="mlstm",
    )(proj, proj, proj, g_row, head_norm)


def _xattn_kernel(*refs, n_in):
    x_ref, g_ref, wq_ref, mem_ref, gm_ref, wkv_ref, wo_ref, o_ref, kv_ref = refs[2 * n_in:]

    @pl.when(pl.program_id(1) == 0)
    def _():
        memn = _rms(mem_ref[0], gm_ref[...]).astype(BF16)
        for c0 in range(0, kv_ref.shape[1], PROJ_COLS):
            kv_ref[:, c0:c0 + PROJ_COLS] = _dot(memn, wkv_ref[:, c0:c0 + PROJ_COLS]).astype(BF16)

    x = x_ref[0]
    for a_ref, w_ref in zip(refs[:n_in], refs[n_in:2 * n_in]):
        x = x + _dot(a_ref[0], w_ref[...])
    xn = _rms(x, g_ref[...]).astype(BF16)
    q = (_dot(xn, wq_ref[...]) * (X_DIM ** -0.5)).astype(BF16)
    heads = []
    for h in range(X_HEADS):
        cols = slice(h * X_DIM, (h + 1) * X_DIM)
        s = _dot_nt(q[:, cols], kv_ref[:, cols])
        p = jnp.exp(s - jnp.max(s, axis=1, keepdims=True))
        l = jnp.sum(p, axis=1, keepdims=True)
        v = kv_ref[:, D_MODEL + h * X_DIM:D_MODEL + (h + 1) * X_DIM]
        heads.append((_dot(p.astype(BF16), v) / l).astype(BF16))
    o_ref[0] = x + _dot(jnp.concatenate(heads, axis=1), wo_ref[...])


def mix_out_cross_attention(acts, w_mix, x, gain, wq, mem, mem_gain, wkv, wo, *, tq=X_TILE):
    B, S, D = x.shape
    M = mem.shape[1]
    (gain_spec, gain), (wq_spec, wq), (gm_spec, mem_gain), (wkv_spec, wkv), (wo_spec, wo) = map(
        _whole, (gain, wq, mem_gain, wkv, wo))
    return pl.pallas_call(
        functools.partial(_xattn_kernel, n_in=len(acts)),
        grid=(B, S // tq),
        in_specs=([pl.BlockSpec((1, tq, a.shape[2]), lambda b, i: (b, i, 0)) for a in acts]
                  + [w[0] for w in w_mix]
                  + [pl.BlockSpec((1, tq, D), lambda b, i: (b, i, 0)),
                     gain_spec, wq_spec,
                     pl.BlockSpec((1, M, D), lambda b, i: (b, 0, 0)),
                     gm_spec, wkv_spec, wo_spec]),
        out_specs=pl.BlockSpec((1, tq, D), lambda b, i: (b, i, 0)),
        out_shape=jax.ShapeDtypeStruct((B, S, D), F32),
        scratch_shapes=[pltpu.VMEM((M, 2 * D), BF16)],
        compiler_params=_params("parallel", "arbitrary"),
        name="mix_out_cross_attention",
    )(*acts, *[w[1] for w in w_mix], x, gain, wq, mem, mem_gain, wkv, wo)


def _gelu_tanh(x):
    k = -2.0 * math.sqrt(2.0 / math.pi) * math.log2(math.e)
    return x / (1.0 + jnp.exp2(x * (k * 0.044715 * (x * x) + k)))


def _ffn_kernel(x_ref, halo_ref, g_ref, wup_ref, cw_ref, cb_ref, wd_ref, fg_ref, o_ref, up_ref, act_ref, nat_ref,
                *, tm, tiles_per_seq, final_norm):
    i = pl.program_id(0)
    halo = _rms(halo_ref[...], g_ref[...])
    xe = jnp.concatenate([jnp.where(i % tiles_per_seq == 0, 0.0, halo).astype(BF16),
                          _rms(x_ref[...], g_ref[...]).astype(BF16)], axis=0)

    n_chunks = D_FF // FFN_CHUNK
    slabs = FFN_CHUNK // LANES
    rows = tm // ROW_PHASES

    def up_project(c):
        for half in range(2):
            col0 = half * D_FF + c * FFN_CHUNK
            up = _dot(xe, wup_ref[:, col0:col0 + FFN_CHUNK])
            for s in range(slabs):
                up_ref[c % UP_BUFFERS, half, s] = up[:, s * LANES:(s + 1) * LANES]

    def conv(c, half, s, phase):
        col0 = half * D_FF + c * FFN_CHUNK + s * LANES
        cw = cw_ref[:, col0:col0 + LANES]
        out = cb_ref[:, col0:col0 + LANES]
        for tap in range(FFN_CONV):
            first = BF16_ROWS + phase - (FFN_CONV - 1 - tap)
            out = out + cw[tap:tap + 1] * up_ref[c % UP_BUFFERS, half, s, pl.ds(first, rows, stride=ROW_PHASES), :]
        return out

    acc = None
    piece_start = 0
    up_project(0)
    for c in range(n_chunks):
        if c + 1 < n_chunks:
            up_project(c + 1)
        for phase in range(ROW_PHASES):
            for s in range(slabs):
                act = _gelu_tanh(conv(c, 0, s, phase)) * conv(c, 1, s, phase)
                act_ref[phase * rows:(phase + 1) * rows,
                        c * FFN_CHUNK + s * LANES:c * FFN_CHUNK + (s + 1) * LANES] = act.astype(BF16)
        if (c + 1) % DOWN_CHUNKS == 0 or c + 1 == n_chunks:
            piece = slice(piece_start * FFN_CHUNK, (c + 1) * FFN_CHUNK)
            part = _dot(act_ref[:, piece], wd_ref[piece, :])
            acc = part if acc is None else acc + part
            piece_start = c + 1

    for phase in range(ROW_PHASES):
        for s in range(D_MODEL // LANES):
            nat_ref[s, pl.ds(phase, rows, stride=ROW_PHASES), :] = acc[phase * rows:(phase + 1) * rows,
                                                                       s * LANES:(s + 1) * LANES]
    y = x_ref[...] + jnp.concatenate([nat_ref[s] for s in range(D_MODEL // LANES)], axis=1)
    o_ref[...] = _rms(y, fg_ref[...]) if final_norm else y


def conv_ffn(x, gain, w_up, conv_w, conv_b, w_down, seq_len, final_gain=None, *, tm=ROW_TILE):
    T, D = x.shape
    halo_blocks = tm // BF16_ROWS
    final_norm = final_gain is not None
    specs, operands = zip(*map(_whole, (gain, w_up, conv_w, conv_b, w_down, final_gain if final_norm else gain)))
    return pl.pallas_call(
        functools.partial(_ffn_kernel, tm=tm, tiles_per_seq=seq_len // tm, final_norm=final_norm),
        grid=(T // tm,),
        in_specs=[pl.BlockSpec((tm, D), lambda i: (i, 0)),
                  pl.BlockSpec((BF16_ROWS, D), lambda i: (jnp.maximum(i * halo_blocks - 1, 0), 0)),
                  *specs],
        out_specs=pl.BlockSpec((tm, D), lambda i: (i, 0)),
        out_shape=jax.ShapeDtypeStruct((T, D), F32),
        scratch_shapes=[pltpu.VMEM((UP_BUFFERS, 2, FFN_CHUNK // LANES, tm + BF16_ROWS, LANES), F32),
                        pltpu.VMEM((tm, D_FF), BF16),
                        pltpu.VMEM((D // LANES, tm, LANES), F32)],
        compiler_params=_params("parallel"),
        name="conv_ffn",
    )(x, x, *operands)


def _gate_bias_rows(bias):
    return jnp.broadcast_to(jnp.pad(bias, (0, BF16_ROWS - bias.shape[0]))[:, None], (BF16_ROWS, LANES))


def _fox_diff_mixer(x, B, S, j, gain, w_in, fox_bf, lq1, lk1, lq2, lk2, subln, w_out, lambda_init):
    fw = FOX_HEADS * FOX_DIM
    proj, fq, fk = fox_projection(x, gain, w_in, _gate_bias_rows(fox_bf), S)
    proj = proj.reshape(B, S, -1)
    fox = fox_attention(proj, fq.reshape(B, S, LANES), fk.reshape(B, S, LANES))
    dif = diff_attention(proj, lq1, lk1, lq2, lk2, subln, lambda_init)
    row_block = lambda r: (pl.BlockSpec((None, fw, D_MODEL), lambda *_: (j, r, 0), pipeline_mode=pl.Buffered(1)), w_out)
    return [fox, dif], [row_block(0), row_block(1)]


def _mlstm_mixer(x, B, S, j, gain, w_in, conv_qk, b_i, b_f, head_norm, w_out):
    proj, g_row = mlstm_projection(x, gain, w_in, _gate_bias_rows(jnp.concatenate([b_i, b_f])), conv_qk, S)
    h = mlstm(proj.reshape(B, S, -1), g_row, head_norm)
    return [h], [_whole((w_out, j))]


def kernel(x, mem, mix_norm, xattn_norm, mem_norm, ffn_norm, attn_w_in, attn_fox_bf, diff_lq1, diff_lk1, diff_lq2, diff_lk2, diff_subln, attn_w_out, mlstm_w_in, mlstm_conv_qk, mlstm_b_i, mlstm_b_f, mlstm_head_norm, mlstm_w_out, xattn_wq, xattn_wkv, xattn_wo, ffn_w_up, ffn_conv_w, ffn_conv_b, ffn_w_down, final_norm):
    B, S, D = x.shape
    depth = mix_norm.shape[0]
    x = x.reshape(B * S, D)
    rows = lambda a: a.reshape(a.shape[0], 1, -1)
    to_bf16 = lambda a: a.astype(BF16)
    mix_norm, xattn_norm, mem_norm, ffn_norm = map(rows, (mix_norm, xattn_norm, mem_norm, ffn_norm))
    diff_lq1, diff_lk1, diff_lq2, diff_lk2, diff_subln = map(rows, (diff_lq1, diff_lk1, diff_lq2, diff_lk2, diff_subln))
    mlstm_head_norm, ffn_conv_b = rows(mlstm_head_norm), rows(ffn_conv_b)
    attn_w_out, mlstm_w_out, xattn_wq, xattn_wkv, xattn_wo, ffn_w_up, ffn_w_down = map(
        to_bf16, (attn_w_out, mlstm_w_out, xattn_wq, xattn_wkv, xattn_wo, ffn_w_up, ffn_w_down))
    for layer in range(depth):
        j = layer // 2
        if layer % 2 == 0:
            lambda_init = 0.8 - 0.6 * math.exp(-0.3 * layer)
            mixed, w_mix = _fox_diff_mixer(x, B, S, j, (mix_norm, layer), (attn_w_in, j), attn_fox_bf[j],
                                           (diff_lq1, j), (diff_lk1, j), (diff_lq2, j), (diff_lk2, j),
                                           (diff_subln, j), attn_w_out, lambda_init)
        else:
            mixed, w_mix = _mlstm_mixer(x, B, S, j, (mix_norm, layer), (mlstm_w_in, j), (mlstm_conv_qk, j),
                                        mlstm_b_i[j], mlstm_b_f[j], (mlstm_head_norm, j), mlstm_w_out)
        x = mix_out_cross_attention(mixed, w_mix, x.reshape(B, S, D), (xattn_norm, layer), (xattn_wq, layer), mem,
                                    (mem_norm, layer), (xattn_wkv, layer), (xattn_wo, layer)).reshape(B * S, D)
        x = conv_ffn(x, (ffn_norm, layer), (ffn_w_up, layer), (ffn_conv_w, layer), (ffn_conv_b, layer),
                     (ffn_w_down, layer), S, final_norm.reshape(1, D) if layer == depth - 1 else None)
    return x.reshape(B, S, D)
```

```python
import functools
import math

import jax
import jax.numpy as jnp
from jax import lax
from jax.experimental import pallas as pl
from jax.experimental.pallas import tpu as pltpu

F32 = jnp.float32
BF16 = jnp.bfloat16

D_MODEL = 1024
RMS_EPS = 1e-6
NEG_INF = -1e30
CHUNK = 64
FOX_HEADS, FOX_DIM = 8, 64
DIFF_HEADS, DIFF_QK, DIFF_V = 4, 64, 128
ML_HEADS, ML_QK, ML_V, ML_CONV = 4, 128, 256, 4
X_HEADS, X_DIM = 4, 256
D_FF = 2816
FFN_CONV = 3
LANES = 128
BF16_ROWS = 16
VMEM_LIMIT = 56 * 1024 * 1024

ROW_TILE = 512
ATT_Q_TILE = 512
ATT_K_TILE = 512
LOG2_E = math.log2(math.e)
ML_CHUNK = 256
ML_SEQS = 2
X_TILE = 1024
PROJ_COLS = 1024
FFN_CHUNK = 256
ROW_PHASES = 4
DOWN_CHUNKS = 4
TRANSPOSE_COLS = 512
F_PIECES = 3
FOX_BIAS_K_LANE = 0
FOX_BIAS_Q_LANE = 32


def _params(*sem):
    return pltpu.CompilerParams(dimension_semantics=sem, vmem_limit_bytes=VMEM_LIMIT)


def _resident(shape):
    return pl.BlockSpec(shape, lambda *_: (0,) * len(shape), pipeline_mode=pl.Buffered(1))


def _whole(a):
    if isinstance(a, tuple):
        arr, layer = a
        tail = arr.shape[1:]
        spec = pl.BlockSpec((None,) + tail, lambda *_: (layer,) + (0,) * len(tail), pipeline_mode=pl.Buffered(1))
        return spec, arr
    return _resident(a.shape), a


def _cols(a):
    return (a[0] if isinstance(a, tuple) else a).shape[-1]


def _rms(x, gain):
    return x * lax.rsqrt(jnp.mean(x * x, axis=-1, keepdims=True) + RMS_EPS) * gain


def _sigmoid(x):
    return 1.0 / (1.0 + jnp.exp(-x))


def _dot(a, b):
    return jnp.dot(a, b, preferred_element_type=F32)


def _dot_nt(a, b):
    return lax.dot_general(a, b, (((1,), (1,)), ((), ())), preferred_element_type=F32)


def _split3(x):
    hi = x.astype(BF16)
    r1 = x - hi.astype(F32)
    mid = r1.astype(BF16)
    return hi, mid, (r1 - mid.astype(F32)).astype(BF16)


def _dot_select(x, sel):
    hi, mid, lo = _split3(x)
    return _dot(hi, sel) + _dot(mid, sel) + _dot(lo, sel)


def _prefix_matrix(n, block):
    r = lax.broadcasted_iota(jnp.int32, (n, n), 0)
    c = lax.broadcasted_iota(jnp.int32, (n, n), 1)
    return jnp.where((r <= c) & (r // block == c // block), 1.0, 0.0).astype(BF16)


def _tile_lanes(a, n):
    return jnp.concatenate([a] * (n // LANES), axis=1)


def _log_sigmoid(x):
    return jnp.minimum(x, 0.0) - jnp.log1p(jnp.exp(-jnp.abs(x)))


def _prepare_weights(w_ref, wb_ref, wgt_ref, pieces, gate_col, n_gates):
    dst = 0
    for src, width, scale in pieces:
        for c0 in range(0, width, PROJ_COLS):
            n = min(PROJ_COLS, width - c0)
            wb_ref[:, dst + c0:dst + c0 + n] = (w_ref[:, src + c0:src + c0 + n] * scale).astype(BF16)
        dst += width
    slab0 = gate_col // LANES * LANES
    slab_t = w_ref[:, slab0:slab0 + LANES].T
    row = lax.broadcasted_iota(jnp.int32, (LANES, 1), 0)
    first = gate_col - slab0
    gates = jnp.where((row >= first) & (row < first + n_gates), slab_t, 0.0)
    wgt_ref[...] = gates[first:first + BF16_ROWS].astype(BF16)


def _fox_proj_kernel(x_ref, g_ref, w_ref, bf_ref, o_ref, fq_ref, fk_ref, carry_ref, wb_ref, wg_ref,
                     *, tm, tn, tiles_per_seq):
    i = pl.program_id(0)

    @pl.when(i == 0)
    def _():
        fw = FOX_HEADS * FOX_DIM
        dq0 = 3 * fw + FOX_HEADS
        dqw = DIFF_HEADS * 2 * DIFF_QK
        _prepare_weights(w_ref, wb_ref, wg_ref,
                         ((0, fw, FOX_DIM ** -0.5 * LOG2_E), (fw, 2 * fw, 1.0),
                          (dq0, dqw, DIFF_QK ** -0.5 * LOG2_E), (dq0 + dqw, w_ref.shape[1] - dq0 - dqw, 1.0)),
                         3 * fw, FOX_HEADS)

    @pl.when(i % tiles_per_seq == 0)
    def _():
        carry_ref[...] = jnp.zeros_like(carry_ref)

    xn = _rms(x_ref[...], g_ref[...]).astype(BF16)

    def project(part):
        width = o_ref.shape[1] // 3
        for c0 in range(part * width, (part + 1) * width, tn):
            o_ref[:, c0:c0 + tn] = _dot(xn, wb_ref[:, c0:c0 + tn]).astype(o_ref.dtype)

    gates_t = _dot_nt(wg_ref[...], xn)
    project(0)
    log_f = _log_sigmoid(gates_t + _tile_lanes(bf_ref[...], tm)) * LOG2_E
    f_t = _tile_lanes(carry_ref[...], tm) + _dot_select(log_f, _prefix_matrix(tm, tm))
    carry_ref[...] = jnp.broadcast_to(f_t[:, tm - 1:tm], carry_ref.shape)
    f = jnp.concatenate([f_t, jnp.zeros((LANES - BF16_ROWS, tm), F32)], axis=0).T
    project(1)

    src = lax.broadcasted_iota(jnp.int32, (LANES, LANES), 0)
    dst = lax.broadcasted_iota(jnp.int32, (LANES, LANES), 1)
    lane = lax.broadcasted_iota(jnp.int32, (1, LANES), 1)
    n_bias = F_PIECES * FOX_HEADS
    ones_q = jnp.where((lane >= FOX_BIAS_K_LANE) & (lane < FOX_BIAS_K_LANE + n_bias), 1.0, 0.0)
    ones_k = jnp.where((lane >= FOX_BIAS_Q_LANE) & (lane < FOX_BIAS_Q_LANE + n_bias), 1.0, 0.0)
    both = jnp.concatenate([ones_q, ones_k], axis=1)
    for c, piece in enumerate(_split3(f)):
        head = src < FOX_HEADS
        to_q = jnp.where(head & (dst == FOX_BIAS_Q_LANE + F_PIECES * src + c), 1.0, 0.0)
        to_k = jnp.where(head & (dst == FOX_BIAS_K_LANE + F_PIECES * src + c), -1.0, 0.0)
        both = both + _dot(piece, jnp.concatenate([to_q, to_k], axis=1).astype(BF16))
    fq_ref[...] = both[:, 0:LANES].astype(BF16)
    fk_ref[...] = both[:, LANES:2 * LANES].astype(BF16)
    project(2)


def fox_projection(x, gain, w, gate_bias, seq_len, *, tm=ROW_TILE, tn=PROJ_COLS):
    T, D = x.shape
    N = _cols(w) - FOX_HEADS
    row_block = lambda n: pl.BlockSpec((tm, n), lambda i: (i, 0))
    specs, operands = zip(*map(_whole, (gain, w, gate_bias)))
    return pl.pallas_call(
        functools.partial(_fox_proj_kernel, tm=tm, tn=tn, tiles_per_seq=seq_len // tm),
        grid=(T // tm,),
        in_specs=[row_block(D), *specs],
        out_specs=[row_block(N), row_block(LANES), row_block(LANES)],
        out_shape=[jax.ShapeDtypeStruct((T, N), BF16), jax.ShapeDtypeStruct((T, LANES), BF16),
                   jax.ShapeDtypeStruct((T, LANES), BF16)],
        scratch_shapes=[pltpu.VMEM((BF16_ROWS, LANES), F32), pltpu.VMEM((D, N), BF16),
                        pltpu.VMEM((BF16_ROWS, D), BF16)],
        compiler_params=_params("arbitrary"),
        name="fox_projection",
    )(x, *operands)


def _mlstm_proj_kernel(x_ref, halo_ref, g_ref, w_raw_ref, gb_ref, cw_ref, o_ref, gr_ref, up_ref, nat_ref, w_ref,
                       wg_ref, *, tm, tiles_per_seq):
    i = pl.program_id(0)

    @pl.when(i == 0)
    def _():
        g0 = 2 * ML_HEADS * ML_QK + ML_HEADS * ML_V
        og_col = g0 + 2 * ML_HEADS
        _prepare_weights(w_raw_ref, w_ref, wg_ref, ((0, g0, 1.0), (og_col, w_raw_ref.shape[1] - og_col, 1.0)),
                         g0, 2 * ML_HEADS)

    halo = _rms(halo_ref[...], g_ref[...])
    xe = jnp.concatenate([jnp.where(i % tiles_per_seq == 0, 0.0, halo).astype(BF16),
                          _rms(x_ref[...], g_ref[...]).astype(BF16)], axis=0)
    xn = xe[BF16_ROWS:]

    qk_slabs = 2 * ML_HEADS * ML_QK // LANES
    rows = tm // ROW_PHASES
    v0 = qk_slabs * LANES
    og0 = v0 + ML_HEADS * ML_V
    for s0 in range(0, qk_slabs, 2):
        up = _dot(xe, w_ref[:, s0 * LANES:(s0 + 2) * LANES])
        up_ref[s0] = up[:, 0:LANES]
        up_ref[s0 + 1] = up[:, LANES:2 * LANES]
    pre_t = _dot_nt(wg_ref[...], xn) + _tile_lanes(gb_ref[...], tm)
    o_ref[:, v0:og0] = _dot(xn, w_ref[:, v0:og0]).astype(o_ref.dtype)
    for s in range(qk_slabs):
        cw = cw_ref[:, s * LANES:(s + 1) * LANES]
        for phase in range(ROW_PHASES):
            y = None
            for tap in range(ML_CONV):
                first = BF16_ROWS + phase - (ML_CONV - 1 - tap)
                term = cw[tap:tap + 1] * up_ref[s, pl.ds(first, rows, stride=ROW_PHASES), :]
                y = term if y is None else y + term
            y = y * _sigmoid(y)
            if s < qk_slabs // 2:
                y = y * (ML_QK ** -0.5)
            nat_ref[s, pl.ds(phase, rows, stride=ROW_PHASES), :] = y
        o_ref[:, s * LANES:(s + 1) * LANES] = nat_ref[s].astype(o_ref.dtype)

    og = _dot(xn, w_ref[:, og0:])
    row = lax.broadcasted_iota(jnp.int32, (BF16_ROWS, 1), 0)
    val_t = jnp.where(row < ML_HEADS, pre_t, _log_sigmoid(pre_t)) * LOG2_E
    g_t = jnp.where(row < ML_HEADS, val_t, _dot_select(val_t, _prefix_matrix(tm, ML_CHUNK)))
    gr_ref[0] = g_t[0:2 * ML_HEADS]
    o_ref[:, og0:] = _sigmoid(og).astype(o_ref.dtype)


def mlstm_projection(x, gain, w, gate_bias, conv_w, seq_len, *, tm=ROW_TILE):
    T, D = x.shape
    N = _cols(w) - 2 * ML_HEADS
    tiles_per_seq = seq_len // tm
    halo_blocks = tm // BF16_ROWS
    row_block = lambda n: pl.BlockSpec((tm, n), lambda i: (i, 0))
    specs, operands = zip(*map(_whole, (gain, w, gate_bias, conv_w)))
    return pl.pallas_call(
        functools.partial(_mlstm_proj_kernel, tm=tm, tiles_per_seq=tiles_per_seq),
        grid=(T // tm,),
        in_specs=[row_block(D),
                  pl.BlockSpec((BF16_ROWS, D), lambda i: (jnp.maximum(i * halo_blocks - 1, 0), 0)),
                  *specs],
        out_specs=[row_block(N),
                   pl.BlockSpec((1, 2 * ML_HEADS, tm), lambda i: (i // tiles_per_seq, 0, i % tiles_per_seq))],
        out_shape=[jax.ShapeDtypeStruct((T, N), BF16),
                   jax.ShapeDtypeStruct((T // seq_len, 2 * ML_HEADS, seq_len), F32)],
        scratch_shapes=[pltpu.VMEM((2 * ML_HEADS * ML_QK // LANES, tm + BF16_ROWS, LANES), F32),
                        pltpu.VMEM((2 * ML_HEADS * ML_QK // LANES, tm, LANES), F32),
                        pltpu.VMEM((D, N), BF16), pltpu.VMEM((BF16_ROWS, D), BF16)],
        compiler_params=_params("arbitrary"),
        name="mlstm_projection",
    )(x, x, *operands)


def _build_vt(v_ref, vt_ref, n_heads, rows):
    S = v_ref.shape[1]
    r = lax.broadcasted_iota(jnp.int32, (LANES, LANES), 0)
    c = lax.broadcasted_iota(jnp.int32, (LANES, LANES), 1)
    eye = jnp.where(r == c, 1.0, 0.0).astype(BF16)
    per_group = LANES // rows
    for g in range(n_heads // per_group):
        for c0 in range(0, S, TRANSPOSE_COLS):
            cs = slice(c0, c0 + TRANSPOSE_COLS)
            vt = _dot_nt(eye, v_ref[0, cs, g * LANES:(g + 1) * LANES]).astype(BF16)
            for k in range(per_group):
                vt_ref[g * per_group + k, 0:rows, cs] = vt[k * rows:(k + 1) * rows]
    for h in range(n_heads):
        vt_ref[h, rows:rows + BF16_ROWS, :] = jnp.ones((BF16_ROWS, S), BF16)


def _softmax_step(s_t, vt, m_ref, acc_ref, idx, queries=slice(None)):
    m_prev = m_ref[idx, :, queries]
    m_new = jnp.maximum(m_prev, jnp.max(s_t, axis=0, keepdims=True))
    p = jnp.exp2(s_t - m_new).astype(BF16)
    acc_ref[idx, :, queries] = jnp.exp2(m_prev - m_new) * acc_ref[idx, :, queries] + _dot(vt, p)
    m_ref[idx, :, queries] = m_new


def _causal_steps(i, logits, attend, n_streams, buf_a, buf_b):
    def phase(j, src, dst, next_is_diagonal=False):
        for n in range(n_streams):
            logits(j + 1, dst, n, next_is_diagonal)
            attend(j, src, n, False)

    for n in range(n_streams):
        logits(0, buf_a, n, False)

    def body(jj, carry):
        phase(2 * jj, buf_a, buf_b)
        phase(2 * jj + 1, buf_b, buf_a)
        return carry

    lax.fori_loop(0, lax.shift_right_logical(i, 1), body, 0)
    odd = lax.rem(i, 2) == 1

    @pl.when(odd)
    def _():
        phase(i - 1, buf_a, buf_b, True)
        for n in range(n_streams):
            attend(i, buf_b, n, True)

    @pl.when(jnp.logical_not(odd))
    def _():
        for n in range(n_streams):
            attend(i, buf_a, n, True)


def _fox_kernel(q_ref, k_ref, v_ref, fq_ref, fk_ref, o_ref, vt_ref, qc_ref, acc_ref, m_ref, sa_ref, sb_ref, *, tq, tk):
    i = pl.program_id(1)

    @pl.when(i == 0)
    def _():
        _build_vt(v_ref, vt_ref, FOX_HEADS, FOX_DIM)

    lane = lax.broadcasted_iota(jnp.int32, (1, LANES), 1)
    fq = fq_ref[0]
    for h in range(FOX_HEADS):
        pair, half = divmod(h, 2)
        q = q_ref[0, :, pair * LANES:(pair + 1) * LANES]
        in_head = (lane >= half * FOX_DIM) & (lane < (half + 1) * FOX_DIM)
        lo_k, lo_q = FOX_BIAS_K_LANE + F_PIECES * h, FOX_BIAS_Q_LANE + F_PIECES * h
        mine = ((lane >= lo_k) & (lane < lo_k + F_PIECES)) | ((lane >= lo_q) & (lane < lo_q + F_PIECES))
        qc_ref[h, :, 0:LANES] = jnp.where(in_head, q, jnp.zeros_like(q))
        qc_ref[h, :, LANES:2 * LANES] = jnp.where(mine, fq, jnp.zeros_like(fq))
    m_ref[...] = jnp.full(m_ref.shape, NEG_INF, F32)
    acc_ref[...] = jnp.zeros(acc_ref.shape, F32)

    half = tk // 2
    key = lax.broadcasted_iota(jnp.int32, (half, tq), 0)
    qry = lax.broadcasted_iota(jnp.int32, (half, tq), 1)
    causal = key <= qry

    def key_rows(j, part=None):
        if part is None:
            return pl.ds(pl.multiple_of(j * tk, tk), tk)
        return pl.ds(pl.multiple_of(j * tk + part * half, half), half)

    def logits(j, buf, h, diagonal):
        pair = h // 2
        kc = jnp.concatenate([k_ref[0, key_rows(j), pair * LANES:(pair + 1) * LANES], fk_ref[0, key_rows(j), :]],
                             axis=1)
        if diagonal:
            buf[h, 0:half, :] = _dot_nt(kc[0:half], qc_ref[h])
            buf[h, half:tk, half:tq] = _dot_nt(kc[half:tk], qc_ref[h, half:tq, :])
        else:
            buf[h] = _dot_nt(kc, qc_ref[h])

    def attend(j, buf, h, diagonal):
        if diagonal:
            _softmax_step(jnp.where(causal, buf[h, 0:half, :], NEG_INF), vt_ref[h, :, key_rows(j, 0)],
                          m_ref, acc_ref, h)
            _softmax_step(jnp.where(causal[:, 0:tq - half], buf[h, half:tk, half:tq], NEG_INF),
                          vt_ref[h, :, key_rows(j, 1)], m_ref, acc_ref, h, slice(half, tq))
        else:
            _softmax_step(buf[h], vt_ref[h, :, key_rows(j)], m_ref, acc_ref, h)

    _causal_steps(i, logits, attend, FOX_HEADS, sa_ref, sb_ref)

    for pair in range(FOX_HEADS // 2):
        halves = []
        for half in range(2):
            a = acc_ref[2 * pair + half]
            halves.append(a[0:FOX_DIM] / a[FOX_DIM:FOX_DIM + 1])
        o_ref[0, :, pair * LANES:(pair + 1) * LANES] = jnp.concatenate(halves, axis=0).T.astype(o_ref.dtype)


def fox_attention(proj, fq, fk, *, tq=ATT_Q_TILE, tk=ATT_K_TILE):
    B, S, _ = proj.shape
    assert tq == tk
    width = FOX_HEADS * FOX_DIM
    rows = FOX_DIM + BF16_ROWS
    return pl.pallas_call(
        functools.partial(_fox_kernel, tq=tq, tk=tk),
        grid=(B, S // tq),
        in_specs=[pl.BlockSpec((1, tq, width), lambda b, i: (b, i, 0)),
                  pl.BlockSpec((1, S, width), lambda b, i: (b, 0, 1)),
                  pl.BlockSpec((1, S, width), lambda b, i: (b, 0, 2)),
                  pl.BlockSpec((1, tq, LANES), lambda b, i: (b, i, 0)),
                  pl.BlockSpec((1, S, LANES), lambda b, i: (b, 0, 0))],
        out_specs=pl.BlockSpec((1, tq, width), lambda b, i: (b, i, 0)),
        out_shape=jax.ShapeDtypeStruct((B, S, width), BF16),
        scratch_shapes=[pltpu.VMEM((FOX_HEADS, rows, S), BF16),
                        pltpu.VMEM((FOX_HEADS, tq, 2 * LANES), BF16),
                        pltpu.VMEM((FOX_HEADS, rows, tq), F32),
                        pltpu.VMEM((FOX_HEADS, 1, tq), F32),
                        pltpu.VMEM((FOX_HEADS, tk, tq), F32),
                        pltpu.VMEM((FOX_HEADS, tk, tq), F32)],
        compiler_params=_params("parallel", "arbitrary"),
        name="fox_attention",
    )(proj, proj, proj, fq, fk)


def _diff_kernel(q_ref, k_ref, v_ref, lq1_ref, lk1_ref, lq2_ref, lk2_ref, sub_ref, o_ref, vt_ref, qc_ref, acc_ref,
                 m_ref, sa_ref, sb_ref, *, tq, tk, lambda_init):
    i = pl.program_id(1)

    @pl.when(i == 0)
    def _():
        _build_vt(v_ref, vt_ref, DIFF_HEADS, DIFF_V)

    lane = lax.broadcasted_iota(jnp.int32, (1, LANES), 1)
    for h in range(DIFF_HEADS):
        q = q_ref[0, :, h * LANES:(h + 1) * LANES]
        zero = jnp.zeros_like(q)
        qc_ref[2 * h] = jnp.where(lane < DIFF_QK, q, zero)
        qc_ref[2 * h + 1] = jnp.where(lane >= DIFF_QK, q, zero)
    m_ref[...] = jnp.full(m_ref.shape, NEG_INF, F32)
    acc_ref[...] = jnp.zeros(acc_ref.shape, F32)

    half = tk // 2
    key = lax.broadcasted_iota(jnp.int32, (half, tq), 0)
    qry = lax.broadcasted_iota(jnp.int32, (half, tq), 1)
    visible = key // CHUNK <= qry // CHUNK

    def key_rows(j, part=None):
        if part is None:
            return pl.ds(pl.multiple_of(j * tk, tk), tk)
        return pl.ds(pl.multiple_of(j * tk + part * half, half), half)

    def logits(j, buf, n, diagonal):
        k = k_ref[0, key_rows(j), (n // 2) * LANES:(n // 2 + 1) * LANES]
        if diagonal:
            buf[n, 0:half, :] = _dot_nt(k[0:half], qc_ref[n])
            buf[n, half:tk, half:tq] = _dot_nt(k[half:tk], qc_ref[n, half:tq, :])
        else:
            buf[n] = _dot_nt(k, qc_ref[n])

    def attend(j, buf, n, diagonal):
        vt = vt_ref.at[n // 2]
        if diagonal:
            _softmax_step(jnp.where(visible, buf[n, 0:half, :], NEG_INF), vt[:, key_rows(j, 0)], m_ref, acc_ref, n)
            _softmax_step(jnp.where(visible[:, 0:tq - half], buf[n, half:tk, half:tq], NEG_INF),
                          vt[:, key_rows(j, 1)], m_ref, acc_ref, n, slice(half, tq))
        else:
            _softmax_step(buf[n], vt[:, key_rows(j)], m_ref, acc_ref, n)

    _causal_steps(i, logits, attend, 2 * DIFF_HEADS, sa_ref, sb_ref)

    lam = (jnp.exp(jnp.sum(lq1_ref[...] * lk1_ref[...], axis=1, keepdims=True))
           - jnp.exp(jnp.sum(lq2_ref[...] * lk2_ref[...], axis=1, keepdims=True)) + lambda_init)
    for h in range(DIFF_HEADS):
        a1, a2 = acc_ref[2 * h], acc_ref[2 * h + 1]
        o_t = a1[0:DIFF_V] / a1[DIFF_V:DIFF_V + 1] - lam * (a2[0:DIFF_V] / a2[DIFF_V:DIFF_V + 1])
        out = _rms(o_t.T, sub_ref[...]) * (1.0 - lambda_init)
        o_ref[0, :, h * LANES:(h + 1) * LANES] = out.astype(o_ref.dtype)


def diff_attention(proj, lq1, lk1, lq2, lk2, subln, lambda_init, *, tq=ATT_Q_TILE, tk=ATT_K_TILE):
    B, S, _ = proj.shape
    assert tq == tk
    width = DIFF_HEADS * DIFF_V
    rows = DIFF_V + BF16_ROWS
    specs, operands = zip(*map(_whole, (lq1, lk1, lq2, lk2, subln)))
    return pl.pallas_call(
        functools.partial(_diff_kernel, tq=tq, tk=tk, lambda_init=lambda_init),
        grid=(B, S // tq),
        in_specs=[pl.BlockSpec((1, tq, width), lambda b, i: (b, i, 3)),
                  pl.BlockSpec((1, S, width), lambda b, i: (b, 0, 4)),
                  pl.BlockSpec((1, S, width), lambda b, i: (b, 0, 5)),
                  *specs],
        out_specs=pl.BlockSpec((1, tq, width), lambda b, i: (b, i, 0)),
        out_shape=jax.ShapeDtypeStruct((B, S, width), BF16),
        scratch_shapes=[pltpu.VMEM((DIFF_HEADS, rows, S), BF16),
                        pltpu.VMEM((2 * DIFF_HEADS, tq, LANES), BF16),
                        pltpu.VMEM((2 * DIFF_HEADS, rows, tq), F32),
                        pltpu.VMEM((2 * DIFF_HEADS, 1, tq), F32),
                        pltpu.VMEM((2 * DIFF_HEADS, tk, tq), F32),
                        pltpu.VMEM((2 * DIFF_HEADS, tk, tq), F32)],
        compiler_params=_params("parallel", "arbitrary"),
        name="diff_attention",
    )(proj, proj, proj, *operands)


def _mlstm_kernel(qk_ref, v_ref, sg_ref, gr_ref, hn_ref, o_ref, ct_ref, m_ref, vt_ref, *, L, nb):
    c = pl.program_id(1)

    @pl.when(c == 0)
    def _():
        ct_ref[...] = jnp.zeros_like(ct_ref)
        m_ref[...] = jnp.zeros_like(m_ref)
        for n in range(nb * ML_HEADS):
            vt_ref[n, ML_V:ML_V + BF16_ROWS, :] = jnp.ones((BF16_ROWS, L), BF16)

    src = lax.broadcasted_iota(jnp.int32, (L, L), 0)
    dst = lax.broadcasted_iota(jnp.int32, (L, L), 1)
    causal = src <= dst
    eye = jnp.where(src == dst, 1.0, 0.0).astype(BF16)
    k0 = ML_HEADS * ML_QK
    for b in range(nb):
        gr = gr_ref[b]
        gc = jnp.concatenate([gr, jnp.zeros((LANES - gr.shape[0], L), F32)], axis=0).T
        for h in range(ML_HEADS):
            n = b * ML_HEADS + h
            q = qk_ref[b, :, h * ML_QK:(h + 1) * ML_QK]
            k = qk_ref[b, :, k0 + h * ML_QK:k0 + (h + 1) * ML_QK]
            vt_ref[n, 0:ML_V, :] = _dot_nt(eye, v_ref[b, :, h * ML_V:(h + 1) * ML_V]).astype(BF16)
            v_t = vt_ref[n]
            r_col = gc[:, h:h + 1] - gc[:, ML_HEADS + h:ML_HEADS + h + 1]
            b_row = gr[ML_HEADS + h:ML_HEADS + h + 1, :]
            r_row = gr[h:h + 1, :] - b_row
            g = b_row[:, L - 1:L]
            ct = ct_ref[n]
            m = m_ref[n][:, 0:1]

            dm = jnp.where(causal, r_col, NEG_INF)
            mt = jnp.maximum(m, jnp.max(dm, axis=0, keepdims=True))
            s_t = _dot_nt(k, q) * jnp.exp2(dm - mt)
            both = jnp.exp2(m - mt) * _dot_nt(ct.astype(BF16), q) + _dot(v_t, s_t.astype(BF16))
            den = both[ML_V:ML_V + 1]
            hh = both[0:ML_V] / jnp.maximum(jnp.abs(den), jnp.exp2(-(b_row + mt)))
            hh = hh * lax.rsqrt(jnp.mean(hh * hh, axis=0, keepdims=True) + RMS_EPS)

            m_next = jnp.maximum(m, jnp.max(r_row, axis=1, keepdims=True))
            kw = (k.astype(F32) * jnp.exp2(r_col - m_next)).astype(BF16)
            ct_ref[n] = jnp.exp2(m - m_next) * ct + _dot(v_t, kw)
            m_ref[n] = jnp.broadcast_to(g + m_next, (1, LANES))

            vs = slice(h * ML_V, (h + 1) * ML_V)
            o_ref[b, :, vs] = (hh.T * hn_ref[:, vs] * sg_ref[b, :, vs].astype(F32)).astype(o_ref.dtype)


def mlstm(proj, g_row, head_norm, *, L=ML_CHUNK, nb=ML_SEQS):
    B, S, _ = proj.shape
    W = D_MODEL
    assert L == ML_V, "one identity matrix serves the v transposes"
    norm_spec, head_norm = _whole(head_norm)
    return pl.pallas_call(
        functools.partial(_mlstm_kernel, L=L, nb=nb),
        grid=(B // nb, S // L),
        in_specs=[pl.BlockSpec((nb, L, W), lambda b, c: (b, c, 0)),
                  pl.BlockSpec((nb, L, W), lambda b, c: (b, c, 1)),
                  pl.BlockSpec((nb, L, W), lambda b, c: (b, c, 2)),
                  pl.BlockSpec((nb, 2 * ML_HEADS, L), lambda b, c: (b, 0, c)),
                  norm_spec],
        out_specs=pl.BlockSpec((nb, L, W), lambda b, c: (b, c, 0)),
        out_shape=jax.ShapeDtypeStruct((B, S, W), BF16),
        scratch_shapes=[pltpu.VMEM((nb * ML_HEADS, ML_V + BF16_ROWS, ML_QK), F32),
                        pltpu.VMEM((nb * ML_HEADS, 1, LANES), F32),
                        pltpu.VMEM((nb * ML_HEADS, ML_V + BF16_ROWS, L), BF16)],
        compiler_params=_params("parallel", "arbitrary"),
        name="mlstm",
    )(proj, proj, proj, g_row, head_norm)


def _xattn_kernel(*refs, n_in):
    x_ref, g_ref, wq_ref, mem_ref, gm_ref, wkv_ref, wo_ref, o_ref, kv_ref = refs[2 * n_in:]

    @pl.when(pl.program_id(1) == 0)
    def _():
        memn = _rms(mem_ref[0], gm_ref[...]).astype(BF16)
        for c0 in range(0, kv_ref.shape[1], PROJ_COLS):
            kv_ref[:, c0:c0 + PROJ_COLS] = _dot(memn, wkv_ref[:, c0:c0 + PROJ_COLS]).astype(BF16)

    x = x_ref[0]
    for a_ref, w_ref in zip(refs[:n_in], refs[n_in:2 * n_in]):
        x = x + _dot(a_ref[0], w_ref[...])
    xn = _rms(x, g_ref[...]).astype(BF16)
    q = (_dot(xn, wq_ref[...]) * (X_DIM ** -0.5)).astype(BF16)
    heads = []
    for h in range(X_HEADS):
        cols = slice(h * X_DIM, (h + 1) * X_DIM)
        s = _dot_nt(q[:, cols], kv_ref[:, cols])
        p = jnp.exp(s - jnp.max(s, axis=1, keepdims=True))
        l = jnp.sum(p, axis=1, keepdims=True)
        v = kv_ref[:, D_MODEL + h * X_DIM:D_MODEL + (h + 1) * X_DIM]
        heads.append((_dot(p.astype(BF16), v) / l).astype(BF16))
    o_ref[0] = x + _dot(jnp.concatenate(heads, axis=1), wo_ref[...])


def mix_out_cross_attention(acts, w_mix, x, gain, wq, mem, mem_gain, wkv, wo, *, tq=X_TILE):
    B, S, D = x.shape
    M = mem.shape[1]
    (gain_spec, gain), (wq_spec, wq), (gm_spec, mem_gain), (wkv_spec, wkv), (wo_spec, wo) = map(
        _whole, (gain, wq, mem_gain, wkv, wo))
    return pl.pallas_call(
        functools.partial(_xattn_kernel, n_in=len(acts)),
        grid=(B, S // tq),
        in_specs=([pl.BlockSpec((1, tq, a.shape[2]), lambda b, i: (b, i, 0)) for a in acts]
                  + [w[0] for w in w_mix]
                  + [pl.BlockSpec((1, tq, D), lambda b, i: (b, i, 0)),
                     gain_spec, wq_spec,
                     pl.BlockSpec((1, M, D), lambda b, i: (b, 0, 0)),
                     gm_spec, wkv_spec, wo_spec]),
        out_specs=pl.BlockSpec((1, tq, D), lambda b, i: (b, i, 0)),
        out_shape=jax.ShapeDtypeStruct((B, S, D), F32),
        scratch_shapes=[pltpu.VMEM((M, 2 * D), BF16)],
        compiler_params=_params("parallel", "arbitrary"),
        name="mix_out_cross_attention",
    )(*acts, *[w[1] for w in w_mix], x, gain, wq, mem, mem_gain, wkv, wo)


def _gelu_tanh(x):
    k = -2.0 * math.sqrt(2.0 / math.pi) * math.log2(math.e)
    return x / (1.0 + jnp.exp2(x * (k * 0.044715 * (x * x) + k)))


def _ffn_kernel(x_ref, halo_ref, g_ref, wup_ref, cw_ref, cb_ref, wd_ref, fg_ref, o_ref, up_ref, act_ref, nat_ref,
                *, tm, tiles_per_seq, final_norm):
    i = pl.program_id(0)
    halo = _rms(halo_ref[...], g_ref[...])
    xe = jnp.concatenate([jnp.where(i % tiles_per_seq == 0, 0.0, halo).astype(BF16),
                          _rms(x_ref[...], g_ref[...]).astype(BF16)], axis=0)

    n_chunks = D_FF // FFN_CHUNK
    slabs = FFN_CHUNK // LANES
    rows = tm // ROW_PHASES

    def up_project(c):
        for half in range(2):
            col0 = half * D_FF + c * FFN_CHUNK
            up = _dot(xe, wup_ref[:, col0:col0 + FFN_CHUNK])
            for s in range(slabs):
                up_ref[c % 2, half, s] = up[:, s * LANES:(s + 1) * LANES]

    def conv(c, half, s, phase):
        col0 = half * D_FF + c * FFN_CHUNK + s * LANES
        cw = cw_ref[:, col0:col0 + LANES]
        out = cb_ref[:, col0:col0 + LANES]
        for tap in range(FFN_CONV):
            first = BF16_ROWS + phase - (FFN_CONV - 1 - tap)
            out = out + cw[tap:tap + 1] * up_ref[c % 2, half, s, pl.ds(first, rows, stride=ROW_PHASES), :]
        return out

    acc = None
    piece_start = 0
    up_project(0)
    for c in range(n_chunks):
        if c + 1 < n_chunks:
            up_project(c + 1)
        for phase in range(ROW_PHASES):
            for s in range(slabs):
                act = _gelu_tanh(conv(c, 0, s, phase)) * conv(c, 1, s, phase)
                act_ref[phase * rows:(phase + 1) * rows,
                        c * FFN_CHUNK + s * LANES:c * FFN_CHUNK + (s + 1) * LANES] = act.astype(BF16)
        if (c + 1) % DOWN_CHUNKS == 0 or c + 1 == n_chunks:
            piece = slice(piece_start * FFN_CHUNK, (c + 1) * FFN_CHUNK)
            part = _dot(act_ref[:, piece], wd_ref[piece, :])
            acc = part if acc is None else acc + part
            piece_start = c + 1

    for phase in range(ROW_PHASES):
        for s in range(D_MODEL // LANES):
            nat_ref[s, pl.ds(phase, rows, stride=ROW_PHASES), :] = acc[phase * rows:(phase + 1) * rows,
                                                                       s * LANES:(s + 1) * LANES]
    y = x_ref[...] + jnp.concatenate([nat_ref[s] for s in range(D_MODEL // LANES)], axis=1)
    o_ref[...] = _rms(y, fg_ref[...]) if final_norm else y


def conv_ffn(x, gain, w_up, conv_w, conv_b, w_down, seq_len, final_gain=None, *, tm=ROW_TILE):
    T, D = x.shape
    halo_blocks = tm // BF16_ROWS
    final_norm = final_gain is not None
    specs, operands = zip(*map(_whole, (gain, w_up, conv_w, conv_b, w_down, final_gain if final_norm else gain)))
    return pl.pallas_call(
        functools.partial(_ffn_kernel, tm=tm, tiles_per_seq=seq_len // tm, final_norm=final_norm),
        grid=(T // tm,),
        in_specs=[pl.BlockSpec((tm, D), lambda i: (i, 0)),
                  pl.BlockSpec((BF16_ROWS, D), lambda i: (jnp.maximum(i * halo_blocks - 1, 0), 0)),
                  *specs],
        out_specs=pl.BlockSpec((tm, D), lambda i: (i, 0)),
        out_shape=jax.ShapeDtypeStruct((T, D), F32),
        scratch_shapes=[pltpu.VMEM((2, 2, FFN_CHUNK // LANES, tm + BF16_ROWS, LANES), F32),
                        pltpu.VMEM((tm, D_FF), BF16),
                        pltpu.VMEM((D // LANES, tm, LANES), F32)],
        compiler_params=_params("parallel"),
        name="conv_ffn",
    )(x, x, *operands)


def _gate_bias_rows(bias):
    return jnp.broadcast_to(jnp.pad(bias, (0, BF16_ROWS - bias.shape[0]))[:, None], (BF16_ROWS, LANES))


def _fox_diff_mixer(x, B, S, j, gain, w_in, fox_bf, lq1, lk1, lq2, lk2, subln, w_out, lambda_init):
    fw = FOX_HEADS * FOX_DIM
    proj, fq, fk = fox_projection(x, gain, w_in, _gate_bias_rows(fox_bf), S)
    proj = proj.reshape(B, S, -1)
    fox = fox_attention(proj, fq.reshape(B, S, LANES), fk.reshape(B, S, LANES))
    dif = diff_attention(proj, lq1, lk1, lq2, lk2, subln, lambda_init)
    row_block = lambda r: (pl.BlockSpec((None, fw, D_MODEL), lambda *_: (j, r, 0), pipeline_mode=pl.Buffered(1)), w_out)
    return [fox, dif], [row_block(0), row_block(1)]


def _mlstm_mixer(x, B, S, j, gain, w_in, conv_qk, b_i, b_f, head_norm, w_out):
    proj, g_row = mlstm_projection(x, gain, w_in, _gate_bias_rows(jnp.concatenate([b_i, b_f])), conv_qk, S)
    h = mlstm(proj.reshape(B, S, -1), g_row, head_norm)
    return [h], [_whole((w_out, j))]


def kernel(x, mem, mix_norm, xattn_norm, mem_norm, ffn_norm, attn_w_in, attn_fox_bf, diff_lq1, diff_lk1, diff_lq2, diff_lk2, diff_subln, attn_w_out, mlstm_w_in, mlstm_conv_qk, mlstm_b_i, mlstm_b_f, mlstm_head_norm, mlstm_w_out, xattn_wq, xattn_wkv, xattn_wo, ffn_w_up, ffn_conv_w, ffn_conv_b, ffn_w_down, final_norm):
    B, S, D = x.shape
    depth = mix_norm.shape[0]
    x = x.reshape(B * S, D)
    rows = lambda a: a.reshape(a.shape[0], 1, -1)
    to_bf16 = lambda a: a.astype(BF16)
    mix_norm, xattn_norm, mem_norm, ffn_norm = map(rows, (mix_norm, xattn_norm, mem_norm, ffn_norm))
    diff_lq1, diff_lk1, diff_lq2, diff_lk2, diff_subln = map(rows, (diff_lq1, diff_lk1, diff_lq2, diff_lk2, diff_subln))
    mlstm_head_norm, ffn_conv_b = rows(mlstm_head_norm), rows(ffn_conv_b)
    attn_w_out, mlstm_w_out, xattn_wq, xattn_wkv, xattn_wo, ffn_w_up, ffn_w_down = map(
        to_bf16, (attn_w_out, mlstm_w_out, xattn_wq, xattn_wkv, xattn_wo, ffn_w_up, ffn_w_down))
    for layer in range(depth):
        j = layer // 2
        if layer % 2 == 0:
            lambda_init = 0.8 - 0.6 * math.exp(-0.3 * layer)
            mixed, w_mix = _fox_diff_mixer(x, B, S, j, (mix_norm, layer), attn_w_in[j], attn_fox_bf[j],
                                           (diff_lq1, j), (diff_lk1, j), (diff_lq2, j), (diff_lk2, j),
                                           (diff_subln, j), attn_w_out, lambda_init)
        else:
            mixed, w_mix = _mlstm_mixer(x, B, S, j, (mix_norm, layer), mlstm_w_in[j], (mlstm_conv_qk, j),
                                        mlstm_b_i[j], mlstm_b_f[j], (mlstm_head_norm, j), mlstm_w_out)
        x = mix_out_cross_attention(mixed, w_mix, x.reshape(B, S, D), (xattn_norm, layer), (xattn_wq, layer), mem,
                                    (mem_norm, layer), (xattn_wkv, layer), (xattn_wo, layer)).reshape(B * S, D)
        x = conv_ffn(x, (ffn_norm, layer), (ffn_w_up, layer), (ffn_conv_w, layer), (ffn_conv_b, layer),
                     (ffn_w_down, layer), S, final_norm.reshape(1, D) if layer == depth - 1 else None)
    return x.reshape(B, S, D)
```

```python
import functools
import math

import jax
import jax.numpy as jnp
from jax import lax
from jax.experimental import pallas as pl
from jax.experimental.pallas import tpu as pltpu

F32 = jnp.float32
BF16 = jnp.bfloat16

D_MODEL = 1024
RMS_EPS = 1e-6
NEG_INF = -1e30
CHUNK = 64
FOX_HEADS, FOX_DIM = 8, 64
DIFF_HEADS, DIFF_QK, DIFF_V = 4, 64, 128
ML_HEADS, ML_QK, ML_V, ML_CONV = 4, 128, 256, 4
X_HEADS, X_DIM = 4, 256
D_FF = 2816
FFN_CONV = 3
LANES = 128
BF16_ROWS = 16
VMEM_LIMIT = 56 * 1024 * 1024

ROW_TILE = 512
ATT_Q_TILE = 512
ATT_K_TILE = 512
LOG2_E = math.log2(math.e)
ML_CHUNK = 256
ML_SEQS = 2
X_TILE = 1024
PROJ_COLS = 1024
FFN_CHUNK = 256
ROW_PHASES = 4
DOWN_CHUNKS = 4
TRANSPOSE_COLS = 512
F_PIECES = 3
FOX_BIAS_K_LANE = 0
FOX_BIAS_Q_LANE = 32


def _params(*sem):
    return pltpu.CompilerParams(dimension_semantics=sem, vmem_limit_bytes=VMEM_LIMIT)


def _resident(shape):
    return pl.BlockSpec(shape, lambda *_: (0,) * len(shape), pipeline_mode=pl.Buffered(1))


def _whole(a):
    if isinstance(a, tuple):
        arr, layer = a
        tail = arr.shape[1:]
        spec = pl.BlockSpec((None,) + tail, lambda *_: (layer,) + (0,) * len(tail), pipeline_mode=pl.Buffered(1))
        return spec, arr
    return _resident(a.shape), a


def _rows(a):
    return (a[0] if isinstance(a, tuple) else a).shape[-2]


def _rms(x, gain):
    return x * lax.rsqrt(jnp.mean(x * x, axis=-1, keepdims=True) + RMS_EPS) * gain


def _sigmoid(x):
    return 1.0 / (1.0 + jnp.exp(-x))


def _dot(a, b):
    return jnp.dot(a, b, preferred_element_type=F32)


def _dot_nt(a, b):
    return lax.dot_general(a, b, (((1,), (1,)), ((), ())), preferred_element_type=F32)


def _split3(x):
    hi = x.astype(BF16)
    r1 = x - hi.astype(F32)
    mid = r1.astype(BF16)
    return hi, mid, (r1 - mid.astype(F32)).astype(BF16)


def _dot_select(x, sel):
    hi, mid, lo = _split3(x)
    return _dot(hi, sel) + _dot(mid, sel) + _dot(lo, sel)


def _prefix_matrix(n, block):
    r = lax.broadcasted_iota(jnp.int32, (n, n), 0)
    c = lax.broadcasted_iota(jnp.int32, (n, n), 1)
    return jnp.where((r <= c) & (r // block == c // block), 1.0, 0.0).astype(BF16)


def _tile_lanes(a, n):
    return jnp.concatenate([a] * (n // LANES), axis=1)


def _log_sigmoid(x):
    return jnp.minimum(x, 0.0) - jnp.log1p(jnp.exp(-jnp.abs(x)))


def _prepare_weights(wt_ref, wb_ref, wg_ref, pieces, gate_row, n_gates):
    dst = 0
    for src, height, scale in pieces:
        for r0 in range(0, height, PROJ_COLS):
            n = min(PROJ_COLS, height - r0)
            wb_ref[dst + r0:dst + r0 + n, :] = (wt_ref[src + r0:src + r0 + n, :] * scale).astype(BF16)
        dst += height
    gates = wt_ref[gate_row:gate_row + n_gates, :]
    wg_ref[...] = jnp.concatenate([gates, jnp.zeros((BF16_ROWS - n_gates, gates.shape[1]), F32)],
                                  axis=0).astype(BF16)


def _fox_proj_kernel(x_ref, g_ref, w_ref, bf_ref, o_ref, fq_ref, fk_ref, carry_ref, wb_ref, wg_ref,
                     *, tm, tn, tiles_per_seq):
    i = pl.program_id(0)

    @pl.when(i == 0)
    def _():
        fw = FOX_HEADS * FOX_DIM
        dq0 = 3 * fw + FOX_HEADS
        dqw = DIFF_HEADS * 2 * DIFF_QK
        _prepare_weights(w_ref, wb_ref, wg_ref,
                         ((0, fw, FOX_DIM ** -0.5 * LOG2_E), (fw, 2 * fw, 1.0),
                          (dq0, dqw, DIFF_QK ** -0.5 * LOG2_E), (dq0 + dqw, w_ref.shape[0] - dq0 - dqw, 1.0)),
                         3 * fw, FOX_HEADS)

    @pl.when(i % tiles_per_seq == 0)
    def _():
        carry_ref[...] = jnp.zeros_like(carry_ref)

    xn = _rms(x_ref[...], g_ref[...]).astype(BF16)

    def project(part):
        width = o_ref.shape[1] // 3
        for c0 in range(part * width, (part + 1) * width, tn):
            o_ref[:, c0:c0 + tn] = _dot_nt(xn, wb_ref[c0:c0 + tn, :]).astype(o_ref.dtype)

    gates_t = _dot_nt(wg_ref[...], xn)
    project(0)
    log_f = _log_sigmoid(gates_t + _tile_lanes(bf_ref[...], tm)) * LOG2_E
    f_t = _tile_lanes(carry_ref[...], tm) + _dot_select(log_f, _prefix_matrix(tm, tm))
    carry_ref[...] = jnp.broadcast_to(f_t[:, tm - 1:tm], carry_ref.shape)
    f = jnp.concatenate([f_t, jnp.zeros((LANES - BF16_ROWS, tm), F32)], axis=0).T
    project(1)

    src = lax.broadcasted_iota(jnp.int32, (LANES, LANES), 0)
    dst = lax.broadcasted_iota(jnp.int32, (LANES, LANES), 1)
    lane = lax.broadcasted_iota(jnp.int32, (1, LANES), 1)
    n_bias = F_PIECES * FOX_HEADS
    ones_q = jnp.where((lane >= FOX_BIAS_K_LANE) & (lane < FOX_BIAS_K_LANE + n_bias), 1.0, 0.0)
    ones_k = jnp.where((lane >= FOX_BIAS_Q_LANE) & (lane < FOX_BIAS_Q_LANE + n_bias), 1.0, 0.0)
    both = jnp.concatenate([ones_q, ones_k], axis=1)
    for c, piece in enumerate(_split3(f)):
        head = src < FOX_HEADS
        to_q = jnp.where(head & (dst == FOX_BIAS_Q_LANE + F_PIECES * src + c), 1.0, 0.0)
        to_k = jnp.where(head & (dst == FOX_BIAS_K_LANE + F_PIECES * src + c), -1.0, 0.0)
        both = both + _dot(piece, jnp.concatenate([to_q, to_k], axis=1).astype(BF16))
    fq_ref[...] = both[:, 0:LANES].astype(BF16)
    fk_ref[...] = both[:, LANES:2 * LANES].astype(BF16)
    project(2)


def fox_projection(x, gain, w, gate_bias, seq_len, *, tm=ROW_TILE, tn=PROJ_COLS):
    T, D = x.shape
    N = _rows(w) - FOX_HEADS
    row_block = lambda n: pl.BlockSpec((tm, n), lambda i: (i, 0))
    specs, operands = zip(*map(_whole, (gain, w, gate_bias)))
    return pl.pallas_call(
        functools.partial(_fox_proj_kernel, tm=tm, tn=tn, tiles_per_seq=seq_len // tm),
        grid=(T // tm,),
        in_specs=[row_block(D), *specs],
        out_specs=[row_block(N), row_block(LANES), row_block(LANES)],
        out_shape=[jax.ShapeDtypeStruct((T, N), BF16), jax.ShapeDtypeStruct((T, LANES), BF16),
                   jax.ShapeDtypeStruct((T, LANES), BF16)],
        scratch_shapes=[pltpu.VMEM((BF16_ROWS, LANES), F32), pltpu.VMEM((N, D), BF16),
                        pltpu.VMEM((BF16_ROWS, D), BF16)],
        compiler_params=_params("arbitrary"),
        name="fox_projection",
    )(x, *operands)


def _mlstm_proj_kernel(x_ref, halo_ref, g_ref, w_raw_ref, gb_ref, cw_ref, o_ref, gr_ref, up_ref, nat_ref, w_ref,
                       wg_ref, *, tm, tiles_per_seq):
    i = pl.program_id(0)

    @pl.when(i == 0)
    def _():
        g0 = 2 * ML_HEADS * ML_QK + ML_HEADS * ML_V
        og_col = g0 + 2 * ML_HEADS
        _prepare_weights(w_raw_ref, w_ref, wg_ref, ((0, g0, 1.0), (og_col, w_raw_ref.shape[0] - og_col, 1.0)),
                         g0, 2 * ML_HEADS)

    halo = _rms(halo_ref[...], g_ref[...])
    xe = jnp.concatenate([jnp.where(i % tiles_per_seq == 0, 0.0, halo).astype(BF16),
                          _rms(x_ref[...], g_ref[...]).astype(BF16)], axis=0)
    xn = xe[BF16_ROWS:]

    qk_slabs = 2 * ML_HEADS * ML_QK // LANES
    rows = tm // ROW_PHASES
    v0 = qk_slabs * LANES
    og0 = v0 + ML_HEADS * ML_V
    for s0 in range(0, qk_slabs, 2):
        up = _dot_nt(xe, w_ref[s0 * LANES:(s0 + 2) * LANES, :])
        up_ref[s0] = up[:, 0:LANES]
        up_ref[s0 + 1] = up[:, LANES:2 * LANES]
    pre_t = _dot_nt(wg_ref[...], xn) + _tile_lanes(gb_ref[...], tm)
    o_ref[:, v0:og0] = _dot_nt(xn, w_ref[v0:og0, :]).astype(o_ref.dtype)
    for s in range(qk_slabs):
        cw = cw_ref[:, s * LANES:(s + 1) * LANES]
        for phase in range(ROW_PHASES):
            y = None
            for tap in range(ML_CONV):
                first = BF16_ROWS + phase - (ML_CONV - 1 - tap)
                term = cw[tap:tap + 1] * up_ref[s, pl.ds(first, rows, stride=ROW_PHASES), :]
                y = term if y is None else y + term
            y = y * _sigmoid(y)
            if s < qk_slabs // 2:
                y = y * (ML_QK ** -0.5)
            nat_ref[s, pl.ds(phase, rows, stride=ROW_PHASES), :] = y
        o_ref[:, s * LANES:(s + 1) * LANES] = nat_ref[s].astype(o_ref.dtype)

    og = _dot_nt(xn, w_ref[og0:, :])
    row = lax.broadcasted_iota(jnp.int32, (BF16_ROWS, 1), 0)
    val_t = jnp.where(row < ML_HEADS, pre_t, _log_sigmoid(pre_t)) * LOG2_E
    g_t = jnp.where(row < ML_HEADS, val_t, _dot_select(val_t, _prefix_matrix(tm, ML_CHUNK)))
    gr_ref[0] = g_t[0:2 * ML_HEADS]
    o_ref[:, og0:] = _sigmoid(og).astype(o_ref.dtype)


def mlstm_projection(x, gain, w, gate_bias, conv_w, seq_len, *, tm=ROW_TILE):
    T, D = x.shape
    N = _rows(w) - 2 * ML_HEADS
    tiles_per_seq = seq_len // tm
    halo_blocks = tm // BF16_ROWS
    row_block = lambda n: pl.BlockSpec((tm, n), lambda i: (i, 0))
    specs, operands = zip(*map(_whole, (gain, w, gate_bias, conv_w)))
    return pl.pallas_call(
        functools.partial(_mlstm_proj_kernel, tm=tm, tiles_per_seq=tiles_per_seq),
        grid=(T // tm,),
        in_specs=[row_block(D),
                  pl.BlockSpec((BF16_ROWS, D), lambda i: (jnp.maximum(i * halo_blocks - 1, 0), 0)),
                  *specs],
        out_specs=[row_block(N),
                   pl.BlockSpec((1, 2 * ML_HEADS, tm), lambda i: (i // tiles_per_seq, 0, i % tiles_per_seq))],
        out_shape=[jax.ShapeDtypeStruct((T, N), BF16),
                   jax.ShapeDtypeStruct((T // seq_len, 2 * ML_HEADS, seq_len), F32)],
        scratch_shapes=[pltpu.VMEM((2 * ML_HEADS * ML_QK // LANES, tm + BF16_ROWS, LANES), F32),
                        pltpu.VMEM((2 * ML_HEADS * ML_QK // LANES, tm, LANES), F32),
                        pltpu.VMEM((N, D), BF16), pltpu.VMEM((BF16_ROWS, D), BF16)],
        compiler_params=_params("arbitrary"),
        name="mlstm_projection",
    )(x, x, *operands)


def _build_vt(v_ref, vt_ref, n_heads, rows):
    S = v_ref.shape[1]
    r = lax.broadcasted_iota(jnp.int32, (LANES, LANES), 0)
    c = lax.broadcasted_iota(jnp.int32, (LANES, LANES), 1)
    eye = jnp.where(r == c, 1.0, 0.0).astype(BF16)
    per_group = LANES // rows
    for g in range(n_heads // per_group):
        for c0 in range(0, S, TRANSPOSE_COLS):
            cs = slice(c0, c0 + TRANSPOSE_COLS)
            vt = _dot_nt(eye, v_ref[0, cs, g * LANES:(g + 1) * LANES]).astype(BF16)
            for k in range(per_group):
                vt_ref[g * per_group + k, 0:rows, cs] = vt[k * rows:(k + 1) * rows]
    for h in range(n_heads):
        vt_ref[h, rows:rows + BF16_ROWS, :] = jnp.ones((BF16_ROWS, S), BF16)


def _softmax_step(s_t, vt, m_ref, acc_ref, idx, queries=slice(None)):
    m_prev = m_ref[idx, :, queries]
    m_new = jnp.maximum(m_prev, jnp.max(s_t, axis=0, keepdims=True))
    p = jnp.exp2(s_t - m_new).astype(BF16)
    acc_ref[idx, :, queries] = jnp.exp2(m_prev - m_new) * acc_ref[idx, :, queries] + _dot(vt, p)
    m_ref[idx, :, queries] = m_new


def _causal_steps(i, logits, attend, n_streams, buf_a, buf_b):
    def phase(j, src, dst, next_is_diagonal=False):
        for n in range(n_streams):
            logits(j + 1, dst, n, next_is_diagonal)
            attend(j, src, n, False)

    for n in range(n_streams):
        logits(0, buf_a, n, False)

    def body(jj, carry):
        phase(2 * jj, buf_a, buf_b)
        phase(2 * jj + 1, buf_b, buf_a)
        return carry

    lax.fori_loop(0, lax.shift_right_logical(i, 1), body, 0)
    odd = lax.rem(i, 2) == 1

    @pl.when(odd)
    def _():
        phase(i - 1, buf_a, buf_b, True)
        for n in range(n_streams):
            attend(i, buf_b, n, True)

    @pl.when(jnp.logical_not(odd))
    def _():
        for n in range(n_streams):
            attend(i, buf_a, n, True)


def _fox_kernel(q_ref, k_ref, v_ref, fq_ref, fk_ref, o_ref, vt_ref, qc_ref, acc_ref, m_ref, sa_ref, sb_ref, *, tq, tk):
    i = pl.program_id(1)

    @pl.when(i == 0)
    def _():
        _build_vt(v_ref, vt_ref, FOX_HEADS, FOX_DIM)

    lane = lax.broadcasted_iota(jnp.int32, (1, LANES), 1)
    fq = fq_ref[0]
    for h in range(FOX_HEADS):
        pair, half = divmod(h, 2)
        q = q_ref[0, :, pair * LANES:(pair + 1) * LANES]
        in_head = (lane >= half * FOX_DIM) & (lane < (half + 1) * FOX_DIM)
        lo_k, lo_q = FOX_BIAS_K_LANE + F_PIECES * h, FOX_BIAS_Q_LANE + F_PIECES * h
        mine = ((lane >= lo_k) & (lane < lo_k + F_PIECES)) | ((lane >= lo_q) & (lane < lo_q + F_PIECES))
        qc_ref[h, :, 0:LANES] = jnp.where(in_head, q, jnp.zeros_like(q))
        qc_ref[h, :, LANES:2 * LANES] = jnp.where(mine, fq, jnp.zeros_like(fq))
    m_ref[...] = jnp.full(m_ref.shape, NEG_INF, F32)
    acc_ref[...] = jnp.zeros(acc_ref.shape, F32)

    half = tk // 2
    key = lax.broadcasted_iota(jnp.int32, (half, tq), 0)
    qry = lax.broadcasted_iota(jnp.int32, (half, tq), 1)
    causal = key <= qry

    def key_rows(j, part=None):
        if part is None:
            return pl.ds(pl.multiple_of(j * tk, tk), tk)
        return pl.ds(pl.multiple_of(j * tk + part * half, half), half)

    def logits(j, buf, h, diagonal):
        pair = h // 2
        kc = jnp.concatenate([k_ref[0, key_rows(j), pair * LANES:(pair + 1) * LANES], fk_ref[0, key_rows(j), :]],
                             axis=1)
        if diagonal:
            buf[h, 0:half, :] = _dot_nt(kc[0:half], qc_ref[h])
            buf[h, half:tk, half:tq] = _dot_nt(kc[half:tk], qc_ref[h, half:tq, :])
        else:
            buf[h] = _dot_nt(kc, qc_ref[h])

    def attend(j, buf, h, diagonal):
        if diagonal:
            _softmax_step(jnp.where(causal, buf[h, 0:half, :], NEG_INF), vt_ref[h, :, key_rows(j, 0)],
                          m_ref, acc_ref, h)
            _softmax_step(jnp.where(causal[:, 0:tq - half], buf[h, half:tk, half:tq], NEG_INF),
                          vt_ref[h, :, key_rows(j, 1)], m_ref, acc_ref, h, slice(half, tq))
        else:
            _softmax_step(buf[h], vt_ref[h, :, key_rows(j)], m_ref, acc_ref, h)

    _causal_steps(i, logits, attend, FOX_HEADS, sa_ref, sb_ref)

    for pair in range(FOX_HEADS // 2):
        halves = []
        for half in range(2):
            a = acc_ref[2 * pair + half]
            halves.append(a[0:FOX_DIM] / a[FOX_DIM:FOX_DIM + 1])
        o_ref[0, :, pair * LANES:(pair + 1) * LANES] = jnp.concatenate(halves, axis=0).T.astype(o_ref.dtype)


def fox_attention(proj, fq, fk, *, tq=ATT_Q_TILE, tk=ATT_K_TILE):
    B, S, _ = proj.shape
    assert tq == tk
    width = FOX_HEADS * FOX_DIM
    rows = FOX_DIM + BF16_ROWS
    return pl.pallas_call(
        functools.partial(_fox_kernel, tq=tq, tk=tk),
        grid=(B, S // tq),
        in_specs=[pl.BlockSpec((1, tq, width), lambda b, i: (b, i, 0)),
                  pl.BlockSpec((1, S, width), lambda b, i: (b, 0, 1)),
                  pl.BlockSpec((1, S, width), lambda b, i: (b, 0, 2)),
                  pl.BlockSpec((1, tq, LANES), lambda b, i: (b, i, 0)),
                  pl.BlockSpec((1, S, LANES), lambda b, i: (b, 0, 0))],
        out_specs=pl.BlockSpec((1, tq, width), lambda b, i: (b, i, 0)),
        out_shape=jax.ShapeDtypeStruct((B, S, width), BF16),
        scratch_shapes=[pltpu.VMEM((FOX_HEADS, rows, S), BF16),
                        pltpu.VMEM((FOX_HEADS, tq, 2 * LANES), BF16),
                        pltpu.VMEM((FOX_HEADS, rows, tq), F32),
                        pltpu.VMEM((FOX_HEADS, 1, tq), F32),
                        pltpu.VMEM((FOX_HEADS, tk, tq), F32),
                        pltpu.VMEM((FOX_HEADS, tk, tq), F32)],
        compiler_params=_params("parallel", "arbitrary"),
        name="fox_attention",
    )(proj, proj, proj, fq, fk)


def _diff_kernel(q_ref, k_ref, v_ref, lq1_ref, lk1_ref, lq2_ref, lk2_ref, sub_ref, o_ref, vt_ref, qc_ref, acc_ref,
                 m_ref, sa_ref, sb_ref, *, tq, tk, lambda_init):
    i = pl.program_id(1)

    @pl.when(i == 0)
    def _():
        _build_vt(v_ref, vt_ref, DIFF_HEADS, DIFF_V)

    lane = lax.broadcasted_iota(jnp.int32, (1, LANES), 1)
    for h in range(DIFF_HEADS):
        q = q_ref[0, :, h * LANES:(h + 1) * LANES]
        zero = jnp.zeros_like(q)
        qc_ref[2 * h] = jnp.where(lane < DIFF_QK, q, zero)
        qc_ref[2 * h + 1] = jnp.where(lane >= DIFF_QK, q, zero)
    m_ref[...] = jnp.full(m_ref.shape, NEG_INF, F32)
    acc_ref[...] = jnp.zeros(acc_ref.shape, F32)

    half = tk // 2
    key = lax.broadcasted_iota(jnp.int32, (half, tq), 0)
    qry = lax.broadcasted_iota(jnp.int32, (half, tq), 1)
    visible = key // CHUNK <= qry // CHUNK

    def key_rows(j, part=None):
        if part is None:
            return pl.ds(pl.multiple_of(j * tk, tk), tk)
        return pl.ds(pl.multiple_of(j * tk + part * half, half), half)

    def logits(j, buf, n, diagonal):
        k = k_ref[0, key_rows(j), (n // 2) * LANES:(n // 2 + 1) * LANES]
        if diagonal:
            buf[n, 0:half, :] = _dot_nt(k[0:half], qc_ref[n])
            buf[n, half:tk, half:tq] = _dot_nt(k[half:tk], qc_ref[n, half:tq, :])
        else:
            buf[n] = _dot_nt(k, qc_ref[n])

    def attend(j, buf, n, diagonal):
        vt = vt_ref.at[n // 2]
        if diagonal:
            _softmax_step(jnp.where(visible, buf[n, 0:half, :], NEG_INF), vt[:, key_rows(j, 0)], m_ref, acc_ref, n)
            _softmax_step(jnp.where(visible[:, 0:tq - half], buf[n, half:tk, half:tq], NEG_INF),
                          vt[:, key_rows(j, 1)], m_ref, acc_ref, n, slice(half, tq))
        else:
            _softmax_step(buf[n], vt[:, key_rows(j)], m_ref, acc_ref, n)

    _causal_steps(i, logits, attend, 2 * DIFF_HEADS, sa_ref, sb_ref)

    lam = (jnp.exp(jnp.sum(lq1_ref[...] * lk1_ref[...], axis=1, keepdims=True))
           - jnp.exp(jnp.sum(lq2_ref[...] * lk2_ref[...], axis=1, keepdims=True)) + lambda_init)
    for h in range(DIFF_HEADS):
        a1, a2 = acc_ref[2 * h], acc_ref[2 * h + 1]
        o_t = a1[0:DIFF_V] / a1[DIFF_V:DIFF_V + 1] - lam * (a2[0:DIFF_V] / a2[DIFF_V:DIFF_V + 1])
        out = _rms(o_t.T, sub_ref[...]) * (1.0 - lambda_init)
        o_ref[0, :, h * LANES:(h + 1) * LANES] = out.astype(o_ref.dtype)


def diff_attention(proj, lq1, lk1, lq2, lk2, subln, lambda_init, *, tq=ATT_Q_TILE, tk=ATT_K_TILE):
    B, S, _ = proj.shape
    assert tq == tk
    width = DIFF_HEADS * DIFF_V
    rows = DIFF_V + BF16_ROWS
    specs, operands = zip(*map(_whole, (lq1, lk1, lq2, lk2, subln)))
    return pl.pallas_call(
        functools.partial(_diff_kernel, tq=tq, tk=tk, lambda_init=lambda_init),
        grid=(B, S // tq),
        in_specs=[pl.BlockSpec((1, tq, width), lambda b, i: (b, i, 3)),
                  pl.BlockSpec((1, S, width), lambda b, i: (b, 0, 4)),
                  pl.BlockSpec((1, S, width), lambda b, i: (b, 0, 5)),
                  *specs],
        out_specs=pl.BlockSpec((1, tq, width), lambda b, i: (b, i, 0)),
        out_shape=jax.ShapeDtypeStruct((B, S, width), BF16),
        scratch_shapes=[pltpu.VMEM((DIFF_HEADS, rows, S), BF16),
                        pltpu.VMEM((2 * DIFF_HEADS, tq, LANES), BF16),
                        pltpu.VMEM((2 * DIFF_HEADS, rows, tq), F32),
                        pltpu.VMEM((2 * DIFF_HEADS, 1, tq), F32),
                        pltpu.VMEM((2 * DIFF_HEADS, tk, tq), F32),
                        pltpu.VMEM((2 * DIFF_HEADS, tk, tq), F32)],
        compiler_params=_params("parallel", "arbitrary"),
        name="diff_attention",
    )(proj, proj, proj, *operands)


def _mlstm_kernel(qk_ref, v_ref, sg_ref, gr_ref, hn_ref, o_ref, ct_ref, m_ref, vt_ref, *, L, nb):
    c = pl.program_id(1)

    @pl.when(c == 0)
    def _():
        ct_ref[...] = jnp.zeros_like(ct_ref)
        m_ref[...] = jnp.zeros_like(m_ref)
        for n in range(nb * ML_HEADS):
            vt_ref[n, ML_V:ML_V + BF16_ROWS, :] = jnp.ones((BF16_ROWS, L), BF16)

    src = lax.broadcasted_iota(jnp.int32, (L, L), 0)
    dst = lax.broadcasted_iota(jnp.int32, (L, L), 1)
    causal = src <= dst
    eye = jnp.where(src == dst, 1.0, 0.0).astype(BF16)
    k0 = ML_HEADS * ML_QK
    for b in range(nb):
        gr = gr_ref[b]
        gc = jnp.concatenate([gr, jnp.zeros((LANES - gr.shape[0], L), F32)], axis=0).T
        for h in range(ML_HEADS):
            n = b * ML_HEADS + h
            q = qk_ref[b, :, h * ML_QK:(h + 1) * ML_QK]
            k = qk_ref[b, :, k0 + h * ML_QK:k0 + (h + 1) * ML_QK]
            vt_ref[n, 0:ML_V, :] = _dot_nt(eye, v_ref[b, :, h * ML_V:(h + 1) * ML_V]).astype(BF16)
            v_t = vt_ref[n]
            r_col = gc[:, h:h + 1] - gc[:, ML_HEADS + h:ML_HEADS + h + 1]
            b_row = gr[ML_HEADS + h:ML_HEADS + h + 1, :]
            r_row = gr[h:h + 1, :] - b_row
            g = b_row[:, L - 1:L]
            ct = ct_ref[n]
            m = m_ref[n][:, 0:1]

            dm = jnp.where(causal, r_col, NEG_INF)
            mt = jnp.maximum(m, jnp.max(dm, axis=0, keepdims=True))
            s_t = _dot_nt(k, q) * jnp.exp2(dm - mt)
            both = jnp.exp2(m - mt) * _dot_nt(ct.astype(BF16), q) + _dot(v_t, s_t.astype(BF16))
            den = both[ML_V:ML_V + 1]
            hh = both[0:ML_V] / jnp.maximum(jnp.abs(den), jnp.exp2(-(b_row + mt)))
            hh = hh * lax.rsqrt(jnp.mean(hh * hh, axis=0, keepdims=True) + RMS_EPS)

            m_next = jnp.maximum(m, jnp.max(r_row, axis=1, keepdims=True))
            kw = (k.astype(F32) * jnp.exp2(r_col - m_next)).astype(BF16)
            ct_ref[n] = jnp.exp2(m - m_next) * ct + _dot(v_t, kw)
            m_ref[n] = jnp.broadcast_to(g + m_next, (1, LANES))

            vs = slice(h * ML_V, (h + 1) * ML_V)
            o_ref[b, :, vs] = (hh.T * hn_ref[:, vs] * sg_ref[b, :, vs].astype(F32)).astype(o_ref.dtype)


def mlstm(proj, g_row, head_norm, *, L=ML_CHUNK, nb=ML_SEQS):
    B, S, _ = proj.shape
    W = D_MODEL
    assert L == ML_V, "one identity matrix serves the v transposes"
    norm_spec, head_norm = _whole(head_norm)
    return pl.pallas_call(
        functools.partial(_mlstm_kernel, L=L, nb=nb),
        grid=(B // nb, S // L),
        in_specs=[pl.BlockSpec((nb, L, W), lambda b, c: (b, c, 0)),
                  pl.BlockSpec((nb, L, W), lambda b, c: (b, c, 1)),
                  pl.BlockSpec((nb, L, W), lambda b, c: (b, c, 2)),
                  pl.BlockSpec((nb, 2 * ML_HEADS, L), lambda b, c: (b, 0, c)),
                  norm_spec],
        out_specs=pl.BlockSpec((nb, L, W), lambda b, c: (b, c, 0)),
        out_shape=jax.ShapeDtypeStruct((B, S, W), BF16),
        scratch_shapes=[pltpu.VMEM((nb * ML_HEADS, ML_V + BF16_ROWS, ML_QK), F32),
                        pltpu.VMEM((nb * ML_HEADS, 1, LANES), F32),
                        pltpu.VMEM((nb * ML_HEADS, ML_V + BF16_ROWS, L), BF16)],
        compiler_params=_params("parallel", "arbitrary"),
        name="mlstm",
    )(proj, proj, proj, g_row, head_norm)


def _xattn_kernel(*refs, n_in):
    x_ref, g_ref, wq_ref, mem_ref, gm_ref, wkv_ref, wo_ref, o_ref, kv_ref = refs[2 * n_in:]

    @pl.when(pl.program_id(1) == 0)
    def _():
        memn = _rms(mem_ref[0], gm_ref[...]).astype(BF16)
        for c0 in range(0, kv_ref.shape[1], PROJ_COLS):
            kv_ref[:, c0:c0 + PROJ_COLS] = _dot(memn, wkv_ref[:, c0:c0 + PROJ_COLS]).astype(BF16)

    x = x_ref[0]
    for a_ref, w_ref in zip(refs[:n_in], refs[n_in:2 * n_in]):
        x = x + _dot(a_ref[0], w_ref[...])
    xn = _rms(x, g_ref[...]).astype(BF16)
    q = (_dot(xn, wq_ref[...]) * (X_DIM ** -0.5)).astype(BF16)
    heads = []
    for h in range(X_HEADS):
        cols = slice(h * X_DIM, (h + 1) * X_DIM)
        s = _dot_nt(q[:, cols], kv_ref[:, cols])
        p = jnp.exp(s - jnp.max(s, axis=1, keepdims=True))
        l = jnp.sum(p, axis=1, keepdims=True)
        v = kv_ref[:, D_MODEL + h * X_DIM:D_MODEL + (h + 1) * X_DIM]
        heads.append((_dot(p.astype(BF16), v) / l).astype(BF16))
    o_ref[0] = x + _dot(jnp.concatenate(heads, axis=1), wo_ref[...])


def mix_out_cross_attention(acts, w_mix, x, gain, wq, mem, mem_gain, wkv, wo, *, tq=X_TILE):
    B, S, D = x.shape
    M = mem.shape[1]
    (gain_spec, gain), (wq_spec, wq), (gm_spec, mem_gain), (wkv_spec, wkv), (wo_spec, wo) = map(
        _whole, (gain, wq, mem_gain, wkv, wo))
    return pl.pallas_call(
        functools.partial(_xattn_kernel, n_in=len(acts)),
        grid=(B, S // tq),
        in_specs=([pl.BlockSpec((1, tq, a.shape[2]), lambda b, i: (b, i, 0)) for a in acts]
                  + [w[0] for w in w_mix]
                  + [pl.BlockSpec((1, tq, D), lambda b, i: (b, i, 0)),
                     gain_spec, wq_spec,
                     pl.BlockSpec((1, M, D), lambda b, i: (b, 0, 0)),
                     gm_spec, wkv_spec, wo_spec]),
        out_specs=pl.BlockSpec((1, tq, D), lambda b, i: (b, i, 0)),
        out_shape=jax.ShapeDtypeStruct((B, S, D), F32),
        scratch_shapes=[pltpu.VMEM((M, 2 * D), BF16)],
        compiler_params=_params("parallel", "arbitrary"),
        name="mix_out_cross_attention",
    )(*acts, *[w[1] for w in w_mix], x, gain, wq, mem, mem_gain, wkv, wo)


def _gelu_tanh(x):
    k = -2.0 * math.sqrt(2.0 / math.pi) * math.log2(math.e)
    return x / (1.0 + jnp.exp2(x * (k * 0.044715 * (x * x) + k)))


def _ffn_kernel(x_ref, halo_ref, g_ref, wup_ref, cw_ref, cb_ref, wd_ref, fg_ref, o_ref, up_ref, act_ref, nat_ref,
                *, tm, tiles_per_seq, final_norm):
    i = pl.program_id(0)
    halo = _rms(halo_ref[...], g_ref[...])
    xe = jnp.concatenate([jnp.where(i % tiles_per_seq == 0, 0.0, halo).astype(BF16),
                          _rms(x_ref[...], g_ref[...]).astype(BF16)], axis=0)

    n_chunks = D_FF // FFN_CHUNK
    slabs = FFN_CHUNK // LANES
    rows = tm // ROW_PHASES

    def up_project(c):
        for half in range(2):
            col0 = half * D_FF + c * FFN_CHUNK
            up = _dot(xe, wup_ref[:, col0:col0 + FFN_CHUNK])
            for s in range(slabs):
                up_ref[c % 2, half, s] = up[:, s * LANES:(s + 1) * LANES]

    def conv(c, half, s, phase):
        col0 = half * D_FF + c * FFN_CHUNK + s * LANES
        cw = cw_ref[:, col0:col0 + LANES]
        out = cb_ref[:, col0:col0 + LANES]
        for tap in range(FFN_CONV):
            first = BF16_ROWS + phase - (FFN_CONV - 1 - tap)
            out = out + cw[tap:tap + 1] * up_ref[c % 2, half, s, pl.ds(first, rows, stride=ROW_PHASES), :]
        return out

    acc = None
    piece_start = 0
    up_project(0)
    for c in range(n_chunks):
        if c + 1 < n_chunks:
            up_project(c + 1)
        for phase in range(ROW_PHASES):
            for s in range(slabs):
                act = _gelu_tanh(conv(c, 0, s, phase)) * conv(c, 1, s, phase)
                act_ref[phase * rows:(phase + 1) * rows,
                        c * FFN_CHUNK + s * LANES:c * FFN_CHUNK + (s + 1) * LANES] = act.astype(BF16)
        if (c + 1) % DOWN_CHUNKS == 0 or c + 1 == n_chunks:
            piece = slice(piece_start * FFN_CHUNK, (c + 1) * FFN_CHUNK)
            part = _dot(act_ref[:, piece], wd_ref[piece, :])
            acc = part if acc is None else acc + part
            piece_start = c + 1

    for phase in range(ROW_PHASES):
        for s in range(D_MODEL // LANES):
            nat_ref[s, pl.ds(phase, rows, stride=ROW_PHASES), :] = acc[phase * rows:(phase + 1) * rows,
                                                                       s * LANES:(s + 1) * LANES]
    y = x_ref[...] + jnp.concatenate([nat_ref[s] for s in range(D_MODEL // LANES)], axis=1)
    o_ref[...] = _rms(y, fg_ref[...]) if final_norm else y


def conv_ffn(x, gain, w_up, conv_w, conv_b, w_down, seq_len, final_gain=None, *, tm=ROW_TILE):
    T, D = x.shape
    halo_blocks = tm // BF16_ROWS
    final_norm = final_gain is not None
    specs, operands = zip(*map(_whole, (gain, w_up, conv_w, conv_b, w_down, final_gain if final_norm else gain)))
    return pl.pallas_call(
        functools.partial(_ffn_kernel, tm=tm, tiles_per_seq=seq_len // tm, final_norm=final_norm),
        grid=(T // tm,),
        in_specs=[pl.BlockSpec((tm, D), lambda i: (i, 0)),
                  pl.BlockSpec((BF16_ROWS, D), lambda i: (jnp.maximum(i * halo_blocks - 1, 0), 0)),
                  *specs],
        out_specs=pl.BlockSpec((tm, D), lambda i: (i, 0)),
        out_shape=jax.ShapeDtypeStruct((T, D), F32),
        scratch_shapes=[pltpu.VMEM((2, 2, FFN_CHUNK // LANES, tm + BF16_ROWS, LANES), F32),
                        pltpu.VMEM((tm, D_FF), BF16),
                        pltpu.VMEM((D // LANES, tm, LANES), F32)],
        compiler_params=_params("parallel"),
        name="conv_ffn",
    )(x, x, *operands)


def _gate_bias_rows(bias):
    return jnp.broadcast_to(jnp.pad(bias, (0, BF16_ROWS - bias.shape[0]))[:, None], (BF16_ROWS, LANES))


def _fox_diff_mixer(x, B, S, j, gain, w_in, fox_bf, lq1, lk1, lq2, lk2, subln, w_out, lambda_init):
    fw = FOX_HEADS * FOX_DIM
    proj, fq, fk = fox_projection(x, gain, w_in, _gate_bias_rows(fox_bf), S)
    proj = proj.reshape(B, S, -1)
    fox = fox_attention(proj, fq.reshape(B, S, LANES), fk.reshape(B, S, LANES))
    dif = diff_attention(proj, lq1, lk1, lq2, lk2, subln, lambda_init)
    row_block = lambda r: (pl.BlockSpec((None, fw, D_MODEL), lambda *_: (j, r, 0), pipeline_mode=pl.Buffered(1)), w_out)
    return [fox, dif], [row_block(0), row_block(1)]


def _mlstm_mixer(x, B, S, j, gain, w_in, conv_qk, b_i, b_f, head_norm, w_out):
    proj, g_row = mlstm_projection(x, gain, w_in, _gate_bias_rows(jnp.concatenate([b_i, b_f])), conv_qk, S)
    h = mlstm(proj.reshape(B, S, -1), g_row, head_norm)
    return [h], [_whole((w_out, j))]


def kernel(x, mem, mix_norm, xattn_norm, mem_norm, ffn_norm, attn_w_in, attn_fox_bf, diff_lq1, diff_lk1, diff_lq2, diff_lk2, diff_subln, attn_w_out, mlstm_w_in, mlstm_conv_qk, mlstm_b_i, mlstm_b_f, mlstm_head_norm, mlstm_w_out, xattn_wq, xattn_wkv, xattn_wo, ffn_w_up, ffn_conv_w, ffn_conv_b, ffn_w_down, final_norm):
    B, S, D = x.shape
    depth = mix_norm.shape[0]
    x = x.reshape(B * S, D)
    rows = lambda a: a.reshape(a.shape[0], 1, -1)
    to_bf16 = lambda a: a.astype(BF16)
    mix_norm, xattn_norm, mem_norm, ffn_norm = map(rows, (mix_norm, xattn_norm, mem_norm, ffn_norm))
    diff_lq1, diff_lk1, diff_lq2, diff_lk2, diff_subln = map(rows, (diff_lq1, diff_lk1, diff_lq2, diff_lk2, diff_subln))
    mlstm_head_norm, ffn_conv_b = rows(mlstm_head_norm), rows(ffn_conv_b)
    attn_w_out, mlstm_w_out, xattn_wq, xattn_wkv, xattn_wo, ffn_w_up, ffn_w_down = map(
        to_bf16, (attn_w_out, mlstm_w_out, xattn_wq, xattn_wkv, xattn_wo, ffn_w_up, ffn_w_down))
    attn_w_in, mlstm_w_in = jnp.swapaxes(attn_w_in, 1, 2), jnp.swapaxes(mlstm_w_in, 1, 2)
    for layer in range(depth):
        j = layer // 2
        if layer % 2 == 0:
            lambda_init = 0.8 - 0.6 * math.exp(-0.3 * layer)
            mixed, w_mix = _fox_diff_mixer(x, B, S, j, (mix_norm, layer), (attn_w_in, j), attn_fox_bf[j],
                                           (diff_lq1, j), (diff_lk1, j), (diff_lq2, j), (diff_lk2, j),
                                           (diff_subln, j), attn_w_out, lambda_init)
        else:
            mixed, w_mix = _mlstm_mixer(x, B, S, j, (mix_norm, layer), (mlstm_w_in, j), (mlstm_conv_qk, j),
                                        mlstm_b_i[j], mlstm_b_f[j], (mlstm_head_norm, j), mlstm_w_out)
        x = mix_out_cross_attention(mixed, w_mix, x.reshape(B, S, D), (xattn_norm, layer), (xattn_wq, layer), mem,
                                    (mem_norm, layer), (xattn_wkv, layer), (xattn_wo, layer)).reshape(B * S, D)
        x = conv_ffn(x, (ffn_norm, layer), (ffn_w_up, layer), (ffn_conv_w, layer), (ffn_conv_b, layer),
                     (ffn_w_down, layer), S, final_norm.reshape(1, D) if layer == depth - 1 else None)
    return x.reshape(B, S, D)
```

```python
import functools
import math

import jax
import jax.numpy as jnp
from jax import lax
from jax.experimental import pallas as pl
from jax.experimental.pallas import tpu as pltpu

F32 = jnp.float32
BF16 = jnp.bfloat16

D_MODEL = 1024
RMS_EPS = 1e-6
NEG_INF = -1e30
CHUNK = 64
FOX_HEADS, FOX_DIM = 8, 64
DIFF_HEADS, DIFF_QK, DIFF_V = 4, 64, 128
ML_HEADS, ML_QK, ML_V, ML_CONV = 4, 128, 256, 4
X_HEADS, X_DIM = 4, 256
D_FF = 2816
FFN_CONV = 3
LANES = 128
BF16_ROWS = 16
VMEM_LIMIT = 56 * 1024 * 1024

ROW_TILE = 512
ATT_Q_TILE = 512
ATT_K_TILE = 512
LOG2_E = math.log2(math.e)
ML_CHUNK = 256
ML_SEQS = 2
X_TILE = 1024
PROJ_COLS = 1024
FFN_CHUNK = 256
ROW_PHASES = 4
DOWN_CHUNKS = 4
TRANSPOSE_COLS = 512
F_PIECES = 3
FOX_BIAS_K_LANE = 0
FOX_BIAS_Q_LANE = 32


def _params(*sem):
    return pltpu.CompilerParams(dimension_semantics=sem, vmem_limit_bytes=VMEM_LIMIT)


def _resident(shape):
    return pl.BlockSpec(shape, lambda *_: (0,) * len(shape), pipeline_mode=pl.Buffered(1))


def _whole(a):
    if isinstance(a, tuple):
        arr, layer = a
        tail = arr.shape[1:]
        spec = pl.BlockSpec((None,) + tail, lambda *_: (layer,) + (0,) * len(tail), pipeline_mode=pl.Buffered(1))
        return spec, arr
    return _resident(a.shape), a


def _rows(a):
    return (a[0] if isinstance(a, tuple) else a).shape[-2]


def _rms(x, gain):
    return x * lax.rsqrt(jnp.mean(x * x, axis=-1, keepdims=True) + RMS_EPS) * gain


def _sigmoid(x):
    return 1.0 / (1.0 + jnp.exp(-x))


def _dot(a, b):
    return jnp.dot(a, b, preferred_element_type=F32)


def _dot_nt(a, b):
    return lax.dot_general(a, b, (((1,), (1,)), ((), ())), preferred_element_type=F32)


def _split3(x):
    hi = x.astype(BF16)
    r1 = x - hi.astype(F32)
    mid = r1.astype(BF16)
    return hi, mid, (r1 - mid.astype(F32)).astype(BF16)


def _dot_select(x, sel):
    hi, mid, lo = _split3(x)
    return _dot(hi, sel) + _dot(mid, sel) + _dot(lo, sel)


def _prefix_matrix(n, block):
    r = lax.broadcasted_iota(jnp.int32, (n, n), 0)
    c = lax.broadcasted_iota(jnp.int32, (n, n), 1)
    return jnp.where((r <= c) & (r // block == c // block), 1.0, 0.0).astype(BF16)


def _tile_lanes(a, n):
    return jnp.concatenate([a] * (n // LANES), axis=1)


def _log_sigmoid(x):
    return jnp.minimum(x, 0.0) - jnp.log1p(jnp.exp(-jnp.abs(x)))


def _prepare_weights(wt_ref, wb_ref, wg_ref, pieces, gate_row, n_gates):
    dst = 0
    for src, height, scale in pieces:
        for r0 in range(0, height, PROJ_COLS):
            n = min(PROJ_COLS, height - r0)
            wb_ref[dst + r0:dst + r0 + n, :] = (wt_ref[src + r0:src + r0 + n, :] * scale).astype(BF16)
        dst += height
    gates = wt_ref[gate_row:gate_row + n_gates, :]
    wg_ref[...] = jnp.concatenate([gates, jnp.zeros((BF16_ROWS - n_gates, gates.shape[1]), F32)],
                                  axis=0).astype(BF16)


def _fox_proj_kernel(x_ref, g_ref, w_ref, bf_ref, o_ref, fq_ref, fk_ref, carry_ref, wb_ref, wg_ref,
                     *, tm, tn, tiles_per_seq):
    i = pl.program_id(0)

    @pl.when(i == 0)
    def _():
        fw = FOX_HEADS * FOX_DIM
        dq0 = 3 * fw + FOX_HEADS
        dqw = DIFF_HEADS * 2 * DIFF_QK
        _prepare_weights(w_ref, wb_ref, wg_ref,
                         ((0, fw, FOX_DIM ** -0.5 * LOG2_E), (fw, 2 * fw, 1.0),
                          (dq0, dqw, DIFF_QK ** -0.5 * LOG2_E), (dq0 + dqw, w_ref.shape[0] - dq0 - dqw, 1.0)),
                         3 * fw, FOX_HEADS)

    @pl.when(i % tiles_per_seq == 0)
    def _():
        carry_ref[...] = jnp.zeros_like(carry_ref)

    xn = _rms(x_ref[...], g_ref[...]).astype(BF16)

    def project(part):
        width = o_ref.shape[1] // 3
        for c0 in range(part * width, (part + 1) * width, tn):
            o_ref[:, c0:c0 + tn] = _dot_nt(xn, wb_ref[c0:c0 + tn, :]).astype(o_ref.dtype)

    gates_t = _dot_nt(wg_ref[...], xn)
    project(0)
    log_f = _log_sigmoid(gates_t + _tile_lanes(bf_ref[...], tm)) * LOG2_E
    f_t = _tile_lanes(carry_ref[...], tm) + _dot_select(log_f, _prefix_matrix(tm, tm))
    carry_ref[...] = jnp.broadcast_to(f_t[:, tm - 1:tm], carry_ref.shape)
    f = jnp.concatenate([f_t, jnp.zeros((LANES - BF16_ROWS, tm), F32)], axis=0).T
    project(1)

    src = lax.broadcasted_iota(jnp.int32, (LANES, LANES), 0)
    dst = lax.broadcasted_iota(jnp.int32, (LANES, LANES), 1)
    lane = lax.broadcasted_iota(jnp.int32, (1, LANES), 1)
    n_bias = F_PIECES * FOX_HEADS
    ones_q = jnp.where((lane >= FOX_BIAS_K_LANE) & (lane < FOX_BIAS_K_LANE + n_bias), 1.0, 0.0)
    ones_k = jnp.where((lane >= FOX_BIAS_Q_LANE) & (lane < FOX_BIAS_Q_LANE + n_bias), 1.0, 0.0)
    both = jnp.concatenate([ones_q, ones_k], axis=1)
    for c, piece in enumerate(_split3(f)):
        head = src < FOX_HEADS
        to_q = jnp.where(head & (dst == FOX_BIAS_Q_LANE + F_PIECES * src + c), 1.0, 0.0)
        to_k = jnp.where(head & (dst == FOX_BIAS_K_LANE + F_PIECES * src + c), -1.0, 0.0)
        both = both + _dot(piece, jnp.concatenate([to_q, to_k], axis=1).astype(BF16))
    fq_ref[...] = both[:, 0:LANES].astype(BF16)
    fk_ref[...] = both[:, LANES:2 * LANES].astype(BF16)
    project(2)


def fox_projection(x, gain, w, gate_bias, seq_len, *, tm=ROW_TILE, tn=PROJ_COLS):
    T, D = x.shape
    N = _rows(w) - FOX_HEADS
    row_block = lambda n: pl.BlockSpec((tm, n), lambda i: (i, 0))
    specs, operands = zip(*map(_whole, (gain, w, gate_bias)))
    return pl.pallas_call(
        functools.partial(_fox_proj_kernel, tm=tm, tn=tn, tiles_per_seq=seq_len // tm),
        grid=(T // tm,),
        in_specs=[row_block(D), *specs],
        out_specs=[row_block(N), row_block(LANES), row_block(LANES)],
        out_shape=[jax.ShapeDtypeStruct((T, N), BF16), jax.ShapeDtypeStruct((T, LANES), BF16),
                   jax.ShapeDtypeStruct((T, LANES), BF16)],
        scratch_shapes=[pltpu.VMEM((BF16_ROWS, LANES), F32), pltpu.VMEM((N, D), BF16),
                        pltpu.VMEM((BF16_ROWS, D), BF16)],
        compiler_params=_params("arbitrary"),
        name="fox_projection",
    )(x, *operands)


def _mlstm_proj_kernel(x_ref, halo_ref, g_ref, w_raw_ref, gb_ref, cw_ref, o_ref, gr_ref, up_ref, nat_ref, w_ref,
                       wg_ref, *, tm, tiles_per_seq):
    i = pl.program_id(0)

    @pl.when(i == 0)
    def _():
        g0 = 2 * ML_HEADS * ML_QK + ML_HEADS * ML_V
        og_col = g0 + 2 * ML_HEADS
        _prepare_weights(w_raw_ref, w_ref, wg_ref, ((0, g0, 1.0), (og_col, w_raw_ref.shape[0] - og_col, 1.0)),
                         g0, 2 * ML_HEADS)

    halo = _rms(halo_ref[...], g_ref[...])
    xe = jnp.concatenate([jnp.where(i % tiles_per_seq == 0, 0.0, halo).astype(BF16),
                          _rms(x_ref[...], g_ref[...]).astype(BF16)], axis=0)
    xn = xe[BF16_ROWS:]

    qk_slabs = 2 * ML_HEADS * ML_QK // LANES
    rows = tm // ROW_PHASES
    v0 = qk_slabs * LANES
    og0 = v0 + ML_HEADS * ML_V
    for s0 in range(0, qk_slabs, 2):
        up = _dot_nt(xe, w_ref[s0 * LANES:(s0 + 2) * LANES, :])
        up_ref[s0] = up[:, 0:LANES]
        up_ref[s0 + 1] = up[:, LANES:2 * LANES]
    pre_t = _dot_nt(wg_ref[...], xn) + _tile_lanes(gb_ref[...], tm)
    o_ref[:, v0:og0] = _dot_nt(xn, w_ref[v0:og0, :]).astype(o_ref.dtype)
    for s in range(qk_slabs):
        cw = cw_ref[:, s * LANES:(s + 1) * LANES]
        for phase in range(ROW_PHASES):
            y = None
            for tap in range(ML_CONV):
                first = BF16_ROWS + phase - (ML_CONV - 1 - tap)
                term = cw[tap:tap + 1] * up_ref[s, pl.ds(first, rows, stride=ROW_PHASES), :]
                y = term if y is None else y + term
            y = y * _sigmoid(y)
            if s < qk_slabs // 2:
                y = y * (ML_QK ** -0.5)
            nat_ref[s, pl.ds(phase, rows, stride=ROW_PHASES), :] = y
        o_ref[:, s * LANES:(s + 1) * LANES] = nat_ref[s].astype(o_ref.dtype)

    og = _dot_nt(xn, w_ref[og0:, :])
    row = lax.broadcasted_iota(jnp.int32, (BF16_ROWS, 1), 0)
    val_t = jnp.where(row < ML_HEADS, pre_t, _log_sigmoid(pre_t)) * LOG2_E
    g_t = jnp.where(row < ML_HEADS, val_t, _dot_select(val_t, _prefix_matrix(tm, ML_CHUNK)))
    gr_ref[0] = g_t[0:2 * ML_HEADS]
    o_ref[:, og0:] = _sigmoid(og).astype(o_ref.dtype)


def mlstm_projection(x, gain, w, gate_bias, conv_w, seq_len, *, tm=ROW_TILE):
    T, D = x.shape
    N = _rows(w) - 2 * ML_HEADS
    tiles_per_seq = seq_len // tm
    halo_blocks = tm // BF16_ROWS
    row_block = lambda n: pl.BlockSpec((tm, n), lambda i: (i, 0))
    specs, operands = zip(*map(_whole, (gain, w, gate_bias, conv_w)))
    return pl.pallas_call(
        functools.partial(_mlstm_proj_kernel, tm=tm, tiles_per_seq=tiles_per_seq),
        grid=(T // tm,),
        in_specs=[row_block(D),
                  pl.BlockSpec((BF16_ROWS, D), lambda i: (jnp.maximum(i * halo_blocks - 1, 0), 0)),
                  *specs],
        out_specs=[row_block(N),
                   pl.BlockSpec((1, 2 * ML_HEADS, tm), lambda i: (i // tiles_per_seq, 0, i % tiles_per_seq))],
        out_shape=[jax.ShapeDtypeStruct((T, N), BF16),
                   jax.ShapeDtypeStruct((T // seq_len, 2 * ML_HEADS, seq_len), F32)],
        scratch_shapes=[pltpu.VMEM((2 * ML_HEADS * ML_QK // LANES, tm + BF16_ROWS, LANES), F32),
                        pltpu.VMEM((2 * ML_HEADS * ML_QK // LANES, tm, LANES), F32),
                        pltpu.VMEM((N, D), BF16), pltpu.VMEM((BF16_ROWS, D), BF16)],
        compiler_params=_params("arbitrary"),
        name="mlstm_projection",
    )(x, x, *operands)


def _cast_blocks(arrays, grid):
    steps = grid[0] * grid[1]
    views = [a.reshape(-1, a.shape[-1]) for a in arrays]
    assert all(v.shape[0] % (steps * BF16_ROWS) == 0 for v in views)
    specs = [pl.BlockSpec((v.shape[0] // steps, v.shape[1]), lambda b, i: (b * grid[1] + i, 0)) for v in views]
    return specs, views, [jax.ShapeDtypeStruct(v.shape, BF16) for v in views]


def _with_casts(kernel, n_in, n_cast):
    def wrapped(*refs):
        cast_in = refs[n_in:n_in + n_cast]
        cast_out = refs[n_in + n_cast + 1:n_in + 2 * n_cast + 1]
        for src, dst in zip(cast_in, cast_out):
            dst[...] = src[...].astype(BF16)
        kernel(*refs[:n_in], refs[n_in + n_cast], *refs[n_in + 2 * n_cast + 1:])
    return wrapped


def _build_vt(v_ref, vt_ref, n_heads, rows):
    S = v_ref.shape[1]
    r = lax.broadcasted_iota(jnp.int32, (LANES, LANES), 0)
    c = lax.broadcasted_iota(jnp.int32, (LANES, LANES), 1)
    eye = jnp.where(r == c, 1.0, 0.0).astype(BF16)
    per_group = LANES // rows
    for g in range(n_heads // per_group):
        for c0 in range(0, S, TRANSPOSE_COLS):
            cs = slice(c0, c0 + TRANSPOSE_COLS)
            vt = _dot_nt(eye, v_ref[0, cs, g * LANES:(g + 1) * LANES]).astype(BF16)
            for k in range(per_group):
                vt_ref[g * per_group + k, 0:rows, cs] = vt[k * rows:(k + 1) * rows]
    for h in range(n_heads):
        vt_ref[h, rows:rows + BF16_ROWS, :] = jnp.ones((BF16_ROWS, S), BF16)


def _softmax_step(s_t, vt, m_ref, acc_ref, idx, queries=slice(None)):
    m_prev = m_ref[idx, :, queries]
    m_new = jnp.maximum(m_prev, jnp.max(s_t, axis=0, keepdims=True))
    p = jnp.exp2(s_t - m_new).astype(BF16)
    acc_ref[idx, :, queries] = jnp.exp2(m_prev - m_new) * acc_ref[idx, :, queries] + _dot(vt, p)
    m_ref[idx, :, queries] = m_new


def _causal_steps(i, logits, attend, n_streams, buf_a, buf_b):
    def phase(j, src, dst, next_is_diagonal=False):
        for n in range(n_streams):
            logits(j + 1, dst, n, next_is_diagonal)
            attend(j, src, n, False)

    for n in range(n_streams):
        logits(0, buf_a, n, False)

    def body(jj, carry):
        phase(2 * jj, buf_a, buf_b)
        phase(2 * jj + 1, buf_b, buf_a)
        return carry

    lax.fori_loop(0, lax.shift_right_logical(i, 1), body, 0)
    odd = lax.rem(i, 2) == 1

    @pl.when(odd)
    def _():
        phase(i - 1, buf_a, buf_b, True)
        for n in range(n_streams):
            attend(i, buf_b, n, True)

    @pl.when(jnp.logical_not(odd))
    def _():
        for n in range(n_streams):
            attend(i, buf_a, n, True)


def _fox_kernel(q_ref, k_ref, v_ref, fq_ref, fk_ref, o_ref, vt_ref, qc_ref, acc_ref, m_ref, sa_ref, sb_ref, *, tq, tk):
    i = pl.program_id(1)

    @pl.when(i == 0)
    def _():
        _build_vt(v_ref, vt_ref, FOX_HEADS, FOX_DIM)

    lane = lax.broadcasted_iota(jnp.int32, (1, LANES), 1)
    fq = fq_ref[0]
    for h in range(FOX_HEADS):
        pair, half = divmod(h, 2)
        q = q_ref[0, :, pair * LANES:(pair + 1) * LANES]
        in_head = (lane >= half * FOX_DIM) & (lane < (half + 1) * FOX_DIM)
        lo_k, lo_q = FOX_BIAS_K_LANE + F_PIECES * h, FOX_BIAS_Q_LANE + F_PIECES * h
        mine = ((lane >= lo_k) & (lane < lo_k + F_PIECES)) | ((lane >= lo_q) & (lane < lo_q + F_PIECES))
        qc_ref[h, :, 0:LANES] = jnp.where(in_head, q, jnp.zeros_like(q))
        qc_ref[h, :, LANES:2 * LANES] = jnp.where(mine, fq, jnp.zeros_like(fq))
    m_ref[...] = jnp.full(m_ref.shape, NEG_INF, F32)
    acc_ref[...] = jnp.zeros(acc_ref.shape, F32)

    half = tk // 2
    key = lax.broadcasted_iota(jnp.int32, (half, tq), 0)
    qry = lax.broadcasted_iota(jnp.int32, (half, tq), 1)
    causal = key <= qry

    def key_rows(j, part=None):
        if part is None:
            return pl.ds(pl.multiple_of(j * tk, tk), tk)
        return pl.ds(pl.multiple_of(j * tk + part * half, half), half)

    def logits(j, buf, h, diagonal):
        pair = h // 2
        kc = jnp.concatenate([k_ref[0, key_rows(j), pair * LANES:(pair + 1) * LANES], fk_ref[0, key_rows(j), :]],
                             axis=1)
        if diagonal:
            buf[h, 0:half, :] = _dot_nt(kc[0:half], qc_ref[h])
            buf[h, half:tk, half:tq] = _dot_nt(kc[half:tk], qc_ref[h, half:tq, :])
        else:
            buf[h] = _dot_nt(kc, qc_ref[h])

    def attend(j, buf, h, diagonal):
        if diagonal:
            _softmax_step(jnp.where(causal, buf[h, 0:half, :], NEG_INF), vt_ref[h, :, key_rows(j, 0)],
                          m_ref, acc_ref, h)
            _softmax_step(jnp.where(causal[:, 0:tq - half], buf[h, half:tk, half:tq], NEG_INF),
                          vt_ref[h, :, key_rows(j, 1)], m_ref, acc_ref, h, slice(half, tq))
        else:
            _softmax_step(buf[h], vt_ref[h, :, key_rows(j)], m_ref, acc_ref, h)

    _causal_steps(i, logits, attend, FOX_HEADS, sa_ref, sb_ref)

    for pair in range(FOX_HEADS // 2):
        halves = []
        for half in range(2):
            a = acc_ref[2 * pair + half]
            halves.append(a[0:FOX_DIM] / a[FOX_DIM:FOX_DIM + 1])
        o_ref[0, :, pair * LANES:(pair + 1) * LANES] = jnp.concatenate(halves, axis=0).T.astype(o_ref.dtype)


def fox_attention(proj, fq, fk, cast=(), *, tq=ATT_Q_TILE, tk=ATT_K_TILE):
    B, S, _ = proj.shape
    assert tq == tk
    width = FOX_HEADS * FOX_DIM
    rows = FOX_DIM + BF16_ROWS
    grid = (B, S // tq)
    cast_specs, cast_views, cast_shapes = _cast_blocks(cast, grid)
    out, *cast_out = pl.pallas_call(
        _with_casts(functools.partial(_fox_kernel, tq=tq, tk=tk), 5, len(cast)),
        grid=grid,
        in_specs=[pl.BlockSpec((1, tq, width), lambda b, i: (b, i, 0)),
                  pl.BlockSpec((1, S, width), lambda b, i: (b, 0, 1)),
                  pl.BlockSpec((1, S, width), lambda b, i: (b, 0, 2)),
                  pl.BlockSpec((1, tq, LANES), lambda b, i: (b, i, 0)),
                  pl.BlockSpec((1, S, LANES), lambda b, i: (b, 0, 0)),
                  *cast_specs],
        out_specs=[pl.BlockSpec((1, tq, width), lambda b, i: (b, i, 0)), *cast_specs],
        out_shape=[jax.ShapeDtypeStruct((B, S, width), BF16), *cast_shapes],
        scratch_shapes=[pltpu.VMEM((FOX_HEADS, rows, S), BF16),
                        pltpu.VMEM((FOX_HEADS, tq, 2 * LANES), BF16),
                        pltpu.VMEM((FOX_HEADS, rows, tq), F32),
                        pltpu.VMEM((FOX_HEADS, 1, tq), F32),
                        pltpu.VMEM((FOX_HEADS, tk, tq), F32),
                        pltpu.VMEM((FOX_HEADS, tk, tq), F32)],
        compiler_params=_params("parallel", "arbitrary"),
        name="fox_attention",
    )(proj, proj, proj, fq, fk, *cast_views)
    return out, [c.reshape(a.shape) for c, a in zip(cast_out, cast)]


def _diff_kernel(q_ref, k_ref, v_ref, lq1_ref, lk1_ref, lq2_ref, lk2_ref, sub_ref, o_ref, vt_ref, qc_ref, acc_ref,
                 m_ref, sa_ref, sb_ref, *, tq, tk, lambda_init):
    i = pl.program_id(1)

    @pl.when(i == 0)
    def _():
        _build_vt(v_ref, vt_ref, DIFF_HEADS, DIFF_V)

    lane = lax.broadcasted_iota(jnp.int32, (1, LANES), 1)
    for h in range(DIFF_HEADS):
        q = q_ref[0, :, h * LANES:(h + 1) * LANES]
        zero = jnp.zeros_like(q)
        qc_ref[2 * h] = jnp.where(lane < DIFF_QK, q, zero)
        qc_ref[2 * h + 1] = jnp.where(lane >= DIFF_QK, q, zero)
    m_ref[...] = jnp.full(m_ref.shape, NEG_INF, F32)
    acc_ref[...] = jnp.zeros(acc_ref.shape, F32)

    half = tk // 2
    key = lax.broadcasted_iota(jnp.int32, (half, tq), 0)
    qry = lax.broadcasted_iota(jnp.int32, (half, tq), 1)
    visible = key // CHUNK <= qry // CHUNK

    def key_rows(j, part=None):
        if part is None:
            return pl.ds(pl.multiple_of(j * tk, tk), tk)
        return pl.ds(pl.multiple_of(j * tk + part * half, half), half)

    def logits(j, buf, n, diagonal):
        k = k_ref[0, key_rows(j), (n // 2) * LANES:(n // 2 + 1) * LANES]
        if diagonal:
            buf[n, 0:half, :] = _dot_nt(k[0:half], qc_ref[n])
            buf[n, half:tk, half:tq] = _dot_nt(k[half:tk], qc_ref[n, half:tq, :])
        else:
            buf[n] = _dot_nt(k, qc_ref[n])

    def attend(j, buf, n, diagonal):
        vt = vt_ref.at[n // 2]
        if diagonal:
            _softmax_step(jnp.where(visible, buf[n, 0:half, :], NEG_INF), vt[:, key_rows(j, 0)], m_ref, acc_ref, n)
            _softmax_step(jnp.where(visible[:, 0:tq - half], buf[n, half:tk, half:tq], NEG_INF),
                          vt[:, key_rows(j, 1)], m_ref, acc_ref, n, slice(half, tq))
        else:
            _softmax_step(buf[n], vt[:, key_rows(j)], m_ref, acc_ref, n)

    _causal_steps(i, logits, attend, 2 * DIFF_HEADS, sa_ref, sb_ref)

    lam = (jnp.exp(jnp.sum(lq1_ref[...] * lk1_ref[...], axis=1, keepdims=True))
           - jnp.exp(jnp.sum(lq2_ref[...] * lk2_ref[...], axis=1, keepdims=True)) + lambda_init)
    for h in range(DIFF_HEADS):
        a1, a2 = acc_ref[2 * h], acc_ref[2 * h + 1]
        o_t = a1[0:DIFF_V] / a1[DIFF_V:DIFF_V + 1] - lam * (a2[0:DIFF_V] / a2[DIFF_V:DIFF_V + 1])
        out = _rms(o_t.T, sub_ref[...]) * (1.0 - lambda_init)
        o_ref[0, :, h * LANES:(h + 1) * LANES] = out.astype(o_ref.dtype)


def diff_attention(proj, lq1, lk1, lq2, lk2, subln, lambda_init, cast=(), *, tq=ATT_Q_TILE, tk=ATT_K_TILE):
    B, S, _ = proj.shape
    assert tq == tk
    width = DIFF_HEADS * DIFF_V
    rows = DIFF_V + BF16_ROWS
    specs, operands = zip(*map(_whole, (lq1, lk1, lq2, lk2, subln)))
    grid = (B, S // tq)
    cast_specs, cast_views, cast_shapes = _cast_blocks(cast, grid)
    out, *cast_out = pl.pallas_call(
        _with_casts(functools.partial(_diff_kernel, tq=tq, tk=tk, lambda_init=lambda_init), 8, len(cast)),
        grid=grid,
        in_specs=[pl.BlockSpec((1, tq, width), lambda b, i: (b, i, 3)),
                  pl.BlockSpec((1, S, width), lambda b, i: (b, 0, 4)),
                  pl.BlockSpec((1, S, width), lambda b, i: (b, 0, 5)),
                  *specs, *cast_specs],
        out_specs=[pl.BlockSpec((1, tq, width), lambda b, i: (b, i, 0)), *cast_specs],
        out_shape=[jax.ShapeDtypeStruct((B, S, width), BF16), *cast_shapes],
        scratch_shapes=[pltpu.VMEM((DIFF_HEADS, rows, S), BF16),
                        pltpu.VMEM((2 * DIFF_HEADS, tq, LANES), BF16),
                        pltpu.VMEM((2 * DIFF_HEADS, rows, tq), F32),
                        pltpu.VMEM((2 * DIFF_HEADS, 1, tq), F32),
                        pltpu.VMEM((2 * DIFF_HEADS, tk, tq), F32),
                        pltpu.VMEM((2 * DIFF_HEADS, tk, tq), F32)],
        compiler_params=_params("parallel", "arbitrary"),
        name="diff_attention",
    )(proj, proj, proj, *operands, *cast_views)
    return out, [c.reshape(a.shape) for c, a in zip(cast_out, cast)]


def _mlstm_kernel(qk_ref, v_ref, sg_ref, gr_ref, hn_ref, o_ref, ct_ref, m_ref, vt_ref, *, L, nb):
    c = pl.program_id(1)

    @pl.when(c == 0)
    def _():
        ct_ref[...] = jnp.zeros_like(ct_ref)
        m_ref[...] = jnp.zeros_like(m_ref)
        for n in range(nb * ML_HEADS):
            vt_ref[n, ML_V:ML_V + BF16_ROWS, :] = jnp.ones((BF16_ROWS, L), BF16)

    src = lax.broadcasted_iota(jnp.int32, (L, L), 0)
    dst = lax.broadcasted_iota(jnp.int32, (L, L), 1)
    causal = src <= dst
    eye = jnp.where(src == dst, 1.0, 0.0).astype(BF16)
    k0 = ML_HEADS * ML_QK
    for b in range(nb):
        gr = gr_ref[b]
        gc = jnp.concatenate([gr, jnp.zeros((LANES - gr.shape[0], L), F32)], axis=0).T
        for h in range(ML_HEADS):
            n = b * ML_HEADS + h
            q = qk_ref[b, :, h * ML_QK:(h + 1) * ML_QK]
            k = qk_ref[b, :, k0 + h * ML_QK:k0 + (h + 1) * ML_QK]
            vt_ref[n, 0:ML_V, :] = _dot_nt(eye, v_ref[b, :, h * ML_V:(h + 1) * ML_V]).astype(BF16)
            v_t = vt_ref[n]
            r_col = gc[:, h:h + 1] - gc[:, ML_HEADS + h:ML_HEADS + h + 1]
            b_row = gr[ML_HEADS + h:ML_HEADS + h + 1, :]
            r_row = gr[h:h + 1, :] - b_row
            g = b_row[:, L - 1:L]
            ct = ct_ref[n]
            m = m_ref[n][:, 0:1]

            dm = jnp.where(causal, r_col, NEG_INF)
            mt = jnp.maximum(m, jnp.max(dm, axis=0, keepdims=True))
            s_t = _dot_nt(k, q) * jnp.exp2(dm - mt)
            both = jnp.exp2(m - mt) * _dot_nt(ct.astype(BF16), q) + _dot(v_t, s_t.astype(BF16))
            den = both[ML_V:ML_V + 1]
            hh = both[0:ML_V] / jnp.maximum(jnp.abs(den), jnp.exp2(-(b_row + mt)))
            hh = hh * lax.rsqrt(jnp.mean(hh * hh, axis=0, keepdims=True) + RMS_EPS)

            m_next = jnp.maximum(m, jnp.max(r_row, axis=1, keepdims=True))
            kw = (k.astype(F32) * jnp.exp2(r_col - m_next)).astype(BF16)
            ct_ref[n] = jnp.exp2(m - m_next) * ct + _dot(v_t, kw)
            m_ref[n] = jnp.broadcast_to(g + m_next, (1, LANES))

            vs = slice(h * ML_V, (h + 1) * ML_V)
            o_ref[b, :, vs] = (hh.T * hn_ref[:, vs] * sg_ref[b, :, vs].astype(F32)).astype(o_ref.dtype)


def mlstm(proj, g_row, head_norm, *, L=ML_CHUNK, nb=ML_SEQS):
    B, S, _ = proj.shape
    W = D_MODEL
    assert L == ML_V, "one identity matrix serves the v transposes"
    norm_spec, head_norm = _whole(head_norm)
    return pl.pallas_call(
        functools.partial(_mlstm_kernel, L=L, nb=nb),
        grid=(B // nb, S // L),
        in_specs=[pl.BlockSpec((nb, L, W), lambda b, c: (b, c, 0)),
                  pl.BlockSpec((nb, L, W), lambda b, c: (b, c, 1)),
                  pl.BlockSpec((nb, L, W), lambda b, c: (b, c, 2)),
                  pl.BlockSpec((nb, 2 * ML_HEADS, L), lambda b, c: (b, 0, c)),
                  norm_spec],
        out_specs=pl.BlockSpec((nb, L, W), lambda b, c: (b, c, 0)),
        out_shape=jax.ShapeDtypeStruct((B, S, W), BF16),
        scratch_shapes=[pltpu.VMEM((nb * ML_HEADS, ML_V + BF16_ROWS, ML_QK), F32),
                        pltpu.VMEM((nb * ML_HEADS, 1, LANES), F32),
                        pltpu.VMEM((nb * ML_HEADS, ML_V + BF16_ROWS, L), BF16)],
        compiler_params=_params("parallel", "arbitrary"),
        name="mlstm",
    )(proj, proj, proj, g_row, head_norm)


def _xattn_kernel(*refs, n_in):
    x_ref, g_ref, wq_ref, mem_ref, gm_ref, wkv_ref, wo_ref, o_ref, kv_ref = refs[2 * n_in:]

    @pl.when(pl.program_id(1) == 0)
    def _():
        memn = _rms(mem_ref[0], gm_ref[...]).astype(BF16)
        for c0 in range(0, kv_ref.shape[1], PROJ_COLS):
            kv_ref[:, c0:c0 + PROJ_COLS] = _dot(memn, wkv_ref[:, c0:c0 + PROJ_COLS]).astype(BF16)

    x = x_ref[0]
    for a_ref, w_ref in zip(refs[:n_in], refs[n_in:2 * n_in]):
        x = x + _dot(a_ref[0], w_ref[...])
    xn = _rms(x, g_ref[...]).astype(BF16)
    q = (_dot(xn, wq_ref[...]) * (X_DIM ** -0.5)).astype(BF16)
    heads = []
    for h in range(X_HEADS):
        cols = slice(h * X_DIM, (h + 1) * X_DIM)
        s = _dot_nt(q[:, cols], kv_ref[:, cols])
        p = jnp.exp(s - jnp.max(s, axis=1, keepdims=True))
        l = jnp.sum(p, axis=1, keepdims=True)
        v = kv_ref[:, D_MODEL + h * X_DIM:D_MODEL + (h + 1) * X_DIM]
        heads.append((_dot(p.astype(BF16), v) / l).astype(BF16))
    o_ref[0] = x + _dot(jnp.concatenate(heads, axis=1), wo_ref[...])


def mix_out_cross_attention(acts, w_mix, x, gain, wq, mem, mem_gain, wkv, wo, *, tq=X_TILE):
    B, S, D = x.shape
    M = mem.shape[1]
    (gain_spec, gain), (wq_spec, wq), (gm_spec, mem_gain), (wkv_spec, wkv), (wo_spec, wo) = map(
        _whole, (gain, wq, mem_gain, wkv, wo))
    return pl.pallas_call(
        functools.partial(_xattn_kernel, n_in=len(acts)),
        grid=(B, S // tq),
        in_specs=([pl.BlockSpec((1, tq, a.shape[2]), lambda b, i: (b, i, 0)) for a in acts]
                  + [w[0] for w in w_mix]
                  + [pl.BlockSpec((1, tq, D), lambda b, i: (b, i, 0)),
                     gain_spec, wq_spec,
                     pl.BlockSpec((1, M, D), lambda b, i: (b, 0, 0)),
                     gm_spec, wkv_spec, wo_spec]),
        out_specs=pl.BlockSpec((1, tq, D), lambda b, i: (b, i, 0)),
        out_shape=jax.ShapeDtypeStruct((B, S, D), F32),
        scratch_shapes=[pltpu.VMEM((M, 2 * D), BF16)],
        compiler_params=_params("parallel", "arbitrary"),
        name="mix_out_cross_attention",
    )(*acts, *[w[1] for w in w_mix], x, gain, wq, mem, mem_gain, wkv, wo)


def _gelu_tanh(x):
    k = -2.0 * math.sqrt(2.0 / math.pi) * math.log2(math.e)
    return x / (1.0 + jnp.exp2(x * (k * 0.044715 * (x * x) + k)))


def _ffn_kernel(x_ref, halo_ref, g_ref, wup_ref, cw_ref, cb_ref, wd_ref, fg_ref, o_ref, up_ref, act_ref, nat_ref,
                *, tm, tiles_per_seq, final_norm):
    i = pl.program_id(0)
    halo = _rms(halo_ref[...], g_ref[...])
    xe = jnp.concatenate([jnp.where(i % tiles_per_seq == 0, 0.0, halo).astype(BF16),
                          _rms(x_ref[...], g_ref[...]).astype(BF16)], axis=0)

    n_chunks = D_FF // FFN_CHUNK
    slabs = FFN_CHUNK // LANES
    rows = tm // ROW_PHASES

    def up_project(c):
        for half in range(2):
            col0 = half * D_FF + c * FFN_CHUNK
            up = _dot(xe, wup_ref[:, col0:col0 + FFN_CHUNK])
            for s in range(slabs):
                up_ref[c % 2, half, s] = up[:, s * LANES:(s + 1) * LANES]

    def conv(c, half, s, phase):
        col0 = half * D_FF + c * FFN_CHUNK + s * LANES
        cw = cw_ref[:, col0:col0 + LANES]
        out = cb_ref[:, col0:col0 + LANES]
        for tap in range(FFN_CONV):
            first = BF16_ROWS + phase - (FFN_CONV - 1 - tap)
            out = out + cw[tap:tap + 1] * up_ref[c % 2, half, s, pl.ds(first, rows, stride=ROW_PHASES), :]
        return out

    acc = None
    piece_start = 0
    up_project(0)
    for c in range(n_chunks):
        if c + 1 < n_chunks:
            up_project(c + 1)
        for phase in range(ROW_PHASES):
            for s in range(slabs):
                act = _gelu_tanh(conv(c, 0, s, phase)) * conv(c, 1, s, phase)
                act_ref[phase * rows:(phase + 1) * rows,
                        c * FFN_CHUNK + s * LANES:c * FFN_CHUNK + (s + 1) * LANES] = act.astype(BF16)
        if (c + 1) % DOWN_CHUNKS == 0 or c + 1 == n_chunks:
            piece = slice(piece_start * FFN_CHUNK, (c + 1) * FFN_CHUNK)
            part = _dot(act_ref[:, piece], wd_ref[piece, :])
            acc = part if acc is None else acc + part
            piece_start = c + 1

    for phase in range(ROW_PHASES):
        for s in range(D_MODEL // LANES):
            nat_ref[s, pl.ds(phase, rows, stride=ROW_PHASES), :] = acc[phase * rows:(phase + 1) * rows,
                                                                       s * LANES:(s + 1) * LANES]
    y = x_ref[...] + jnp.concatenate([nat_ref[s] for s in range(D_MODEL // LANES)], axis=1)
    o_ref[...] = _rms(y, fg_ref[...]) if final_norm else y


def conv_ffn(x, gain, w_up, conv_w, conv_b, w_down, seq_len, final_gain=None, *, tm=ROW_TILE):
    T, D = x.shape
    halo_blocks = tm // BF16_ROWS
    final_norm = final_gain is not None
    specs, operands = zip(*map(_whole, (gain, w_up, conv_w, conv_b, w_down, final_gain if final_norm else gain)))
    return pl.pallas_call(
        functools.partial(_ffn_kernel, tm=tm, tiles_per_seq=seq_len // tm, final_norm=final_norm),
        grid=(T // tm,),
        in_specs=[pl.BlockSpec((tm, D), lambda i: (i, 0)),
                  pl.BlockSpec((BF16_ROWS, D), lambda i: (jnp.maximum(i * halo_blocks - 1, 0), 0)),
                  *specs],
        out_specs=pl.BlockSpec((tm, D), lambda i: (i, 0)),
        out_shape=jax.ShapeDtypeStruct((T, D), F32),
        scratch_shapes=[pltpu.VMEM((2, 2, FFN_CHUNK // LANES, tm + BF16_ROWS, LANES), F32),
                        pltpu.VMEM((tm, D_FF), BF16),
                        pltpu.VMEM((D // LANES, tm, LANES), F32)],
        compiler_params=_params("parallel"),
        name="conv_ffn",
    )(x, x, *operands)


def _gate_bias_rows(bias):
    return jnp.broadcast_to(jnp.pad(bias, (0, BF16_ROWS - bias.shape[0]))[:, None], (BF16_ROWS, LANES))


def _fox_diff_mixer(x, B, S, gain, w_in, fox_bf, lq1, lk1, lq2, lk2, subln, lambda_init, cast_fox, cast_diff):
    proj, fq, fk = fox_projection(x, gain, w_in, _gate_bias_rows(fox_bf), S)
    proj = proj.reshape(B, S, -1)
    fox, cast_fox = fox_attention(proj, fq.reshape(B, S, LANES), fk.reshape(B, S, LANES), cast_fox)
    dif, cast_diff = diff_attention(proj, lq1, lk1, lq2, lk2, subln, lambda_init, cast_diff)
    return [fox, dif], cast_fox, cast_diff


def _row_block(w_out, j, r, rows):
    return pl.BlockSpec((None, rows, D_MODEL), lambda *_: (j, r, 0), pipeline_mode=pl.Buffered(1)), w_out


def _mlstm_mixer(x, B, S, j, gain, w_in, conv_qk, b_i, b_f, head_norm, w_out):
    proj, g_row = mlstm_projection(x, gain, w_in, _gate_bias_rows(jnp.concatenate([b_i, b_f])), conv_qk, S)
    h = mlstm(proj.reshape(B, S, -1), g_row, head_norm)
    return [h], [_whole((w_out, j))]


def kernel(x, mem, mix_norm, xattn_norm, mem_norm, ffn_norm, attn_w_in, attn_fox_bf, diff_lq1, diff_lk1, diff_lq2, diff_lk2, diff_subln, attn_w_out, mlstm_w_in, mlstm_conv_qk, mlstm_b_i, mlstm_b_f, mlstm_head_norm, mlstm_w_out, xattn_wq, xattn_wkv, xattn_wo, ffn_w_up, ffn_conv_w, ffn_conv_b, ffn_w_down, final_norm):
    B, S, D = x.shape
    depth = mix_norm.shape[0]
    x = x.reshape(B * S, D)
    rows = lambda a: a.reshape(a.shape[0], 1, -1)
    mix_norm, xattn_norm, mem_norm, ffn_norm = map(rows, (mix_norm, xattn_norm, mem_norm, ffn_norm))
    diff_lq1, diff_lk1, diff_lq2, diff_lk2, diff_subln = map(rows, (diff_lq1, diff_lk1, diff_lq2, diff_lk2, diff_subln))
    mlstm_head_norm, ffn_conv_b = rows(mlstm_head_norm), rows(ffn_conv_b)
    attn_w_in, mlstm_w_in = jnp.swapaxes(attn_w_in, 1, 2), jnp.swapaxes(mlstm_w_in, 1, 2)
    for layer in range(depth):
        j = layer // 2
        if layer % 2 == 0:
            lambda_init = 0.8 - 0.6 * math.exp(-0.3 * layer)
            cast_fox = (ffn_w_up, ffn_w_down) if layer == 0 else ()
            cast_diff = (attn_w_out, mlstm_w_out, xattn_wq, xattn_wkv, xattn_wo) if layer == 0 else ()
            mixed, cast_fox, cast_diff = _fox_diff_mixer(
                x, B, S, (mix_norm, layer), (attn_w_in, j), attn_fox_bf[j], (diff_lq1, j), (diff_lk1, j),
                (diff_lq2, j), (diff_lk2, j), (diff_subln, j), lambda_init, cast_fox, cast_diff)
            if layer == 0:
                ffn_w_up, ffn_w_down = cast_fox
                attn_w_out, mlstm_w_out, xattn_wq, xattn_wkv, xattn_wo = cast_diff
            w_mix = [_row_block(attn_w_out, j, r, FOX_HEADS * FOX_DIM) for r in range(2)]
        else:
            mixed, w_mix = _mlstm_mixer(x, B, S, j, (mix_norm, layer), (mlstm_w_in, j), (mlstm_conv_qk, j),
                                        mlstm_b_i[j], mlstm_b_f[j], (mlstm_head_norm, j), mlstm_w_out)
        x = mix_out_cross_attention(mixed, w_mix, x.reshape(B, S, D), (xattn_norm, layer), (xattn_wq, layer), mem,
                                    (mem_norm, layer), (xattn_wkv, layer), (xattn_wo, layer)).reshape(B * S, D)
        x = conv_ffn(x, (ffn_norm, layer), (ffn_w_up, layer), (ffn_conv_w, layer), (ffn_conv_b, layer),
                     (ffn_w_down, layer), S, final_norm.reshape(1, D) if layer == depth - 1 else None)
    return x.reshape(B, S, D)
```

```python
import functools
import math

import jax
import jax.numpy as jnp
from jax import lax
from jax.experimental import pallas as pl
from jax.experimental.pallas import tpu as pltpu

F32 = jnp.float32
BF16 = jnp.bfloat16

D_MODEL = 1024
RMS_EPS = 1e-6
NEG_INF = -1e30
CHUNK = 64
FOX_HEADS, FOX_DIM = 8, 64
DIFF_HEADS, DIFF_QK, DIFF_V = 4, 64, 128
ML_HEADS, ML_QK, ML_V, ML_CONV = 4, 128, 256, 4
X_HEADS, X_DIM = 4, 256
D_FF = 2816
FFN_CONV = 3
LANES = 128
BF16_ROWS = 16
VMEM_LIMIT = 56 * 1024 * 1024

ROW_TILE = 512
ATT_Q_TILE = 512
ATT_K_TILE = 512
LOG2_E = math.log2(math.e)
ML_CHUNK = 256
ML_SEQS = 2
X_TILE = 1024
PROJ_COLS = 1024
FFN_CHUNK = 256
ROW_PHASES = 4
DOWN_CHUNKS = 4
TRANSPOSE_COLS = 512
F_PIECES = 3
FOX_BIAS_K_LANE = 0
FOX_BIAS_Q_LANE = 32


def _params(*sem):
    return pltpu.CompilerParams(dimension_semantics=sem, vmem_limit_bytes=VMEM_LIMIT)


def _resident(shape):
    return pl.BlockSpec(shape, lambda *_: (0,) * len(shape), pipeline_mode=pl.Buffered(1))


def _whole(a):
    if isinstance(a, tuple):
        arr, layer = a
        tail = arr.shape[1:]
        spec = pl.BlockSpec((None,) + tail, lambda *_: (layer,) + (0,) * len(tail), pipeline_mode=pl.Buffered(1))
        return spec, arr
    return _resident(a.shape), a


def _rows(a):
    return (a[0] if isinstance(a, tuple) else a).shape[-2]


def _rms(x, gain):
    return x * lax.rsqrt(jnp.mean(x * x, axis=-1, keepdims=True) + RMS_EPS) * gain


def _sigmoid(x):
    return 1.0 / (1.0 + jnp.exp(-x))


def _dot(a, b):
    return jnp.dot(a, b, preferred_element_type=F32)


def _dot_nt(a, b):
    return lax.dot_general(a, b, (((1,), (1,)), ((), ())), preferred_element_type=F32)


def _split3(x):
    hi = x.astype(BF16)
    r1 = x - hi.astype(F32)
    mid = r1.astype(BF16)
    return hi, mid, (r1 - mid.astype(F32)).astype(BF16)


def _dot_select(x, sel):
    hi, mid, lo = _split3(x)
    return _dot(hi, sel) + _dot(mid, sel) + _dot(lo, sel)


def _prefix_matrix(n, block):
    r = lax.broadcasted_iota(jnp.int32, (n, n), 0)
    c = lax.broadcasted_iota(jnp.int32, (n, n), 1)
    return jnp.where((r <= c) & (r // block == c // block), 1.0, 0.0).astype(BF16)


def _tile_lanes(a, n):
    return jnp.concatenate([a] * (n // LANES), axis=1)


def _log_sigmoid(x):
    return jnp.minimum(x, 0.0) - jnp.log1p(jnp.exp(-jnp.abs(x)))


def _prepare_weights(wt_ref, wb_ref, wg_ref, pieces, gate_row, n_gates):
    dst = 0
    for src, height, scale in pieces:
        for r0 in range(0, height, PROJ_COLS):
            n = min(PROJ_COLS, height - r0)
            wb_ref[dst + r0:dst + r0 + n, :] = (wt_ref[src + r0:src + r0 + n, :] * scale).astype(BF16)
        dst += height
    gates = wt_ref[gate_row:gate_row + n_gates, :]
    wg_ref[...] = jnp.concatenate([gates, jnp.zeros((BF16_ROWS - n_gates, gates.shape[1]), F32)],
                                  axis=0).astype(BF16)


def _fox_proj_kernel(x_ref, g_ref, w_ref, bf_ref, o_ref, fq_ref, fk_ref, carry_ref, wb_ref, wg_ref,
                     *, tm, tn, tiles_per_seq):
    i = pl.program_id(0)

    @pl.when(i == 0)
    def _():
        fw = FOX_HEADS * FOX_DIM
        dq0 = 3 * fw + FOX_HEADS
        dqw = DIFF_HEADS * 2 * DIFF_QK
        _prepare_weights(w_ref, wb_ref, wg_ref,
                         ((0, fw, FOX_DIM ** -0.5 * LOG2_E), (fw, 2 * fw, 1.0),
                          (dq0, dqw, DIFF_QK ** -0.5 * LOG2_E), (dq0 + dqw, w_ref.shape[0] - dq0 - dqw, 1.0)),
                         3 * fw, FOX_HEADS)

    @pl.when(i % tiles_per_seq == 0)
    def _():
        carry_ref[...] = jnp.zeros_like(carry_ref)

    xn = _rms(x_ref[...], g_ref[...]).astype(BF16)

    def project(part):
        width = o_ref.shape[1] // 3
        for c0 in range(part * width, (part + 1) * width, tn):
            o_ref[:, c0:c0 + tn] = _dot_nt(xn, wb_ref[c0:c0 + tn, :]).astype(o_ref.dtype)

    gates_t = _dot_nt(wg_ref[...], xn)
    project(0)
    log_f = _log_sigmoid(gates_t + _tile_lanes(bf_ref[...], tm)) * LOG2_E
    f_t = _tile_lanes(carry_ref[...], tm) + _dot_select(log_f, _prefix_matrix(tm, tm))
    carry_ref[...] = jnp.broadcast_to(f_t[:, tm - 1:tm], carry_ref.shape)
    f = jnp.concatenate([f_t, jnp.zeros((LANES - BF16_ROWS, tm), F32)], axis=0).T
    project(1)

    src = lax.broadcasted_iota(jnp.int32, (LANES, LANES), 0)
    dst = lax.broadcasted_iota(jnp.int32, (LANES, LANES), 1)
    lane = lax.broadcasted_iota(jnp.int32, (1, LANES), 1)
    n_bias = F_PIECES * FOX_HEADS
    ones_q = jnp.where((lane >= FOX_BIAS_K_LANE) & (lane < FOX_BIAS_K_LANE + n_bias), 1.0, 0.0)
    ones_k = jnp.where((lane >= FOX_BIAS_Q_LANE) & (lane < FOX_BIAS_Q_LANE + n_bias), 1.0, 0.0)
    both = jnp.concatenate([ones_q, ones_k], axis=1)
    for c, piece in enumerate(_split3(f)):
        head = src < FOX_HEADS
        to_q = jnp.where(head & (dst == FOX_BIAS_Q_LANE + F_PIECES * src + c), 1.0, 0.0)
        to_k = jnp.where(head & (dst == FOX_BIAS_K_LANE + F_PIECES * src + c), -1.0, 0.0)
        both = both + _dot(piece, jnp.concatenate([to_q, to_k], axis=1).astype(BF16))
    fq_ref[...] = both[:, 0:LANES].astype(BF16)
    fk_ref[...] = both[:, LANES:2 * LANES].astype(BF16)
    project(2)


def fox_projection(x, gain, w, gate_bias, seq_len, *, tm=ROW_TILE, tn=PROJ_COLS):
    T, D = x.shape
    N = _rows(w) - FOX_HEADS
    row_block = lambda n: pl.BlockSpec((tm, n), lambda i: (i, 0))
    specs, operands = zip(*map(_whole, (gain, w, gate_bias)))
    return pl.pallas_call(
        functools.partial(_fox_proj_kernel, tm=tm, tn=tn, tiles_per_seq=seq_len // tm),
        grid=(T // tm,),
        in_specs=[row_block(D), *specs],
        out_specs=[row_block(N), row_block(LANES), row_block(LANES)],
        out_shape=[jax.ShapeDtypeStruct((T, N), BF16), jax.ShapeDtypeStruct((T, LANES), BF16),
                   jax.ShapeDtypeStruct((T, LANES), BF16)],
        scratch_shapes=[pltpu.VMEM((BF16_ROWS, LANES), F32), pltpu.VMEM((N, D), BF16),
                        pltpu.VMEM((BF16_ROWS, D), BF16)],
        compiler_params=_params("arbitrary"),
        name="fox_projection",
    )(x, *operands)


def _mlstm_proj_kernel(x_ref, halo_ref, g_ref, w_raw_ref, gb_ref, cw_ref, o_ref, gr_ref, up_ref, nat_ref, w_ref,
                       wg_ref, *, tm, tiles_per_seq):
    i = pl.program_id(0)

    @pl.when(i == 0)
    def _():
        g0 = 2 * ML_HEADS * ML_QK + ML_HEADS * ML_V
        og_col = g0 + 2 * ML_HEADS
        _prepare_weights(w_raw_ref, w_ref, wg_ref, ((0, g0, 1.0), (og_col, w_raw_ref.shape[0] - og_col, 1.0)),
                         g0, 2 * ML_HEADS)

    halo = _rms(halo_ref[...], g_ref[...])
    xe = jnp.concatenate([jnp.where(i % tiles_per_seq == 0, 0.0, halo).astype(BF16),
                          _rms(x_ref[...], g_ref[...]).astype(BF16)], axis=0)
    xn = xe[BF16_ROWS:]

    qk_slabs = 2 * ML_HEADS * ML_QK // LANES
    rows = tm // ROW_PHASES
    v0 = qk_slabs * LANES
    og0 = v0 + ML_HEADS * ML_V
    for s0 in range(0, qk_slabs, 2):
        up = _dot_nt(xe, w_ref[s0 * LANES:(s0 + 2) * LANES, :])
        up_ref[s0] = up[:, 0:LANES]
        up_ref[s0 + 1] = up[:, LANES:2 * LANES]
    pre_t = _dot_nt(wg_ref[...], xn) + _tile_lanes(gb_ref[...], tm)
    o_ref[:, v0:og0] = _dot_nt(xn, w_ref[v0:og0, :]).astype(o_ref.dtype)
    for s in range(qk_slabs):
        cw = cw_ref[:, s * LANES:(s + 1) * LANES]
        for phase in range(ROW_PHASES):
            y = None
            for tap in range(ML_CONV):
                first = BF16_ROWS + phase - (ML_CONV - 1 - tap)
                term = cw[tap:tap + 1] * up_ref[s, pl.ds(first, rows, stride=ROW_PHASES), :]
                y = term if y is None else y + term
            y = y * _sigmoid(y)
            if s < qk_slabs // 2:
                y = y * (ML_QK ** -0.5)
            nat_ref[s, pl.ds(phase, rows, stride=ROW_PHASES), :] = y
        o_ref[:, s * LANES:(s + 1) * LANES] = nat_ref[s].astype(o_ref.dtype)

    og = _dot_nt(xn, w_ref[og0:, :])
    row = lax.broadcasted_iota(jnp.int32, (BF16_ROWS, 1), 0)
    val_t = jnp.where(row < ML_HEADS, pre_t, _log_sigmoid(pre_t)) * LOG2_E
    g_t = jnp.where(row < ML_HEADS, val_t, _dot_select(val_t, _prefix_matrix(tm, ML_CHUNK)))
    gr_ref[0] = g_t[0:2 * ML_HEADS]
    o_ref[:, og0:] = _sigmoid(og).astype(o_ref.dtype)


def mlstm_projection(x, gain, w, gate_bias, conv_w, seq_len, *, tm=ROW_TILE):
    T, D = x.shape
    N = _rows(w) - 2 * ML_HEADS
    tiles_per_seq = seq_len // tm
    halo_blocks = tm // BF16_ROWS
    row_block = lambda n: pl.BlockSpec((tm, n), lambda i: (i, 0))
    specs, operands = zip(*map(_whole, (gain, w, gate_bias, conv_w)))
    return pl.pallas_call(
        functools.partial(_mlstm_proj_kernel, tm=tm, tiles_per_seq=tiles_per_seq),
        grid=(T // tm,),
        in_specs=[row_block(D),
                  pl.BlockSpec((BF16_ROWS, D), lambda i: (jnp.maximum(i * halo_blocks - 1, 0), 0)),
                  *specs],
        out_specs=[row_block(N),
                   pl.BlockSpec((1, 2 * ML_HEADS, tm), lambda i: (i // tiles_per_seq, 0, i % tiles_per_seq))],
        out_shape=[jax.ShapeDtypeStruct((T, N), BF16),
                   jax.ShapeDtypeStruct((T // seq_len, 2 * ML_HEADS, seq_len), F32)],
        scratch_shapes=[pltpu.VMEM((2 * ML_HEADS * ML_QK // LANES, tm + BF16_ROWS, LANES), F32),
                        pltpu.VMEM((2 * ML_HEADS * ML_QK // LANES, tm, LANES), F32),
                        pltpu.VMEM((N, D), BF16), pltpu.VMEM((BF16_ROWS, D), BF16)],
        compiler_params=_params("arbitrary"),
        name="mlstm_projection",
    )(x, x, *operands)


def _cast_blocks(arrays, grid):
    steps = grid[0] * grid[1]
    views = [a.reshape(-1, a.shape[-1]) for a in arrays]
    assert all(v.shape[0] % (steps * BF16_ROWS) == 0 for v in views)
    specs = [pl.BlockSpec((v.shape[0] // steps, v.shape[1]), lambda b, i: (b * grid[1] + i, 0)) for v in views]
    return specs, views, [jax.ShapeDtypeStruct(v.shape, BF16) for v in views]


def _with_casts(kernel, n_in, n_cast):
    def wrapped(*refs):
        cast_in = refs[n_in:n_in + n_cast]
        cast_out = refs[n_in + n_cast + 1:n_in + 2 * n_cast + 1]
        for src, dst in zip(cast_in, cast_out):
            dst[...] = src[...].astype(BF16)
        kernel(*refs[:n_in], refs[n_in + n_cast], *refs[n_in + 2 * n_cast + 1:])
    return wrapped


def _build_vt(v_ref, vt_ref, n_heads, rows):
    S = v_ref.shape[1]
    r = lax.broadcasted_iota(jnp.int32, (LANES, LANES), 0)
    c = lax.broadcasted_iota(jnp.int32, (LANES, LANES), 1)
    eye = jnp.where(r == c, 1.0, 0.0).astype(BF16)
    per_group = LANES // rows
    for g in range(n_heads // per_group):
        for c0 in range(0, S, TRANSPOSE_COLS):
            cs = slice(c0, c0 + TRANSPOSE_COLS)
            vt = _dot_nt(eye, v_ref[0, cs, g * LANES:(g + 1) * LANES]).astype(BF16)
            for k in range(per_group):
                vt_ref[g * per_group + k, 0:rows, cs] = vt[k * rows:(k + 1) * rows]
    for h in range(n_heads):
        vt_ref[h, rows:rows + BF16_ROWS, :] = jnp.ones((BF16_ROWS, S), BF16)


def _softmax_step(s_t, vt, m_ref, acc_ref, idx, queries=slice(None)):
    m_prev = m_ref[idx, :, queries]
    m_new = jnp.maximum(m_prev, jnp.max(s_t, axis=0, keepdims=True))
    p = jnp.exp2(s_t - m_new).astype(BF16)
    acc_ref[idx, :, queries] = jnp.exp2(m_prev - m_new) * acc_ref[idx, :, queries] + _dot(vt, p)
    m_ref[idx, :, queries] = m_new


def _causal_steps(i, logits, attend, n_streams, buf_a, buf_b):
    def phase(j, src, dst, next_is_diagonal=False):
        for n in range(n_streams):
            logits(j + 1, dst, n, next_is_diagonal)
            attend(j, src, n, False)

    for n in range(n_streams):
        logits(0, buf_a, n, False)

    def body(jj, carry):
        phase(2 * jj, buf_a, buf_b)
        phase(2 * jj + 1, buf_b, buf_a)
        return carry

    lax.fori_loop(0, lax.shift_right_logical(i, 1), body, 0)
    odd = lax.rem(i, 2) == 1

    @pl.when(odd)
    def _():
        phase(i - 1, buf_a, buf_b, True)
        for n in range(n_streams):
            attend(i, buf_b, n, True)

    @pl.when(jnp.logical_not(odd))
    def _():
        for n in range(n_streams):
            attend(i, buf_a, n, True)


def _fox_kernel(q_ref, k_ref, v_ref, fq_ref, fk_ref, o_ref, vt_ref, qc_ref, acc_ref, m_ref, sa_ref, sb_ref, *, tq, tk):
    i = pl.program_id(1)

    @pl.when(i == 0)
    def _():
        _build_vt(v_ref, vt_ref, FOX_HEADS, FOX_DIM)

    lane = lax.broadcasted_iota(jnp.int32, (1, LANES), 1)
    fq = fq_ref[0]
    for h in range(FOX_HEADS):
        pair, half = divmod(h, 2)
        q = q_ref[0, :, pair * LANES:(pair + 1) * LANES]
        in_head = (lane >= half * FOX_DIM) & (lane < (half + 1) * FOX_DIM)
        lo_k, lo_q = FOX_BIAS_K_LANE + F_PIECES * h, FOX_BIAS_Q_LANE + F_PIECES * h
        mine = ((lane >= lo_k) & (lane < lo_k + F_PIECES)) | ((lane >= lo_q) & (lane < lo_q + F_PIECES))
        qc_ref[h, :, 0:LANES] = jnp.where(in_head, q, jnp.zeros_like(q))
        qc_ref[h, :, LANES:2 * LANES] = jnp.where(mine, fq, jnp.zeros_like(fq))
    m_ref[...] = jnp.full(m_ref.shape, NEG_INF, F32)
    acc_ref[...] = jnp.zeros(acc_ref.shape, F32)

    half = tk // 2
    key = lax.broadcasted_iota(jnp.int32, (half, tq), 0)
    qry = lax.broadcasted_iota(jnp.int32, (half, tq), 1)
    causal = key <= qry

    def key_rows(j, part=None):
        if part is None:
            return pl.ds(pl.multiple_of(j * tk, tk), tk)
        return pl.ds(pl.multiple_of(j * tk + part * half, half), half)

    def logits(j, buf, h, diagonal):
        pair = h // 2
        kc = jnp.concatenate([k_ref[0, key_rows(j), pair * LANES:(pair + 1) * LANES], fk_ref[0, key_rows(j), :]],
                             axis=1)
        if diagonal:
            buf[h, 0:half, :] = _dot_nt(kc[0:half], qc_ref[h])
            buf[h, half:tk, half:tq] = _dot_nt(kc[half:tk], qc_ref[h, half:tq, :])
        else:
            buf[h] = _dot_nt(kc, qc_ref[h])

    def attend(j, buf, h, diagonal):
        if diagonal:
            _softmax_step(jnp.where(causal, buf[h, 0:half, :], NEG_INF), vt_ref[h, :, key_rows(j, 0)],
                          m_ref, acc_ref, h)
            _softmax_step(jnp.where(causal[:, 0:tq - half], buf[h, half:tk, half:tq], NEG_INF),
                          vt_ref[h, :, key_rows(j, 1)], m_ref, acc_ref, h, slice(half, tq))
        else:
            _softmax_step(buf[h], vt_ref[h, :, key_rows(j)], m_ref, acc_ref, h)

    _causal_steps(i, logits, attend, FOX_HEADS, sa_ref, sb_ref)

    for pair in range(FOX_HEADS // 2):
        halves = []
        for half in range(2):
            a = acc_ref[2 * pair + half]
            halves.append(a[0:FOX_DIM] / a[FOX_DIM:FOX_DIM + 1])
        o_ref[0, :, pair * LANES:(pair + 1) * LANES] = jnp.concatenate(halves, axis=0).T.astype(o_ref.dtype)


def fox_attention(proj, fq, fk, cast=(), *, tq=ATT_Q_TILE, tk=ATT_K_TILE):
    B, S, _ = proj.shape
    assert tq == tk
    width = FOX_HEADS * FOX_DIM
    rows = FOX_DIM + BF16_ROWS
    grid = (B, S // tq)
    cast_specs, cast_views, cast_shapes = _cast_blocks(cast, grid)
    out, *cast_out = pl.pallas_call(
        _with_casts(functools.partial(_fox_kernel, tq=tq, tk=tk), 5, len(cast)),
        grid=grid,
        in_specs=[pl.BlockSpec((1, tq, width), lambda b, i: (b, i, 0)),
                  pl.BlockSpec((1, S, width), lambda b, i: (b, 0, 1)),
                  pl.BlockSpec((1, S, width), lambda b, i: (b, 0, 2)),
                  pl.BlockSpec((1, tq, LANES), lambda b, i: (b, i, 0)),
                  pl.BlockSpec((1, S, LANES), lambda b, i: (b, 0, 0)),
                  *cast_specs],
        out_specs=[pl.BlockSpec((1, tq, width), lambda b, i: (b, i, 0)), *cast_specs],
        out_shape=[jax.ShapeDtypeStruct((B, S, width), BF16), *cast_shapes],
        scratch_shapes=[pltpu.VMEM((FOX_HEADS, rows, S), BF16),
                        pltpu.VMEM((FOX_HEADS, tq, 2 * LANES), BF16),
                        pltpu.VMEM((FOX_HEADS, rows, tq), F32),
                        pltpu.VMEM((FOX_HEADS, 1, tq), F32),
                        pltpu.VMEM((FOX_HEADS, tk, tq), F32),
                        pltpu.VMEM((FOX_HEADS, tk, tq), F32)],
        compiler_params=_params("parallel", "arbitrary"),
        name="fox_attention",
    )(proj, proj, proj, fq, fk, *cast_views)
    return out, [c.reshape(a.shape) for c, a in zip(cast_out, cast)]


def _diff_kernel(q_ref, k_ref, v_ref, lq1_ref, lk1_ref, lq2_ref, lk2_ref, sub_ref, o_ref, vt_ref, qc_ref, acc_ref,
                 m_ref, sa_ref, sb_ref, *, tq, tk, lambda_init):
    i = pl.program_id(1)

    @pl.when(i == 0)
    def _():
        _build_vt(v_ref, vt_ref, DIFF_HEADS, DIFF_V)

    lane = lax.broadcasted_iota(jnp.int32, (1, LANES), 1)
    for h in range(DIFF_HEADS):
        q = q_ref[0, :, h * LANES:(h + 1) * LANES]
        zero = jnp.zeros_like(q)
        qc_ref[2 * h] = jnp.where(lane < DIFF_QK, q, zero)
        qc_ref[2 * h + 1] = jnp.where(lane >= DIFF_QK, q, zero)
    m_ref[...] = jnp.full(m_ref.shape, NEG_INF, F32)
    acc_ref[...] = jnp.zeros(acc_ref.shape, F32)

    half = tk // 2
    key = lax.broadcasted_iota(jnp.int32, (half, tq), 0)
    qry = lax.broadcasted_iota(jnp.int32, (half, tq), 1)
    visible = key // CHUNK <= qry // CHUNK

    def key_rows(j, part=None):
        if part is None:
            return pl.ds(pl.multiple_of(j * tk, tk), tk)
        return pl.ds(pl.multiple_of(j * tk + part * half, half), half)

    def logits(j, buf, n, diagonal):
        k = k_ref[0, key_rows(j), (n // 2) * LANES:(n // 2 + 1) * LANES]
        if diagonal:
            buf[n, 0:half, :] = _dot_nt(k[0:half], qc_ref[n])
            buf[n, half:tk, half:tq] = _dot_nt(k[half:tk], qc_ref[n, half:tq, :])
        else:
            buf[n] = _dot_nt(k, qc_ref[n])

    def attend(j, buf, n, diagonal):
        vt = vt_ref.at[n // 2]
        if diagonal:
            _softmax_step(jnp.where(visible, buf[n, 0:half, :], NEG_INF), vt[:, key_rows(j, 0)], m_ref, acc_ref, n)
            _softmax_step(jnp.where(visible[:, 0:tq - half], buf[n, half:tk, half:tq], NEG_INF),
                          vt[:, key_rows(j, 1)], m_ref, acc_ref, n, slice(half, tq))
        else:
            _softmax_step(buf[n], vt[:, key_rows(j)], m_ref, acc_ref, n)

    _causal_steps(i, logits, attend, 2 * DIFF_HEADS, sa_ref, sb_ref)

    lam = (jnp.exp(jnp.sum(lq1_ref[...] * lk1_ref[...], axis=1, keepdims=True))
           - jnp.exp(jnp.sum(lq2_ref[...] * lk2_ref[...], axis=1, keepdims=True)) + lambda_init)
    for h in range(DIFF_HEADS):
        a1, a2 = acc_ref[2 * h], acc_ref[2 * h + 1]
        o_t = a1[0:DIFF_V] / a1[DIFF_V:DIFF_V + 1] - lam * (a2[0:DIFF_V] / a2[DIFF_V:DIFF_V + 1])
        out = _rms(o_t.T, sub_ref[...]) * (1.0 - lambda_init)
        o_ref[0, :, h * LANES:(h + 1) * LANES] = out.astype(o_ref.dtype)


def diff_attention(proj, lq1, lk1, lq2, lk2, subln, lambda_init, cast=(), *, tq=ATT_Q_TILE, tk=ATT_K_TILE):
    B, S, _ = proj.shape
    assert tq == tk
    width = DIFF_HEADS * DIFF_V
    rows = DIFF_V + BF16_ROWS
    specs, operands = zip(*map(_whole, (lq1, lk1, lq2, lk2, subln)))
    grid = (B, S // tq)
    cast_specs, cast_views, cast_shapes = _cast_blocks(cast, grid)
    out, *cast_out = pl.pallas_call(
        _with_casts(functools.partial(_diff_kernel, tq=tq, tk=tk, lambda_init=lambda_init), 8, len(cast)),
        grid=grid,
        in_specs=[pl.BlockSpec((1, tq, width), lambda b, i: (b, i, 3)),
                  pl.BlockSpec((1, S, width), lambda b, i: (b, 0, 4)),
                  pl.BlockSpec((1, S, width), lambda b, i: (b, 0, 5)),
                  *specs, *cast_specs],
        out_specs=[pl.BlockSpec((1, tq, width), lambda b, i: (b, i, 0)), *cast_specs],
        out_shape=[jax.ShapeDtypeStruct((B, S, width), BF16), *cast_shapes],
        scratch_shapes=[pltpu.VMEM((DIFF_HEADS, rows, S), BF16),
                        pltpu.VMEM((2 * DIFF_HEADS, tq, LANES), BF16),
                        pltpu.VMEM((2 * DIFF_HEADS, rows, tq), F32),
                        pltpu.VMEM((2 * DIFF_HEADS, 1, tq), F32),
                        pltpu.VMEM((2 * DIFF_HEADS, tk, tq), F32),
                        pltpu.VMEM((2 * DIFF_HEADS, tk, tq), F32)],
        compiler_params=_params("parallel", "arbitrary"),
        name="diff_attention",
    )(proj, proj, proj, *operands, *cast_views)
    return out, [c.reshape(a.shape) for c, a in zip(cast_out, cast)]


def _mlstm_kernel(qk_ref, v_ref, sg_ref, gr_ref, hn_ref, o_ref, ct_ref, m_ref, vt_ref, *, L, nb):
    c = pl.program_id(1)

    @pl.when(c == 0)
    def _():
        ct_ref[...] = jnp.zeros_like(ct_ref)
        m_ref[...] = jnp.zeros_like(m_ref)
        for n in range(nb * ML_HEADS):
            vt_ref[n, ML_V:ML_V + BF16_ROWS, :] = jnp.ones((BF16_ROWS, L), BF16)

    src = lax.broadcasted_iota(jnp.int32, (L, L), 0)
    dst = lax.broadcasted_iota(jnp.int32, (L, L), 1)
    causal = src <= dst
    eye = jnp.where(src == dst, 1.0, 0.0).astype(BF16)
    k0 = ML_HEADS * ML_QK
    for b in range(nb):
        gr = gr_ref[b]
        gc = jnp.concatenate([gr, jnp.zeros((LANES - gr.shape[0], L), F32)], axis=0).T
        for h in range(ML_HEADS):
            n = b * ML_HEADS + h
            q = qk_ref[b, :, h * ML_QK:(h + 1) * ML_QK]
            k = qk_ref[b, :, k0 + h * ML_QK:k0 + (h + 1) * ML_QK]
            vt_ref[n, 0:ML_V, :] = _dot_nt(eye, v_ref[b, :, h * ML_V:(h + 1) * ML_V]).astype(BF16)
            v_t = vt_ref[n]
            r_col = gc[:, h:h + 1] - gc[:, ML_HEADS + h:ML_HEADS + h + 1]
            b_row = gr[ML_HEADS + h:ML_HEADS + h + 1, :]
            r_row = gr[h:h + 1, :] - b_row
            g = b_row[:, L - 1:L]
            ct = ct_ref[n]
            m = m_ref[n][:, 0:1]

            dm = jnp.where(causal, r_col, NEG_INF)
            mt = jnp.maximum(m, jnp.max(dm, axis=0, keepdims=True))
            s_t = _dot_nt(k, q) * jnp.exp2(dm - mt)
            both = jnp.exp2(m - mt) * _dot_nt(ct.astype(BF16), q) + _dot(v_t, s_t.astype(BF16))
            den = both[ML_V:ML_V + 1]
            hh = both[0:ML_V] / jnp.maximum(jnp.abs(den), jnp.exp2(-(b_row + mt)))
            hh = hh * lax.rsqrt(jnp.mean(hh * hh, axis=0, keepdims=True) + RMS_EPS)

            m_next = jnp.maximum(m, jnp.max(r_row, axis=1, keepdims=True))
            kw = (k.astype(F32) * jnp.exp2(r_col - m_next)).astype(BF16)
            ct_ref[n] = jnp.exp2(m - m_next) * ct + _dot(v_t, kw)
            m_ref[n] = jnp.broadcast_to(g + m_next, (1, LANES))

            vs = slice(h * ML_V, (h + 1) * ML_V)
            o_ref[b, :, vs] = (hh.T * hn_ref[:, vs] * sg_ref[b, :, vs].astype(F32)).astype(o_ref.dtype)


def mlstm(proj, g_row, head_norm, *, L=ML_CHUNK, nb=ML_SEQS):
    B, S, _ = proj.shape
    W = D_MODEL
    assert L == ML_V, "one identity matrix serves the v transposes"
    norm_spec, head_norm = _whole(head_norm)
    return pl.pallas_call(
        functools.partial(_mlstm_kernel, L=L, nb=nb),
        grid=(B // nb, S // L),
        in_specs=[pl.BlockSpec((nb, L, W), lambda b, c: (b, c, 0)),
                  pl.BlockSpec((nb, L, W), lambda b, c: (b, c, 1)),
                  pl.BlockSpec((nb, L, W), lambda b, c: (b, c, 2)),
                  pl.BlockSpec((nb, 2 * ML_HEADS, L), lambda b, c: (b, 0, c)),
                  norm_spec],
        out_specs=pl.BlockSpec((nb, L, W), lambda b, c: (b, c, 0)),
        out_shape=jax.ShapeDtypeStruct((B, S, W), BF16),
        scratch_shapes=[pltpu.VMEM((nb * ML_HEADS, ML_V + BF16_ROWS, ML_QK), F32),
                        pltpu.VMEM((nb * ML_HEADS, 1, LANES), F32),
                        pltpu.VMEM((nb * ML_HEADS, ML_V + BF16_ROWS, L), BF16)],
        compiler_params=_params("parallel", "arbitrary"),
        name="mlstm",
    )(proj, proj, proj, g_row, head_norm)


def _xattn_kernel(*refs, n_in):
    x_ref, g_ref, wq_ref, mem_ref, gm_ref, wkv_ref, wo_ref, o_ref, kv_ref = refs[2 * n_in:]

    @pl.when(pl.program_id(1) == 0)
    def _():
        memn = _rms(mem_ref[0], gm_ref[...]).astype(BF16)
        for c0 in range(0, kv_ref.shape[1], PROJ_COLS):
            kv_ref[:, c0:c0 + PROJ_COLS] = _dot(memn, wkv_ref[:, c0:c0 + PROJ_COLS]).astype(BF16)

    x = x_ref[0]
    for a_ref, w_ref in zip(refs[:n_in], refs[n_in:2 * n_in]):
        x = x + _dot(a_ref[0], w_ref[...])
    xn = _rms(x, g_ref[...]).astype(BF16)
    q = (_dot(xn, wq_ref[...]) * (X_DIM ** -0.5)).astype(BF16)
    heads = []
    for h in range(X_HEADS):
        cols = slice(h * X_DIM, (h + 1) * X_DIM)
        s = _dot_nt(q[:, cols], kv_ref[:, cols])
        p = jnp.exp(s - jnp.max(s, axis=1, keepdims=True))
        l = jnp.sum(p, axis=1, keepdims=True)
        v = kv_ref[:, D_MODEL + h * X_DIM:D_MODEL + (h + 1) * X_DIM]
        heads.append((_dot(p.astype(BF16), v) / l).astype(BF16))
    o_ref[0] = x + _dot(jnp.concatenate(heads, axis=1), wo_ref[...])


def mix_out_cross_attention(acts, w_mix, x, gain, wq, mem, mem_gain, wkv, wo, *, tq=X_TILE):
    B, S, D = x.shape
    M = mem.shape[1]
    (gain_spec, gain), (wq_spec, wq), (gm_spec, mem_gain), (wkv_spec, wkv), (wo_spec, wo) = map(
        _whole, (gain, wq, mem_gain, wkv, wo))
    return pl.pallas_call(
        functools.partial(_xattn_kernel, n_in=len(acts)),
        grid=(B, S // tq),
        in_specs=([pl.BlockSpec((1, tq, a.shape[2]), lambda b, i: (b, i, 0)) for a in acts]
                  + [w[0] for w in w_mix]
                  + [pl.BlockSpec((1, tq, D), lambda b, i: (b, i, 0)),
                     gain_spec, wq_spec,
                     pl.BlockSpec((1, M, D), lambda b, i: (b, 0, 0)),
                     gm_spec, wkv_spec, wo_spec]),
        out_specs=pl.BlockSpec((1, tq, D), lambda b, i: (b, i, 0)),
        out_shape=jax.ShapeDtypeStruct((B, S, D), F32),
        scratch_shapes=[pltpu.VMEM((M, 2 * D), BF16)],
        compiler_params=_params("parallel", "arbitrary"),
        name="mix_out_cross_attention",
    )(*acts, *[w[1] for w in w_mix], x, gain, wq, mem, mem_gain, wkv, wo)


def _gelu_tanh(x):
    k = -2.0 * math.sqrt(2.0 / math.pi) * math.log2(math.e)
    return x / (1.0 + jnp.exp2(x * (k * 0.044715 * (x * x) + k)))


def _ffn_kernel(x_ref, g_ref, wup_ref, cw_ref, cb_ref, wd_ref, fg_ref, o_ref, up_ref, act_ref, nat_ref, carry_ref,
                *, tm, tiles_per_seq, final_norm):
    i = pl.program_id(0)

    @pl.when(i % tiles_per_seq == 0)
    def _():
        carry_ref[...] = jnp.zeros_like(carry_ref)

    xn = _rms(x_ref[...], g_ref[...]).astype(BF16)
    n_chunks = D_FF // FFN_CHUNK
    slabs = FFN_CHUNK // LANES
    rows = tm // ROW_PHASES

    def up_project(c):
        for half in range(2):
            col0 = half * D_FF + c * FFN_CHUNK
            up = _dot(xn, wup_ref[:, col0:col0 + FFN_CHUNK])
            for s in range(slabs):
                k = col0 // LANES + s
                up_ref[c % 2, half, s, 0:BF16_ROWS, :] = carry_ref[k]
                up_ref[c % 2, half, s, BF16_ROWS:, :] = up[:, s * LANES:(s + 1) * LANES]
                carry_ref[k] = up[tm - BF16_ROWS:, s * LANES:(s + 1) * LANES]

    def conv(c, half, s, phase):
        col0 = half * D_FF + c * FFN_CHUNK + s * LANES
        cw = cw_ref[:, col0:col0 + LANES]
        out = cb_ref[:, col0:col0 + LANES]
        for tap in range(FFN_CONV):
            first = BF16_ROWS + phase - (FFN_CONV - 1 - tap)
            out = out + cw[tap:tap + 1] * up_ref[c % 2, half, s, pl.ds(first, rows, stride=ROW_PHASES), :]
        return out

    acc = None
    piece_start = 0
    up_project(0)
    for c in range(n_chunks):
        if c + 1 < n_chunks:
            up_project(c + 1)
        for phase in range(ROW_PHASES):
            for s in range(slabs):
                act = _gelu_tanh(conv(c, 0, s, phase)) * conv(c, 1, s, phase)
                act_ref[phase * rows:(phase + 1) * rows,
                        c * FFN_CHUNK + s * LANES:c * FFN_CHUNK + (s + 1) * LANES] = act.astype(BF16)
        if (c + 1) % DOWN_CHUNKS == 0 or c + 1 == n_chunks:
            piece = slice(piece_start * FFN_CHUNK, (c + 1) * FFN_CHUNK)
            part = _dot(act_ref[:, piece], wd_ref[piece, :])
            acc = part if acc is None else acc + part
            piece_start = c + 1

    for phase in range(ROW_PHASES):
        for s in range(D_MODEL // LANES):
            nat_ref[s, pl.ds(phase, rows, stride=ROW_PHASES), :] = acc[phase * rows:(phase + 1) * rows,
                                                                       s * LANES:(s + 1) * LANES]
    y = x_ref[...] + jnp.concatenate([nat_ref[s] for s in range(D_MODEL // LANES)], axis=1)
    o_ref[...] = _rms(y, fg_ref[...]) if final_norm else y


def conv_ffn(x, gain, w_up, conv_w, conv_b, w_down, seq_len, final_gain=None, *, tm=ROW_TILE):
    T, D = x.shape
    final_norm = final_gain is not None
    specs, operands = zip(*map(_whole, (gain, w_up, conv_w, conv_b, w_down, final_gain if final_norm else gain)))
    return pl.pallas_call(
        functools.partial(_ffn_kernel, tm=tm, tiles_per_seq=seq_len // tm, final_norm=final_norm),
        grid=(T // tm,),
        in_specs=[pl.BlockSpec((tm, D), lambda i: (i, 0)), *specs],
        out_specs=pl.BlockSpec((tm, D), lambda i: (i, 0)),
        out_shape=jax.ShapeDtypeStruct((T, D), F32),
        scratch_shapes=[pltpu.VMEM((2, 2, FFN_CHUNK // LANES, tm + BF16_ROWS, LANES), F32),
                        pltpu.VMEM((tm, D_FF), BF16),
                        pltpu.VMEM((D // LANES, tm, LANES), F32),
                        pltpu.VMEM((2 * D_FF // LANES, BF16_ROWS, LANES), F32)],
        compiler_params=_params("arbitrary"),
        name="conv_ffn",
    )(x, *operands)


def _gate_bias_rows(bias):
    return jnp.broadcast_to(jnp.pad(bias, (0, BF16_ROWS - bias.shape[0]))[:, None], (BF16_ROWS, LANES))


def _fox_diff_mixer(x, B, S, gain, w_in, fox_bf, lq1, lk1, lq2, lk2, subln, lambda_init, cast_fox, cast_diff):
    proj, fq, fk = fox_projection(x, gain, w_in, _gate_bias_rows(fox_bf), S)
    proj = proj.reshape(B, S, -1)
    fox, cast_fox = fox_attention(proj, fq.reshape(B, S, LANES), fk.reshape(B, S, LANES), cast_fox)
    dif, cast_diff = diff_attention(proj, lq1, lk1, lq2, lk2, subln, lambda_init, cast_diff)
    return [fox, dif], cast_fox, cast_diff


def _row_block(w_out, j, r, rows):
    return pl.BlockSpec((None, rows, D_MODEL), lambda *_: (j, r, 0), pipeline_mode=pl.Buffered(1)), w_out


def _mlstm_mixer(x, B, S, j, gain, w_in, conv_qk, b_i, b_f, head_norm, w_out):
    proj, g_row = mlstm_projection(x, gain, w_in, _gate_bias_rows(jnp.concatenate([b_i, b_f])), conv_qk, S)
    h = mlstm(proj.reshape(B, S, -1), g_row, head_norm)
    return [h], [_whole((w_out, j))]


def kernel(x, mem, mix_norm, xattn_norm, mem_norm, ffn_norm, attn_w_in, attn_fox_bf, diff_lq1, diff_lk1, diff_lq2, diff_lk2, diff_subln, attn_w_out, mlstm_w_in, mlstm_conv_qk, mlstm_b_i, mlstm_b_f, mlstm_head_norm, mlstm_w_out, xattn_wq, xattn_wkv, xattn_wo, ffn_w_up, ffn_conv_w, ffn_conv_b, ffn_w_down, final_norm):
    B, S, D = x.shape
    depth = mix_norm.shape[0]
    x = x.reshape(B * S, D)
    rows = lambda a: a.reshape(a.shape[0], 1, -1)
    mix_norm, xattn_norm, mem_norm, ffn_norm = map(rows, (mix_norm, xattn_norm, mem_norm, ffn_norm))
    diff_lq1, diff_lk1, diff_lq2, diff_lk2, diff_subln = map(rows, (diff_lq1, diff_lk1, diff_lq2, diff_lk2, diff_subln))
    mlstm_head_norm, ffn_conv_b = rows(mlstm_head_norm), rows(ffn_conv_b)
    attn_w_in, mlstm_w_in = jnp.swapaxes(attn_w_in, 1, 2), jnp.swapaxes(mlstm_w_in, 1, 2)
    for layer in range(depth):
        j = layer // 2
        if layer % 2 == 0:
            lambda_init = 0.8 - 0.6 * math.exp(-0.3 * layer)
            cast_fox = (ffn_w_up, ffn_w_down) if layer == 0 else ()
            cast_diff = (attn_w_out, mlstm_w_out, xattn_wq, xattn_wkv, xattn_wo) if layer == 0 else ()
            mixed, cast_fox, cast_diff = _fox_diff_mixer(
                x, B, S, (mix_norm, layer), (attn_w_in, j), attn_fox_bf[j], (diff_lq1, j), (diff_lk1, j),
                (diff_lq2, j), (diff_lk2, j), (diff_subln, j), lambda_init, cast_fox, cast_diff)
            if layer == 0:
                ffn_w_up, ffn_w_down = cast_fox
                attn_w_out, mlstm_w_out, xattn_wq, xattn_wkv, xattn_wo = cast_diff
            w_mix = [_row_block(attn_w_out, j, r, FOX_HEADS * FOX_DIM) for r in range(2)]
        else:
            mixed, w_mix = _mlstm_mixer(x, B, S, j, (mix_norm, layer), (mlstm_w_in, j), (mlstm_conv_qk, j),
                                        mlstm_b_i[j], mlstm_b_f[j], (mlstm_head_norm, j), mlstm_w_out)
        x = mix_out_cross_attention(mixed, w_mix, x.reshape(B, S, D), (xattn_norm, layer), (xattn_wq, layer), mem,
                                    (mem_norm, layer), (xattn_wkv, layer), (xattn_wo, layer)).reshape(B * S, D)
        x = conv_ffn(x, (ffn_norm, layer), (ffn_w_up, layer), (ffn_conv_w, layer), (ffn_conv_b, layer),
                     (ffn_w_down, layer), S, final_norm.reshape(1, D) if layer == depth - 1 else None)
    return x.reshape(B, S, D)
```

```python
import functools
import math

import jax
import jax.numpy as jnp
from jax import lax
from jax.experimental import pallas as pl
from jax.experimental.pallas import tpu as pltpu

F32 = jnp.float32
BF16 = jnp.bfloat16

D_MODEL = 1024
RMS_EPS = 1e-6
NEG_INF = -1e30
CHUNK = 64
FOX_HEADS, FOX_DIM = 8, 64
DIFF_HEADS, DIFF_QK, DIFF_V = 4, 64, 128
ML_HEADS, ML_QK, ML_V, ML_CONV = 4, 128, 256, 4
X_HEADS, X_DIM = 4, 256
D_FF = 2816
FFN_CONV = 3
LANES = 128
BF16_ROWS = 16
VMEM_LIMIT = 56 * 1024 * 1024

ROW_TILE = 512
ATT_Q_TILE = 512
ATT_K_TILE = 512
LOG2_E = math.log2(math.e)
ML_CHUNK = 256
ML_SEQS = 2
X_TILE = 1024
PROJ_COLS = 1024
FFN_CHUNK = 256
ROW_PHASES = 4
DOWN_CHUNKS = 6
TRANSPOSE_COLS = 512
F_PIECES = 3
FOX_BIAS_K_LANE = 0
FOX_BIAS_Q_LANE = 32


def _params(*sem):
    return pltpu.CompilerParams(dimension_semantics=sem, vmem_limit_bytes=VMEM_LIMIT)


def _resident(shape):
    return pl.BlockSpec(shape, lambda *_: (0,) * len(shape), pipeline_mode=pl.Buffered(1))


def _whole(a):
    if isinstance(a, tuple):
        arr, layer = a
        tail = arr.shape[1:]
        spec = pl.BlockSpec((None,) + tail, lambda *_: (layer,) + (0,) * len(tail), pipeline_mode=pl.Buffered(1))
        return spec, arr
    return _resident(a.shape), a


def _rows(a):
    return (a[0] if isinstance(a, tuple) else a).shape[-2]


def _rms(x, gain):
    return x * lax.rsqrt(jnp.mean(x * x, axis=-1, keepdims=True) + RMS_EPS) * gain


def _sigmoid(x):
    return 1.0 / (1.0 + jnp.exp(-x))


def _dot(a, b):
    return jnp.dot(a, b, preferred_element_type=F32)


def _dot_nt(a, b):
    return lax.dot_general(a, b, (((1,), (1,)), ((), ())), preferred_element_type=F32)


def _split3(x):
    hi = x.astype(BF16)
    r1 = x - hi.astype(F32)
    mid = r1.astype(BF16)
    return hi, mid, (r1 - mid.astype(F32)).astype(BF16)


def _dot_select(x, sel):
    hi, mid, lo = _split3(x)
    return _dot(hi, sel) + _dot(mid, sel) + _dot(lo, sel)


def _prefix_matrix(n, block):
    r = lax.broadcasted_iota(jnp.int32, (n, n), 0)
    c = lax.broadcasted_iota(jnp.int32, (n, n), 1)
    return jnp.where((r <= c) & (r // block == c // block), 1.0, 0.0).astype(BF16)


def _tile_lanes(a, n):
    return jnp.concatenate([a] * (n // LANES), axis=1)


def _log_sigmoid(x):
    return jnp.minimum(x, 0.0) - jnp.log1p(jnp.exp(-jnp.abs(x)))


def _prepare_weights(wt_ref, wb_ref, wg_ref, pieces, gate_row, n_gates):
    dst = 0
    for src, height, scale in pieces:
        for r0 in range(0, height, PROJ_COLS):
            n = min(PROJ_COLS, height - r0)
            wb_ref[dst + r0:dst + r0 + n, :] = (wt_ref[src + r0:src + r0 + n, :] * scale).astype(BF16)
        dst += height
    gates = wt_ref[gate_row:gate_row + n_gates, :]
    wg_ref[...] = jnp.concatenate([gates, jnp.zeros((BF16_ROWS - n_gates, gates.shape[1]), F32)],
                                  axis=0).astype(BF16)


def _fox_proj_kernel(x_ref, g_ref, w_ref, bf_ref, o_ref, fq_ref, fk_ref, carry_ref, wb_ref, wg_ref,
                     *, tm, tn, tiles_per_seq):
    i = pl.program_id(0)

    @pl.when(i == 0)
    def _():
        fw = FOX_HEADS * FOX_DIM
        dq0 = 3 * fw + FOX_HEADS
        dqw = DIFF_HEADS * 2 * DIFF_QK
        _prepare_weights(w_ref, wb_ref, wg_ref,
                         ((0, fw, FOX_DIM ** -0.5 * LOG2_E), (fw, 2 * fw, 1.0),
                          (dq0, dqw, DIFF_QK ** -0.5 * LOG2_E), (dq0 + dqw, w_ref.shape[0] - dq0 - dqw, 1.0)),
                         3 * fw, FOX_HEADS)

    @pl.when(i % tiles_per_seq == 0)
    def _():
        carry_ref[...] = jnp.zeros_like(carry_ref)

    xn = _rms(x_ref[...], g_ref[...]).astype(BF16)

    def project(part):
        width = o_ref.shape[1] // 3
        for c0 in range(part * width, (part + 1) * width, tn):
            o_ref[:, c0:c0 + tn] = _dot_nt(xn, wb_ref[c0:c0 + tn, :]).astype(o_ref.dtype)

    gates_t = _dot_nt(wg_ref[...], xn)
    project(0)
    log_f = _log_sigmoid(gates_t + _tile_lanes(bf_ref[...], tm)) * LOG2_E
    f_t = _tile_lanes(carry_ref[...], tm) + _dot_select(log_f, _prefix_matrix(tm, tm))
    carry_ref[...] = jnp.broadcast_to(f_t[:, tm - 1:tm], carry_ref.shape)
    f = jnp.concatenate([f_t, jnp.zeros((LANES - BF16_ROWS, tm), F32)], axis=0).T
    project(1)

    src = lax.broadcasted_iota(jnp.int32, (LANES, LANES), 0)
    dst = lax.broadcasted_iota(jnp.int32, (LANES, LANES), 1)
    lane = lax.broadcasted_iota(jnp.int32, (1, LANES), 1)
    n_bias = F_PIECES * FOX_HEADS
    ones_q = jnp.where((lane >= FOX_BIAS_K_LANE) & (lane < FOX_BIAS_K_LANE + n_bias), 1.0, 0.0)
    ones_k = jnp.where((lane >= FOX_BIAS_Q_LANE) & (lane < FOX_BIAS_Q_LANE + n_bias), 1.0, 0.0)
    both = jnp.concatenate([ones_q, ones_k], axis=1)
    for c, piece in enumerate(_split3(f)):
        head = src < FOX_HEADS
        to_q = jnp.where(head & (dst == FOX_BIAS_Q_LANE + F_PIECES * src + c), 1.0, 0.0)
        to_k = jnp.where(head & (dst == FOX_BIAS_K_LANE + F_PIECES * src + c), -1.0, 0.0)
        both = both + _dot(piece, jnp.concatenate([to_q, to_k], axis=1).astype(BF16))
    fq_ref[...] = both[:, 0:LANES].astype(BF16)
    fk_ref[...] = both[:, LANES:2 * LANES].astype(BF16)
    project(2)


def fox_projection(x, gain, w, gate_bias, seq_len, *, tm=ROW_TILE, tn=PROJ_COLS):
    T, D = x.shape
    N = _rows(w) - FOX_HEADS
    row_block = lambda n: pl.BlockSpec((tm, n), lambda i: (i, 0))
    specs, operands = zip(*map(_whole, (gain, w, gate_bias)))
    return pl.pallas_call(
        functools.partial(_fox_proj_kernel, tm=tm, tn=tn, tiles_per_seq=seq_len // tm),
        grid=(T // tm,),
        in_specs=[row_block(D), *specs],
        out_specs=[row_block(N), row_block(LANES), row_block(LANES)],
        out_shape=[jax.ShapeDtypeStruct((T, N), BF16), jax.ShapeDtypeStruct((T, LANES), BF16),
                   jax.ShapeDtypeStruct((T, LANES), BF16)],
        scratch_shapes=[pltpu.VMEM((BF16_ROWS, LANES), F32), pltpu.VMEM((N, D), BF16),
                        pltpu.VMEM((BF16_ROWS, D), BF16)],
        compiler_params=_params("arbitrary"),
        name="fox_projection",
    )(x, *operands)


def _mlstm_proj_kernel(x_ref, halo_ref, g_ref, w_raw_ref, gb_ref, cw_ref, o_ref, gr_ref, up_ref, nat_ref, w_ref,
                       wg_ref, *, tm, tiles_per_seq):
    i = pl.program_id(0)

    @pl.when(i == 0)
    def _():
        g0 = 2 * ML_HEADS * ML_QK + ML_HEADS * ML_V
        og_col = g0 + 2 * ML_HEADS
        _prepare_weights(w_raw_ref, w_ref, wg_ref, ((0, g0, 1.0), (og_col, w_raw_ref.shape[0] - og_col, 1.0)),
                         g0, 2 * ML_HEADS)

    halo = _rms(halo_ref[...], g_ref[...])
    xe = jnp.concatenate([jnp.where(i % tiles_per_seq == 0, 0.0, halo).astype(BF16),
                          _rms(x_ref[...], g_ref[...]).astype(BF16)], axis=0)
    xn = xe[BF16_ROWS:]

    qk_slabs = 2 * ML_HEADS * ML_QK // LANES
    rows = tm // ROW_PHASES
    v0 = qk_slabs * LANES
    og0 = v0 + ML_HEADS * ML_V
    for s0 in range(0, qk_slabs, 2):
        up = _dot_nt(xe, w_ref[s0 * LANES:(s0 + 2) * LANES, :])
        up_ref[s0] = up[:, 0:LANES]
        up_ref[s0 + 1] = up[:, LANES:2 * LANES]
    pre_t = _dot_nt(wg_ref[...], xn) + _tile_lanes(gb_ref[...], tm)
    o_ref[:, v0:og0] = _dot_nt(xn, w_ref[v0:og0, :]).astype(o_ref.dtype)
    for s in range(qk_slabs):
        cw = cw_ref[:, s * LANES:(s + 1) * LANES]
        for phase in range(ROW_PHASES):
            y = None
            for tap in range(ML_CONV):
                first = BF16_ROWS + phase - (ML_CONV - 1 - tap)
                term = cw[tap:tap + 1] * up_ref[s, pl.ds(first, rows, stride=ROW_PHASES), :]
                y = term if y is None else y + term
            y = y * _sigmoid(y)
            if s < qk_slabs // 2:
                y = y * (ML_QK ** -0.5)
            nat_ref[s, pl.ds(phase, rows, stride=ROW_PHASES), :] = y
        o_ref[:, s * LANES:(s + 1) * LANES] = nat_ref[s].astype(o_ref.dtype)

    og = _dot_nt(xn, w_ref[og0:, :])
    row = lax.broadcasted_iota(jnp.int32, (BF16_ROWS, 1), 0)
    val_t = jnp.where(row < ML_HEADS, pre_t, _log_sigmoid(pre_t)) * LOG2_E
    g_t = jnp.where(row < ML_HEADS, val_t, _dot_select(val_t, _prefix_matrix(tm, ML_CHUNK)))
    gr_ref[0] = g_t[0:2 * ML_HEADS]
    o_ref[:, og0:] = _sigmoid(og).astype(o_ref.dtype)


def mlstm_projection(x, gain, w, gate_bias, conv_w, seq_len, *, tm=ROW_TILE):
    T, D = x.shape
    N = _rows(w) - 2 * ML_HEADS
    tiles_per_seq = seq_len // tm
    halo_blocks = tm // BF16_ROWS
    row_block = lambda n: pl.BlockSpec((tm, n), lambda i: (i, 0))
    specs, operands = zip(*map(_whole, (gain, w, gate_bias, conv_w)))
    return pl.pallas_call(
        functools.partial(_mlstm_proj_kernel, tm=tm, tiles_per_seq=tiles_per_seq),
        grid=(T // tm,),
        in_specs=[row_block(D),
                  pl.BlockSpec((BF16_ROWS, D), lambda i: (jnp.maximum(i * halo_blocks - 1, 0), 0)),
                  *specs],
        out_specs=[row_block(N),
                   pl.BlockSpec((1, 2 * ML_HEADS, tm), lambda i: (i // tiles_per_seq, 0, i % tiles_per_seq))],
        out_shape=[jax.ShapeDtypeStruct((T, N), BF16),
                   jax.ShapeDtypeStruct((T // seq_len, 2 * ML_HEADS, seq_len), F32)],
        scratch_shapes=[pltpu.VMEM((2 * ML_HEADS * ML_QK // LANES, tm + BF16_ROWS, LANES), F32),
                        pltpu.VMEM((2 * ML_HEADS * ML_QK // LANES, tm, LANES), F32),
                        pltpu.VMEM((N, D), BF16), pltpu.VMEM((BF16_ROWS, D), BF16)],
        compiler_params=_params("arbitrary"),
        name="mlstm_projection",
    )(x, x, *operands)


def _cast_blocks(arrays, grid):
    steps = grid[0] * grid[1]
    views = [a.reshape(-1, a.shape[-1]) for a in arrays]
    assert all(v.shape[0] % (steps * BF16_ROWS) == 0 for v in views)
    specs = [pl.BlockSpec((v.shape[0] // steps, v.shape[1]), lambda b, i: (b * grid[1] + i, 0)) for v in views]
    return specs, views, [jax.ShapeDtypeStruct(v.shape, BF16) for v in views]


def _with_casts(kernel, n_in, n_cast):
    def wrapped(*refs):
        cast_in = refs[n_in:n_in + n_cast]
        cast_out = refs[n_in + n_cast + 1:n_in + 2 * n_cast + 1]
        for src, dst in zip(cast_in, cast_out):
            dst[...] = src[...].astype(BF16)
        kernel(*refs[:n_in], refs[n_in + n_cast], *refs[n_in + 2 * n_cast + 1:])
    return wrapped


def _build_vt(v_ref, vt_ref, n_heads, rows):
    S = v_ref.shape[1]
    r = lax.broadcasted_iota(jnp.int32, (LANES, LANES), 0)
    c = lax.broadcasted_iota(jnp.int32, (LANES, LANES), 1)
    eye = jnp.where(r == c, 1.0, 0.0).astype(BF16)
    per_group = LANES // rows
    for g in range(n_heads // per_group):
        for c0 in range(0, S, TRANSPOSE_COLS):
            cs = slice(c0, c0 + TRANSPOSE_COLS)
            vt = _dot_nt(eye, v_ref[0, cs, g * LANES:(g + 1) * LANES]).astype(BF16)
            for k in range(per_group):
                vt_ref[g * per_group + k, 0:rows, cs] = vt[k * rows:(k + 1) * rows]
    for h in range(n_heads):
        vt_ref[h, rows:rows + BF16_ROWS, :] = jnp.ones((BF16_ROWS, S), BF16)


def _softmax_step(s_t, vt, m_ref, acc_ref, idx, queries=slice(None)):
    m_prev = m_ref[idx, :, queries]
    m_new = jnp.maximum(m_prev, jnp.max(s_t, axis=0, keepdims=True))
    p = jnp.exp2(s_t - m_new).astype(BF16)
    acc_ref[idx, :, queries] = jnp.exp2(m_prev - m_new) * acc_ref[idx, :, queries] + _dot(vt, p)
    m_ref[idx, :, queries] = m_new


def _causal_steps(i, logits, attend, n_streams, buf_a, buf_b):
    def phase(j, src, dst, next_is_diagonal=False):
        for n in range(n_streams):
            logits(j + 1, dst, n, next_is_diagonal)
            attend(j, src, n, False)

    for n in range(n_streams):
        logits(0, buf_a, n, False)

    def body(jj, carry):
        phase(2 * jj, buf_a, buf_b)
        phase(2 * jj + 1, buf_b, buf_a)
        return carry

    lax.fori_loop(0, lax.shift_right_logical(i, 1), body, 0)
    odd = lax.rem(i, 2) == 1

    @pl.when(odd)
    def _():
        phase(i - 1, buf_a, buf_b, True)
        for n in range(n_streams):
            attend(i, buf_b, n, True)

    @pl.when(jnp.logical_not(odd))
    def _():
        for n in range(n_streams):
            attend(i, buf_a, n, True)


def _fox_kernel(q_ref, k_ref, v_ref, fq_ref, fk_ref, o_ref, vt_ref, qc_ref, acc_ref, m_ref, sa_ref, sb_ref, *, tq, tk):
    i = pl.program_id(1)

    @pl.when(i == 0)
    def _():
        _build_vt(v_ref, vt_ref, FOX_HEADS, FOX_DIM)

    lane = lax.broadcasted_iota(jnp.int32, (1, LANES), 1)
    fq = fq_ref[0]
    for h in range(FOX_HEADS):
        pair, half = divmod(h, 2)
        q = q_ref[0, :, pair * LANES:(pair + 1) * LANES]
        in_head = (lane >= half * FOX_DIM) & (lane < (half + 1) * FOX_DIM)
        lo_k, lo_q = FOX_BIAS_K_LANE + F_PIECES * h, FOX_BIAS_Q_LANE + F_PIECES * h
        mine = ((lane >= lo_k) & (lane < lo_k + F_PIECES)) | ((lane >= lo_q) & (lane < lo_q + F_PIECES))
        qc_ref[h, :, 0:LANES] = jnp.where(in_head, q, jnp.zeros_like(q))
        qc_ref[h, :, LANES:2 * LANES] = jnp.where(mine, fq, jnp.zeros_like(fq))
    m_ref[...] = jnp.full(m_ref.shape, NEG_INF, F32)
    acc_ref[...] = jnp.zeros(acc_ref.shape, F32)

    half = tk // 2
    key = lax.broadcasted_iota(jnp.int32, (half, tq), 0)
    qry = lax.broadcasted_iota(jnp.int32, (half, tq), 1)
    causal = key <= qry

    def key_rows(j, part=None):
        if part is None:
            return pl.ds(pl.multiple_of(j * tk, tk), tk)
        return pl.ds(pl.multiple_of(j * tk + part * half, half), half)

    def logits(j, buf, h, diagonal):
        pair = h // 2
        kc = jnp.concatenate([k_ref[0, key_rows(j), pair * LANES:(pair + 1) * LANES], fk_ref[0, key_rows(j), :]],
                             axis=1)
        if diagonal:
            buf[h, 0:half, :] = _dot_nt(kc[0:half], qc_ref[h])
            buf[h, half:tk, half:tq] = _dot_nt(kc[half:tk], qc_ref[h, half:tq, :])
        else:
            buf[h] = _dot_nt(kc, qc_ref[h])

    def attend(j, buf, h, diagonal):
        if diagonal:
            _softmax_step(jnp.where(causal, buf[h, 0:half, :], NEG_INF), vt_ref[h, :, key_rows(j, 0)],
                          m_ref, acc_ref, h)
            _softmax_step(jnp.where(causal[:, 0:tq - half], buf[h, half:tk, half:tq], NEG_INF),
                          vt_ref[h, :, key_rows(j, 1)], m_ref, acc_ref, h, slice(half, tq))
        else:
            _softmax_step(buf[h], vt_ref[h, :, key_rows(j)], m_ref, acc_ref, h)

    _causal_steps(i, logits, attend, FOX_HEADS, sa_ref, sb_ref)

    for pair in range(FOX_HEADS // 2):
        halves = []
        for half in range(2):
            a = acc_ref[2 * pair + half]
            halves.append(a[0:FOX_DIM] / a[FOX_DIM:FOX_DIM + 1])
        o_ref[0, :, pair * LANES:(pair + 1) * LANES] = jnp.concatenate(halves, axis=0).T.astype(o_ref.dtype)


def fox_attention(proj, fq, fk, cast=(), *, tq=ATT_Q_TILE, tk=ATT_K_TILE):
    B, S, _ = proj.shape
    assert tq == tk
    width = FOX_HEADS * FOX_DIM
    rows = FOX_DIM + BF16_ROWS
    grid = (B, S // tq)
    cast_specs, cast_views, cast_shapes = _cast_blocks(cast, grid)
    out, *cast_out = pl.pallas_call(
        _with_casts(functools.partial(_fox_kernel, tq=tq, tk=tk), 5, len(cast)),
        grid=grid,
        in_specs=[pl.BlockSpec((1, tq, width), lambda b, i: (b, i, 0)),
                  pl.BlockSpec((1, S, width), lambda b, i: (b, 0, 1)),
                  pl.BlockSpec((1, S, width), lambda b, i: (b, 0, 2)),
                  pl.BlockSpec((1, tq, LANES), lambda b, i: (b, i, 0)),
                  pl.BlockSpec((1, S, LANES), lambda b, i: (b, 0, 0)),
                  *cast_specs],
        out_specs=[pl.BlockSpec((1, tq, width), lambda b, i: (b, i, 0)), *cast_specs],
        out_shape=[jax.ShapeDtypeStruct((B, S, width), BF16), *cast_shapes],
        scratch_shapes=[pltpu.VMEM((FOX_HEADS, rows, S), BF16),
                        pltpu.VMEM((FOX_HEADS, tq, 2 * LANES), BF16),
                        pltpu.VMEM((FOX_HEADS, rows, tq), F32),
                        pltpu.VMEM((FOX_HEADS, 1, tq), F32),
                        pltpu.VMEM((FOX_HEADS, tk, tq), F32),
                        pltpu.VMEM((FOX_HEADS, tk, tq), F32)],
        compiler_params=_params("parallel", "arbitrary"),
        name="fox_attention",
    )(proj, proj, proj, fq, fk, *cast_views)
    return out, [c.reshape(a.shape) for c, a in zip(cast_out, cast)]


def _diff_kernel(q_ref, k_ref, v_ref, lq1_ref, lk1_ref, lq2_ref, lk2_ref, sub_ref, o_ref, vt_ref, qc_ref, acc_ref,
                 m_ref, sa_ref, sb_ref, *, tq, tk, lambda_init):
    i = pl.program_id(1)

    @pl.when(i == 0)
    def _():
        _build_vt(v_ref, vt_ref, DIFF_HEADS, DIFF_V)

    lane = lax.broadcasted_iota(jnp.int32, (1, LANES), 1)
    for h in range(DIFF_HEADS):
        q = q_ref[0, :, h * LANES:(h + 1) * LANES]
        zero = jnp.zeros_like(q)
        qc_ref[2 * h] = jnp.where(lane < DIFF_QK, q, zero)
        qc_ref[2 * h + 1] = jnp.where(lane >= DIFF_QK, q, zero)
    m_ref[...] = jnp.full(m_ref.shape, NEG_INF, F32)
    acc_ref[...] = jnp.zeros(acc_ref.shape, F32)

    half = tk // 2
    key = lax.broadcasted_iota(jnp.int32, (half, tq), 0)
    qry = lax.broadcasted_iota(jnp.int32, (half, tq), 1)
    visible = key // CHUNK <= qry // CHUNK

    def key_rows(j, part=None):
        if part is None:
            return pl.ds(pl.multiple_of(j * tk, tk), tk)
        return pl.ds(pl.multiple_of(j * tk + part * half, half), half)

    def logits(j, buf, n, diagonal):
        k = k_ref[0, key_rows(j), (n // 2) * LANES:(n // 2 + 1) * LANES]
        if diagonal:
            buf[n, 0:half, :] = _dot_nt(k[0:half], qc_ref[n])
            buf[n, half:tk, half:tq] = _dot_nt(k[half:tk], qc_ref[n, half:tq, :])
        else:
            buf[n] = _dot_nt(k, qc_ref[n])

    def attend(j, buf, n, diagonal):
        vt = vt_ref.at[n // 2]
        if diagonal:
            _softmax_step(jnp.where(visible, buf[n, 0:half, :], NEG_INF), vt[:, key_rows(j, 0)], m_ref, acc_ref, n)
            _softmax_step(jnp.where(visible[:, 0:tq - half], buf[n, half:tk, half:tq], NEG_INF),
                          vt[:, key_rows(j, 1)], m_ref, acc_ref, n, slice(half, tq))
        else:
            _softmax_step(buf[n], vt[:, key_rows(j)], m_ref, acc_ref, n)

    _causal_steps(i, logits, attend, 2 * DIFF_HEADS, sa_ref, sb_ref)

    lam = (jnp.exp(jnp.sum(lq1_ref[...] * lk1_ref[...], axis=1, keepdims=True))
           - jnp.exp(jnp.sum(lq2_ref[...] * lk2_ref[...], axis=1, keepdims=True)) + lambda_init)
    for h in range(DIFF_HEADS):
        a1, a2 = acc_ref[2 * h], acc_ref[2 * h + 1]
        o_t = a1[0:DIFF_V] / a1[DIFF_V:DIFF_V + 1] - lam * (a2[0:DIFF_V] / a2[DIFF_V:DIFF_V + 1])
        out = _rms(o_t.T, sub_ref[...]) * (1.0 - lambda_init)
        o_ref[0, :, h * LANES:(h + 1) * LANES] = out.astype(o_ref.dtype)


def diff_attention(proj, lq1, lk1, lq2, lk2, subln, lambda_init, cast=(), *, tq=ATT_Q_TILE, tk=ATT_K_TILE):
    B, S, _ = proj.shape
    assert tq == tk
    width = DIFF_HEADS * DIFF_V
    rows = DIFF_V + BF16_ROWS
    specs, operands = zip(*map(_whole, (lq1, lk1, lq2, lk2, subln)))
    grid = (B, S // tq)
    cast_specs, cast_views, cast_shapes = _cast_blocks(cast, grid)
    out, *cast_out = pl.pallas_call(
        _with_casts(functools.partial(_diff_kernel, tq=tq, tk=tk, lambda_init=lambda_init), 8, len(cast)),
        grid=grid,
        in_specs=[pl.BlockSpec((1, tq, width), lambda b, i: (b, i, 3)),
                  pl.BlockSpec((1, S, width), lambda b, i: (b, 0, 4)),
                  pl.BlockSpec((1, S, width), lambda b, i: (b, 0, 5)),
                  *specs, *cast_specs],
        out_specs=[pl.BlockSpec((1, tq, width), lambda b, i: (b, i, 0)), *cast_specs],
        out_shape=[jax.ShapeDtypeStruct((B, S, width), BF16), *cast_shapes],
        scratch_shapes=[pltpu.VMEM((DIFF_HEADS, rows, S), BF16),
                        pltpu.VMEM((2 * DIFF_HEADS, tq, LANES), BF16),
                        pltpu.VMEM((2 * DIFF_HEADS, rows, tq), F32),
                        pltpu.VMEM((2 * DIFF_HEADS, 1, tq), F32),
                        pltpu.VMEM((2 * DIFF_HEADS, tk, tq), F32),
                        pltpu.VMEM((2 * DIFF_HEADS, tk, tq), F32)],
        compiler_params=_params("parallel", "arbitrary"),
        name="diff_attention",
    )(proj, proj, proj, *operands, *cast_views)
    return out, [c.reshape(a.shape) for c, a in zip(cast_out, cast)]


def _mlstm_kernel(qk_ref, v_ref, sg_ref, gr_ref, hn_ref, o_ref, ct_ref, m_ref, vt_ref, *, L, nb):
    c = pl.program_id(1)

    @pl.when(c == 0)
    def _():
        ct_ref[...] = jnp.zeros_like(ct_ref)
        m_ref[...] = jnp.zeros_like(m_ref)
        for n in range(nb * ML_HEADS):
            vt_ref[n, ML_V:ML_V + BF16_ROWS, :] = jnp.ones((BF16_ROWS, L), BF16)

    src = lax.broadcasted_iota(jnp.int32, (L, L), 0)
    dst = lax.broadcasted_iota(jnp.int32, (L, L), 1)
    causal = src <= dst
    eye = jnp.where(src == dst, 1.0, 0.0).astype(BF16)
    k0 = ML_HEADS * ML_QK
    for b in range(nb):
        gr = gr_ref[b]
        gc = jnp.concatenate([gr, jnp.zeros((LANES - gr.shape[0], L), F32)], axis=0).T
        for h in range(ML_HEADS):
            n = b * ML_HEADS + h
            q = qk_ref[b, :, h * ML_QK:(h + 1) * ML_QK]
            k = qk_ref[b, :, k0 + h * ML_QK:k0 + (h + 1) * ML_QK]
            vt_ref[n, 0:ML_V, :] = _dot_nt(eye, v_ref[b, :, h * ML_V:(h + 1) * ML_V]).astype(BF16)
            v_t = vt_ref[n]
            r_col = gc[:, h:h + 1] - gc[:, ML_HEADS + h:ML_HEADS + h + 1]
            b_row = gr[ML_HEADS + h:ML_HEADS + h + 1, :]
            r_row = gr[h:h + 1, :] - b_row
            g = b_row[:, L - 1:L]
            ct = ct_ref[n]
            m = m_ref[n][:, 0:1]

            dm = jnp.where(causal, r_col, NEG_INF)
            mt = jnp.maximum(m, jnp.max(dm, axis=0, keepdims=True))
            s_t = _dot_nt(k, q) * jnp.exp2(dm - mt)
            both = jnp.exp2(m - mt) * _dot_nt(ct.astype(BF16), q) + _dot(v_t, s_t.astype(BF16))
            den = both[ML_V:ML_V + 1]
            hh = both[0:ML_V] / jnp.maximum(jnp.abs(den), jnp.exp2(-(b_row + mt)))
            hh = hh * lax.rsqrt(jnp.mean(hh * hh, axis=0, keepdims=True) + RMS_EPS)

            m_next = jnp.maximum(m, jnp.max(r_row, axis=1, keepdims=True))
            kw = (k.astype(F32) * jnp.exp2(r_col - m_next)).astype(BF16)
            ct_ref[n] = jnp.exp2(m - m_next) * ct + _dot(v_t, kw)
            m_ref[n] = jnp.broadcast_to(g + m_next, (1, LANES))

            vs = slice(h * ML_V, (h + 1) * ML_V)
            o_ref[b, :, vs] = (hh.T * hn_ref[:, vs] * sg_ref[b, :, vs].astype(F32)).astype(o_ref.dtype)


def mlstm(proj, g_row, head_norm, *, L=ML_CHUNK, nb=ML_SEQS):
    B, S, _ = proj.shape
    W = D_MODEL
    assert L == ML_V, "one identity matrix serves the v transposes"
    norm_spec, head_norm = _whole(head_norm)
    return pl.pallas_call(
        functools.partial(_mlstm_kernel, L=L, nb=nb),
        grid=(B // nb, S // L),
        in_specs=[pl.BlockSpec((nb, L, W), lambda b, c: (b, c, 0)),
                  pl.BlockSpec((nb, L, W), lambda b, c: (b, c, 1)),
                  pl.BlockSpec((nb, L, W), lambda b, c: (b, c, 2)),
                  pl.BlockSpec((nb, 2 * ML_HEADS, L), lambda b, c: (b, 0, c)),
                  norm_spec],
        out_specs=pl.BlockSpec((nb, L, W), lambda b, c: (b, c, 0)),
        out_shape=jax.ShapeDtypeStruct((B, S, W), BF16),
        scratch_shapes=[pltpu.VMEM((nb * ML_HEADS, ML_V + BF16_ROWS, ML_QK), F32),
                        pltpu.VMEM((nb * ML_HEADS, 1, LANES), F32),
                        pltpu.VMEM((nb * ML_HEADS, ML_V + BF16_ROWS, L), BF16)],
        compiler_params=_params("parallel", "arbitrary"),
        name="mlstm",
    )(proj, proj, proj, g_row, head_norm)


def _xattn_kernel(*refs, n_in):
    x_ref, g_ref, wq_ref, mem_ref, gm_ref, wkv_ref, wo_ref, o_ref, kv_ref = refs[2 * n_in:]

    @pl.when(pl.program_id(1) == 0)
    def _():
        memn = _rms(mem_ref[0], gm_ref[...]).astype(BF16)
        for c0 in range(0, kv_ref.shape[1], PROJ_COLS):
            kv_ref[:, c0:c0 + PROJ_COLS] = _dot(memn, wkv_ref[:, c0:c0 + PROJ_COLS]).astype(BF16)

    x = x_ref[0]
    for a_ref, w_ref in zip(refs[:n_in], refs[n_in:2 * n_in]):
        x = x + _dot(a_ref[0], w_ref[...])
    xn = _rms(x, g_ref[...]).astype(BF16)
    q = (_dot(xn, wq_ref[...]) * (X_DIM ** -0.5)).astype(BF16)
    heads = []
    for h in range(X_HEADS):
        cols = slice(h * X_DIM, (h + 1) * X_DIM)
        s = _dot_nt(q[:, cols], kv_ref[:, cols])
        p = jnp.exp(s - jnp.max(s, axis=1, keepdims=True))
        l = jnp.sum(p, axis=1, keepdims=True)
        v = kv_ref[:, D_MODEL + h * X_DIM:D_MODEL + (h + 1) * X_DIM]
        heads.append((_dot(p.astype(BF16), v) / l).astype(BF16))
    o_ref[0] = x + _dot(jnp.concatenate(heads, axis=1), wo_ref[...])


def mix_out_cross_attention(acts, w_mix, x, gain, wq, mem, mem_gain, wkv, wo, *, tq=X_TILE):
    B, S, D = x.shape
    M = mem.shape[1]
    (gain_spec, gain), (wq_spec, wq), (gm_spec, mem_gain), (wkv_spec, wkv), (wo_spec, wo) = map(
        _whole, (gain, wq, mem_gain, wkv, wo))
    return pl.pallas_call(
        functools.partial(_xattn_kernel, n_in=len(acts)),
        grid=(B, S // tq),
        in_specs=([pl.BlockSpec((1, tq, a.shape[2]), lambda b, i: (b, i, 0)) for a in acts]
                  + [w[0] for w in w_mix]
                  + [pl.BlockSpec((1, tq, D), lambda b, i: (b, i, 0)),
                     gain_spec, wq_spec,
                     pl.BlockSpec((1, M, D), lambda b, i: (b, 0, 0)),
                     gm_spec, wkv_spec, wo_spec]),
        out_specs=pl.BlockSpec((1, tq, D), lambda b, i: (b, i, 0)),
        out_shape=jax.ShapeDtypeStruct((B, S, D), F32),
        scratch_shapes=[pltpu.VMEM((M, 2 * D), BF16)],
        compiler_params=_params("parallel", "arbitrary"),
        name="mix_out_cross_attention",
    )(*acts, *[w[1] for w in w_mix], x, gain, wq, mem, mem_gain, wkv, wo)


def _gelu_tanh(x):
    k = -2.0 * math.sqrt(2.0 / math.pi) * math.log2(math.e)
    return x / (1.0 + jnp.exp2(x * (k * 0.044715 * (x * x) + k)))


def _ffn_kernel(x_ref, g_ref, wup_ref, cw_ref, cb_ref, wd_ref, fg_ref, o_ref, up_ref, act_ref, nat_ref, carry_ref,
                *, tm, tiles_per_seq, final_norm):
    i = pl.program_id(0)

    @pl.when(i % tiles_per_seq == 0)
    def _():
        carry_ref[...] = jnp.zeros_like(carry_ref)

    xn = _rms(x_ref[...], g_ref[...]).astype(BF16)
    n_chunks = D_FF // FFN_CHUNK
    slabs = FFN_CHUNK // LANES
    rows = tm // ROW_PHASES

    def up_project(c):
        for half in range(2):
            col0 = half * D_FF + c * FFN_CHUNK
            up = _dot(xn, wup_ref[:, col0:col0 + FFN_CHUNK])
            for s in range(slabs):
                k = col0 // LANES + s
                up_ref[c % 2, half, s, 0:BF16_ROWS, :] = carry_ref[k]
                up_ref[c % 2, half, s, BF16_ROWS:, :] = up[:, s * LANES:(s + 1) * LANES]
                carry_ref[k] = up[tm - BF16_ROWS:, s * LANES:(s + 1) * LANES]

    def conv(c, half, s, phase):
        col0 = half * D_FF + c * FFN_CHUNK + s * LANES
        cw = cw_ref[:, col0:col0 + LANES]
        out = cb_ref[:, col0:col0 + LANES]
        for tap in range(FFN_CONV):
            first = BF16_ROWS + phase - (FFN_CONV - 1 - tap)
            out = out + cw[tap:tap + 1] * up_ref[c % 2, half, s, pl.ds(first, rows, stride=ROW_PHASES), :]
        return out

    acc = None
    piece_start = 0
    up_project(0)
    for c in range(n_chunks):
        if c + 1 < n_chunks:
            up_project(c + 1)
        for phase in range(ROW_PHASES):
            for s in range(slabs):
                act = _gelu_tanh(conv(c, 0, s, phase)) * conv(c, 1, s, phase)
                act_ref[phase * rows:(phase + 1) * rows,
                        c * FFN_CHUNK + s * LANES:c * FFN_CHUNK + (s + 1) * LANES] = act.astype(BF16)
        if (c + 1) % DOWN_CHUNKS == 0 or c + 1 == n_chunks:
            piece = slice(piece_start * FFN_CHUNK, (c + 1) * FFN_CHUNK)
            part = _dot(act_ref[:, piece], wd_ref[piece, :])
            acc = part if acc is None else acc + part
            piece_start = c + 1

    for phase in range(ROW_PHASES):
        for s in range(D_MODEL // LANES):
            nat_ref[s, pl.ds(phase, rows, stride=ROW_PHASES), :] = acc[phase * rows:(phase + 1) * rows,
                                                                       s * LANES:(s + 1) * LANES]
    y = x_ref[...] + jnp.concatenate([nat_ref[s] for s in range(D_MODEL // LANES)], axis=1)
    o_ref[...] = _rms(y, fg_ref[...]) if final_norm else y


def conv_ffn(x, gain, w_up, conv_w, conv_b, w_down, seq_len, final_gain=None, *, tm=ROW_TILE):
    T, D = x.shape
    final_norm = final_gain is not None
    specs, operands = zip(*map(_whole, (gain, w_up, conv_w, conv_b, w_down, final_gain if final_norm else gain)))
    return pl.pallas_call(
        functools.partial(_ffn_kernel, tm=tm, tiles_per_seq=seq_len // tm, final_norm=final_norm),
        grid=(T // tm,),
        in_specs=[pl.BlockSpec((tm, D), lambda i: (i, 0)), *specs],
        out_specs=pl.BlockSpec((tm, D), lambda i: (i, 0)),
        out_shape=jax.ShapeDtypeStruct((T, D), F32),
        scratch_shapes=[pltpu.VMEM((2, 2, FFN_CHUNK // LANES, tm + BF16_ROWS, LANES), F32),
                        pltpu.VMEM((tm, D_FF), BF16),
                        pltpu.VMEM((D // LANES, tm, LANES), F32),
                        pltpu.VMEM((2 * D_FF // LANES, BF16_ROWS, LANES), F32)],
        compiler_params=_params("arbitrary"),
        name="conv_ffn",
    )(x, *operands)


def _gate_bias_rows(bias):
    return jnp.broadcast_to(jnp.pad(bias, (0, BF16_ROWS - bias.shape[0]))[:, None], (BF16_ROWS, LANES))


def _fox_diff_mixer(x, B, S, gain, w_in, fox_bf, lq1, lk1, lq2, lk2, subln, lambda_init, cast_fox, cast_diff):
    proj, fq, fk = fox_projection(x, gain, w_in, _gate_bias_rows(fox_bf), S)
    proj = proj.reshape(B, S, -1)
    fox, cast_fox = fox_attention(proj, fq.reshape(B, S, LANES), fk.reshape(B, S, LANES), cast_fox)
    dif, cast_diff = diff_attention(proj, lq1, lk1, lq2, lk2, subln, lambda_init, cast_diff)
    return [fox, dif], cast_fox, cast_diff


def _row_block(w_out, j, r, rows):
    return pl.BlockSpec((None, rows, D_MODEL), lambda *_: (j, r, 0), pipeline_mode=pl.Buffered(1)), w_out


def _mlstm_mixer(x, B, S, j, gain, w_in, conv_qk, b_i, b_f, head_norm, w_out):
    proj, g_row = mlstm_projection(x, gain, w_in, _gate_bias_rows(jnp.concatenate([b_i, b_f])), conv_qk, S)
    h = mlstm(proj.reshape(B, S, -1), g_row, head_norm)
    return [h], [_whole((w_out, j))]


def kernel(x, mem, mix_norm, xattn_norm, mem_norm, ffn_norm, attn_w_in, attn_fox_bf, diff_lq1, diff_lk1, diff_lq2, diff_lk2, diff_subln, attn_w_out, mlstm_w_in, mlstm_conv_qk, mlstm_b_i, mlstm_b_f, mlstm_head_norm, mlstm_w_out, xattn_wq, xattn_wkv, xattn_wo, ffn_w_up, ffn_conv_w, ffn_conv_b, ffn_w_down, final_norm):
    B, S, D = x.shape
    depth = mix_norm.shape[0]
    x = x.reshape(B * S, D)
    rows = lambda a: a.reshape(a.shape[0], 1, -1)
    mix_norm, xattn_norm, mem_norm, ffn_norm = map(rows, (mix_norm, xattn_norm, mem_norm, ffn_norm))
    diff_lq1, diff_lk1, diff_lq2, diff_lk2, diff_subln = map(rows, (diff_lq1, diff_lk1, diff_lq2, diff_lk2, diff_subln))
    mlstm_head_norm, ffn_conv_b = rows(mlstm_head_norm), rows(ffn_conv_b)
    attn_w_in, mlstm_w_in = jnp.swapaxes(attn_w_in, 1, 2), jnp.swapaxes(mlstm_w_in, 1, 2)
    for layer in range(depth):
        j = layer // 2
        if layer % 2 == 0:
            lambda_init = 0.8 - 0.6 * math.exp(-0.3 * layer)
            cast_fox = (ffn_w_up, ffn_w_down) if layer == 0 else ()
            cast_diff = (attn_w_out, mlstm_w_out, xattn_wq, xattn_wkv, xattn_wo) if layer == 0 else ()
            mixed, cast_fox, cast_diff = _fox_diff_mixer(
                x, B, S, (mix_norm, layer), (attn_w_in, j), attn_fox_bf[j], (diff_lq1, j), (diff_lk1, j),
                (diff_lq2, j), (diff_lk2, j), (diff_subln, j), lambda_init, cast_fox, cast_diff)
            if layer == 0:
                ffn_w_up, ffn_w_down = cast_fox
                attn_w_out, mlstm_w_out, xattn_wq, xattn_wkv, xattn_wo = cast_diff
            w_mix = [_row_block(attn_w_out, j, r, FOX_HEADS * FOX_DIM) for r in range(2)]
        else:
            mixed, w_mix = _mlstm_mixer(x, B, S, j, (mix_norm, layer), (mlstm_w_in, j), (mlstm_conv_qk, j),
                                        mlstm_b_i[j], mlstm_b_f[j], (mlstm_head_norm, j), mlstm_w_out)
        x = mix_out_cross_attention(mixed, w_mix, x.reshape(B, S, D), (xattn_norm, layer), (xattn_wq, layer), mem,
                                    (mem_norm, layer), (xattn_wkv, layer), (xattn_wo, layer)).reshape(B * S, D)
        x = conv_ffn(x, (ffn_norm, layer), (ffn_w_up, layer), (ffn_conv_w, layer), (ffn_conv_b, layer),
                     (ffn_w_down, layer), S, final_norm.reshape(1, D) if layer == depth - 1 else None)
    return x.reshape(B, S, D)
```

```python
import functools
import math

import jax
import jax.numpy as jnp
from jax import lax
from jax.experimental import pallas as pl
from jax.experimental.pallas import tpu as pltpu

F32 = jnp.float32
BF16 = jnp.bfloat16

D_MODEL = 1024
RMS_EPS = 1e-6
NEG_INF = -1e30
CHUNK = 64
FOX_HEADS, FOX_DIM = 8, 64
DIFF_HEADS, DIFF_QK, DIFF_V = 4, 64, 128
ML_HEADS, ML_QK, ML_V, ML_CONV = 4, 128, 256, 4
X_HEADS, X_DIM = 4, 256
D_FF = 2816
FFN_CONV = 3
LANES = 128
BF16_ROWS = 16
VMEM_LIMIT = 56 * 1024 * 1024

ROW_TILE = 512
ATT_Q_TILE = 512
ATT_K_TILE = 512
LOG2_E = math.log2(math.e)
ML_CHUNK = 256
ML_SEQS = 2
X_TILE = 1024
PROJ_COLS = 1024
FFN_CHUNK = 256
ROW_PHASES = 4
DOWN_CHUNKS = 7
TRANSPOSE_COLS = 512
F_PIECES = 3
FOX_BIAS_K_LANE = 0
FOX_BIAS_Q_LANE = 32


def _params(*sem):
    return pltpu.CompilerParams(dimension_semantics=sem, vmem_limit_bytes=VMEM_LIMIT)


def _resident(shape):
    return pl.BlockSpec(shape, lambda *_: (0,) * len(shape), pipeline_mode=pl.Buffered(1))


def _whole(a):
    if isinstance(a, tuple):
        arr, layer = a
        tail = arr.shape[1:]
        spec = pl.BlockSpec((None,) + tail, lambda *_: (layer,) + (0,) * len(tail), pipeline_mode=pl.Buffered(1))
        return spec, arr
    return _resident(a.shape), a


def _rows(a):
    return (a[0] if isinstance(a, tuple) else a).shape[-2]


def _rms(x, gain):
    return x * lax.rsqrt(jnp.mean(x * x, axis=-1, keepdims=True) + RMS_EPS) * gain


def _sigmoid(x):
    return 1.0 / (1.0 + jnp.exp(-x))


def _dot(a, b):
    return jnp.dot(a, b, preferred_element_type=F32)


def _dot_nt(a, b):
    return lax.dot_general(a, b, (((1,), (1,)), ((), ())), preferred_element_type=F32)


def _split3(x):
    hi = x.astype(BF16)
    r1 = x - hi.astype(F32)
    mid = r1.astype(BF16)
    return hi, mid, (r1 - mid.astype(F32)).astype(BF16)


def _dot_select(x, sel):
    hi, mid, lo = _split3(x)
    return _dot(hi, sel) + _dot(mid, sel) + _dot(lo, sel)


def _prefix_matrix(n, block):
    r = lax.broadcasted_iota(jnp.int32, (n, n), 0)
    c = lax.broadcasted_iota(jnp.int32, (n, n), 1)
    return jnp.where((r <= c) & (r // block == c // block), 1.0, 0.0).astype(BF16)


def _tile_lanes(a, n):
    return jnp.concatenate([a] * (n // LANES), axis=1)


def _log_sigmoid(x):
    return jnp.minimum(x, 0.0) - jnp.log1p(jnp.exp(-jnp.abs(x)))


def _prepare_weights(wt_ref, wb_ref, wg_ref, pieces, gate_row, n_gates):
    dst = 0
    for src, height, scale in pieces:
        for r0 in range(0, height, PROJ_COLS):
            n = min(PROJ_COLS, height - r0)
            wb_ref[dst + r0:dst + r0 + n, :] = (wt_ref[src + r0:src + r0 + n, :] * scale).astype(BF16)
        dst += height
    gates = wt_ref[gate_row:gate_row + n_gates, :]
    wg_ref[...] = jnp.concatenate([gates, jnp.zeros((BF16_ROWS - n_gates, gates.shape[1]), F32)],
                                  axis=0).astype(BF16)


def _fox_proj_kernel(x_ref, g_ref, w_ref, bf_ref, o_ref, fq_ref, fk_ref, carry_ref, wb_ref, wg_ref,
                     *, tm, tn, tiles_per_seq):
    i = pl.program_id(0)

    @pl.when(i == 0)
    def _():
        fw = FOX_HEADS * FOX_DIM
        dq0 = 3 * fw + FOX_HEADS
        dqw = DIFF_HEADS * 2 * DIFF_QK
        _prepare_weights(w_ref, wb_ref, wg_ref,
                         ((0, fw, FOX_DIM ** -0.5 * LOG2_E), (fw, 2 * fw, 1.0),
                          (dq0, dqw, DIFF_QK ** -0.5 * LOG2_E), (dq0 + dqw, w_ref.shape[0] - dq0 - dqw, 1.0)),
                         3 * fw, FOX_HEADS)

    @pl.when(i % tiles_per_seq == 0)
    def _():
        carry_ref[...] = jnp.zeros_like(carry_ref)

    xn = _rms(x_ref[...], g_ref[...]).astype(BF16)

    def project(part):
        width = o_ref.shape[1] // 3
        for c0 in range(part * width, (part + 1) * width, tn):
            o_ref[:, c0:c0 + tn] = _dot_nt(xn, wb_ref[c0:c0 + tn, :]).astype(o_ref.dtype)

    gates_t = _dot_nt(wg_ref[...], xn)
    project(0)
    log_f = _log_sigmoid(gates_t + _tile_lanes(bf_ref[...], tm)) * LOG2_E
    f_t = _tile_lanes(carry_ref[...], tm) + _dot_select(log_f, _prefix_matrix(tm, tm))
    carry_ref[...] = jnp.broadcast_to(f_t[:, tm - 1:tm], carry_ref.shape)
    f = jnp.concatenate([f_t, jnp.zeros((LANES - BF16_ROWS, tm), F32)], axis=0).T
    project(1)

    src = lax.broadcasted_iota(jnp.int32, (LANES, LANES), 0)
    dst = lax.broadcasted_iota(jnp.int32, (LANES, LANES), 1)
    lane = lax.broadcasted_iota(jnp.int32, (1, LANES), 1)
    n_bias = F_PIECES * FOX_HEADS
    ones_q = jnp.where((lane >= FOX_BIAS_K_LANE) & (lane < FOX_BIAS_K_LANE + n_bias), 1.0, 0.0)
    ones_k = jnp.where((lane >= FOX_BIAS_Q_LANE) & (lane < FOX_BIAS_Q_LANE + n_bias), 1.0, 0.0)
    both = jnp.concatenate([ones_q, ones_k], axis=1)
    for c, piece in enumerate(_split3(f)):
        head = src < FOX_HEADS
        to_q = jnp.where(head & (dst == FOX_BIAS_Q_LANE + F_PIECES * src + c), 1.0, 0.0)
        to_k = jnp.where(head & (dst == FOX_BIAS_K_LANE + F_PIECES * src + c), -1.0, 0.0)
        both = both + _dot(piece, jnp.concatenate([to_q, to_k], axis=1).astype(BF16))
    fq_ref[...] = both[:, 0:LANES].astype(BF16)
    fk_ref[...] = both[:, LANES:2 * LANES].astype(BF16)
    project(2)


def fox_projection(x, gain, w, gate_bias, seq_len, *, tm=ROW_TILE, tn=PROJ_COLS):
    T, D = x.shape
    N = _rows(w) - FOX_HEADS
    row_block = lambda n: pl.BlockSpec((tm, n), lambda i: (i, 0))
    specs, operands = zip(*map(_whole, (gain, w, gate_bias)))
    return pl.pallas_call(
        functools.partial(_fox_proj_kernel, tm=tm, tn=tn, tiles_per_seq=seq_len // tm),
        grid=(T // tm,),
        in_specs=[row_block(D), *specs],
        out_specs=[row_block(N), row_block(LANES), row_block(LANES)],
        out_shape=[jax.ShapeDtypeStruct((T, N), BF16), jax.ShapeDtypeStruct((T, LANES), BF16),
                   jax.ShapeDtypeStruct((T, LANES), BF16)],
        scratch_shapes=[pltpu.VMEM((BF16_ROWS, LANES), F32), pltpu.VMEM((N, D), BF16),
                        pltpu.VMEM((BF16_ROWS, D), BF16)],
        compiler_params=_params("arbitrary"),
        name="fox_projection",
    )(x, *operands)


def _mlstm_proj_kernel(x_ref, halo_ref, g_ref, w_raw_ref, gb_ref, cw_ref, o_ref, gr_ref, up_ref, nat_ref, w_ref,
                       wg_ref, *, tm, tiles_per_seq):
    i = pl.program_id(0)

    @pl.when(i == 0)
    def _():
        g0 = 2 * ML_HEADS * ML_QK + ML_HEADS * ML_V
        og_col = g0 + 2 * ML_HEADS
        _prepare_weights(w_raw_ref, w_ref, wg_ref, ((0, g0, 1.0), (og_col, w_raw_ref.shape[0] - og_col, 1.0)),
                         g0, 2 * ML_HEADS)

    halo = _rms(halo_ref[...], g_ref[...])
    xe = jnp.concatenate([jnp.where(i % tiles_per_seq == 0, 0.0, halo).astype(BF16),
                          _rms(x_ref[...], g_ref[...]).astype(BF16)], axis=0)
    xn = xe[BF16_ROWS:]

    qk_slabs = 2 * ML_HEADS * ML_QK // LANES
    rows = tm // ROW_PHASES
    v0 = qk_slabs * LANES
    og0 = v0 + ML_HEADS * ML_V
    for s0 in range(0, qk_slabs, 2):
        up = _dot_nt(xe, w_ref[s0 * LANES:(s0 + 2) * LANES, :])
        up_ref[s0] = up[:, 0:LANES]
        up_ref[s0 + 1] = up[:, LANES:2 * LANES]
    pre_t = _dot_nt(wg_ref[...], xn) + _tile_lanes(gb_ref[...], tm)
    o_ref[:, v0:og0] = _dot_nt(xn, w_ref[v0:og0, :]).astype(o_ref.dtype)
    for s in range(qk_slabs):
        cw = cw_ref[:, s * LANES:(s + 1) * LANES]
        for phase in range(ROW_PHASES):
            y = None
            for tap in range(ML_CONV):
                first = BF16_ROWS + phase - (ML_CONV - 1 - tap)
                term = cw[tap:tap + 1] * up_ref[s, pl.ds(first, rows, stride=ROW_PHASES), :]
                y = term if y is None else y + term
            y = y * _sigmoid(y)
            if s < qk_slabs // 2:
                y = y * (ML_QK ** -0.5)
            nat_ref[s, pl.ds(phase, rows, stride=ROW_PHASES), :] = y
        o_ref[:, s * LANES:(s + 1) * LANES] = nat_ref[s].astype(o_ref.dtype)

    og = _dot_nt(xn, w_ref[og0:, :])
    row = lax.broadcasted_iota(jnp.int32, (BF16_ROWS, 1), 0)
    val_t = jnp.where(row < ML_HEADS, pre_t, _log_sigmoid(pre_t)) * LOG2_E
    g_t = jnp.where(row < ML_HEADS, val_t, _dot_select(val_t, _prefix_matrix(tm, ML_CHUNK)))
    gr_ref[0] = g_t[0:2 * ML_HEADS]
    o_ref[:, og0:] = _sigmoid(og).astype(o_ref.dtype)


def mlstm_projection(x, gain, w, gate_bias, conv_w, seq_len, *, tm=ROW_TILE):
    T, D = x.shape
    N = _rows(w) - 2 * ML_HEADS
    tiles_per_seq = seq_len // tm
    halo_blocks = tm // BF16_ROWS
    row_block = lambda n: pl.BlockSpec((tm, n), lambda i: (i, 0))
    specs, operands = zip(*map(_whole, (gain, w, gate_bias, conv_w)))
    return pl.pallas_call(
        functools.partial(_mlstm_proj_kernel, tm=tm, tiles_per_seq=tiles_per_seq),
        grid=(T // tm,),
        in_specs=[row_block(D),
                  pl.BlockSpec((BF16_ROWS, D), lambda i: (jnp.maximum(i * halo_blocks - 1, 0), 0)),
                  *specs],
        out_specs=[row_block(N),
                   pl.BlockSpec((1, 2 * ML_HEADS, tm), lambda i: (i // tiles_per_seq, 0, i % tiles_per_seq))],
        out_shape=[jax.ShapeDtypeStruct((T, N), BF16),
                   jax.ShapeDtypeStruct((T // seq_len, 2 * ML_HEADS, seq_len), F32)],
        scratch_shapes=[pltpu.VMEM((2 * ML_HEADS * ML_QK // LANES, tm + BF16_ROWS, LANES), F32),
                        pltpu.VMEM((2 * ML_HEADS * ML_QK // LANES, tm, LANES), F32),
                        pltpu.VMEM((N, D), BF16), pltpu.VMEM((BF16_ROWS, D), BF16)],
        compiler_params=_params("arbitrary"),
        name="mlstm_projection",
    )(x, x, *operands)


def _cast_blocks(arrays, grid):
    steps = grid[0] * grid[1]
    views = [a.reshape(-1, a.shape[-1]) for a in arrays]
    assert all(v.shape[0] % (steps * BF16_ROWS) == 0 for v in views)
    specs = [pl.BlockSpec((v.shape[0] // steps, v.shape[1]), lambda b, i: (b * grid[1] + i, 0)) for v in views]
    return specs, views, [jax.ShapeDtypeStruct(v.shape, BF16) for v in views]


def _with_casts(kernel, n_in, n_cast):
    def wrapped(*refs):
        cast_in = refs[n_in:n_in + n_cast]
        cast_out = refs[n_in + n_cast + 1:n_in + 2 * n_cast + 1]
        for src, dst in zip(cast_in, cast_out):
            dst[...] = src[...].astype(BF16)
        kernel(*refs[:n_in], refs[n_in + n_cast], *refs[n_in + 2 * n_cast + 1:])
    return wrapped


def _build_vt(v_ref, vt_ref, n_heads, rows):
    S = v_ref.shape[1]
    r = lax.broadcasted_iota(jnp.int32, (LANES, LANES), 0)
    c = lax.broadcasted_iota(jnp.int32, (LANES, LANES), 1)
    eye = jnp.where(r == c, 1.0, 0.0).astype(BF16)
    per_group = LANES // rows
    for g in range(n_heads // per_group):
        for c0 in range(0, S, TRANSPOSE_COLS):
            cs = slice(c0, c0 + TRANSPOSE_COLS)
            vt = _dot_nt(eye, v_ref[0, cs, g * LANES:(g + 1) * LANES]).astype(BF16)
            for k in range(per_group):
                vt_ref[g * per_group + k, 0:rows, cs] = vt[k * rows:(k + 1) * rows]
    for h in range(n_heads):
        vt_ref[h, rows:rows + BF16_ROWS, :] = jnp.ones((BF16_ROWS, S), BF16)


def _softmax_step(s_t, vt, m_ref, acc_ref, idx, queries=slice(None)):
    m_prev = m_ref[idx, :, queries]
    m_new = jnp.maximum(m_prev, jnp.max(s_t, axis=0, keepdims=True))
    p = jnp.exp2(s_t - m_new).astype(BF16)
    acc_ref[idx, :, queries] = jnp.exp2(m_prev - m_new) * acc_ref[idx, :, queries] + _dot(vt, p)
    m_ref[idx, :, queries] = m_new


def _causal_steps(i, logits, attend, n_streams, buf_a, buf_b):
    def phase(j, src, dst, next_is_diagonal=False):
        for n in range(n_streams):
            logits(j + 1, dst, n, next_is_diagonal)
            attend(j, src, n, False)

    for n in range(n_streams):
        logits(0, buf_a, n, False)

    def body(jj, carry):
        phase(2 * jj, buf_a, buf_b)
        phase(2 * jj + 1, buf_b, buf_a)
        return carry

    lax.fori_loop(0, lax.shift_right_logical(i, 1), body, 0)
    odd = lax.rem(i, 2) == 1

    @pl.when(odd)
    def _():
        phase(i - 1, buf_a, buf_b, True)
        for n in range(n_streams):
            attend(i, buf_b, n, True)

    @pl.when(jnp.logical_not(odd))
    def _():
        for n in range(n_streams):
            attend(i, buf_a, n, True)


def _fox_kernel(q_ref, k_ref, v_ref, fq_ref, fk_ref, o_ref, vt_ref, qc_ref, acc_ref, m_ref, sa_ref, sb_ref, *, tq, tk):
    i = pl.program_id(1)

    @pl.when(i == 0)
    def _():
        _build_vt(v_ref, vt_ref, FOX_HEADS, FOX_DIM)

    lane = lax.broadcasted_iota(jnp.int32, (1, LANES), 1)
    fq = fq_ref[0]
    for h in range(FOX_HEADS):
        pair, half = divmod(h, 2)
        q = q_ref[0, :, pair * LANES:(pair + 1) * LANES]
        in_head = (lane >= half * FOX_DIM) & (lane < (half + 1) * FOX_DIM)
        lo_k, lo_q = FOX_BIAS_K_LANE + F_PIECES * h, FOX_BIAS_Q_LANE + F_PIECES * h
        mine = ((lane >= lo_k) & (lane < lo_k + F_PIECES)) | ((lane >= lo_q) & (lane < lo_q + F_PIECES))
        qc_ref[h, :, 0:LANES] = jnp.where(in_head, q, jnp.zeros_like(q))
        qc_ref[h, :, LANES:2 * LANES] = jnp.where(mine, fq, jnp.zeros_like(fq))
    m_ref[...] = jnp.full(m_ref.shape, NEG_INF, F32)
    acc_ref[...] = jnp.zeros(acc_ref.shape, F32)

    half = tk // 2
    key = lax.broadcasted_iota(jnp.int32, (half, tq), 0)
    qry = lax.broadcasted_iota(jnp.int32, (half, tq), 1)
    causal = key <= qry

    def key_rows(j, part=None):
        if part is None:
            return pl.ds(pl.multiple_of(j * tk, tk), tk)
        return pl.ds(pl.multiple_of(j * tk + part * half, half), half)

    def logits(j, buf, h, diagonal):
        pair = h // 2
        kc = jnp.concatenate([k_ref[0, key_rows(j), pair * LANES:(pair + 1) * LANES], fk_ref[0, key_rows(j), :]],
                             axis=1)
        if diagonal:
            buf[h, 0:half, :] = _dot_nt(kc[0:half], qc_ref[h])
            buf[h, half:tk, half:tq] = _dot_nt(kc[half:tk], qc_ref[h, half:tq, :])
        else:
            buf[h] = _dot_nt(kc, qc_ref[h])

    def attend(j, buf, h, diagonal):
        if diagonal:
            _softmax_step(jnp.where(causal, buf[h, 0:half, :], NEG_INF), vt_ref[h, :, key_rows(j, 0)],
                          m_ref, acc_ref, h)
            _softmax_step(jnp.where(causal[:, 0:tq - half], buf[h, half:tk, half:tq], NEG_INF),
                          vt_ref[h, :, key_rows(j, 1)], m_ref, acc_ref, h, slice(half, tq))
        else:
            _softmax_step(buf[h], vt_ref[h, :, key_rows(j)], m_ref, acc_ref, h)

    _causal_steps(i, logits, attend, FOX_HEADS, sa_ref, sb_ref)

    for pair in range(FOX_HEADS // 2):
        halves = []
        for half in range(2):
            a = acc_ref[2 * pair + half]
            halves.append(a[0:FOX_DIM] / a[FOX_DIM:FOX_DIM + 1])
        o_ref[0, :, pair * LANES:(pair + 1) * LANES] = jnp.concatenate(halves, axis=0).T.astype(o_ref.dtype)


def fox_attention(proj, fq, fk, cast=(), *, tq=ATT_Q_TILE, tk=ATT_K_TILE):
    B, S, _ = proj.shape
    assert tq == tk
    width = FOX_HEADS * FOX_DIM
    rows = FOX_DIM + BF16_ROWS
    grid = (B, S // tq)
    cast_specs, cast_views, cast_shapes = _cast_blocks(cast, grid)
    out, *cast_out = pl.pallas_call(
        _with_casts(functools.partial(_fox_kernel, tq=tq, tk=tk), 5, len(cast)),
        grid=grid,
        in_specs=[pl.BlockSpec((1, tq, width), lambda b, i: (b, i, 0)),
                  pl.BlockSpec((1, S, width), lambda b, i: (b, 0, 1)),
                  pl.BlockSpec((1, S, width), lambda b, i: (b, 0, 2)),
                  pl.BlockSpec((1, tq, LANES), lambda b, i: (b, i, 0)),
                  pl.BlockSpec((1, S, LANES), lambda b, i: (b, 0, 0)),
                  *cast_specs],
        out_specs=[pl.BlockSpec((1, tq, width), lambda b, i: (b, i, 0)), *cast_specs],
        out_shape=[jax.ShapeDtypeStruct((B, S, width), BF16), *cast_shapes],
        scratch_shapes=[pltpu.VMEM((FOX_HEADS, rows, S), BF16),
                        pltpu.VMEM((FOX_HEADS, tq, 2 * LANES), BF16),
                        pltpu.VMEM((FOX_HEADS, rows, tq), F32),
                        pltpu.VMEM((FOX_HEADS, 1, tq), F32),
                        pltpu.VMEM((FOX_HEADS, tk, tq), F32),
                        pltpu.VMEM((FOX_HEADS, tk, tq), F32)],
        compiler_params=_params("parallel", "arbitrary"),
        name="fox_attention",
    )(proj, proj, proj, fq, fk, *cast_views)
    return out, [c.reshape(a.shape) for c, a in zip(cast_out, cast)]


def _diff_kernel(q_ref, k_ref, v_ref, lq1_ref, lk1_ref, lq2_ref, lk2_ref, sub_ref, o_ref, vt_ref, qc_ref, acc_ref,
                 m_ref, sa_ref, sb_ref, *, tq, tk, lambda_init):
    i = pl.program_id(1)

    @pl.when(i == 0)
    def _():
        _build_vt(v_ref, vt_ref, DIFF_HEADS, DIFF_V)

    lane = lax.broadcasted_iota(jnp.int32, (1, LANES), 1)
    for h in range(DIFF_HEADS):
        q = q_ref[0, :, h * LANES:(h + 1) * LANES]
        zero = jnp.zeros_like(q)
        qc_ref[2 * h] = jnp.where(lane < DIFF_QK, q, zero)
        qc_ref[2 * h + 1] = jnp.where(lane >= DIFF_QK, q, zero)
    m_ref[...] = jnp.full(m_ref.shape, NEG_INF, F32)
    acc_ref[...] = jnp.zeros(acc_ref.shape, F32)

    half = tk // 2
    key = lax.broadcasted_iota(jnp.int32, (half, tq), 0)
    qry = lax.broadcasted_iota(jnp.int32, (half, tq), 1)
    visible = key // CHUNK <= qry // CHUNK

    def key_rows(j, part=None):
        if part is None:
            return pl.ds(pl.multiple_of(j * tk, tk), tk)
        return pl.ds(pl.multiple_of(j * tk + part * half, half), half)

    def logits(j, buf, n, diagonal):
        k = k_ref[0, key_rows(j), (n // 2) * LANES:(n // 2 + 1) * LANES]
        if diagonal:
            buf[n, 0:half, :] = _dot_nt(k[0:half], qc_ref[n])
            buf[n, half:tk, half:tq] = _dot_nt(k[half:tk], qc_ref[n, half:tq, :])
        else:
            buf[n] = _dot_nt(k, qc_ref[n])

    def attend(j, buf, n, diagonal):
        vt = vt_ref.at[n // 2]
        if diagonal:
            _softmax_step(jnp.where(visible, buf[n, 0:half, :], NEG_INF), vt[:, key_rows(j, 0)], m_ref, acc_ref, n)
            _softmax_step(jnp.where(visible[:, 0:tq - half], buf[n, half:tk, half:tq], NEG_INF),
                          vt[:, key_rows(j, 1)], m_ref, acc_ref, n, slice(half, tq))
        else:
            _softmax_step(buf[n], vt[:, key_rows(j)], m_ref, acc_ref, n)

    _causal_steps(i, logits, attend, 2 * DIFF_HEADS, sa_ref, sb_ref)

    lam = (jnp.exp(jnp.sum(lq1_ref[...] * lk1_ref[...], axis=1, keepdims=True))
           - jnp.exp(jnp.sum(lq2_ref[...] * lk2_ref[...], axis=1, keepdims=True)) + lambda_init)
    for h in range(DIFF_HEADS):
        a1, a2 = acc_ref[2 * h], acc_ref[2 * h + 1]
        o_t = a1[0:DIFF_V] / a1[DIFF_V:DIFF_V + 1] - lam * (a2[0:DIFF_V] / a2[DIFF_V:DIFF_V + 1])
        out = _rms(o_t.T, sub_ref[...]) * (1.0 - lambda_init)
        o_ref[0, :, h * LANES:(h + 1) * LANES] = out.astype(o_ref.dtype)


def diff_attention(proj, lq1, lk1, lq2, lk2, subln, lambda_init, cast=(), *, tq=ATT_Q_TILE, tk=ATT_K_TILE):
    B, S, _ = proj.shape
    assert tq == tk
    width = DIFF_HEADS * DIFF_V
    rows = DIFF_V + BF16_ROWS
    specs, operands = zip(*map(_whole, (lq1, lk1, lq2, lk2, subln)))
    grid = (B, S // tq)
    cast_specs, cast_views, cast_shapes = _cast_blocks(cast, grid)
    out, *cast_out = pl.pallas_call(
        _with_casts(functools.partial(_diff_kernel, tq=tq, tk=tk, lambda_init=lambda_init), 8, len(cast)),
        grid=grid,
        in_specs=[pl.BlockSpec((1, tq, width), lambda b, i: (b, i, 3)),
                  pl.BlockSpec((1, S, width), lambda b, i: (b, 0, 4)),
                  pl.BlockSpec((1, S, width), lambda b, i: (b, 0, 5)),
                  *specs, *cast_specs],
        out_specs=[pl.BlockSpec((1, tq, width), lambda b, i: (b, i, 0)), *cast_specs],
        out_shape=[jax.ShapeDtypeStruct((B, S, width), BF16), *cast_shapes],
        scratch_shapes=[pltpu.VMEM((DIFF_HEADS, rows, S), BF16),
                        pltpu.VMEM((2 * DIFF_HEADS, tq, LANES), BF16),
                        pltpu.VMEM((2 * DIFF_HEADS, rows, tq), F32),
                        pltpu.VMEM((2 * DIFF_HEADS, 1, tq), F32),
                        pltpu.VMEM((2 * DIFF_HEADS, tk, tq), F32),
                        pltpu.VMEM((2 * DIFF_HEADS, tk, tq), F32)],
        compiler_params=_params("parallel", "arbitrary"),
        name="diff_attention",
    )(proj, proj, proj, *operands, *cast_views)
    return out, [c.reshape(a.shape) for c, a in zip(cast_out, cast)]


def _mlstm_kernel(qk_ref, v_ref, sg_ref, gr_ref, hn_ref, o_ref, ct_ref, m_ref, vt_ref, *, L, nb):
    c = pl.program_id(1)

    @pl.when(c == 0)
    def _():
        ct_ref[...] = jnp.zeros_like(ct_ref)
        m_ref[...] = jnp.zeros_like(m_ref)
        for n in range(nb * ML_HEADS):
            vt_ref[n, ML_V:ML_V + BF16_ROWS, :] = jnp.ones((BF16_ROWS, L), BF16)

    src = lax.broadcasted_iota(jnp.int32, (L, L), 0)
    dst = lax.broadcasted_iota(jnp.int32, (L, L), 1)
    causal = src <= dst
    eye = jnp.where(src == dst, 1.0, 0.0).astype(BF16)
    k0 = ML_HEADS * ML_QK
    for b in range(nb):
        gr = gr_ref[b]
        gc = jnp.concatenate([gr, jnp.zeros((LANES - gr.shape[0], L), F32)], axis=0).T
        for h in range(ML_HEADS):
            n = b * ML_HEADS + h
            q = qk_ref[b, :, h * ML_QK:(h + 1) * ML_QK]
            k = qk_ref[b, :, k0 + h * ML_QK:k0 + (h + 1) * ML_QK]
            vt_ref[n, 0:ML_V, :] = _dot_nt(eye, v_ref[b, :, h * ML_V:(h + 1) * ML_V]).astype(BF16)
            v_t = vt_ref[n]
            r_col = gc[:, h:h + 1] - gc[:, ML_HEADS + h:ML_HEADS + h + 1]
            b_row = gr[ML_HEADS + h:ML_HEADS + h + 1, :]
            r_row = gr[h:h + 1, :] - b_row
            g = b_row[:, L - 1:L]
            ct = ct_ref[n]
            m = m_ref[n][:, 0:1]

            dm = jnp.where(causal, r_col, NEG_INF)
            mt = jnp.maximum(m, jnp.max(dm, axis=0, keepdims=True))
            s_t = _dot_nt(k, q) * jnp.exp2(dm - mt)
            both = jnp.exp2(m - mt) * _dot_nt(ct.astype(BF16), q) + _dot(v_t, s_t.astype(BF16))
            den = both[ML_V:ML_V + 1]
            hh = both[0:ML_V] / jnp.maximum(jnp.abs(den), jnp.exp2(-(b_row + mt)))
            hh = hh * lax.rsqrt(jnp.mean(hh * hh, axis=0, keepdims=True) + RMS_EPS)

            m_next = jnp.maximum(m, jnp.max(r_row, axis=1, keepdims=True))
            kw = (k.astype(F32) * jnp.exp2(r_col - m_next)).astype(BF16)
            ct_ref[n] = jnp.exp2(m - m_next) * ct + _dot(v_t, kw)
            m_ref[n] = jnp.broadcast_to(g + m_next, (1, LANES))

            vs = slice(h * ML_V, (h + 1) * ML_V)
            o_ref[b, :, vs] = (hh.T * hn_ref[:, vs] * sg_ref[b, :, vs].astype(F32)).astype(o_ref.dtype)


def mlstm(proj, g_row, head_norm, *, L=ML_CHUNK, nb=ML_SEQS):
    B, S, _ = proj.shape
    W = D_MODEL
    assert L == ML_V, "one identity matrix serves the v transposes"
    norm_spec, head_norm = _whole(head_norm)
    return pl.pallas_call(
        functools.partial(_mlstm_kernel, L=L, nb=nb),
        grid=(B // nb, S // L),
        in_specs=[pl.BlockSpec((nb, L, W), lambda b, c: (b, c, 0)),
                  pl.BlockSpec((nb, L, W), lambda b, c: (b, c, 1)),
                  pl.BlockSpec((nb, L, W), lambda b, c: (b, c, 2)),
                  pl.BlockSpec((nb, 2 * ML_HEADS, L), lambda b, c: (b, 0, c)),
                  norm_spec],
        out_specs=pl.BlockSpec((nb, L, W), lambda b, c: (b, c, 0)),
        out_shape=jax.ShapeDtypeStruct((B, S, W), BF16),
        scratch_shapes=[pltpu.VMEM((nb * ML_HEADS, ML_V + BF16_ROWS, ML_QK), F32),
                        pltpu.VMEM((nb * ML_HEADS, 1, LANES), F32),
                        pltpu.VMEM((nb * ML_HEADS, ML_V + BF16_ROWS, L), BF16)],
        compiler_params=_params("parallel", "arbitrary"),
        name="mlstm",
    )(proj, proj, proj, g_row, head_norm)


def _xattn_kernel(*refs, n_in):
    x_ref, g_ref, wq_ref, mem_ref, gm_ref, wkv_ref, wo_ref, o_ref, kv_ref = refs[2 * n_in:]

    @pl.when(pl.program_id(1) == 0)
    def _():
        memn = _rms(mem_ref[0], gm_ref[...]).astype(BF16)
        for c0 in range(0, kv_ref.shape[1], PROJ_COLS):
            kv_ref[:, c0:c0 + PROJ_COLS] = _dot(memn, wkv_ref[:, c0:c0 + PROJ_COLS]).astype(BF16)

    x = x_ref[0]
    for a_ref, w_ref in zip(refs[:n_in], refs[n_in:2 * n_in]):
        x = x + _dot(a_ref[0], w_ref[...])
    xn = _rms(x, g_ref[...]).astype(BF16)
    q = (_dot(xn, wq_ref[...]) * (X_DIM ** -0.5)).astype(BF16)
    heads = []
    for h in range(X_HEADS):
        cols = slice(h * X_DIM, (h + 1) * X_DIM)
        s = _dot_nt(q[:, cols], kv_ref[:, cols])
        p = jnp.exp(s - jnp.max(s, axis=1, keepdims=True))
        l = jnp.sum(p, axis=1, keepdims=True)
        v = kv_ref[:, D_MODEL + h * X_DIM:D_MODEL + (h + 1) * X_DIM]
        heads.append((_dot(p.astype(BF16), v) / l).astype(BF16))
    o_ref[0] = x + _dot(jnp.concatenate(heads, axis=1), wo_ref[...])


def mix_out_cross_attention(acts, w_mix, x, gain, wq, mem, mem_gain, wkv, wo, *, tq=X_TILE):
    B, S, D = x.shape
    M = mem.shape[1]
    (gain_spec, gain), (wq_spec, wq), (gm_spec, mem_gain), (wkv_spec, wkv), (wo_spec, wo) = map(
        _whole, (gain, wq, mem_gain, wkv, wo))
    return pl.pallas_call(
        functools.partial(_xattn_kernel, n_in=len(acts)),
        grid=(B, S // tq),
        in_specs=([pl.BlockSpec((1, tq, a.shape[2]), lambda b, i: (b, i, 0)) for a in acts]
                  + [w[0] for w in w_mix]
                  + [pl.BlockSpec((1, tq, D), lambda b, i: (b, i, 0)),
                     gain_spec, wq_spec,
                     pl.BlockSpec((1, M, D), lambda b, i: (b, 0, 0)),
                     gm_spec, wkv_spec, wo_spec]),
        out_specs=pl.BlockSpec((1, tq, D), lambda b, i: (b, i, 0)),
        out_shape=jax.ShapeDtypeStruct((B, S, D), F32),
        scratch_shapes=[pltpu.VMEM((M, 2 * D), BF16)],
        compiler_params=_params("parallel", "arbitrary"),
        name="mix_out_cross_attention",
    )(*acts, *[w[1] for w in w_mix], x, gain, wq, mem, mem_gain, wkv, wo)


def _gelu_tanh(x):
    k = -2.0 * math.sqrt(2.0 / math.pi) * math.log2(math.e)
    return x / (1.0 + jnp.exp2(x * (k * 0.044715 * (x * x) + k)))


def _ffn_kernel(x_ref, g_ref, wup_ref, cw_ref, cb_ref, wd_ref, fg_ref, o_ref, up_ref, act_ref, nat_ref, carry_ref,
                *, tm, tiles_per_seq, final_norm):
    i = pl.program_id(0)

    @pl.when(i % tiles_per_seq == 0)
    def _():
        carry_ref[...] = jnp.zeros_like(carry_ref)

    xn = _rms(x_ref[...], g_ref[...]).astype(BF16)
    n_chunks = D_FF // FFN_CHUNK
    slabs = FFN_CHUNK // LANES
    rows = tm // ROW_PHASES

    def up_project(c):
        for half in range(2):
            col0 = half * D_FF + c * FFN_CHUNK
            up = _dot(xn, wup_ref[:, col0:col0 + FFN_CHUNK])
            for s in range(slabs):
                k = col0 // LANES + s
                up_ref[c % 2, half, s, 0:BF16_ROWS, :] = carry_ref[k]
                up_ref[c % 2, half, s, BF16_ROWS:, :] = up[:, s * LANES:(s + 1) * LANES]
                carry_ref[k] = up[tm - BF16_ROWS:, s * LANES:(s + 1) * LANES]

    def conv(c, half, s, phase):
        col0 = half * D_FF + c * FFN_CHUNK + s * LANES
        cw = cw_ref[:, col0:col0 + LANES]
        out = cb_ref[:, col0:col0 + LANES]
        for tap in range(FFN_CONV):
            first = BF16_ROWS + phase - (FFN_CONV - 1 - tap)
            out = out + cw[tap:tap + 1] * up_ref[c % 2, half, s, pl.ds(first, rows, stride=ROW_PHASES), :]
        return out

    acc = None
    piece_start = 0
    up_project(0)
    for c in range(n_chunks):
        if c + 1 < n_chunks:
            up_project(c + 1)
        for phase in range(ROW_PHASES):
            for s in range(slabs):
                act = _gelu_tanh(conv(c, 0, s, phase)) * conv(c, 1, s, phase)
                act_ref[phase * rows:(phase + 1) * rows,
                        c * FFN_CHUNK + s * LANES:c * FFN_CHUNK + (s + 1) * LANES] = act.astype(BF16)
        if (c + 1) % DOWN_CHUNKS == 0 or c + 1 == n_chunks:
            piece = slice(piece_start * FFN_CHUNK, (c + 1) * FFN_CHUNK)
            part = _dot(act_ref[:, piece], wd_ref[piece, :])
            acc = part if acc is None else acc + part
            piece_start = c + 1

    for phase in range(ROW_PHASES):
        for s in range(D_MODEL // LANES):
            nat_ref[s, pl.ds(phase, rows, stride=ROW_PHASES), :] = acc[phase * rows:(phase + 1) * rows,
                                                                       s * LANES:(s + 1) * LANES]
    y = x_ref[...] + jnp.concatenate([nat_ref[s] for s in range(D_MODEL // LANES)], axis=1)
    o_ref[...] = _rms(y, fg_ref[...]) if final_norm else y


def conv_ffn(x, gain, w_up, conv_w, conv_b, w_down, seq_len, final_gain=None, *, tm=ROW_TILE):
    T, D = x.shape
    final_norm = final_gain is not None
    specs, operands = zip(*map(_whole, (gain, w_up, conv_w, conv_b, w_down, final_gain if final_norm else gain)))
    return pl.pallas_call(
        functools.partial(_ffn_kernel, tm=tm, tiles_per_seq=seq_len // tm, final_norm=final_norm),
        grid=(T // tm,),
        in_specs=[pl.BlockSpec((tm, D), lambda i: (i, 0)), *specs],
        out_specs=pl.BlockSpec((tm, D), lambda i: (i, 0)),
        out_shape=jax.ShapeDtypeStruct((T, D), F32),
        scratch_shapes=[pltpu.VMEM((2, 2, FFN_CHUNK // LANES, tm + BF16_ROWS, LANES), F32),
                        pltpu.VMEM((tm, D_FF), BF16),
                        pltpu.VMEM((D // LANES, tm, LANES), F32),
                        pltpu.VMEM((2 * D_FF // LANES, BF16_ROWS, LANES), F32)],
        compiler_params=_params("arbitrary"),
        name="conv_ffn",
    )(x, *operands)


def _gate_bias_rows(bias):
    return jnp.broadcast_to(jnp.pad(bias, (0, BF16_ROWS - bias.shape[0]))[:, None], (BF16_ROWS, LANES))


def _fox_diff_mixer(x, B, S, gain, w_in, fox_bf, lq1, lk1, lq2, lk2, subln, lambda_init, cast_fox, cast_diff):
    proj, fq, fk = fox_projection(x, gain, w_in, _gate_bias_rows(fox_bf), S)
    proj = proj.reshape(B, S, -1)
    fox, cast_fox = fox_attention(proj, fq.reshape(B, S, LANES), fk.reshape(B, S, LANES), cast_fox)
    dif, cast_diff = diff_attention(proj, lq1, lk1, lq2, lk2, subln, lambda_init, cast_diff)
    return [fox, dif], cast_fox, cast_diff


def _row_block(w_out, j, r, rows):
    return pl.BlockSpec((None, rows, D_MODEL), lambda *_: (j, r, 0), pipeline_mode=pl.Buffered(1)), w_out


def _mlstm_mixer(x, B, S, j, gain, w_in, conv_qk, b_i, b_f, head_norm, w_out):
    proj, g_row = mlstm_projection(x, gain, w_in, _gate_bias_rows(jnp.concatenate([b_i, b_f])), conv_qk, S)
    h = mlstm(proj.reshape(B, S, -1), g_row, head_norm)
    return [h], [_whole((w_out, j))]


def kernel(x, mem, mix_norm, xattn_norm, mem_norm, ffn_norm, attn_w_in, attn_fox_bf, diff_lq1, diff_lk1, diff_lq2, diff_lk2, diff_subln, attn_w_out, mlstm_w_in, mlstm_conv_qk, mlstm_b_i, mlstm_b_f, mlstm_head_norm, mlstm_w_out, xattn_wq, xattn_wkv, xattn_wo, ffn_w_up, ffn_conv_w, ffn_conv_b, ffn_w_down, final_norm):
    B, S, D = x.shape
    depth = mix_norm.shape[0]
    x = x.reshape(B * S, D)
    rows = lambda a: a.reshape(a.shape[0], 1, -1)
    mix_norm, xattn_norm, mem_norm, ffn_norm = map(rows, (mix_norm, xattn_norm, mem_norm, ffn_norm))
    diff_lq1, diff_lk1, diff_lq2, diff_lk2, diff_subln = map(rows, (diff_lq1, diff_lk1, diff_lq2, diff_lk2, diff_subln))
    mlstm_head_norm, ffn_conv_b = rows(mlstm_head_norm), rows(ffn_conv_b)
    attn_w_in, mlstm_w_in = jnp.swapaxes(attn_w_in, 1, 2), jnp.swapaxes(mlstm_w_in, 1, 2)
    for layer in range(depth):
        j = layer // 2
        if layer % 2 == 0:
            lambda_init = 0.8 - 0.6 * math.exp(-0.3 * layer)
            cast_fox = (ffn_w_up, ffn_w_down) if layer == 0 else ()
            cast_diff = (attn_w_out, mlstm_w_out, xattn_wq, xattn_wkv, xattn_wo) if layer == 0 else ()
            mixed, cast_fox, cast_diff = _fox_diff_mixer(
                x, B, S, (mix_norm, layer), (attn_w_in, j), attn_fox_bf[j], (diff_lq1, j), (diff_lk1, j),
                (diff_lq2, j), (diff_lk2, j), (diff_subln, j), lambda_init, cast_fox, cast_diff)
            if layer == 0:
                ffn_w_up, ffn_w_down = cast_fox
                attn_w_out, mlstm_w_out, xattn_wq, xattn_wkv, xattn_wo = cast_diff
            w_mix = [_row_block(attn_w_out, j, r, FOX_HEADS * FOX_DIM) for r in range(2)]
        else:
            mixed, w_mix = _mlstm_mixer(x, B, S, j, (mix_norm, layer), (mlstm_w_in, j), (mlstm_conv_qk, j),
                                        mlstm_b_i[j], mlstm_b_f[j], (mlstm_head_norm, j), mlstm_w_out)
        x = mix_out_cross_attention(mixed, w_mix, x.reshape(B, S, D), (xattn_norm, layer), (xattn_wq, layer), mem,
                                    (mem_norm, layer), (xattn_wkv, layer), (xattn_wo, layer)).reshape(B * S, D)
        x = conv_ffn(x, (ffn_norm, layer), (ffn_w_up, layer), (ffn_conv_w, layer), (ffn_conv_b, layer),
                     (ffn_w_down, layer), S, final_norm.reshape(1, D) if layer == depth - 1 else None)
    return x.reshape(B, S, D)
```
